```python
import math
import jax
import jax.numpy as jnp
from jax import lax
import numpy as np

D_MODEL = 2048
BATCH = 1
SEQ = 8192
DEPTH = 4

HEAD_DIM = 128
MIX_HEADS = D_MODEL // HEAD_DIM
FOX_HEADS = MIX_HEADS // 4
SWA_HEADS = MIX_HEADS // 2
SWA_KV_HEADS = SWA_HEADS // 4
GLA_HEADS = MIX_HEADS - FOX_HEADS - SWA_HEADS
GLA_DK = HEAD_DIM // 2
GLA_DV = HEAD_DIM
GLA_GATE_RANK = 16
GLA_TAU = 16.0
GLA_CHUNK = 64
MIX_WIDTH = MIX_HEADS * HEAD_DIM
Q_BLOCK = 128
SWA_WINDOW = 128
NUM_BUCKETS = 32
T5_MAX_DISTANCE = 128
XATTN_HEADS = 4
MEM_LEN = 256
D_FF = ((8 * D_MODEL // 3 + 255) // 256) * 256
CONV_WIDTH = 3
EPS = 1e-6

IN_SPLITS = (FOX_HEADS * HEAD_DIM, FOX_HEADS * HEAD_DIM, FOX_HEADS * HEAD_DIM, FOX_HEADS,
             SWA_HEADS * HEAD_DIM, SWA_KV_HEADS * HEAD_DIM, SWA_KV_HEADS * HEAD_DIM,
             GLA_HEADS * GLA_DK, GLA_HEADS * GLA_DK, GLA_HEADS * GLA_DV, GLA_HEADS * GLA_DV,
             GLA_GATE_RANK)
IN_COLS = sum(IN_SPLITS)
IN_SPLIT_POINTS = tuple(int(v) for v in np.cumsum(IN_SPLITS)[:-1])

kernel_name = 'hybrid_fox_swa_gla_trunk'


def rms_norm(x, g):
    xf = x.astype(jnp.float32)
    y = xf * lax.rsqrt(jnp.mean(xf * xf, axis=-1, keepdims=True) + EPS)
    return (y * g.astype(jnp.float32)).astype(x.dtype)


def fox_attention(q, k, v, f_logit):
    B, H, T, Dh = q.shape
    nb = T // Q_BLOCK
    c = jnp.cumsum(jax.nn.log_sigmoid(f_logit.astype(jnp.float32)), axis=-1)
    qb = q.reshape(B, H, nb, Q_BLOCK, Dh).transpose(2, 0, 1, 3, 4)
    cb = c.reshape(B, H, nb, Q_BLOCK).transpose(2, 0, 1, 3)
    kpos = jnp.arange(T)
    scale = Dh ** -0.5

    def block(args):
        qi, ci, i = args
        s = jnp.einsum('bhqd,bhkd->bhqk', qi, k).astype(jnp.float32) * scale
        s = s + ci[..., None] - c[:, :, None, :]
        qpos = i * Q_BLOCK + jnp.arange(Q_BLOCK)
        s = jnp.where(kpos[None, :] <= qpos[:, None], s, -jnp.inf)
        p = jax.nn.softmax(s, axis=-1).astype(v.dtype)
        return jnp.einsum('bhqk,bhkd->bhqd', p, v)

    o = lax.map(block, (qb, cb, jnp.arange(nb)))
    return o.transpose(1, 2, 0, 3, 4).reshape(B, H, T, Dh)


def t5_bucket(rel):
    n = jnp.maximum(rel, 0)
    max_exact = NUM_BUCKETS // 2
    nf = jnp.maximum(n, 1).astype(jnp.float32)
    large = max_exact + (jnp.log(nf / max_exact) / math.log(T5_MAX_DISTANCE / max_exact)
                         * (NUM_BUCKETS - max_exact)).astype(jnp.int32)
    large = jnp.minimum(large, NUM_BUCKETS - 1)
    return jnp.where(n < max_exact, n, large)


def swa_attention(q, k, v, sinks, t5_bias):
    B, Hq, T, Dh = q.shape
    Hkv = k.shape[1]
    G = Hq // Hkv
    W = SWA_WINDOW
    nb = T // W
    qb = q.reshape(B, Hkv, G, nb, W, Dh)

    def band(a):
        ab = a.reshape(B, Hkv, nb, W, Dh)
        prev = jnp.pad(ab[:, :, :-1], ((0, 0), (0, 0), (1, 0), (0, 0), (0, 0)))
        return jnp.concatenate([prev, ab], axis=3)

    kb, vb = band(k), band(v)
    s = jnp.einsum('bhgnqd,bhnkd->bhgnqk', qb, kb).astype(jnp.float32) * (Dh ** -0.5)
    i = jnp.arange(W)
    j = jnp.arange(2 * W)
    rel = (W + i)[:, None] - j[None, :]
    bias = t5_bias[t5_bucket(rel)].astype(jnp.float32)
    bias = jnp.transpose(bias, (2, 0, 1)).reshape(Hkv, G, 1, W, 2 * W)
    s = s + bias
    in_window = (rel >= 0) & (rel < W)
    blk = jnp.arange(nb)
    valid = in_window[None] & ((blk[:, None, None] > 0) | (j[None, None, :] >= W))
    s = jnp.where(valid, s, -jnp.inf)
    sink = jnp.broadcast_to(sinks.astype(jnp.float32).reshape(1, Hkv, G, 1, 1, 1), s.shape[:-1] + (1,))
    p = jax.nn.softmax(jnp.concatenate([s, sink], axis=-1), axis=-1)[..., :-1].astype(v.dtype)
    o = jnp.einsum('bhgnqk,bhnkd->bhgnqd', p, vb)
    return o.reshape(B, Hq, T, Dh)


def gla_chunked(q, k, v, log_a):
    B, H, T, dk = q.shape
    dv = v.shape[-1]
    C = GLA_CHUNK
    nc = T // C
    f32 = jnp.float32
    qc = (q.astype(f32) * (dk ** -0.5)).reshape(B, H, nc, C, dk)
    kc = k.astype(f32).reshape(B, H, nc, C, dk)
    vc = v.astype(f32).reshape(B, H, nc, C, dv)
    gc = log_a.astype(f32).reshape(B, H, nc, C, dk)
    b = jnp.cumsum(gc, axis=3)
    b_last = b[:, :, :, -1:, :]
    q_t = qc * jnp.exp(b)
    k_t = kc * jnp.exp(-b)
    k_end = kc * jnp.exp(b_last - b)
    causal = jnp.tril(jnp.ones((C, C), dtype=bool))
    A = jnp.where(causal, jnp.einsum('bhnqd,bhnkd->bhnqk', q_t, k_t), 0.0)
    o_intra = jnp.einsum('bhnqk,bhnkd->bhnqd', A, vc)
    kv = jnp.einsum('bhnkd,bhnke->bhnde', k_end, vc)
    decay = jnp.exp(b_last[:, :, :, 0, :])

    def step(S, inp):
        dec, kv_n = inp
        return dec[..., None] * S + kv_n, S

    S0 = jnp.zeros((B, H, dk, dv), f32)
    _, S_prev = lax.scan(step, S0, (jnp.moveaxis(decay, 2, 0), jnp.moveaxis(kv, 2, 0)))
    S_prev = jnp.moveaxis(S_prev, 0, 2)
    o_inter = jnp.einsum('bhnqd,bhnde->bhnqe', q_t, S_prev)
    return (o_intra + o_inter).reshape(B, H, T, dv).astype(v.dtype)


def hybrid_mixer(xn, w_in, b_f, sinks, t5_bias, w_gate, b_gate, gla_g, w_out):
    B, T, _ = xn.shape
    proj = xn @ w_in
    (fq, fk, fv, ff, sq, sk, sv, gq, gk, gv, gr, glr) = jnp.split(proj, IN_SPLIT_POINTS, axis=-1)

    def heads(a, n):
        return a.reshape(B, T, n, -1).transpose(0, 2, 1, 3)

    def merge(a):
        return a.transpose(0, 2, 1, 3).reshape(B, T, -1)

    o_fox = fox_attention(heads(fq, FOX_HEADS), heads(fk, FOX_HEADS), heads(fv, FOX_HEADS),
                          (ff + b_f).transpose(0, 2, 1))
    o_swa = swa_attention(heads(sq, SWA_HEADS), heads(sk, SWA_KV_HEADS), heads(sv, SWA_KV_HEADS),
                          sinks, t5_bias)
    log_a = jax.nn.log_sigmoid((glr @ w_gate + b_gate).astype(jnp.float32)) / GLA_TAU
    o_gla = gla_chunked(heads(gq, GLA_HEADS), heads(gk, GLA_HEADS), heads(gv, GLA_HEADS),
                        heads(log_a, GLA_HEADS))
    o_gla = merge(rms_norm(o_gla, gla_g)) * jax.nn.silu(gr)
    o = jnp.concatenate([merge(o_fox), merge(o_swa), o_gla], axis=-1)
    return o @ w_out


def cross_attention(xn, memn, wq, wkv, wo):
    B, T, _ = xn.shape
    M = memn.shape[1]
    q = (xn @ wq).reshape(B, T, XATTN_HEADS, HEAD_DIM)
    k, v = jnp.split((memn @ wkv).reshape(B, M, 2 * XATTN_HEADS, HEAD_DIM), 2, axis=2)
    s = jnp.einsum('bthd,bmhd->bhtm', q, k).astype(jnp.float32) * (HEAD_DIM ** -0.5)
    p = jax.nn.softmax(s, axis=-1).astype(v.dtype)
    o = jnp.einsum('bhtm,bmhd->bthd', p, v).reshape(B, T, XATTN_HEADS * HEAD_DIM)
    return o @ wo


def conv_ffn(xn, w_up, conv_w, conv_b, w_down):
    T = xn.shape[1]
    u = xn @ w_up
    up = jnp.pad(u, ((0, 0), (CONV_WIDTH - 1, 0), (0, 0)))
    uc = conv_b + conv_w[CONV_WIDTH - 1] * u
    for j in range(CONV_WIDTH - 1):
        uc = uc + conv_w[j] * up[:, j:j + T]
    gate, val = jnp.split(uc, 2, axis=-1)
    return (jax.nn.silu(gate) * val) @ w_down


def setup_inputs(seed: int = 0) -> dict:
    key = jax.random.key(seed)
    ks = jax.random.split(key, 24)
    nrm = jax.random.normal
    f32 = jnp.float32
    XW = XATTN_HEADS * HEAD_DIM
    return {
        'x': nrm(ks[0], (BATCH, SEQ, D_MODEL), f32),
        'mem': nrm(ks[1], (BATCH, MEM_LEN, D_MODEL), f32),
        'w_in': nrm(ks[2], (DEPTH, D_MODEL, IN_COLS), f32) * D_MODEL ** -0.5,
        'b_fox_f': 3.0 + 0.5 * nrm(ks[3], (DEPTH, FOX_HEADS), f32),
        'swa_sinks': 0.5 * nrm(ks[4], (DEPTH, SWA_HEADS), f32),
        't5_bias': 0.5 * nrm(ks[5], (NUM_BUCKETS, SWA_HEADS), f32),
        'w_gla_gate': nrm(ks[6], (DEPTH, GLA_GATE_RANK, GLA_HEADS * GLA_DK), f32) * GLA_GATE_RANK ** -0.5,
        'b_gla_gate': 0.1 * nrm(ks[7], (DEPTH, GLA_HEADS * GLA_DK), f32),
        'gla_norm': 1.0 + 0.02 * nrm(ks[8], (DEPTH, GLA_DV), f32),
        'w_mix_out': nrm(ks[9], (DEPTH, MIX_WIDTH, D_MODEL), f32) * MIX_WIDTH ** -0.5,
        'norm_mix': 1.0 + 0.02 * nrm(ks[10], (DEPTH, D_MODEL), f32),
        'norm_xattn': 1.0 + 0.02 * nrm(ks[11], (DEPTH, D_MODEL), f32),
        'norm_mem': 1.0 + 0.02 * nrm(ks[12], (DEPTH, D_MODEL), f32),
        'wq_x': nrm(ks[13], (DEPTH, D_MODEL, XW), f32) * D_MODEL ** -0.5,
        'wkv_x': nrm(ks[14], (DEPTH, D_MODEL, 2 * XW), f32) * D_MODEL ** -0.5,
        'wo_x': nrm(ks[15], (DEPTH, XW, D_MODEL), f32) * XW ** -0.5,
        'norm_ffn': 1.0 + 0.02 * nrm(ks[16], (DEPTH, D_MODEL), f32),
        'w_up': nrm(ks[17], (DEPTH, D_MODEL, 2 * D_FF), f32) * D_MODEL ** -0.5,
        'conv_w': nrm(ks[18], (DEPTH, CONV_WIDTH, 2 * D_FF), f32) * CONV_WIDTH ** -0.5,
        'conv_b': 0.02 * nrm(ks[19], (DEPTH, 2 * D_FF), f32),
        'w_down': nrm(ks[20], (DEPTH, D_FF, D_MODEL), f32) * D_FF ** -0.5,
        'final_norm': 1.0 + 0.02 * nrm(ks[21], (D_MODEL,), f32),
    }


def reference(x, mem, w_in, b_fox_f, swa_sinks, t5_bias, w_gla_gate, b_gla_gate, gla_norm,
              w_mix_out, norm_mix, norm_xattn, norm_mem, wq_x, wkv_x, wo_x, norm_ffn,
              w_up, conv_w, conv_b, w_down, final_norm):
    h = x
    for l in range(DEPTH):
        h = h + hybrid_mixer(rms_norm(h, norm_mix[l]), w_in[l], b_fox_f[l], swa_sinks[l], t5_bias,
                             w_gla_gate[l], b_gla_gate[l], gla_norm[l], w_mix_out[l])
        h = h + cross_attention(rms_norm(h, norm_xattn[l]), rms_norm(mem, norm_mem[l]),
                                wq_x[l], wkv_x[l], wo_x[l])
        h = h + conv_ffn(rms_norm(h, norm_ffn[l]), w_up[l], conv_w[l], conv_b[l], w_down[l])
    return rms_norm(h, final_norm)
```

```python
import functools
import math

import numpy as np
import jax
import jax.numpy as jnp
from jax import lax
from jax.experimental import pallas as pl
from jax.experimental.pallas import tpu as pltpu

F32 = jnp.float32
BF16 = jnp.bfloat16

HEAD_DIM = 128
FOX_HEADS = 4
SWA_HEADS = 8
SWA_KV_HEADS = 2
SWA_GROUP = SWA_HEADS // SWA_KV_HEADS
GLA_HEADS = 4
GLA_DK = 64
GLA_DV = 128
GLA_GATE_RANK = 16
GLA_TAU = 16.0
GLA_CHUNK = 64
SWA_WINDOW = 128
NUM_BUCKETS = 32
T5_MAX_DISTANCE = 128
XATTN_HEADS = 4
CONV_WIDTH = 3
EPS = 1e-6

LANES = 128
SUBLANES = 8
BF16_ROWS = 16
VMEM_LIMIT = 56 * 1024 * 1024

_FOX_W = FOX_HEADS * HEAD_DIM
_OFF_FF = 3 * _FOX_W
_OFF_SQ = _OFF_FF + FOX_HEADS
_MAIN_B = (SWA_HEADS + 2 * SWA_KV_HEADS) * HEAD_DIM + 2 * GLA_HEADS * GLA_DK + 2 * GLA_HEADS * GLA_DV
_OFF_GLR = _OFF_SQ + _MAIN_B
C_FQ, C_FK, C_FV = 0, 512, 1024
C_SQ, C_SK, C_SV = 1536, 2560, 2816
C_GQ, C_GK, C_GV, C_GR = 3072, 3328, 3584, 4096
MAIN_COLS = 4608
S_FF, S_GLR = 0, FOX_HEADS


def _cparams(sem):
    return pltpu.CompilerParams(dimension_semantics=sem, vmem_limit_bytes=VMEM_LIMIT)


def _rms(x, g):
    return x * lax.rsqrt(jnp.mean(x * x, axis=-1, keepdims=True) + EPS) * g


def _split3(x):
    hi = x.astype(BF16)
    r1 = x - hi.astype(F32)
    mid = r1.astype(BF16)
    lo = (r1 - mid.astype(F32)).astype(BF16)
    return hi, mid, lo


def _dot(a, b):
    return jnp.dot(a, b, preferred_element_type=F32)


def _dot_nt(a, b):
    return lax.dot_general(a, b, (((1,), (1,)), ((), ())), preferred_element_type=F32)


def _dot_tn(a, b):
    return lax.dot_general(a, b, (((0,), (0,)), ((), ())), preferred_element_type=F32)


def _sel_dot(sel, x):
    hi, mid, lo = _split3(x)
    return _dot(sel, hi) + _dot(sel, mid) + _dot(sel, lo)


def _norm_proj_kernel(*refs, has_small):
    if has_small:
        x_ref, g_ref, w_ref, cs_ref, ws_ref, o_ref, os_ref, xn_ref = refs
    else:
        x_ref, g_ref, w_ref, cs_ref, o_ref, xn_ref = refs

    @pl.when(pl.program_id(1) == 0)
    def _():
        xn = _rms(x_ref[...], g_ref[...]).astype(BF16)
        xn_ref[...] = xn
        if has_small:
            os_ref[...] = _dot(xn, ws_ref[...])

    o_ref[...] = (_dot(xn_ref[...], w_ref[...]) * cs_ref[...]).astype(o_ref.dtype)


def norm_proj(x, g, w, colscale, w_small=None, *, tm, tn):
    T, D = x.shape
    N = w.shape[1]
    has_small = w_small is not None
    in_specs = [
        pl.BlockSpec((tm, D), lambda i, j: (i, 0)),
        pl.BlockSpec((1, D), lambda i, j: (0, 0)),
        pl.BlockSpec((D, tn), lambda i, j: (0, j)),
        pl.BlockSpec((1, tn), lambda i, j: (0, j)),
    ]
    args = [x, g.reshape(1, D), w, colscale]
    out_shape = [jax.ShapeDtypeStruct((T, N), BF16)]
    out_specs = [pl.BlockSpec((tm, tn), lambda i, j: (i, j))]
    if has_small:
        in_specs.append(pl.BlockSpec((D, LANES), lambda i, j: (0, 0)))
        args.append(w_small)
        out_shape.append(jax.ShapeDtypeStruct((T, LANES), F32))
        out_specs.append(pl.BlockSpec((tm, LANES), lambda i, j: (i, 0)))
    outs = pl.pallas_call(
        functools.partial(_norm_proj_kernel, has_small=has_small),
        out_shape=out_shape,
        grid=(T // tm, N // tn),
        in_specs=in_specs,
        out_specs=out_specs,
        scratch_shapes=[pltpu.VMEM((tm, D), BF16)],
        compiler_params=_cparams(("arbitrary", "arbitrary")),
        name="norm_proj_small" if has_small else "norm_proj",
    )(*args)
    return outs if has_small else outs[0]


def _fox_gate_kernel(ps_ref, b_ref, col_ref, row_ref, carry_ref, *, tb):
    @pl.when(pl.program_id(0) == 0)
    def _():
        carry_ref[...] = jnp.zeros_like(carry_ref)

    x = jax.nn.log_sigmoid(ps_ref[...] + b_ref[...])
    r = lax.broadcasted_iota(jnp.int32, (tb, tb), 0)
    c = lax.broadcasted_iota(jnp.int32, (tb, tb), 1)
    tril = jnp.where(c <= r, 1.0, 0.0).astype(BF16)
    csum = _sel_dot(tril, x) + carry_ref[0:1, :]
    carry_ref[...] = jnp.broadcast_to(csum[tb - 1:tb, :], carry_ref.shape)
    hi, mid, lo = _split3(csum)
    lane_r = lax.broadcasted_iota(jnp.int32, (LANES, LANES), 0)
    lane_s = lax.broadcasted_iota(jnp.int32, (SUBLANES, LANES), 1)
    for h in range(FOX_HEADS):
        spread = jnp.where(lane_r == S_FF + h, 1.0, 0.0).astype(BF16)
        col_ref[h] = _dot(hi, spread) + _dot(mid, spread) + _dot(lo, spread)
        pick = jnp.where(lane_s == S_FF + h, 1.0, 0.0).astype(BF16)
        row_ref[h, 0] = _dot_nt(pick, hi) + _dot_nt(pick, mid) + _dot_nt(pick, lo)


def fox_gate(ps, bvec, *, tb):
    T = ps.shape[0]
    nb = T // tb
    return pl.pallas_call(
        functools.partial(_fox_gate_kernel, tb=tb),
        out_shape=[jax.ShapeDtypeStruct((FOX_HEADS, T, LANES), F32),
                   jax.ShapeDtypeStruct((FOX_HEADS, nb, SUBLANES, tb), F32)],
        grid=(nb,),
        in_specs=[pl.BlockSpec((tb, LANES), lambda i: (i, 0)),
                  pl.BlockSpec((1, LANES), lambda i: (0, 0))],
        out_specs=[pl.BlockSpec((FOX_HEADS, tb, LANES), lambda i: (0, i, 0)),
                   pl.BlockSpec((FOX_HEADS, 1, SUBLANES, tb), lambda i: (0, i, 0, 0))],
        scratch_shapes=[pltpu.VMEM((SUBLANES, LANES), F32)],
        compiler_params=_cparams(("arbitrary",)),
        name="fox_gate",
    )(ps, bvec)


def _fox_attn_kernel(q_ref, k_ref, v_ref, col_ref, row_ref, o_ref, m_ref, l_ref, acc_ref, *, tq):
    i = pl.program_id(1)
    q = q_ref[...]
    ct = jnp.concatenate([col_ref[...]] * (tq // LANES), axis=1)
    m_ref[...] = jnp.full(m_ref.shape, -jnp.inf, F32)
    l_ref[...] = jnp.zeros(l_ref.shape, F32)
    acc_ref[...] = jnp.zeros(acc_ref.shape, F32)

    def block(j, masked):
        start = pl.multiple_of(j * tq, tq)
        k = k_ref[pl.ds(start, tq), :]
        v = v_ref[pl.ds(start, tq), :]
        z = _dot_nt(q, k) + ct - row_ref[j, 0:1, :]
        if masked:
            r = lax.broadcasted_iota(jnp.int32, (tq, tq), 0)
            c = lax.broadcasted_iota(jnp.int32, (tq, tq), 1)
            z = jnp.where(c <= r, z, -jnp.inf)
        m_old = m_ref[...]
        m_new = jnp.maximum(m_old, jnp.max(z, axis=1, keepdims=True))
        alpha = jnp.exp(m_old - m_new)
        p = jnp.exp(z - m_new)
        l_ref[...] = alpha * l_ref[...] + jnp.sum(p, axis=1, keepdims=True)
        acc_ref[...] = alpha * acc_ref[...] + _dot(p.astype(BF16), v)
        m_ref[...] = m_new

    def body(j, carry):
        block(j, False)
        return carry

    lax.fori_loop(0, i, body, 0)
    block(i, True)
    o_ref[...] = (acc_ref[...] / l_ref[...]).astype(o_ref.dtype)


def fox_attention(proj, col, row, *, tq):
    T = proj.shape[0]
    nq = T // tq
    qb, kb, vb = C_FQ // HEAD_DIM, C_FK // HEAD_DIM, C_FV // HEAD_DIM
    return pl.pallas_call(
        functools.partial(_fox_attn_kernel, tq=tq),
        out_shape=jax.ShapeDtypeStruct((T, FOX_HEADS * HEAD_DIM), BF16),
        grid=(FOX_HEADS, nq),
        in_specs=[
            pl.BlockSpec((tq, HEAD_DIM), lambda h, i: (i, qb + h)),
            pl.BlockSpec((T, HEAD_DIM), lambda h, i: (0, kb + h)),
            pl.BlockSpec((T, HEAD_DIM), lambda h, i: (0, vb + h)),
            pl.BlockSpec((None, tq, LANES), lambda h, i: (h, i, 0)),
            pl.BlockSpec((None, nq, SUBLANES, tq), lambda h, i: (h, 0, 0, 0)),
        ],
        out_specs=pl.BlockSpec((tq, HEAD_DIM), lambda h, i: (i, h)),
        scratch_shapes=[pltpu.VMEM((tq, 1), F32), pltpu.VMEM((tq, 1), F32),
                        pltpu.VMEM((tq, HEAD_DIM), F32)],
        compiler_params=_cparams(("arbitrary", "arbitrary")),
        name="fox_attn",
    )(proj, proj, proj, col, row)


def _t5_table_kernel(t5_ref, o_ref):
    h = pl.program_id(0)
    W = SWA_WINDOW
    i = lax.broadcasted_iota(jnp.int32, (W, 2 * W), 0)
    j = lax.broadcasted_iota(jnp.int32, (W, 2 * W), 1)
    rel = (W + i) - j
    n = jnp.maximum(rel, 0)
    max_exact = NUM_BUCKETS // 2
    nf = jnp.maximum(n, 1).astype(F32)
    large = max_exact + (jnp.log(nf / max_exact) / math.log(T5_MAX_DISTANCE / max_exact)
                         * (NUM_BUCKETS - max_exact)).astype(jnp.int32)
    large = jnp.minimum(large, NUM_BUCKETS - 1)
    bucket = jnp.where(n < max_exact, n, large)
    bias = jnp.zeros((W, 2 * W), F32)
    for b in range(NUM_BUCKETS):
        bias = jnp.where(bucket == b, t5_ref[b, h], bias)
    in_window = (rel >= 0) & (rel < W)
    o_ref[...] = jnp.where(in_window, bias, -jnp.inf)


def t5_table(t5_bias):
    W = SWA_WINDOW
    return pl.pallas_call(
        _t5_table_kernel,
        out_shape=jax.ShapeDtypeStruct((SWA_HEADS, W, 2 * W), F32),
        grid=(SWA_HEADS,),
        in_specs=[pl.BlockSpec(memory_space=pltpu.SMEM)],
        out_specs=pl.BlockSpec((None, W, 2 * W), lambda h: (h, 0, 0)),
        compiler_params=_cparams(("arbitrary",)),
        name="t5_table",
    )(t5_bias)


def _swa_kernel(sink_ref, q_ref, kc_ref, kp_ref, vc_ref, vp_ref, tab_ref, o_ref, *, nsub):
    g = pl.program_id(0)
    n = pl.program_id(1)
    W = SWA_WINDOW
    first_prev = lax.broadcasted_iota(jnp.int32, (W, 2 * W), 1) < W
    for sub in range(nsub):
        if sub == 0:
            kp, vp = kp_ref[...], vp_ref[...]
        else:
            kp, vp = kc_ref[(sub - 1) * W:sub * W, :], vc_ref[(sub - 1) * W:sub * W, :]
        kk = jnp.concatenate([kp, kc_ref[sub * W:(sub + 1) * W, :]], axis=0)
        vv = jnp.concatenate([vp, vc_ref[sub * W:(sub + 1) * W, :]], axis=0)
        for hh in range(SWA_GROUP):
            q = q_ref[sub * W:(sub + 1) * W, hh * HEAD_DIM:(hh + 1) * HEAD_DIM]
            s = _dot_nt(q, kk) + tab_ref[hh]
            if sub == 0:
                s = jnp.where(jnp.logical_and(first_prev, n == 0), -jnp.inf, s)
            sink = sink_ref[0, g * SWA_GROUP + hh]
            m = jnp.maximum(jnp.max(s, axis=1, keepdims=True), sink)
            p = jnp.exp(s - m)
            l = jnp.sum(p, axis=1, keepdims=True) + jnp.exp(sink - m)
            o = _dot(p.astype(BF16), vv) / l
            o_ref[sub * W:(sub + 1) * W, hh * HEAD_DIM:(hh + 1) * HEAD_DIM] = o.astype(o_ref.dtype)


def swa_attention(proj, table, sinks, *, tb):
    T = proj.shape[0]
    W = SWA_WINDOW
    nsub = tb // W
    gw = SWA_GROUP * HEAD_DIM
    qb, kb, vb = C_SQ // gw, C_SK // HEAD_DIM, C_SV // HEAD_DIM
    prev = lambda n: jnp.maximum(n * nsub - 1, 0)
    return pl.pallas_call(
        functools.partial(_swa_kernel, nsub=nsub),
        out_shape=jax.ShapeDtypeStruct((T, SWA_HEADS * HEAD_DIM), BF16),
        grid=(SWA_KV_HEADS, T // tb),
        in_specs=[
            pl.BlockSpec(memory_space=pltpu.SMEM),
            pl.BlockSpec((tb, gw), lambda g, n: (n, qb + g)),
            pl.BlockSpec((tb, HEAD_DIM), lambda g, n: (n, kb + g)),
            pl.BlockSpec((W, HEAD_DIM), lambda g, n: (prev(n), kb + g)),
            pl.BlockSpec((tb, HEAD_DIM), lambda g, n: (n, vb + g)),
            pl.BlockSpec((W, HEAD_DIM), lambda g, n: (prev(n), vb + g)),
            pl.BlockSpec((SWA_GROUP, W, 2 * W), lambda g, n: (g, 0, 0)),
        ],
        out_specs=pl.BlockSpec((tb, gw), lambda g, n: (n, g)),
        compiler_params=_cparams(("arbitrary", "arbitrary")),
        name="swa_attn",
    )(sinks.reshape(1, SWA_HEADS), proj, proj, proj, proj, proj, table)


def _gla_kernel(q_ref, k_ref, v_ref, r_ref, ps_ref, wg_ref, bg_ref, gn_ref, o_ref, st_ref, oc_ref, *, tb):
    C = GLA_CHUNK

    @pl.when(pl.program_id(0) == 0)
    def _():
        st_ref[...] = jnp.zeros_like(st_ref)

    glr = ps_ref[...].astype(BF16)
    g = jax.nn.log_sigmoid(_dot(glr, wg_ref[...]) + bg_ref[...]) / GLA_TAU
    r = lax.broadcasted_iota(jnp.int32, (tb, tb), 0)
    c = lax.broadcasted_iota(jnp.int32, (tb, tb), 1)
    same = (r // C) == (c // C)
    tril = jnp.where(jnp.logical_and(same, c <= r), 1.0, 0.0).astype(BF16)
    whole = jnp.where(same, 1.0, 0.0).astype(BF16)
    gh, gm, gl = _split3(g)
    b = _dot(tril, gh) + _dot(tril, gm) + _dot(tril, gl)
    b_last = _dot(whole, gh) + _dot(whole, gm) + _dot(whole, gl)
    q_t = (q_ref[...].astype(F32) * jnp.exp(b)).astype(BF16)
    kf = k_ref[...].astype(F32)
    k_t = kf * jnp.exp(-b)
    k_end = kf * jnp.exp(b_last - b)
    decay = jnp.exp(b_last)
    lane = lax.broadcasted_iota(jnp.int32, (1, LANES), 1)
    causal = (lax.broadcasted_iota(jnp.int32, (C, C), 1) <= lax.broadcasted_iota(jnp.int32, (C, C), 0))
    for h in range(GLA_HEADS):
        pair = slice((h // 2) * LANES, (h // 2 + 1) * LANES)
        mine = (lane // GLA_DK) == (h % 2)
        ktm = jnp.where(mine, k_t[:, pair], 0.0).astype(BF16)
        kem = jnp.where(mine, k_end[:, pair], 0.0).astype(BF16)
        qh = q_t[:, pair]
        dec = decay[:, pair]
        vh = v_ref[:, h * GLA_DV:(h + 1) * GLA_DV]
        st = st_ref[h]
        for n in range(tb // C):
            rows = slice(n * C, (n + 1) * C)
            a = jnp.where(causal, _dot_nt(qh[rows], ktm[rows]), 0.0)
            o = _dot(a.astype(BF16), vh[rows]) + _dot_nt(qh[rows], st.astype(BF16))
            oc_ref[rows, h * GLA_DV:(h + 1) * GLA_DV] = o
            kv_t = _dot_tn(vh[rows], kem[rows])
            st = dec[n * C:n * C + 1, :] * st + kv_t
        st_ref[h] = st
    for h in range(GLA_HEADS):
        cols = slice(h * GLA_DV, (h + 1) * GLA_DV)
        o = _rms(oc_ref[:, cols], gn_ref[...])
        o_ref[:, cols] = (o * jax.nn.silu(r_ref[:, cols].astype(F32))).astype(o_ref.dtype)


def gla(proj, ps, wg_pad, bg, gnorm, *, tb):
    T = proj.shape[0]
    qw = GLA_HEADS * GLA_DK
    vw = GLA_HEADS * GLA_DV
    return pl.pallas_call(
        functools.partial(_gla_kernel, tb=tb),
        out_shape=jax.ShapeDtypeStruct((T, vw), BF16),
        grid=(T // tb,),
        in_specs=[
            pl.BlockSpec((tb, qw), lambda n: (n, C_GQ // qw)),
            pl.BlockSpec((tb, qw), lambda n: (n, C_GK // qw)),
            pl.BlockSpec((tb, vw), lambda n: (n, C_GV // vw)),
            pl.BlockSpec((tb, vw), lambda n: (n, C_GR // vw)),
            pl.BlockSpec((tb, LANES), lambda n: (n, 0)),
            pl.BlockSpec((LANES, qw), lambda n: (0, 0)),
            pl.BlockSpec((1, qw), lambda n: (0, 0)),
            pl.BlockSpec((1, GLA_DV), lambda n: (0, 0)),
        ],
        out_specs=pl.BlockSpec((tb, vw), lambda n: (n, 0)),
        scratch_shapes=[pltpu.VMEM((GLA_HEADS, GLA_DV, LANES), F32),
                        pltpu.VMEM((tb, vw), F32)],
        compiler_params=_cparams(("arbitrary",)),
        name="gla",
    )(proj, proj, proj, proj, ps, wg_pad, bg, gnorm)


def _mix_xattn_kernel(h_ref, of_ref, os_ref, og_ref, wo_ref, gx_ref, wq_ref, kv_ref, wx_ref, o_ref):
    nf, ns = of_ref.shape[1], os_ref.shape[1]
    mix = (_dot(of_ref[...], wo_ref[0:nf, :]) + _dot(os_ref[...], wo_ref[nf:nf + ns, :])
           + _dot(og_ref[...], wo_ref[nf + ns:, :]))
    h1 = h_ref[...] + mix
    xn = _rms(h1, gx_ref[...]).astype(BF16)
    q = (_dot(xn, wq_ref[...]) * (HEAD_DIM ** -0.5)).astype(BF16)
    xw = XATTN_HEADS * HEAD_DIM
    outs = []
    for hh in range(XATTN_HEADS):
        cols = slice(hh * HEAD_DIM, (hh + 1) * HEAD_DIM)
        k = kv_ref[:, cols]
        v = kv_ref[:, xw + hh * HEAD_DIM:xw + (hh + 1) * HEAD_DIM]
        s = _dot_nt(q[:, cols], k)
        p = jnp.exp(s - jnp.max(s, axis=1, keepdims=True))
        l = jnp.sum(p, axis=1, keepdims=True)
        outs.append((_dot(p.astype(BF16), v) / l).astype(BF16))
    o = jnp.concatenate(outs, axis=1)
    o_ref[...] = h1 + _dot(o, wx_ref[...])


def mix_xattn(h, o_fox, o_swa, o_gla, w_out, gx, wq, kv, wo, *, tm):
    T, D = h.shape
    const = lambda shape: pl.BlockSpec(shape, lambda i: (0, 0))
    row = lambda w: pl.BlockSpec((tm, w), lambda i: (i, 0))
    return pl.pallas_call(
        _mix_xattn_kernel,
        out_shape=jax.ShapeDtypeStruct((T, D), F32),
        grid=(T // tm,),
        in_specs=[row(D), row(o_fox.shape[1]), row(o_swa.shape[1]), row(o_gla.shape[1]),
                  const(w_out.shape), const((1, D)), const(wq.shape), const(kv.shape), const(wo.shape)],
        out_specs=row(D),
        compiler_params=_cparams(("arbitrary",)),
        name="mix_xattn",
    )(h, o_fox, o_swa, o_gla, w_out, gx.reshape(1, D), wq, kv, wo)


def _ffn_kernel(x_ref, xh_ref, g_ref, wg_ref, wv_ref, cwg_ref, cwv_ref, cbg_ref, cbv_ref, wd_ref, fg_ref,
                o_ref, xn_ref, acc_ref, *, tm, final):
    i = pl.program_id(0)
    f = pl.program_id(1)
    H = BF16_ROWS

    @pl.when(f == 0)
    def _():
        halo = _rms(xh_ref[...], g_ref[...])
        xn_ref[0:H, :] = jnp.where(i > 0, halo, 0.0).astype(BF16)
        xn_ref[H:, :] = _rms(x_ref[...], g_ref[...]).astype(BF16)
        acc_ref[...] = jnp.zeros_like(acc_ref)

    xe = xn_ref[...]

    def conv(u, cw_ref, cb_ref):
        return (cb_ref[...] + cw_ref[2:3, :] * u[H:H + tm] + cw_ref[1:2, :] * u[H - 1:H - 1 + tm]
                + cw_ref[0:1, :] * u[H - 2:H - 2 + tm])

    gate = conv(_dot(xe, wg_ref[...]), cwg_ref, cbg_ref)
    val = conv(_dot(xe, wv_ref[...]), cwv_ref, cbv_ref)
    act = (jax.nn.silu(gate) * val).astype(BF16)
    acc_ref[...] += _dot(act, wd_ref[...])

    @pl.when(f == pl.num_programs(1) - 1)
    def _():
        y = x_ref[...] + acc_ref[...]
        if final:
            y = _rms(y, fg_ref[...])
        o_ref[...] = y


def conv_ffn(x, g, w_up, conv_w, conv_b, w_down, final_g, *, tm, fc, final):
    T, D = x.shape
    dff = w_down.shape[0]
    nf = dff // fc
    H = BF16_ROWS
    cb = conv_b.reshape(1, 2 * dff)
    return pl.pallas_call(
        functools.partial(_ffn_kernel, tm=tm, final=final),
        out_shape=jax.ShapeDtypeStruct((T, D), F32),
        grid=(T // tm, nf),
        in_specs=[
            pl.BlockSpec((tm, D), lambda i, f: (i, 0)),
            pl.BlockSpec((H, D), lambda i, f: (jnp.maximum(i * (tm // H) - 1, 0), 0)),
            pl.BlockSpec((1, D), lambda i, f: (0, 0)),
            pl.BlockSpec((D, fc), lambda i, f: (0, f)),
            pl.BlockSpec((D, fc), lambda i, f: (0, nf + f)),
            pl.BlockSpec((CONV_WIDTH, fc), lambda i, f: (0, f)),
            pl.BlockSpec((CONV_WIDTH, fc), lambda i, f: (0, nf + f)),
            pl.BlockSpec((1, fc), lambda i, f: (0, f)),
            pl.BlockSpec((1, fc), lambda i, f: (0, nf + f)),
            pl.BlockSpec((fc, D), lambda i, f: (f, 0)),
            pl.BlockSpec((1, D), lambda i, f: (0, 0)),
        ],
        out_specs=pl.BlockSpec((tm, D), lambda i, f: (i, 0)),
        scratch_shapes=[pltpu.VMEM((tm + H, D), BF16), pltpu.VMEM((tm, D), F32)],
        compiler_params=_cparams(("arbitrary", "arbitrary")),
        name="conv_ffn",
    )(x, x, g.reshape(1, D), w_up, w_up, conv_w, conv_w, cb, cb, w_down, final_g.reshape(1, D))


TM_PROJ, TN_PROJ = 1024, 1536
TB_GATE = 512
TQ_FOX = 512
TB_SWA = 512
TB_GLA = 512
TM_MIX = 512
TM_FFN, FC_FFN = 512, 512


def _main_colscale():
    cs = np.ones((1, MAIN_COLS), np.float32)
    cs[0, C_FQ:C_FK] = HEAD_DIM ** -0.5
    cs[0, C_SQ:C_SK] = HEAD_DIM ** -0.5
    cs[0, C_GQ:C_GK] = GLA_DK ** -0.5
    return jnp.asarray(cs)


def kernel(x, mem, w_in, b_fox_f, swa_sinks, t5_bias, w_gla_gate, b_gla_gate, gla_norm, w_mix_out, norm_mix,
           norm_xattn, norm_mem, wq_x, wkv_x, wo_x, norm_ffn, w_up, conv_w, conv_b, w_down, final_norm):
    depth = w_in.shape[0]
    _, T, D = x.shape
    M = mem.shape[1]
    h = x.reshape(T, D)
    memf = mem.reshape(M, D)
    colscale = _main_colscale()
    ones_kv = jnp.ones((1, wkv_x.shape[2]), F32)
    table = t5_table(t5_bias)
    pad_lanes = LANES - FOX_HEADS - GLA_GATE_RANK
    for l in range(depth):
        wl = w_in[l]
        w_main = jnp.concatenate([wl[:, :_OFF_FF], wl[:, _OFF_SQ:_OFF_GLR]], axis=1).astype(BF16)
        w_small = jnp.concatenate([wl[:, _OFF_FF:_OFF_SQ], wl[:, _OFF_GLR:],
                                   jnp.zeros((D, pad_lanes), F32)], axis=1).astype(BF16)
        bvec = jnp.concatenate([b_fox_f[l], jnp.zeros((LANES - FOX_HEADS,), F32)]).reshape(1, LANES)
        wg_pad = jnp.concatenate([jnp.zeros((S_GLR, GLA_HEADS * GLA_DK), F32), w_gla_gate[l],
                                  jnp.zeros((pad_lanes, GLA_HEADS * GLA_DK), F32)], axis=0).astype(BF16)

        proj, ps = norm_proj(h, norm_mix[l], w_main, colscale, w_small, tm=TM_PROJ, tn=TN_PROJ)
        col, row = fox_gate(ps, bvec, tb=TB_GATE)
        o_fox = fox_attention(proj, col, row, tq=TQ_FOX)
        o_swa = swa_attention(proj, table, swa_sinks[l], tb=TB_SWA)
        o_gla = gla(proj, ps, wg_pad, b_gla_gate[l].reshape(1, -1), gla_norm[l].reshape(1, -1), tb=TB_GLA)
        kv = norm_proj(memf, norm_mem[l], wkv_x[l].astype(BF16), ones_kv, tm=M, tn=wkv_x.shape[2])
        h = mix_xattn(h, o_fox, o_swa, o_gla, w_mix_out[l].astype(BF16), norm_xattn[l],
                      wq_x[l].astype(BF16), kv, wo_x[l].astype(BF16), tm=TM_MIX)
        h = conv_ffn(h, norm_ffn[l], w_up[l].astype(BF16), conv_w[l], conv_b[l], w_down[l].astype(BF16),
                     final_norm, tm=TM_FFN, fc=FC_FFN, final=(l == depth - 1))
    return h.reshape(x.shape)
```

```python
import functools
import math

import numpy as np
import jax
import jax.numpy as jnp
from jax import lax
from jax.experimental import pallas as pl
from jax.experimental.pallas import tpu as pltpu

F32 = jnp.float32
BF16 = jnp.bfloat16

HEAD_DIM = 128
FOX_HEADS = 4
SWA_HEADS = 8
SWA_KV_HEADS = 2
SWA_GROUP = SWA_HEADS // SWA_KV_HEADS
GLA_HEADS = 4
GLA_DK = 64
GLA_DV = 128
GLA_GATE_RANK = 16
GLA_TAU = 16.0
GLA_CHUNK = 64
SWA_WINDOW = 128
NUM_BUCKETS = 32
T5_MAX_DISTANCE = 128
XATTN_HEADS = 4
CONV_WIDTH = 3
EPS = 1e-6
LOG2E = math.log2(math.e)

LANES = 128
SUBLANES = 8
BF16_ROWS = 16
VMEM_LIMIT = 56 * 1024 * 1024

_FOX_W = FOX_HEADS * HEAD_DIM
_OFF_FF = 3 * _FOX_W
_OFF_SQ = _OFF_FF + FOX_HEADS
_MAIN_B = (SWA_HEADS + 2 * SWA_KV_HEADS) * HEAD_DIM + 2 * GLA_HEADS * GLA_DK + 2 * GLA_HEADS * GLA_DV
_OFF_GLR = _OFF_SQ + _MAIN_B
C_FQ, C_FK, C_FV = 0, 512, 1024
C_SQ, C_SK, C_SV = 1536, 2560, 2816
C_GQ, C_GK, C_GV, C_GR = 3072, 3328, 3584, 4096
MAIN_COLS = 4608
S_FF, S_GLR = 0, FOX_HEADS


def _cparams(sem):
    return pltpu.CompilerParams(dimension_semantics=sem, vmem_limit_bytes=VMEM_LIMIT)


def _rms(x, g):
    return x * lax.rsqrt(jnp.mean(x * x, axis=-1, keepdims=True) + EPS) * g


def _split3(x):
    hi = x.astype(BF16)
    r1 = x - hi.astype(F32)
    mid = r1.astype(BF16)
    lo = (r1 - mid.astype(F32)).astype(BF16)
    return hi, mid, lo


def _dot(a, b):
    return jnp.dot(a, b, preferred_element_type=F32)


def _dot_nt(a, b):
    return lax.dot_general(a, b, (((1,), (1,)), ((), ())), preferred_element_type=F32)


def _dot_tn(a, b):
    return lax.dot_general(a, b, (((0,), (0,)), ((), ())), preferred_element_type=F32)


def _sel_dot(sel, x):
    hi, mid, lo = _split3(x)
    return _dot(sel, hi) + _dot(sel, mid) + _dot(sel, lo)


def _prep_w_in_kernel(cur_ref, nxt_ref, ff_ref, glr_ref, wm_ref, ws_ref, *, tn):
    j = pl.program_id(2)
    n_plain = _OFF_FF // tn

    @pl.when(j < n_plain)
    def _():
        wm_ref[...] = cur_ref[...].astype(BF16)

    @pl.when(j >= n_plain)
    def _():
        cat = jnp.concatenate([cur_ref[...], nxt_ref[...]], axis=1)
        wm_ref[...] = cat[:, FOX_HEADS:FOX_HEADS + tn].astype(BF16)

    @pl.when(j == 0)
    def _():
        lane = lax.broadcasted_iota(jnp.int32, ff_ref.shape, 1)
        ws = jnp.where(lane < S_GLR, ff_ref[...], jnp.where(lane < S_GLR + GLA_GATE_RANK, glr_ref[...], 0.0))
        ws_ref[...] = ws.astype(BF16)


def prep_w_in(w_in, *, tr, tn):
    L, D, _ = w_in.shape
    per = tn // LANES
    assert _OFF_FF % tn == 0 and MAIN_COLS % tn == 0 and _OFF_SQ - _OFF_FF == S_GLR
    assert _OFF_GLR % LANES == S_GLR
    return pl.pallas_call(
        functools.partial(_prep_w_in_kernel, tn=tn),
        out_shape=[jax.ShapeDtypeStruct((L, D, MAIN_COLS), BF16),
                   jax.ShapeDtypeStruct((L, D, LANES), BF16)],
        grid=(L, D // tr, MAIN_COLS // tn),
        in_specs=[
            pl.BlockSpec((None, tr, tn), lambda l, r, j: (l, r, j)),
            pl.BlockSpec((None, tr, LANES), lambda l, r, j: (l, r, (j + 1) * per)),
            pl.BlockSpec((None, tr, LANES), lambda l, r, j: (l, r, _OFF_FF // LANES)),
            pl.BlockSpec((None, tr, LANES), lambda l, r, j: (l, r, _OFF_GLR // LANES)),
        ],
        out_specs=[pl.BlockSpec((None, tr, tn), lambda l, r, j: (l, r, j)),
                   pl.BlockSpec((None, tr, LANES), lambda l, r, j: (l, r, 0))],
        compiler_params=_cparams(("arbitrary", "arbitrary", "arbitrary")),
        name="prep_w_in",
    )(w_in, w_in, w_in, w_in)


def _norm_proj_kernel(*refs, has_small):
    if has_small:
        x_ref, g_ref, w_ref, cs_ref, ws_ref, o_ref, os_ref, xn_ref = refs
    else:
        x_ref, g_ref, w_ref, cs_ref, o_ref, xn_ref = refs

    @pl.when(pl.program_id(1) == 0)
    def _():
        xn = _rms(x_ref[...], g_ref[...]).astype(BF16)
        xn_ref[...] = xn
        if has_small:
            os_ref[...] = _dot(xn, ws_ref[...])

    o_ref[...] = (_dot(xn_ref[...], w_ref[...]) * cs_ref[...]).astype(o_ref.dtype)


def norm_proj(x, g, w, layer, colscale, w_small=None, *, tm, tn):
    T, D = x.shape
    N = w.shape[2]
    has_small = w_small is not None
    in_specs = [
        pl.BlockSpec((tm, D), lambda i, j: (i, 0)),
        pl.BlockSpec((1, D), lambda i, j: (0, 0)),
        pl.BlockSpec((None, D, tn), lambda i, j: (layer, 0, j)),
        pl.BlockSpec((1, tn), lambda i, j: (0, j)),
    ]
    args = [x, g.reshape(1, D), w, colscale]
    out_shape = [jax.ShapeDtypeStruct((T, N), BF16)]
    out_specs = [pl.BlockSpec((tm, tn), lambda i, j: (i, j))]
    if has_small:
        in_specs.append(pl.BlockSpec((None, D, LANES), lambda i, j: (layer, 0, 0)))
        args.append(w_small)
        out_shape.append(jax.ShapeDtypeStruct((T, LANES), F32))
        out_specs.append(pl.BlockSpec((tm, LANES), lambda i, j: (i, 0)))
    outs = pl.pallas_call(
        functools.partial(_norm_proj_kernel, has_small=has_small),
        out_shape=out_shape,
        grid=(T // tm, N // tn),
        in_specs=in_specs,
        out_specs=out_specs,
        scratch_shapes=[pltpu.VMEM((tm, D), BF16)],
        compiler_params=_cparams(("arbitrary", "arbitrary")),
        name="norm_proj_small" if has_small else "norm_proj",
    )(*args)
    return outs if has_small else outs[0]


def _fox_gate_kernel(ps_ref, b_ref, col_ref, row_ref, carry_ref, *, tb):
    @pl.when(pl.program_id(0) == 0)
    def _():
        carry_ref[...] = jnp.zeros_like(carry_ref)

    x = jax.nn.log_sigmoid(ps_ref[...] + b_ref[...])
    r = lax.broadcasted_iota(jnp.int32, (tb, tb), 0)
    c = lax.broadcasted_iota(jnp.int32, (tb, tb), 1)
    tril = jnp.where(c <= r, 1.0, 0.0).astype(BF16)
    csum = _sel_dot(tril, x) + carry_ref[0:1, :]
    carry_ref[...] = jnp.broadcast_to(csum[tb - 1:tb, :], carry_ref.shape)
    hi, mid, lo = _split3(csum * LOG2E)
    lane_r = lax.broadcasted_iota(jnp.int32, (LANES, LANES), 0)
    lane_s = lax.broadcasted_iota(jnp.int32, (SUBLANES, LANES), 1)
    for h in range(FOX_HEADS):
        spread = jnp.where(lane_r == S_FF + h, 1.0, 0.0).astype(BF16)
        col_ref[h] = _dot(hi, spread) + _dot(mid, spread) + _dot(lo, spread)
        pick = jnp.where(lane_s == S_FF + h, 1.0, 0.0).astype(BF16)
        row_ref[h, 0] = _dot_nt(pick, hi) + _dot_nt(pick, mid) + _dot_nt(pick, lo)


def fox_gate(ps, bvec, *, tb):
    T = ps.shape[0]
    nb = T // tb
    return pl.pallas_call(
        functools.partial(_fox_gate_kernel, tb=tb),
        out_shape=[jax.ShapeDtypeStruct((FOX_HEADS, T, LANES), F32),
                   jax.ShapeDtypeStruct((FOX_HEADS, nb, SUBLANES, tb), F32)],
        grid=(nb,),
        in_specs=[pl.BlockSpec((tb, LANES), lambda i: (i, 0)),
                  pl.BlockSpec((1, LANES), lambda i: (0, 0))],
        out_specs=[pl.BlockSpec((FOX_HEADS, tb, LANES), lambda i: (0, i, 0)),
                   pl.BlockSpec((FOX_HEADS, 1, SUBLANES, tb), lambda i: (0, i, 0, 0))],
        scratch_shapes=[pltpu.VMEM((SUBLANES, LANES), F32)],
        compiler_params=_cparams(("arbitrary",)),
        name="fox_gate",
    )(ps, bvec)


def _fox_attn_kernel(q_ref, k_ref, v_ref, ck_ref, cq_ref, o_ref, vt_ref, ya_ref, yb_ref, m_ref, l_ref, acc_ref, *, tq, nk):
    i = pl.program_id(1)

    @pl.when(i == 0)
    def _():
        for n in range(nk):
            vt_ref[n] = v_ref[n * tq:(n + 1) * tq, :].T

    q = q_ref[...]
    cq = cq_ref[0, 0:1, :]
    m_ref[...] = jnp.full(m_ref.shape, -jnp.inf, F32)
    l_ref[...] = jnp.zeros(l_ref.shape, F32)
    acc_ref[...] = jnp.zeros(acc_ref.shape, F32)

    def scores(j):
        start = pl.multiple_of(j * tq, tq)
        ck = ck_ref[pl.ds(start, tq), :]
        return _dot_nt(k_ref[pl.ds(start, tq), :], q) - jnp.concatenate([ck] * (tq // LANES), axis=1)

    def update(j, y):
        m_old = m_ref[...]
        m_new = jnp.maximum(m_old, jnp.max(y, axis=0, keepdims=True) + cq)
        alpha = jnp.exp2(m_old - m_new)
        p = jnp.exp2(y + (cq - m_new))
        l_ref[...] = alpha * l_ref[...] + jnp.sum(p, axis=0, keepdims=True)
        acc_ref[...] = alpha * acc_ref[...] + _dot(vt_ref[j], p.astype(BF16))
        m_ref[...] = m_new

    ya_ref[...] = scores(0)

    def step(j, cur_ref, nxt_ref):
        nxt_ref[...] = scores(j + 1)
        update(j, cur_ref[...])

    def body(j, carry):
        pl.when(j % 2 == 0)(lambda: step(j, ya_ref, yb_ref))
        pl.when(j % 2 == 1)(lambda: step(j, yb_ref, ya_ref))
        return carry

    lax.fori_loop(0, i, body, 0)
    key = lax.broadcasted_iota(jnp.int32, (tq, tq), 0)
    qry = lax.broadcasted_iota(jnp.int32, (tq, tq), 1)

    def diagonal(cur_ref):
        update(i, jnp.where(key <= qry, cur_ref[...], -jnp.inf))

    pl.when(i % 2 == 0)(lambda: diagonal(ya_ref))
    pl.when(i % 2 == 1)(lambda: diagonal(yb_ref))
    o_ref[...] = (acc_ref[...] / l_ref[...]).T.astype(o_ref.dtype)


def fox_attention(proj, col, row, *, tq):
    T = proj.shape[0]
    nq = T // tq
    qb, kb, vb = C_FQ // HEAD_DIM, C_FK // HEAD_DIM, C_FV // HEAD_DIM
    return pl.pallas_call(
        functools.partial(_fox_attn_kernel, tq=tq, nk=nq),
        out_shape=jax.ShapeDtypeStruct((T, FOX_HEADS * HEAD_DIM), BF16),
        grid=(FOX_HEADS, nq),
        in_specs=[
            pl.BlockSpec((tq, HEAD_DIM), lambda h, i: (i, qb + h)),
            pl.BlockSpec((T, HEAD_DIM), lambda h, i: (0, kb + h)),
            pl.BlockSpec((T, HEAD_DIM), lambda h, i: (0, vb + h)),
            pl.BlockSpec((None, T, LANES), lambda h, i: (h, 0, 0)),
            pl.BlockSpec((None, 1, SUBLANES, tq), lambda h, i: (h, i, 0, 0)),
        ],
        out_specs=pl.BlockSpec((tq, HEAD_DIM), lambda h, i: (i, h)),
        scratch_shapes=[pltpu.VMEM((nq, HEAD_DIM, tq), BF16), pltpu.VMEM((tq, tq), F32), pltpu.VMEM((tq, tq), F32),
                        pltpu.VMEM((1, tq), F32), pltpu.VMEM((1, tq), F32),
                        pltpu.VMEM((HEAD_DIM, tq), F32)],
        compiler_params=_cparams(("arbitrary", "arbitrary")),
        name="fox_attn",
    )(proj, proj, proj, col, row)


def _t5_table_kernel(t5_ref, o_ref):
    h = pl.program_id(0)
    W = SWA_WINDOW
    i = lax.broadcasted_iota(jnp.int32, (W, 2 * W), 0)
    j = lax.broadcasted_iota(jnp.int32, (W, 2 * W), 1)
    rel = (W + i) - j
    n = jnp.maximum(rel, 0)
    max_exact = NUM_BUCKETS // 2
    nf = jnp.maximum(n, 1).astype(F32)
    large = max_exact + (jnp.log(nf / max_exact) / math.log(T5_MAX_DISTANCE / max_exact)
                         * (NUM_BUCKETS - max_exact)).astype(jnp.int32)
    large = jnp.minimum(large, NUM_BUCKETS - 1)
    bucket = jnp.where(n < max_exact, n, large)
    bias = jnp.zeros((W, 2 * W), F32)
    for b in range(NUM_BUCKETS):
        bias = jnp.where(bucket == b, t5_ref[b, h], bias)
    in_window = (rel >= 0) & (rel < W)
    o_ref[...] = jnp.where(in_window, bias, -jnp.inf)


def t5_table(t5_bias):
    W = SWA_WINDOW
    return pl.pallas_call(
        _t5_table_kernel,
        out_shape=jax.ShapeDtypeStruct((SWA_HEADS, W, 2 * W), F32),
        grid=(SWA_HEADS,),
        in_specs=[pl.BlockSpec(memory_space=pltpu.SMEM)],
        out_specs=pl.BlockSpec((None, W, 2 * W), lambda h: (h, 0, 0)),
        compiler_params=_cparams(("arbitrary",)),
        name="t5_table",
    )(t5_bias)


def _swa_kernel(sink_ref, q_ref, kc_ref, kp_ref, vc_ref, vp_ref, tab_ref, o_ref, *, nsub):
    g = pl.program_id(0)
    n = pl.program_id(1)
    W = SWA_WINDOW
    first_prev = lax.broadcasted_iota(jnp.int32, (W, 2 * W), 1) < W
    for sub in range(nsub):
        if sub == 0:
            kp, vp = kp_ref[...], vp_ref[...]
        else:
            kp, vp = kc_ref[(sub - 1) * W:sub * W, :], vc_ref[(sub - 1) * W:sub * W, :]
        kk = jnp.concatenate([kp, kc_ref[sub * W:(sub + 1) * W, :]], axis=0)
        vv = jnp.concatenate([vp, vc_ref[sub * W:(sub + 1) * W, :]], axis=0)
        for hh in range(SWA_GROUP):
            q = q_ref[sub * W:(sub + 1) * W, hh * HEAD_DIM:(hh + 1) * HEAD_DIM]
            s = _dot_nt(q, kk) + tab_ref[hh]
            if sub == 0:
                s = jnp.where(jnp.logical_and(first_prev, n == 0), -jnp.inf, s)
            sink = sink_ref[0, g * SWA_GROUP + hh]
            m = jnp.maximum(jnp.max(s, axis=1, keepdims=True), sink)
            p = jnp.exp(s - m)
            l = jnp.sum(p, axis=1, keepdims=True) + jnp.exp(sink - m)
            o = _dot(p.astype(BF16), vv) / l
            o_ref[sub * W:(sub + 1) * W, hh * HEAD_DIM:(hh + 1) * HEAD_DIM] = o.astype(o_ref.dtype)


def swa_attention(proj, table, sinks, *, tb):
    T = proj.shape[0]
    W = SWA_WINDOW
    nsub = tb // W
    gw = SWA_GROUP * HEAD_DIM
    qb, kb, vb = C_SQ // gw, C_SK // HEAD_DIM, C_SV // HEAD_DIM
    prev = lambda n: jnp.maximum(n * nsub - 1, 0)
    return pl.pallas_call(
        functools.partial(_swa_kernel, nsub=nsub),
        out_shape=jax.ShapeDtypeStruct((T, SWA_HEADS * HEAD_DIM), BF16),
        grid=(SWA_KV_HEADS, T // tb),
        in_specs=[
            pl.BlockSpec(memory_space=pltpu.SMEM),
            pl.BlockSpec((tb, gw), lambda g, n: (n, qb + g)),
            pl.BlockSpec((tb, HEAD_DIM), lambda g, n: (n, kb + g)),
            pl.BlockSpec((W, HEAD_DIM), lambda g, n: (prev(n), kb + g)),
            pl.BlockSpec((tb, HEAD_DIM), lambda g, n: (n, vb + g)),
            pl.BlockSpec((W, HEAD_DIM), lambda g, n: (prev(n), vb + g)),
            pl.BlockSpec((SWA_GROUP, W, 2 * W), lambda g, n: (g, 0, 0)),
        ],
        out_specs=pl.BlockSpec((tb, gw), lambda g, n: (n, g)),
        compiler_params=_cparams(("arbitrary", "arbitrary")),
        name="swa_attn",
    )(sinks.reshape(1, SWA_HEADS), proj, proj, proj, proj, proj, table)


def _gla_kernel(q_ref, k_ref, v_ref, r_ref, ps_ref, wg_ref, bg_ref, gn_ref, o_ref, st_ref, oc_ref, *, tb):
    C = GLA_CHUNK

    @pl.when(pl.program_id(0) == 0)
    def _():
        st_ref[...] = jnp.zeros_like(st_ref)

    glr = ps_ref[...].astype(BF16)
    g = jax.nn.log_sigmoid(_dot(glr, wg_ref[...]) + bg_ref[...]) / GLA_TAU
    r = lax.broadcasted_iota(jnp.int32, (tb, tb), 0)
    c = lax.broadcasted_iota(jnp.int32, (tb, tb), 1)
    same = (r // C) == (c // C)
    tril = jnp.where(jnp.logical_and(same, c <= r), 1.0, 0.0).astype(BF16)
    whole = jnp.where(same, 1.0, 0.0).astype(BF16)
    gh, gm, gl = _split3(g)
    b = _dot(tril, gh) + _dot(tril, gm) + _dot(tril, gl)
    b_last = _dot(whole, gh) + _dot(whole, gm) + _dot(whole, gl)
    q_t = (q_ref[...].astype(F32) * jnp.exp(b)).astype(BF16)
    kf = k_ref[...].astype(F32)
    k_t = kf * jnp.exp(-b)
    k_end = kf * jnp.exp(b_last - b)
    decay = jnp.exp(b_last)
    lane = lax.broadcasted_iota(jnp.int32, (1, LANES), 1)
    causal = (lax.broadcasted_iota(jnp.int32, (C, C), 1) <= lax.broadcasted_iota(jnp.int32, (C, C), 0))
    for h in range(GLA_HEADS):
        pair = slice((h // 2) * LANES, (h // 2 + 1) * LANES)
        mine = (lane // GLA_DK) == (h % 2)
        ktm = jnp.where(mine, k_t[:, pair], 0.0).astype(BF16)
        kem = jnp.where(mine, k_end[:, pair], 0.0).astype(BF16)
        qh = q_t[:, pair]
        dec = decay[:, pair]
        vh = v_ref[:, h * GLA_DV:(h + 1) * GLA_DV]
        st = st_ref[h]
        for n in range(tb // C):
            rows = slice(n * C, (n + 1) * C)
            a = jnp.where(causal, _dot_nt(qh[rows], ktm[rows]), 0.0)
            o = _dot(a.astype(BF16), vh[rows]) + _dot_nt(qh[rows], st.astype(BF16))
            oc_ref[rows, h * GLA_DV:(h + 1) * GLA_DV] = o
            kv_t = _dot_tn(vh[rows], kem[rows])
            st = dec[n * C:n * C + 1, :] * st + kv_t
        st_ref[h] = st
    for h in range(GLA_HEADS):
        cols = slice(h * GLA_DV, (h + 1) * GLA_DV)
        o = _rms(oc_ref[:, cols], gn_ref[...])
        o_ref[:, cols] = (o * jax.nn.silu(r_ref[:, cols].astype(F32))).astype(o_ref.dtype)


def gla(proj, ps, wg_pad, bg, gnorm, *, tb):
    T = proj.shape[0]
    qw = GLA_HEADS * GLA_DK
    vw = GLA_HEADS * GLA_DV
    return pl.pallas_call(
        functools.partial(_gla_kernel, tb=tb),
        out_shape=jax.ShapeDtypeStruct((T, vw), BF16),
        grid=(T // tb,),
        in_specs=[
            pl.BlockSpec((tb, qw), lambda n: (n, C_GQ // qw)),
            pl.BlockSpec((tb, qw), lambda n: (n, C_GK // qw)),
            pl.BlockSpec((tb, vw), lambda n: (n, C_GV // vw)),
            pl.BlockSpec((tb, vw), lambda n: (n, C_GR // vw)),
            pl.BlockSpec((tb, LANES), lambda n: (n, 0)),
            pl.BlockSpec((LANES, qw), lambda n: (0, 0)),
            pl.BlockSpec((1, qw), lambda n: (0, 0)),
            pl.BlockSpec((1, GLA_DV), lambda n: (0, 0)),
        ],
        out_specs=pl.BlockSpec((tb, vw), lambda n: (n, 0)),
        scratch_shapes=[pltpu.VMEM((GLA_HEADS, GLA_DV, LANES), F32),
                        pltpu.VMEM((tb, vw), F32)],
        compiler_params=_cparams(("arbitrary",)),
        name="gla",
    )(proj, proj, proj, proj, ps, wg_pad, bg, gnorm)


def _mix_xattn_kernel(h_ref, of_ref, os_ref, og_ref, wo_ref, gx_ref, wq_ref, kv_ref, wx_ref, o_ref):
    nf, ns = of_ref.shape[1], os_ref.shape[1]
    mix = (_dot(of_ref[...], wo_ref[0:nf, :]) + _dot(os_ref[...], wo_ref[nf:nf + ns, :])
           + _dot(og_ref[...], wo_ref[nf + ns:, :]))
    h1 = h_ref[...] + mix
    xn = _rms(h1, gx_ref[...]).astype(BF16)
    q = (_dot(xn, wq_ref[...]) * (HEAD_DIM ** -0.5)).astype(BF16)
    xw = XATTN_HEADS * HEAD_DIM
    outs = []
    for hh in range(XATTN_HEADS):
        cols = slice(hh * HEAD_DIM, (hh + 1) * HEAD_DIM)
        k = kv_ref[:, cols]
        v = kv_ref[:, xw + hh * HEAD_DIM:xw + (hh + 1) * HEAD_DIM]
        s = _dot_nt(q[:, cols], k)
        p = jnp.exp(s - jnp.max(s, axis=1, keepdims=True))
        l = jnp.sum(p, axis=1, keepdims=True)
        outs.append((_dot(p.astype(BF16), v) / l).astype(BF16))
    o = jnp.concatenate(outs, axis=1)
    o_ref[...] = h1 + _dot(o, wx_ref[...])


def mix_xattn(h, o_fox, o_swa, o_gla, w_out, layer, gx, wq, kv, wo, *, tm):
    T, D = h.shape
    const = lambda shape: pl.BlockSpec(shape, lambda i: (0, 0))
    stacked = lambda w: pl.BlockSpec((None,) + w.shape[1:], lambda i: (layer, 0, 0))
    row = lambda w: pl.BlockSpec((tm, w), lambda i: (i, 0))
    return pl.pallas_call(
        _mix_xattn_kernel,
        out_shape=jax.ShapeDtypeStruct((T, D), F32),
        grid=(T // tm,),
        in_specs=[row(D), row(o_fox.shape[1]), row(o_swa.shape[1]), row(o_gla.shape[1]),
                  stacked(w_out), const((1, D)), stacked(wq), const(kv.shape), stacked(wo)],
        out_specs=row(D),
        compiler_params=_cparams(("arbitrary",)),
        name="mix_xattn",
    )(h, o_fox, o_swa, o_gla, w_out, gx.reshape(1, D), wq, kv, wo)


def _ffn_kernel(x_ref, xh_ref, g_ref, wg_ref, wv_ref, cwg_ref, cwv_ref, cbg_ref, cbv_ref, wd_ref, fg_ref,
                o_ref, xn_ref, acc_ref, *, tm, final):
    i = pl.program_id(0)
    f = pl.program_id(1)
    H = BF16_ROWS

    @pl.when(f == 0)
    def _():
        halo = _rms(xh_ref[...], g_ref[...])
        xn_ref[0:H, :] = jnp.where(i > 0, halo, 0.0).astype(BF16)
        xn_ref[H:, :] = _rms(x_ref[...], g_ref[...]).astype(BF16)
        acc_ref[...] = jnp.zeros_like(acc_ref)

    xe = xn_ref[...]

    def conv(u, cw_ref, cb_ref):
        return (cb_ref[...] + cw_ref[2:3, :] * u[H:H + tm] + cw_ref[1:2, :] * u[H - 1:H - 1 + tm]
                + cw_ref[0:1, :] * u[H - 2:H - 2 + tm])

    gate = conv(_dot(xe, wg_ref[...]), cwg_ref, cbg_ref)
    val = conv(_dot(xe, wv_ref[...]), cwv_ref, cbv_ref)
    act = (jax.nn.silu(gate) * val).astype(BF16)
    acc_ref[...] += _dot(act, wd_ref[...])

    @pl.when(f == pl.num_programs(1) - 1)
    def _():
        y = x_ref[...] + acc_ref[...]
        if final:
            y = _rms(y, fg_ref[...])
        o_ref[...] = y


def conv_ffn(x, g, w_up, conv_w, conv_b, w_down, layer, final_g, *, tm, fc, final):
    T, D = x.shape
    L, dff, _ = w_down.shape
    nf = dff // fc
    H = BF16_ROWS
    cb = conv_b.reshape(L, 1, 2 * dff)
    return pl.pallas_call(
        functools.partial(_ffn_kernel, tm=tm, final=final),
        out_shape=jax.ShapeDtypeStruct((T, D), F32),
        grid=(T // tm, nf),
        in_specs=[
            pl.BlockSpec((tm, D), lambda i, f: (i, 0)),
            pl.BlockSpec((H, D), lambda i, f: (jnp.maximum(i * (tm // H) - 1, 0), 0)),
            pl.BlockSpec((1, D), lambda i, f: (0, 0)),
            pl.BlockSpec((None, D, fc), lambda i, f: (layer, 0, f)),
            pl.BlockSpec((None, D, fc), lambda i, f: (layer, 0, nf + f)),
            pl.BlockSpec((None, CONV_WIDTH, fc), lambda i, f: (layer, 0, f)),
            pl.BlockSpec((None, CONV_WIDTH, fc), lambda i, f: (layer, 0, nf + f)),
            pl.BlockSpec((None, 1, fc), lambda i, f: (layer, 0, f)),
            pl.BlockSpec((None, 1, fc), lambda i, f: (layer, 0, nf + f)),
            pl.BlockSpec((None, fc, D), lambda i, f: (layer, f, 0)),
            pl.BlockSpec((1, D), lambda i, f: (0, 0)),
        ],
        out_specs=pl.BlockSpec((tm, D), lambda i, f: (i, 0)),
        scratch_shapes=[pltpu.VMEM((tm + H, D), BF16), pltpu.VMEM((tm, D), F32)],
        compiler_params=_cparams(("arbitrary", "arbitrary")),
        name="conv_ffn",
    )(x, x, g.reshape(1, D), w_up, w_up, conv_w, conv_w, cb, cb, w_down, final_g.reshape(1, D))


TR_PREP, TN_PREP = 512, 512
TM_PROJ, TN_PROJ = 1024, 1536
TB_GATE = 512
TQ_FOX = 512
TB_SWA = 512
TB_GLA = 512
TM_MIX = 512
TM_FFN, FC_FFN = 512, 512


def _main_colscale():
    cs = np.ones((1, MAIN_COLS), np.float32)
    cs[0, C_FQ:C_FK] = HEAD_DIM ** -0.5 * LOG2E
    cs[0, C_SQ:C_SK] = HEAD_DIM ** -0.5
    cs[0, C_GQ:C_GK] = GLA_DK ** -0.5
    return jnp.asarray(cs)


def kernel(x, mem, w_in, b_fox_f, swa_sinks, t5_bias, w_gla_gate, b_gla_gate, gla_norm, w_mix_out, norm_mix,
           norm_xattn, norm_mem, wq_x, wkv_x, wo_x, norm_ffn, w_up, conv_w, conv_b, w_down, final_norm):
    depth = w_in.shape[0]
    _, T, D = x.shape
    M = mem.shape[1]
    h = x.reshape(T, D)
    memf = mem.reshape(M, D)
    colscale = _main_colscale()
    ones_kv = jnp.ones((1, wkv_x.shape[2]), F32)
    table = t5_table(t5_bias)
    pad_lanes = LANES - FOX_HEADS - GLA_GATE_RANK
    w_main, w_small = prep_w_in(w_in, tr=TR_PREP, tn=TN_PREP)
    w_out_b, wq_b, wkv_b, wo_b = (w.astype(BF16) for w in (w_mix_out, wq_x, wkv_x, wo_x))
    w_up_b, w_down_b = w_up.astype(BF16), w_down.astype(BF16)
    for l in range(depth):
        bvec = jnp.concatenate([b_fox_f[l], jnp.zeros((LANES - FOX_HEADS,), F32)]).reshape(1, LANES)
        wg_pad = jnp.concatenate([jnp.zeros((S_GLR, GLA_HEADS * GLA_DK), F32), w_gla_gate[l],
                                  jnp.zeros((pad_lanes, GLA_HEADS * GLA_DK), F32)], axis=0).astype(BF16)

        proj, ps = norm_proj(h, norm_mix[l], w_main, l, colscale, w_small, tm=TM_PROJ, tn=TN_PROJ)
        col, row = fox_gate(ps, bvec, tb=TB_GATE)
        o_fox = fox_attention(proj, col, row, tq=TQ_FOX)
        o_swa = swa_attention(proj, table, swa_sinks[l], tb=TB_SWA)
        o_gla = gla(proj, ps, wg_pad, b_gla_gate[l].reshape(1, -1), gla_norm[l].reshape(1, -1), tb=TB_GLA)
        kv = norm_proj(memf, norm_mem[l], wkv_b, l, ones_kv, tm=M, tn=wkv_x.shape[2])
        h = mix_xattn(h, o_fox, o_swa, o_gla, w_out_b, l, norm_xattn[l], wq_b, kv, wo_b, tm=TM_MIX)
        h = conv_ffn(h, norm_ffn[l], w_up_b, conv_w, conv_b, w_down_b, l, final_norm,
                     tm=TM_FFN, fc=FC_FFN, final=(l == depth - 1))
    return h.reshape(x.shape)
```

```python
import functools
import math

import numpy as np
import jax
import jax.numpy as jnp
from jax import lax
from jax.experimental import pallas as pl
from jax.experimental.pallas import tpu as pltpu

F32 = jnp.float32
BF16 = jnp.bfloat16

HEAD_DIM = 128
FOX_HEADS = 4
SWA_HEADS = 8
SWA_KV_HEADS = 2
SWA_GROUP = SWA_HEADS // SWA_KV_HEADS
GLA_HEADS = 4
GLA_DK = 64
GLA_DV = 128
GLA_GATE_RANK = 16
GLA_TAU = 16.0
GLA_CHUNK = 64
SWA_WINDOW = 128
NUM_BUCKETS = 32
T5_MAX_DISTANCE = 128
XATTN_HEADS = 4
CONV_WIDTH = 3
EPS = 1e-6
LOG2E = math.log2(math.e)

LANES = 128
SUBLANES = 8
BF16_ROWS = 16
MXU_COLS = 256
VMEM_LIMIT = 56 * 1024 * 1024

_FOX_W = FOX_HEADS * HEAD_DIM
_OFF_FF = 3 * _FOX_W
_OFF_SQ = _OFF_FF + FOX_HEADS
_MAIN_B = (SWA_HEADS + 2 * SWA_KV_HEADS) * HEAD_DIM + 2 * GLA_HEADS * GLA_DK + 2 * GLA_HEADS * GLA_DV
_OFF_GLR = _OFF_SQ + _MAIN_B
C_FQ, C_FK, C_FV = 0, 512, 1024
C_SQ, C_SK, C_SV = 1536, 2560, 2816
C_GQ, C_GK, C_GV, C_GR = 3072, 3328, 3584, 4096
MAIN_COLS = 4608
S_FF, S_GLR = 0, FOX_HEADS


def _cparams(sem, flags=None):
    return pltpu.CompilerParams(dimension_semantics=sem, vmem_limit_bytes=VMEM_LIMIT, flags=flags)


def _rms(x, g):
    return x * lax.rsqrt(jnp.mean(x * x, axis=-1, keepdims=True) + EPS) * g


N_SPLIT = 3


def _split3(x):
    hi = x.astype(BF16)
    r1 = x - hi.astype(F32)
    mid = r1.astype(BF16)
    lo = (r1 - mid.astype(F32)).astype(BF16)
    return hi, mid, lo


def _dot(a, b):
    return jnp.dot(a, b, preferred_element_type=F32)


def _dot_nt(a, b):
    return lax.dot_general(a, b, (((1,), (1,)), ((), ())), preferred_element_type=F32)


def _dot_tn(a, b):
    return lax.dot_general(a, b, (((0,), (0,)), ((), ())), preferred_element_type=F32)


def _sel_dot(sel, x):
    hi, mid, lo = _split3(x)
    return _dot(sel, hi) + _dot(sel, mid) + _dot(sel, lo)


def _prep_w_in_kernel(cur_ref, nxt_ref, ff_ref, glr_ref, wm_ref, ws_ref, *, tn):
    j = pl.program_id(2)
    n_plain = _OFF_FF // tn

    @pl.when(j < n_plain)
    def _():
        wm_ref[...] = cur_ref[...].astype(BF16)

    @pl.when(j >= n_plain)
    def _():
        cat = jnp.concatenate([cur_ref[...], nxt_ref[...]], axis=1)
        wm_ref[...] = cat[:, FOX_HEADS:FOX_HEADS + tn].astype(BF16)

    @pl.when(j == 0)
    def _():
        lane = lax.broadcasted_iota(jnp.int32, ff_ref.shape, 1)
        ws = jnp.where(lane < S_GLR, ff_ref[...], jnp.where(lane < S_GLR + GLA_GATE_RANK, glr_ref[...], 0.0))
        ws_ref[...] = ws.astype(BF16)


def prep_w_in(w_in, *, tr, tn):
    L, D, _ = w_in.shape
    per = tn // LANES
    assert _OFF_FF % tn == 0 and MAIN_COLS % tn == 0 and _OFF_SQ - _OFF_FF == S_GLR
    assert _OFF_GLR % LANES == S_GLR
    return pl.pallas_call(
        functools.partial(_prep_w_in_kernel, tn=tn),
        out_shape=[jax.ShapeDtypeStruct((L, D, MAIN_COLS), BF16),
                   jax.ShapeDtypeStruct((L, D, LANES), BF16)],
        grid=(L, D // tr, MAIN_COLS // tn),
        in_specs=[
            pl.BlockSpec((None, tr, tn), lambda l, r, j: (l, r, j)),
            pl.BlockSpec((None, tr, LANES), lambda l, r, j: (l, r, (j + 1) * per)),
            pl.BlockSpec((None, tr, LANES), lambda l, r, j: (l, r, _OFF_FF // LANES)),
            pl.BlockSpec((None, tr, LANES), lambda l, r, j: (l, r, _OFF_GLR // LANES)),
        ],
        out_specs=[pl.BlockSpec((None, tr, tn), lambda l, r, j: (l, r, j)),
                   pl.BlockSpec((None, tr, LANES), lambda l, r, j: (l, r, 0))],
        compiler_params=_cparams(("arbitrary", "arbitrary", "arbitrary")),
        name="prep_w_in",
    )(w_in, w_in, w_in, w_in)


def _norm_proj_kernel(*refs, has_small):
    if has_small:
        x_ref, g_ref, w_ref, cs_ref, ws_ref, o_ref, os_ref, xn_ref = refs
    else:
        x_ref, g_ref, w_ref, cs_ref, o_ref, xn_ref = refs

    @pl.when(pl.program_id(1) == 0)
    def _():
        xn = _rms(x_ref[...], g_ref[...]).astype(BF16)
        xn_ref[...] = xn
        if has_small:
            os_ref[...] = _dot(xn, ws_ref[...])

    o_ref[...] = (_dot(xn_ref[...], w_ref[...]) * cs_ref[...]).astype(o_ref.dtype)


def norm_proj(x, g, w, layer, colscale, w_small=None, *, tm, tn):
    T, D = x.shape
    N = w.shape[2]
    has_small = w_small is not None
    in_specs = [
        pl.BlockSpec((tm, D), lambda i, j: (i, 0)),
        pl.BlockSpec((1, D), lambda i, j: (0, 0)),
        pl.BlockSpec((None, D, tn), lambda i, j: (layer, 0, j)),
        pl.BlockSpec((1, tn), lambda i, j: (0, j)),
    ]
    args = [x, g.reshape(1, D), w, colscale]
    out_shape = [jax.ShapeDtypeStruct((T, N), BF16)]
    out_specs = [pl.BlockSpec((tm, tn), lambda i, j: (i, j))]
    if has_small:
        in_specs.append(pl.BlockSpec((None, D, LANES), lambda i, j: (layer, 0, 0)))
        args.append(w_small)
        out_shape.append(jax.ShapeDtypeStruct((T, LANES), F32))
        out_specs.append(pl.BlockSpec((tm, LANES), lambda i, j: (i, 0)))
    outs = pl.pallas_call(
        functools.partial(_norm_proj_kernel, has_small=has_small),
        out_shape=out_shape,
        grid=(T // tm, N // tn),
        in_specs=in_specs,
        out_specs=out_specs,
        scratch_shapes=[pltpu.VMEM((tm, D), BF16)],
        compiler_params=_cparams(("arbitrary", "arbitrary")),
        name="norm_proj_small" if has_small else "norm_proj",
    )(*args)
    return outs if has_small else outs[0]


def _fox_gate_kernel(ps_ref, b_ref, kx_ref, row_ref, carry_ref, *, tb):
    @pl.when(pl.program_id(0) == 0)
    def _():
        carry_ref[...] = jnp.zeros_like(carry_ref)

    x = jax.nn.log_sigmoid(ps_ref[...] + b_ref[...])
    r = lax.broadcasted_iota(jnp.int32, (tb, tb), 0)
    c = lax.broadcasted_iota(jnp.int32, (tb, tb), 1)
    tril = jnp.where(c <= r, 1.0, 0.0).astype(BF16)
    csum = _sel_dot(tril, x) + carry_ref[0:1, :]
    carry_ref[...] = jnp.broadcast_to(csum[tb - 1:tb, :], carry_ref.shape)
    pieces = _split3(csum * LOG2E)
    neg = [-p for p in pieces]
    lane_r = lax.broadcasted_iota(jnp.int32, (LANES, LANES), 0)
    lane_c = lax.broadcasted_iota(jnp.int32, (LANES, LANES), 1)
    lane_s = lax.broadcasted_iota(jnp.int32, (SUBLANES, LANES), 1)
    for h in range(FOX_HEADS):
        kx = sum(_dot(neg[n], jnp.where((lane_r == S_FF + h) & (lane_c == n), 1.0, 0.0).astype(BF16))
                 for n in range(len(neg)))
        kx_ref[h] = kx.astype(BF16)
        pick = jnp.where(lane_s == S_FF + h, 1.0, 0.0).astype(BF16)
        row_ref[h, 0] = sum(_dot_nt(pick, p) for p in pieces)


def fox_gate(ps, bvec, *, tb):
    T = ps.shape[0]
    nb = T // tb
    return pl.pallas_call(
        functools.partial(_fox_gate_kernel, tb=tb),
        out_shape=[jax.ShapeDtypeStruct((FOX_HEADS, T, LANES), BF16),
                   jax.ShapeDtypeStruct((FOX_HEADS, nb, SUBLANES, tb), F32)],
        grid=(nb,),
        in_specs=[pl.BlockSpec((tb, LANES), lambda i: (i, 0)),
                  pl.BlockSpec((1, LANES), lambda i: (0, 0))],
        out_specs=[pl.BlockSpec((FOX_HEADS, tb, LANES), lambda i: (0, i, 0)),
                   pl.BlockSpec((FOX_HEADS, 1, SUBLANES, tb), lambda i: (0, i, 0, 0))],
        scratch_shapes=[pltpu.VMEM((SUBLANES, LANES), F32)],
        compiler_params=_cparams(("arbitrary",)),
        name="fox_gate",
    )(ps, bvec)


def _fox_attn_kernel(q_ref, k_ref, v_ref, kx_ref, cq_ref, o_ref, vt_ref, ya_ref, yb_ref, m_ref, l_ref, acc_ref,
                     *, tq, tk, nk, nh):
    i = pl.program_id(1)
    D = HEAD_DIM
    heads = range(nh)
    n_full = i // (tk // tq)

    @pl.when(i == 0)
    def _():
        for hh in heads:
            for n in range(nk):
                vt_ref[hh, n] = v_ref[n * tk:(n + 1) * tk, hh * D:(hh + 1) * D].T

    lane = lax.broadcasted_iota(jnp.int32, (tq, LANES), 1)
    ones = jnp.where(lane < N_SPLIT, 1.0, 0.0).astype(BF16)
    q_aug = [jnp.concatenate([q_ref[:, hh * D:(hh + 1) * D], ones], axis=1) for hh in heads]
    cq = [cq_ref[hh, 0, 0:1, :] for hh in heads]
    m_ref[...] = jnp.full(m_ref.shape, -jnp.inf, F32)
    l_ref[...] = jnp.zeros(l_ref.shape, F32)
    acc_ref[...] = jnp.zeros(acc_ref.shape, F32)

    def scores(hh, j):
        start = pl.multiple_of(j * tk, tk)
        k_aug = jnp.concatenate([k_ref[pl.ds(start, tk), hh * D:(hh + 1) * D], kx_ref[hh, pl.ds(start, tk), :]],
                                axis=1)
        return _dot_nt(k_aug, q_aug[hh])

    def update(hh, j, y):
        m_old = m_ref[hh]
        m_new = jnp.maximum(m_old, jnp.max(y, axis=0, keepdims=True) + cq[hh])
        alpha = jnp.exp2(m_old - m_new)
        p = jnp.exp2(y + (cq[hh] - m_new))
        l_ref[hh] = alpha * l_ref[hh] + jnp.sum(p, axis=0, keepdims=True)
        acc_ref[hh] = alpha * acc_ref[hh] + _dot(vt_ref[hh, j], p.astype(BF16))
        m_ref[hh] = m_new

    for hh in heads:
        ya_ref[hh] = scores(hh, 0)

    def step(j, cur_ref, nxt_ref):
        for hh in heads:
            nxt_ref[hh] = scores(hh, j + 1)
        for hh in heads:
            update(hh, j, cur_ref[hh])

    def body(j, carry):
        pl.when(j % 2 == 0)(lambda: step(j, ya_ref, yb_ref))
        pl.when(j % 2 == 1)(lambda: step(j, yb_ref, ya_ref))
        return carry

    lax.fori_loop(0, n_full, body, 0)
    key = n_full * tk + lax.broadcasted_iota(jnp.int32, (tk, tq), 0)
    qry = i * tq + lax.broadcasted_iota(jnp.int32, (tk, tq), 1)

    def diagonal(cur_ref):
        for hh in heads:
            update(hh, n_full, jnp.where(key <= qry, cur_ref[hh], -jnp.inf))

    pl.when(n_full % 2 == 0)(lambda: diagonal(ya_ref))
    pl.when(n_full % 2 == 1)(lambda: diagonal(yb_ref))
    for hh in heads:
        o_ref[:, hh * D:(hh + 1) * D] = (acc_ref[hh] / l_ref[hh]).T.astype(o_ref.dtype)


def fox_attention(proj, kx, row, *, tq, tk, nh):
    T = proj.shape[0]
    nq = T // tq
    gw = nh * HEAD_DIM
    qb, kb, vb = C_FQ // gw, C_FK // gw, C_FV // gw
    assert tk % tq == 0 and T % tk == 0
    return pl.pallas_call(
        functools.partial(_fox_attn_kernel, tq=tq, tk=tk, nk=T // tk, nh=nh),
        out_shape=jax.ShapeDtypeStruct((T, FOX_HEADS * HEAD_DIM), BF16),
        grid=(FOX_HEADS // nh, nq),
        in_specs=[
            pl.BlockSpec((tq, gw), lambda g, i: (i, qb + g)),
            pl.BlockSpec((T, gw), lambda g, i: (0, kb + g)),
            pl.BlockSpec((T, gw), lambda g, i: (0, vb + g)),
            pl.BlockSpec((nh, T, LANES), lambda g, i: (g, 0, 0)),
            pl.BlockSpec((nh, 1, SUBLANES, tq), lambda g, i: (g, i, 0, 0)),
        ],
        out_specs=pl.BlockSpec((tq, gw), lambda g, i: (i, g)),
        scratch_shapes=[pltpu.VMEM((nh, T // tk, HEAD_DIM, tk), BF16),
                        pltpu.VMEM((nh, tk, tq), F32), pltpu.VMEM((nh, tk, tq), F32),
                        pltpu.VMEM((nh, 1, tq), F32), pltpu.VMEM((nh, 1, tq), F32),
                        pltpu.VMEM((nh, HEAD_DIM, tq), F32)],
        compiler_params=_cparams(("arbitrary", "arbitrary")),
        name="fox_attn",
    )(proj, proj, proj, kx, row)


def _t5_table_kernel(t5_ref, o_ref):
    h = pl.program_id(0)
    W = SWA_WINDOW
    i = lax.broadcasted_iota(jnp.int32, (W, 2 * W), 0)
    j = lax.broadcasted_iota(jnp.int32, (W, 2 * W), 1)
    rel = (W + i) - j
    n = jnp.maximum(rel, 0)
    max_exact = NUM_BUCKETS // 2
    nf = jnp.maximum(n, 1).astype(F32)
    large = max_exact + (jnp.log(nf / max_exact) / math.log(T5_MAX_DISTANCE / max_exact)
                         * (NUM_BUCKETS - max_exact)).astype(jnp.int32)
    large = jnp.minimum(large, NUM_BUCKETS - 1)
    bucket = jnp.where(n < max_exact, n, large)
    bias = jnp.zeros((W, 2 * W), F32)
    for b in range(NUM_BUCKETS):
        bias = jnp.where(bucket == b, t5_ref[b, h], bias)
    in_window = (rel >= 0) & (rel < W)
    o_ref[...] = jnp.where(in_window, bias, -jnp.inf)


def t5_table(t5_bias):
    W = SWA_WINDOW
    return pl.pallas_call(
        _t5_table_kernel,
        out_shape=jax.ShapeDtypeStruct((SWA_HEADS, W, 2 * W), F32),
        grid=(SWA_HEADS,),
        in_specs=[pl.BlockSpec(memory_space=pltpu.SMEM)],
        out_specs=pl.BlockSpec((None, W, 2 * W), lambda h: (h, 0, 0)),
        compiler_params=_cparams(("arbitrary",)),
        name="t5_table",
    )(t5_bias)


def _swa_kernel(sink_ref, q_ref, kc_ref, kp_ref, vc_ref, vp_ref, tab_ref, o_ref, *, nsub):
    g = pl.program_id(0)
    n = pl.program_id(1)
    W = SWA_WINDOW
    first_prev = lax.broadcasted_iota(jnp.int32, (W, 2 * W), 1) < W
    for sub in range(nsub):
        if sub == 0:
            kp, vp = kp_ref[...], vp_ref[...]
        else:
            kp, vp = kc_ref[(sub - 1) * W:sub * W, :], vc_ref[(sub - 1) * W:sub * W, :]
        kk = jnp.concatenate([kp, kc_ref[sub * W:(sub + 1) * W, :]], axis=0)
        vv = jnp.concatenate([vp, vc_ref[sub * W:(sub + 1) * W, :]], axis=0)
        for hh in range(SWA_GROUP):
            q = q_ref[sub * W:(sub + 1) * W, hh * HEAD_DIM:(hh + 1) * HEAD_DIM]
            s = _dot_nt(q, kk) + tab_ref[hh]
            if sub == 0:
                s = jnp.where(jnp.logical_and(first_prev, n == 0), -jnp.inf, s)
            sink = sink_ref[0, g * SWA_GROUP + hh]
            m = jnp.maximum(jnp.max(s, axis=1, keepdims=True), sink)
            p = jnp.exp(s - m)
            l = jnp.sum(p, axis=1, keepdims=True) + jnp.exp(sink - m)
            o = _dot(p.astype(BF16), vv) / l
            o_ref[sub * W:(sub + 1) * W, hh * HEAD_DIM:(hh + 1) * HEAD_DIM] = o.astype(o_ref.dtype)


def swa_attention(proj, table, sinks, *, tb):
    T = proj.shape[0]
    W = SWA_WINDOW
    nsub = tb // W
    gw = SWA_GROUP * HEAD_DIM
    qb, kb, vb = C_SQ // gw, C_SK // HEAD_DIM, C_SV // HEAD_DIM
    prev = lambda n: jnp.maximum(n * nsub - 1, 0)
    return pl.pallas_call(
        functools.partial(_swa_kernel, nsub=nsub),
        out_shape=jax.ShapeDtypeStruct((T, SWA_HEADS * HEAD_DIM), BF16),
        grid=(SWA_KV_HEADS, T // tb),
        in_specs=[
            pl.BlockSpec(memory_space=pltpu.SMEM),
            pl.BlockSpec((tb, gw), lambda g, n: (n, qb + g)),
            pl.BlockSpec((tb, HEAD_DIM), lambda g, n: (n, kb + g)),
            pl.BlockSpec((W, HEAD_DIM), lambda g, n: (prev(n), kb + g)),
            pl.BlockSpec((tb, HEAD_DIM), lambda g, n: (n, vb + g)),
            pl.BlockSpec((W, HEAD_DIM), lambda g, n: (prev(n), vb + g)),
            pl.BlockSpec((SWA_GROUP, W, 2 * W), lambda g, n: (g, 0, 0)),
        ],
        out_specs=pl.BlockSpec((tb, gw), lambda g, n: (n, g)),
        compiler_params=_cparams(("arbitrary", "arbitrary")),
        name="swa_attn",
    )(sinks.reshape(1, SWA_HEADS), proj, proj, proj, proj, proj, table)


def _gla_kernel(q_ref, k_ref, v_ref, r_ref, ps_ref, wg_ref, bg_ref, gn_ref, o_ref, st_ref, oc_ref, *, tb):
    C = GLA_CHUNK

    @pl.when(pl.program_id(0) == 0)
    def _():
        st_ref[...] = jnp.zeros_like(st_ref)

    glr = ps_ref[...].astype(BF16)
    g = jax.nn.log_sigmoid(_dot(glr, wg_ref[...]) + bg_ref[...]) / GLA_TAU
    r = lax.broadcasted_iota(jnp.int32, (tb, tb), 0)
    c = lax.broadcasted_iota(jnp.int32, (tb, tb), 1)
    same = (r // C) == (c // C)
    tril = jnp.where(jnp.logical_and(same, c <= r), 1.0, 0.0).astype(BF16)
    whole = jnp.where(same, 1.0, 0.0).astype(BF16)
    gh, gm, gl = _split3(g)
    b = _dot(tril, gh) + _dot(tril, gm) + _dot(tril, gl)
    b_last = _dot(whole, gh) + _dot(whole, gm) + _dot(whole, gl)
    q_t = (q_ref[...].astype(F32) * jnp.exp(b)).astype(BF16)
    kf = k_ref[...].astype(F32)
    k_t = kf * jnp.exp(-b)
    k_end = kf * jnp.exp(b_last - b)
    decay = jnp.exp(b_last)
    lane = lax.broadcasted_iota(jnp.int32, (1, LANES), 1)
    causal = (lax.broadcasted_iota(jnp.int32, (C, C), 1) <= lax.broadcasted_iota(jnp.int32, (C, C), 0))
    for h in range(GLA_HEADS):
        pair = slice((h // 2) * LANES, (h // 2 + 1) * LANES)
        mine = (lane // GLA_DK) == (h % 2)
        ktm = jnp.where(mine, k_t[:, pair], 0.0).astype(BF16)
        kem = jnp.where(mine, k_end[:, pair], 0.0).astype(BF16)
        qh = q_t[:, pair]
        dec = decay[:, pair]
        vh = v_ref[:, h * GLA_DV:(h + 1) * GLA_DV]
        st = st_ref[h]
        for n in range(tb // C):
            rows = slice(n * C, (n + 1) * C)
            a = jnp.where(causal, _dot_nt(qh[rows], ktm[rows]), 0.0)
            o = _dot(a.astype(BF16), vh[rows]) + _dot_nt(qh[rows], st.astype(BF16))
            oc_ref[rows, h * GLA_DV:(h + 1) * GLA_DV] = o
            kv_t = _dot_tn(vh[rows], kem[rows])
            st = dec[n * C:n * C + 1, :] * st + kv_t
        st_ref[h] = st
    for h in range(GLA_HEADS):
        cols = slice(h * GLA_DV, (h + 1) * GLA_DV)
        o = _rms(oc_ref[:, cols], gn_ref[...])
        o_ref[:, cols] = (o * jax.nn.silu(r_ref[:, cols].astype(F32))).astype(o_ref.dtype)


def gla(proj, ps, wg_pad, bg, gnorm, *, tb):
    T = proj.shape[0]
    qw = GLA_HEADS * GLA_DK
    vw = GLA_HEADS * GLA_DV
    return pl.pallas_call(
        functools.partial(_gla_kernel, tb=tb),
        out_shape=jax.ShapeDtypeStruct((T, vw), BF16),
        grid=(T // tb,),
        in_specs=[
            pl.BlockSpec((tb, qw), lambda n: (n, C_GQ // qw)),
            pl.BlockSpec((tb, qw), lambda n: (n, C_GK // qw)),
            pl.BlockSpec((tb, vw), lambda n: (n, C_GV // vw)),
            pl.BlockSpec((tb, vw), lambda n: (n, C_GR // vw)),
            pl.BlockSpec((tb, LANES), lambda n: (n, 0)),
            pl.BlockSpec((LANES, qw), lambda n: (0, 0)),
            pl.BlockSpec((1, qw), lambda n: (0, 0)),
            pl.BlockSpec((1, GLA_DV), lambda n: (0, 0)),
        ],
        out_specs=pl.BlockSpec((tb, vw), lambda n: (n, 0)),
        scratch_shapes=[pltpu.VMEM((GLA_HEADS, GLA_DV, LANES), F32),
                        pltpu.VMEM((tb, vw), F32)],
        compiler_params=_cparams(("arbitrary",)),
        name="gla",
    )(proj, proj, proj, proj, ps, wg_pad, bg, gnorm)


def _mix_xattn_kernel(h_ref, of_ref, os_ref, og_ref, wo_ref, gx_ref, wq_ref, kv_ref, wx_ref, o_ref):
    nf, ns = of_ref.shape[1], os_ref.shape[1]
    mix = (_dot(of_ref[...], wo_ref[0:nf, :]) + _dot(os_ref[...], wo_ref[nf:nf + ns, :])
           + _dot(og_ref[...], wo_ref[nf + ns:, :]))
    h1 = h_ref[...] + mix
    xn = _rms(h1, gx_ref[...]).astype(BF16)
    q = (_dot(xn, wq_ref[...]) * (HEAD_DIM ** -0.5)).astype(BF16)
    xw = XATTN_HEADS * HEAD_DIM
    outs = []
    for hh in range(XATTN_HEADS):
        cols = slice(hh * HEAD_DIM, (hh + 1) * HEAD_DIM)
        k = kv_ref[:, cols]
        v = kv_ref[:, xw + hh * HEAD_DIM:xw + (hh + 1) * HEAD_DIM]
        s = _dot_nt(q[:, cols], k)
        p = jnp.exp(s - jnp.max(s, axis=1, keepdims=True))
        l = jnp.sum(p, axis=1, keepdims=True)
        outs.append((_dot(p.astype(BF16), v) / l).astype(BF16))
    o = jnp.concatenate(outs, axis=1)
    o_ref[...] = h1 + _dot(o, wx_ref[...])


def mix_xattn(h, o_fox, o_swa, o_gla, w_out, layer, gx, wq, kv, wo, *, tm):
    T, D = h.shape
    const = lambda shape: pl.BlockSpec(shape, lambda i: (0, 0))
    stacked = lambda w: pl.BlockSpec((None,) + w.shape[1:], lambda i: (layer, 0, 0))
    row = lambda w: pl.BlockSpec((tm, w), lambda i: (i, 0))
    return pl.pallas_call(
        _mix_xattn_kernel,
        out_shape=jax.ShapeDtypeStruct((T, D), F32),
        grid=(T // tm,),
        in_specs=[row(D), row(o_fox.shape[1]), row(o_swa.shape[1]), row(o_gla.shape[1]),
                  stacked(w_out), const((1, D)), stacked(wq), const(kv.shape), stacked(wo)],
        out_specs=row(D),
        compiler_params=_cparams(("arbitrary",)),
        name="mix_xattn",
    )(h, o_fox, o_swa, o_gla, w_out, gx.reshape(1, D), wq, kv, wo)


def _ffn_kernel(x_ref, g_ref, wg_ref, wv_ref, cwg_ref, cwv_ref, cbg_ref, cbv_ref, wd_ref, fg_ref,
                o_ref, xn_ref, hg_ref, hv_ref, ug_ref, uv_ref, *, tm, sub, final):
    i = pl.program_id(0)
    f = pl.program_id(1)
    H = SUBLANES
    fc = wd_ref.shape[0]

    @pl.when(f == 0)
    def _():
        x = x_ref[...]
        xn_ref[...] = _rms(x, g_ref[...]).astype(BF16)
        o_ref[...] = x

    @pl.when(i == 0)
    def _():
        hg_ref[f] = jnp.zeros(hg_ref.shape[1:], F32)
        hv_ref[f] = jnp.zeros(hv_ref.shape[1:], F32)

    xn = xn_ref[...]

    def up(c, slot):
        ug_ref[slot, H:, :] = _dot(xn, wg_ref[:, c:c + sub])
        uv_ref[slot, H:, :] = _dot(xn, wv_ref[:, c:c + sub])

    def conv(u_ref, slot, h_ref, cw_ref, cb_ref, c):
        u_ref[slot, 0:H, :] = h_ref[f, :, c:c + sub]
        h_ref[f, :, c:c + sub] = u_ref[slot, tm:tm + H, :]
        return (cb_ref[:, c:c + sub] + cw_ref[2:3, c:c + sub] * u_ref[slot, H:H + tm, :]
                + cw_ref[1:2, c:c + sub] * u_ref[slot, H - 1:H - 1 + tm, :]
                + cw_ref[0:1, c:c + sub] * u_ref[slot, H - 2:H - 2 + tm, :])

    def down(c, slot):
        gate = conv(ug_ref, slot, hg_ref, cwg_ref, cbg_ref, c)
        val = conv(uv_ref, slot, hv_ref, cwv_ref, cbv_ref, c)
        act = (jax.nn.silu(gate) * val).astype(BF16)
        o_ref[...] += _dot(act, wd_ref[c:c + sub, :])

    up(0, 0)
    for n, c in enumerate(range(0, fc, sub)):
        if c + sub < fc:
            up(c + sub, (n + 1) % 2)
        down(c, n % 2)

    if final:
        @pl.when(f == pl.num_programs(1) - 1)
        def _():
            o_ref[...] = _rms(o_ref[...], fg_ref[...])


def conv_ffn(x, g, w_up, conv_w, conv_b, w_down, layer, final_g, *, tm, fc, final):
    T, D = x.shape
    L, dff, _ = w_down.shape
    nf = dff // fc
    cb = conv_b.reshape(L, 1, 2 * dff)
    return pl.pallas_call(
        functools.partial(_ffn_kernel, tm=tm, sub=MXU_COLS, final=final),
        out_shape=jax.ShapeDtypeStruct((T, D), F32),
        grid=(T // tm, nf),
        in_specs=[
            pl.BlockSpec((tm, D), lambda i, f: (i, 0)),
            pl.BlockSpec((1, D), lambda i, f: (0, 0)),
            pl.BlockSpec((None, D, fc), lambda i, f: (layer, 0, f)),
            pl.BlockSpec((None, D, fc), lambda i, f: (layer, 0, nf + f)),
            pl.BlockSpec((None, CONV_WIDTH, fc), lambda i, f: (layer, 0, f)),
            pl.BlockSpec((None, CONV_WIDTH, fc), lambda i, f: (layer, 0, nf + f)),
            pl.BlockSpec((None, 1, fc), lambda i, f: (layer, 0, f)),
            pl.BlockSpec((None, 1, fc), lambda i, f: (layer, 0, nf + f)),
            pl.BlockSpec((None, fc, D), lambda i, f: (layer, f, 0)),
            pl.BlockSpec((1, D), lambda i, f: (0, 0)),
        ],
        out_specs=pl.BlockSpec((tm, D), lambda i, f: (i, 0)),
        scratch_shapes=[pltpu.VMEM((tm, D), BF16), pltpu.VMEM((nf, SUBLANES, fc), F32),
                        pltpu.VMEM((nf, SUBLANES, fc), F32),
                        pltpu.VMEM((2, tm + SUBLANES, MXU_COLS), F32),
                        pltpu.VMEM((2, tm + SUBLANES, MXU_COLS), F32)],
        compiler_params=_cparams(("arbitrary", "arbitrary")),
        name="conv_ffn",
    )(x, g.reshape(1, D), w_up, w_up, conv_w, conv_w, cb, cb, w_down, final_g.reshape(1, D))


TR_PREP, TN_PREP = 512, 512
TM_PROJ, TN_PROJ = 1024, 1536
TB_GATE = 512
TQ_FOX, TK_FOX, NH_FOX = 512, 512, 2
TB_SWA = 512
TB_GLA = 512
TM_MIX = 512
TM_FFN, FC_FFN = 512, 512


def _main_colscale():
    cs = np.ones((1, MAIN_COLS), np.float32)
    cs[0, C_FQ:C_FK] = HEAD_DIM ** -0.5 * LOG2E
    cs[0, C_SQ:C_SK] = HEAD_DIM ** -0.5
    cs[0, C_GQ:C_GK] = GLA_DK ** -0.5
    return jnp.asarray(cs)


def kernel(x, mem, w_in, b_fox_f, swa_sinks, t5_bias, w_gla_gate, b_gla_gate, gla_norm, w_mix_out, norm_mix,
           norm_xattn, norm_mem, wq_x, wkv_x, wo_x, norm_ffn, w_up, conv_w, conv_b, w_down, final_norm):
    depth = w_in.shape[0]
    _, T, D = x.shape
    M = mem.shape[1]
    h = x.reshape(T, D)
    memf = mem.reshape(M, D)
    colscale = _main_colscale()
    ones_kv = jnp.ones((1, wkv_x.shape[2]), F32)
    table = t5_table(t5_bias)
    pad_lanes = LANES - FOX_HEADS - GLA_GATE_RANK
    w_main, w_small = prep_w_in(w_in, tr=TR_PREP, tn=TN_PREP)
    w_out_b, wq_b, wkv_b, wo_b = (w.astype(BF16) for w in (w_mix_out, wq_x, wkv_x, wo_x))
    w_up_b, w_down_b = w_up.astype(BF16), w_down.astype(BF16)
    for l in range(depth):
        bvec = jnp.concatenate([b_fox_f[l], jnp.zeros((LANES - FOX_HEADS,), F32)]).reshape(1, LANES)
        wg_pad = jnp.concatenate([jnp.zeros((S_GLR, GLA_HEADS * GLA_DK), F32), w_gla_gate[l],
                                  jnp.zeros((pad_lanes, GLA_HEADS * GLA_DK), F32)], axis=0).astype(BF16)

        proj, ps = norm_proj(h, norm_mix[l], w_main, l, colscale, w_small, tm=TM_PROJ, tn=TN_PROJ)
        kx, row = fox_gate(ps, bvec, tb=TB_GATE)
        o_fox = fox_attention(proj, kx, row, tq=TQ_FOX, tk=TK_FOX, nh=NH_FOX)
        o_swa = swa_attention(proj, table, swa_sinks[l], tb=TB_SWA)
        o_gla = gla(proj, ps, wg_pad, b_gla_gate[l].reshape(1, -1), gla_norm[l].reshape(1, -1), tb=TB_GLA)
        kv = norm_proj(memf, norm_mem[l], wkv_b, l, ones_kv, tm=M, tn=wkv_x.shape[2])
        h = mix_xattn(h, o_fox, o_swa, o_gla, w_out_b, l, norm_xattn[l], wq_b, kv, wo_b, tm=TM_MIX)
        h = conv_ffn(h, norm_ffn[l], w_up_b, conv_w, conv_b, w_down_b, l, final_norm,
                     tm=TM_FFN, fc=FC_FFN, final=(l == depth - 1))
    return h.reshape(x.shape)
```

```python
import functools
import math

import numpy as np
import jax
import jax.numpy as jnp
from jax import lax
from jax.experimental import pallas as pl
from jax.experimental.pallas import tpu as pltpu

F32 = jnp.float32
BF16 = jnp.bfloat16

HEAD_DIM = 128
FOX_HEADS = 4
SWA_HEADS = 8
SWA_KV_HEADS = 2
SWA_GROUP = SWA_HEADS // SWA_KV_HEADS
GLA_HEADS = 4
GLA_DK = 64
GLA_DV = 128
GLA_GATE_RANK = 16
GLA_TAU = 16.0
GLA_CHUNK = 64
SWA_WINDOW = 128
NUM_BUCKETS = 32
T5_MAX_DISTANCE = 128
XATTN_HEADS = 4
CONV_WIDTH = 3
EPS = 1e-6
LOG2E = math.log2(math.e)

LANES = 128
SUBLANES = 8
BF16_ROWS = 16
MXU_COLS = 256
VMEM_LIMIT = 56 * 1024 * 1024

_FOX_W = FOX_HEADS * HEAD_DIM
_OFF_FF = 3 * _FOX_W
_OFF_SQ = _OFF_FF + FOX_HEADS
_MAIN_B = (SWA_HEADS + 2 * SWA_KV_HEADS) * HEAD_DIM + 2 * GLA_HEADS * GLA_DK + 2 * GLA_HEADS * GLA_DV
_OFF_GLR = _OFF_SQ + _MAIN_B
C_FQ, C_FK, C_FV = 0, 512, 1024
C_SQ, C_SK, C_SV = 1536, 2560, 2816
C_GQ, C_GK, C_GV, C_GR = 3072, 3328, 3584, 4096
MAIN_COLS = 4608
S_FF, S_GLR = 0, FOX_HEADS


def _cparams(sem, flags=None):
    return pltpu.CompilerParams(dimension_semantics=sem, vmem_limit_bytes=VMEM_LIMIT, flags=flags)


def _rms(x, g):
    return x * lax.rsqrt(jnp.mean(x * x, axis=-1, keepdims=True) + EPS) * g


N_SPLIT = 3


def _split3(x):
    hi = x.astype(BF16)
    r1 = x - hi.astype(F32)
    mid = r1.astype(BF16)
    lo = (r1 - mid.astype(F32)).astype(BF16)
    return hi, mid, lo


def _dot(a, b):
    return jnp.dot(a, b, preferred_element_type=F32)


def _dot_nt(a, b):
    return lax.dot_general(a, b, (((1,), (1,)), ((), ())), preferred_element_type=F32)


def _dot_tn(a, b):
    return lax.dot_general(a, b, (((0,), (0,)), ((), ())), preferred_element_type=F32)


def _sel_dot(sel, x):
    hi, mid, lo = _split3(x)
    return _dot(sel, hi) + _dot(sel, mid) + _dot(sel, lo)


def _prep_w_in_kernel(cur_ref, nxt_ref, ff_ref, glr_ref, wm_ref, ws_ref, *, tn):
    j = pl.program_id(2)
    n_plain = _OFF_FF // tn

    @pl.when(j < n_plain)
    def _():
        wm_ref[...] = cur_ref[...].astype(BF16)

    @pl.when(j >= n_plain)
    def _():
        cat = jnp.concatenate([cur_ref[...], nxt_ref[...]], axis=1)
        wm_ref[...] = cat[:, FOX_HEADS:FOX_HEADS + tn].astype(BF16)

    @pl.when(j == 0)
    def _():
        lane = lax.broadcasted_iota(jnp.int32, ff_ref.shape, 1)
        ws = jnp.where(lane < S_GLR, ff_ref[...], jnp.where(lane < S_GLR + GLA_GATE_RANK, glr_ref[...], 0.0))
        ws_ref[...] = ws.astype(BF16)


def prep_w_in(w_in, *, tr, tn):
    L, D, _ = w_in.shape
    per = tn // LANES
    assert _OFF_FF % tn == 0 and MAIN_COLS % tn == 0 and _OFF_SQ - _OFF_FF == S_GLR
    assert _OFF_GLR % LANES == S_GLR
    return pl.pallas_call(
        functools.partial(_prep_w_in_kernel, tn=tn),
        out_shape=[jax.ShapeDtypeStruct((L, D, MAIN_COLS), BF16),
                   jax.ShapeDtypeStruct((L, D, LANES), BF16)],
        grid=(L, D // tr, MAIN_COLS // tn),
        in_specs=[
            pl.BlockSpec((None, tr, tn), lambda l, r, j: (l, r, j)),
            pl.BlockSpec((None, tr, LANES), lambda l, r, j: (l, r, (j + 1) * per)),
            pl.BlockSpec((None, tr, LANES), lambda l, r, j: (l, r, _OFF_FF // LANES)),
            pl.BlockSpec((None, tr, LANES), lambda l, r, j: (l, r, _OFF_GLR // LANES)),
        ],
        out_specs=[pl.BlockSpec((None, tr, tn), lambda l, r, j: (l, r, j)),
                   pl.BlockSpec((None, tr, LANES), lambda l, r, j: (l, r, 0))],
        compiler_params=_cparams(("arbitrary", "arbitrary", "arbitrary")),
        name="prep_w_in",
    )(w_in, w_in, w_in, w_in)


def _norm_proj_kernel(*refs, has_small):
    if has_small:
        x_ref, g_ref, w_ref, cs_ref, ws_ref, o_ref, os_ref, xn_ref = refs
    else:
        x_ref, g_ref, w_ref, cs_ref, o_ref, xn_ref = refs

    @pl.when(pl.program_id(1) == 0)
    def _():
        xn = _rms(x_ref[...], g_ref[...]).astype(BF16)
        xn_ref[...] = xn
        if has_small:
            os_ref[...] = _dot(xn, ws_ref[...])

    o_ref[...] = (_dot(xn_ref[...], w_ref[...]) * cs_ref[...]).astype(o_ref.dtype)


def norm_proj(x, g, w, layer, colscale, w_small=None, *, tm, tn):
    T, D = x.shape
    N = w.shape[2]
    has_small = w_small is not None
    in_specs = [
        pl.BlockSpec((tm, D), lambda i, j: (i, 0)),
        pl.BlockSpec((1, D), lambda i, j: (0, 0)),
        pl.BlockSpec((None, D, tn), lambda i, j: (layer, 0, j)),
        pl.BlockSpec((1, tn), lambda i, j: (0, j)),
    ]
    args = [x, g.reshape(1, D), w, colscale]
    out_shape = [jax.ShapeDtypeStruct((T, N), BF16)]
    out_specs = [pl.BlockSpec((tm, tn), lambda i, j: (i, j))]
    if has_small:
        in_specs.append(pl.BlockSpec((None, D, LANES), lambda i, j: (layer, 0, 0)))
        args.append(w_small)
        out_shape.append(jax.ShapeDtypeStruct((T, LANES), F32))
        out_specs.append(pl.BlockSpec((tm, LANES), lambda i, j: (i, 0)))
    outs = pl.pallas_call(
        functools.partial(_norm_proj_kernel, has_small=has_small),
        out_shape=out_shape,
        grid=(T // tm, N // tn),
        in_specs=in_specs,
        out_specs=out_specs,
        scratch_shapes=[pltpu.VMEM((tm, D), BF16)],
        compiler_params=_cparams(("arbitrary", "arbitrary")),
        name="norm_proj_small" if has_small else "norm_proj",
    )(*args)
    return outs if has_small else outs[0]


def _fox_gate_kernel(ps_ref, b_ref, kx_ref, row_ref, carry_ref, *, tb):
    @pl.when(pl.program_id(0) == 0)
    def _():
        carry_ref[...] = jnp.zeros_like(carry_ref)

    x = jax.nn.log_sigmoid(ps_ref[...] + b_ref[...])
    r = lax.broadcasted_iota(jnp.int32, (tb, tb), 0)
    c = lax.broadcasted_iota(jnp.int32, (tb, tb), 1)
    tril = jnp.where(c <= r, 1.0, 0.0).astype(BF16)
    csum = _sel_dot(tril, x) + carry_ref[0:1, :]
    carry_ref[...] = jnp.broadcast_to(csum[tb - 1:tb, :], carry_ref.shape)
    pieces = _split3(csum * LOG2E)
    neg = [-p for p in pieces]
    lane_r = lax.broadcasted_iota(jnp.int32, (LANES, LANES), 0)
    lane_c = lax.broadcasted_iota(jnp.int32, (LANES, LANES), 1)
    lane_s = lax.broadcasted_iota(jnp.int32, (SUBLANES, LANES), 1)
    for h in range(FOX_HEADS):
        kx = sum(_dot(neg[n], jnp.where((lane_r == S_FF + h) & (lane_c == n), 1.0, 0.0).astype(BF16))
                 for n in range(len(neg)))
        kx_ref[h] = kx.astype(BF16)
        pick = jnp.where(lane_s == S_FF + h, 1.0, 0.0).astype(BF16)
        row_ref[h, 0] = sum(_dot_nt(pick, p) for p in pieces)


def fox_gate(ps, bvec, *, tb):
    T = ps.shape[0]
    nb = T // tb
    return pl.pallas_call(
        functools.partial(_fox_gate_kernel, tb=tb),
        out_shape=[jax.ShapeDtypeStruct((FOX_HEADS, T, LANES), BF16),
                   jax.ShapeDtypeStruct((FOX_HEADS, nb, SUBLANES, tb), F32)],
        grid=(nb,),
        in_specs=[pl.BlockSpec((tb, LANES), lambda i: (i, 0)),
                  pl.BlockSpec((1, LANES), lambda i: (0, 0))],
        out_specs=[pl.BlockSpec((FOX_HEADS, tb, LANES), lambda i: (0, i, 0)),
                   pl.BlockSpec((FOX_HEADS, 1, SUBLANES, tb), lambda i: (0, i, 0, 0))],
        scratch_shapes=[pltpu.VMEM((SUBLANES, LANES), F32)],
        compiler_params=_cparams(("arbitrary",)),
        name="fox_gate",
    )(ps, bvec)


def _fox_attn_kernel(q_ref, k_ref, v_ref, kx_ref, cq_ref, o_ref, vt_ref, ya_ref, yb_ref, m_ref, l_ref, acc_ref,
                     *, tq, tk, nk, nh):
    i = pl.program_id(1)
    D = HEAD_DIM
    heads = range(nh)
    n_full = i // (tk // tq)

    @pl.when(i == 0)
    def _():
        for hh in heads:
            for n in range(nk):
                vt_ref[hh, n] = v_ref[n * tk:(n + 1) * tk, hh * D:(hh + 1) * D].T

    lane = lax.broadcasted_iota(jnp.int32, (tq, LANES), 1)
    ones = jnp.where(lane < N_SPLIT, 1.0, 0.0).astype(BF16)
    q_aug = [jnp.concatenate([q_ref[:, hh * D:(hh + 1) * D], ones], axis=1) for hh in heads]
    cq = [cq_ref[hh, 0, 0:1, :] for hh in heads]
    m_ref[...] = jnp.full(m_ref.shape, -jnp.inf, F32)
    l_ref[...] = jnp.zeros(l_ref.shape, F32)
    acc_ref[...] = jnp.zeros(acc_ref.shape, F32)

    def scores(hh, j):
        start = pl.multiple_of(j * tk, tk)
        k_aug = jnp.concatenate([k_ref[pl.ds(start, tk), hh * D:(hh + 1) * D], kx_ref[hh, pl.ds(start, tk), :]],
                                axis=1)
        return _dot_nt(k_aug, q_aug[hh])

    def update(hh, j, y):
        m_old = m_ref[hh]
        m_new = jnp.maximum(m_old, jnp.max(y, axis=0, keepdims=True) + cq[hh])
        alpha = jnp.exp2(m_old - m_new)
        p = jnp.exp2(y + (cq[hh] - m_new))
        l_ref[hh] = alpha * l_ref[hh] + jnp.sum(p, axis=0, keepdims=True)
        acc_ref[hh] = alpha * acc_ref[hh] + _dot(vt_ref[hh, j], p.astype(BF16))
        m_ref[hh] = m_new

    for hh in heads:
        ya_ref[hh] = scores(hh, 0)

    def step(j, cur_ref, nxt_ref):
        for hh in heads:
            nxt_ref[hh] = scores(hh, j + 1)
        for hh in heads:
            update(hh, j, cur_ref[hh])

    def body(j, carry):
        pl.when(j % 2 == 0)(lambda: step(j, ya_ref, yb_ref))
        pl.when(j % 2 == 1)(lambda: step(j, yb_ref, ya_ref))
        return carry

    lax.fori_loop(0, n_full, body, 0)
    key = n_full * tk + lax.broadcasted_iota(jnp.int32, (tk, tq), 0)
    qry = i * tq + lax.broadcasted_iota(jnp.int32, (tk, tq), 1)

    def diagonal(cur_ref):
        for hh in heads:
            update(hh, n_full, jnp.where(key <= qry, cur_ref[hh], -jnp.inf))

    pl.when(n_full % 2 == 0)(lambda: diagonal(ya_ref))
    pl.when(n_full % 2 == 1)(lambda: diagonal(yb_ref))
    for hh in heads:
        o_ref[:, hh * D:(hh + 1) * D] = (acc_ref[hh] / l_ref[hh]).T.astype(o_ref.dtype)


def fox_attention(proj, kx, row, *, tq, tk, nh):
    T = proj.shape[0]
    nq = T // tq
    gw = nh * HEAD_DIM
    qb, kb, vb = C_FQ // gw, C_FK // gw, C_FV // gw
    assert tk % tq == 0 and T % tk == 0
    return pl.pallas_call(
        functools.partial(_fox_attn_kernel, tq=tq, tk=tk, nk=T // tk, nh=nh),
        out_shape=jax.ShapeDtypeStruct((T, FOX_HEADS * HEAD_DIM), BF16),
        grid=(FOX_HEADS // nh, nq),
        in_specs=[
            pl.BlockSpec((tq, gw), lambda g, i: (i, qb + g)),
            pl.BlockSpec((T, gw), lambda g, i: (0, kb + g)),
            pl.BlockSpec((T, gw), lambda g, i: (0, vb + g)),
            pl.BlockSpec((nh, T, LANES), lambda g, i: (g, 0, 0)),
            pl.BlockSpec((nh, 1, SUBLANES, tq), lambda g, i: (g, i, 0, 0)),
        ],
        out_specs=pl.BlockSpec((tq, gw), lambda g, i: (i, g)),
        scratch_shapes=[pltpu.VMEM((nh, T // tk, HEAD_DIM, tk), BF16),
                        pltpu.VMEM((nh, tk, tq), F32), pltpu.VMEM((nh, tk, tq), F32),
                        pltpu.VMEM((nh, 1, tq), F32), pltpu.VMEM((nh, 1, tq), F32),
                        pltpu.VMEM((nh, HEAD_DIM, tq), F32)],
        compiler_params=_cparams(("arbitrary", "arbitrary")),
        name="fox_attn",
    )(proj, proj, proj, kx, row)


def _t5_table_kernel(t5_ref, o_ref):
    h = pl.program_id(0)
    W = SWA_WINDOW
    i = lax.broadcasted_iota(jnp.int32, (W, 2 * W), 0)
    j = lax.broadcasted_iota(jnp.int32, (W, 2 * W), 1)
    rel = (W + i) - j
    n = jnp.maximum(rel, 0)
    max_exact = NUM_BUCKETS // 2
    nf = jnp.maximum(n, 1).astype(F32)
    large = max_exact + (jnp.log(nf / max_exact) / math.log(T5_MAX_DISTANCE / max_exact)
                         * (NUM_BUCKETS - max_exact)).astype(jnp.int32)
    large = jnp.minimum(large, NUM_BUCKETS - 1)
    bucket = jnp.where(n < max_exact, n, large)
    bias = jnp.zeros((W, 2 * W), F32)
    for b in range(NUM_BUCKETS):
        bias = jnp.where(bucket == b, t5_ref[b, h], bias)
    in_window = (rel >= 0) & (rel < W)
    o_ref[...] = jnp.where(in_window, bias, -jnp.inf)


def t5_table(t5_bias):
    W = SWA_WINDOW
    return pl.pallas_call(
        _t5_table_kernel,
        out_shape=jax.ShapeDtypeStruct((SWA_HEADS, W, 2 * W), F32),
        grid=(SWA_HEADS,),
        in_specs=[pl.BlockSpec(memory_space=pltpu.SMEM)],
        out_specs=pl.BlockSpec((None, W, 2 * W), lambda h: (h, 0, 0)),
        compiler_params=_cparams(("arbitrary",)),
        name="t5_table",
    )(t5_bias)


def _swa_kernel(sink_ref, q_ref, kc_ref, kp_ref, vc_ref, vp_ref, tab_ref, o_ref, *, nsub):
    g = pl.program_id(0)
    n = pl.program_id(1)
    W = SWA_WINDOW
    first_prev = lax.broadcasted_iota(jnp.int32, (W, 2 * W), 1) < W
    for sub in range(nsub):
        if sub == 0:
            kp, vp = kp_ref[...], vp_ref[...]
        else:
            kp, vp = kc_ref[(sub - 1) * W:sub * W, :], vc_ref[(sub - 1) * W:sub * W, :]
        kk = jnp.concatenate([kp, kc_ref[sub * W:(sub + 1) * W, :]], axis=0)
        vv = jnp.concatenate([vp, vc_ref[sub * W:(sub + 1) * W, :]], axis=0)
        for hh in range(SWA_GROUP):
            q = q_ref[sub * W:(sub + 1) * W, hh * HEAD_DIM:(hh + 1) * HEAD_DIM]
            s = _dot_nt(q, kk) + tab_ref[hh]
            if sub == 0:
                s = jnp.where(jnp.logical_and(first_prev, n == 0), -jnp.inf, s)
            sink = sink_ref[0, g * SWA_GROUP + hh]
            m = jnp.maximum(jnp.max(s, axis=1, keepdims=True), sink)
            p = jnp.exp(s - m)
            l = jnp.sum(p, axis=1, keepdims=True) + jnp.exp(sink - m)
            o = _dot(p.astype(BF16), vv) / l
            o_ref[sub * W:(sub + 1) * W, hh * HEAD_DIM:(hh + 1) * HEAD_DIM] = o.astype(o_ref.dtype)


def swa_attention(proj, table, sinks, *, tb):
    T = proj.shape[0]
    W = SWA_WINDOW
    nsub = tb // W
    gw = SWA_GROUP * HEAD_DIM
    qb, kb, vb = C_SQ // gw, C_SK // HEAD_DIM, C_SV // HEAD_DIM
    prev = lambda n: jnp.maximum(n * nsub - 1, 0)
    return pl.pallas_call(
        functools.partial(_swa_kernel, nsub=nsub),
        out_shape=jax.ShapeDtypeStruct((T, SWA_HEADS * HEAD_DIM), BF16),
        grid=(SWA_KV_HEADS, T // tb),
        in_specs=[
            pl.BlockSpec(memory_space=pltpu.SMEM),
            pl.BlockSpec((tb, gw), lambda g, n: (n, qb + g)),
            pl.BlockSpec((tb, HEAD_DIM), lambda g, n: (n, kb + g)),
            pl.BlockSpec((W, HEAD_DIM), lambda g, n: (prev(n), kb + g)),
            pl.BlockSpec((tb, HEAD_DIM), lambda g, n: (n, vb + g)),
            pl.BlockSpec((W, HEAD_DIM), lambda g, n: (prev(n), vb + g)),
            pl.BlockSpec((SWA_GROUP, W, 2 * W), lambda g, n: (g, 0, 0)),
        ],
        out_specs=pl.BlockSpec((tb, gw), lambda g, n: (n, g)),
        compiler_params=_cparams(("arbitrary", "arbitrary")),
        name="swa_attn",
    )(sinks.reshape(1, SWA_HEADS), proj, proj, proj, proj, proj, table)


def _gla_kernel(q_ref, k_ref, v_ref, r_ref, ps_ref, wg_ref, bg_ref, gn_ref, o_ref, st_ref, oc_ref, *, tb):
    C = GLA_CHUNK

    @pl.when(pl.program_id(0) == 0)
    def _():
        st_ref[...] = jnp.zeros_like(st_ref)

    glr = ps_ref[...].astype(BF16)
    g = jax.nn.log_sigmoid(_dot(glr, wg_ref[...]) + bg_ref[...]) / GLA_TAU
    r = lax.broadcasted_iota(jnp.int32, (tb, tb), 0)
    c = lax.broadcasted_iota(jnp.int32, (tb, tb), 1)
    same = (r // C) == (c // C)
    tril = jnp.where(jnp.logical_and(same, c <= r), 1.0, 0.0).astype(BF16)
    whole = jnp.where(same, 1.0, 0.0).astype(BF16)
    gh, gm, gl = _split3(g)
    b = _dot(tril, gh) + _dot(tril, gm) + _dot(tril, gl)
    b_last = _dot(whole, gh) + _dot(whole, gm) + _dot(whole, gl)
    q_t = (q_ref[...].astype(F32) * jnp.exp(b)).astype(BF16)
    kf = k_ref[...].astype(F32)
    k_t = kf * jnp.exp(-b)
    k_end = kf * jnp.exp(b_last - b)
    decay = jnp.exp(b_last)
    lane = lax.broadcasted_iota(jnp.int32, (1, LANES), 1)
    causal = (lax.broadcasted_iota(jnp.int32, (C, C), 1) <= lax.broadcasted_iota(jnp.int32, (C, C), 0))
    for h in range(GLA_HEADS):
        pair = slice((h // 2) * LANES, (h // 2 + 1) * LANES)
        mine = (lane // GLA_DK) == (h % 2)
        ktm = jnp.where(mine, k_t[:, pair], 0.0).astype(BF16)
        kem = jnp.where(mine, k_end[:, pair], 0.0).astype(BF16)
        qh = q_t[:, pair]
        dec = decay[:, pair]
        vh = v_ref[:, h * GLA_DV:(h + 1) * GLA_DV]
        st = st_ref[h]
        for n in range(tb // C):
            rows = slice(n * C, (n + 1) * C)
            a = jnp.where(causal, _dot_nt(qh[rows], ktm[rows]), 0.0)
            o = _dot(a.astype(BF16), vh[rows]) + _dot_nt(qh[rows], st.astype(BF16))
            oc_ref[rows, h * GLA_DV:(h + 1) * GLA_DV] = o
            kv_t = _dot_tn(vh[rows], kem[rows])
            st = dec[n * C:n * C + 1, :] * st + kv_t
        st_ref[h] = st
    for h in range(GLA_HEADS):
        cols = slice(h * GLA_DV, (h + 1) * GLA_DV)
        o = _rms(oc_ref[:, cols], gn_ref[...])
        o_ref[:, cols] = (o * jax.nn.silu(r_ref[:, cols].astype(F32))).astype(o_ref.dtype)


def gla(proj, ps, wg_pad, bg, gnorm, *, tb):
    T = proj.shape[0]
    qw = GLA_HEADS * GLA_DK
    vw = GLA_HEADS * GLA_DV
    return pl.pallas_call(
        functools.partial(_gla_kernel, tb=tb),
        out_shape=jax.ShapeDtypeStruct((T, vw), BF16),
        grid=(T // tb,),
        in_specs=[
            pl.BlockSpec((tb, qw), lambda n: (n, C_GQ // qw)),
            pl.BlockSpec((tb, qw), lambda n: (n, C_GK // qw)),
            pl.BlockSpec((tb, vw), lambda n: (n, C_GV // vw)),
            pl.BlockSpec((tb, vw), lambda n: (n, C_GR // vw)),
            pl.BlockSpec((tb, LANES), lambda n: (n, 0)),
            pl.BlockSpec((LANES, qw), lambda n: (0, 0)),
            pl.BlockSpec((1, qw), lambda n: (0, 0)),
            pl.BlockSpec((1, GLA_DV), lambda n: (0, 0)),
        ],
        out_specs=pl.BlockSpec((tb, vw), lambda n: (n, 0)),
        scratch_shapes=[pltpu.VMEM((GLA_HEADS, GLA_DV, LANES), F32),
                        pltpu.VMEM((tb, vw), F32)],
        compiler_params=_cparams(("arbitrary",)),
        name="gla",
    )(proj, proj, proj, proj, ps, wg_pad, bg, gnorm)


def _mix_xattn_kernel(h_ref, of_ref, os_ref, og_ref, wo_ref, gx_ref, wq_ref, kv_ref, wx_ref, o_ref):
    nf, ns = of_ref.shape[1], os_ref.shape[1]
    mix = (_dot(of_ref[...], wo_ref[0:nf, :]) + _dot(os_ref[...], wo_ref[nf:nf + ns, :])
           + _dot(og_ref[...], wo_ref[nf + ns:, :]))
    h1 = h_ref[...] + mix
    xn = _rms(h1, gx_ref[...]).astype(BF16)
    q = (_dot(xn, wq_ref[...]) * (HEAD_DIM ** -0.5)).astype(BF16)
    xw = XATTN_HEADS * HEAD_DIM
    outs = []
    for hh in range(XATTN_HEADS):
        cols = slice(hh * HEAD_DIM, (hh + 1) * HEAD_DIM)
        k = kv_ref[:, cols]
        v = kv_ref[:, xw + hh * HEAD_DIM:xw + (hh + 1) * HEAD_DIM]
        s = _dot_nt(q[:, cols], k)
        p = jnp.exp(s - jnp.max(s, axis=1, keepdims=True))
        l = jnp.sum(p, axis=1, keepdims=True)
        outs.append((_dot(p.astype(BF16), v) / l).astype(BF16))
    o = jnp.concatenate(outs, axis=1)
    o_ref[...] = h1 + _dot(o, wx_ref[...])


def mix_xattn(h, o_fox, o_swa, o_gla, w_out, layer, gx, wq, kv, wo, *, tm):
    T, D = h.shape
    const = lambda shape: pl.BlockSpec(shape, lambda i: (0, 0))
    stacked = lambda w: pl.BlockSpec((None,) + w.shape[1:], lambda i: (layer, 0, 0))
    row = lambda w: pl.BlockSpec((tm, w), lambda i: (i, 0))
    return pl.pallas_call(
        _mix_xattn_kernel,
        out_shape=jax.ShapeDtypeStruct((T, D), F32),
        grid=(T // tm,),
        in_specs=[row(D), row(o_fox.shape[1]), row(o_swa.shape[1]), row(o_gla.shape[1]),
                  stacked(w_out), const((1, D)), stacked(wq), const(kv.shape), stacked(wo)],
        out_specs=row(D),
        compiler_params=_cparams(("arbitrary",)),
        name="mix_xattn",
    )(h, o_fox, o_swa, o_gla, w_out, gx.reshape(1, D), wq, kv, wo)


def _ffn_kernel(x_ref, g_ref, wg_ref, wv_ref, cwg_ref, cwv_ref, cbg_ref, cbv_ref, wd_ref, fg_ref,
                o_ref, xn_ref, hg_ref, hv_ref, ug_ref, uv_ref, *, tm, rb, sub, final):
    i = pl.program_id(0)
    f = pl.program_id(1)
    H = SUBLANES
    fc = wd_ref.shape[0]
    nrb = tm // rb

    @pl.when(f == 0)
    def _():
        x = x_ref[...]
        xn_ref[...] = _rms(x, g_ref[...]).astype(BF16)
        o_ref[...] = x

    @pl.when(i == 0)
    def _():
        hg_ref[f] = jnp.zeros(hg_ref.shape[1:], F32)
        hv_ref[f] = jnp.zeros(hv_ref.shape[1:], F32)

    def up(r, n):
        c = n * sub
        xn = xn_ref[r * rb:(r + 1) * rb, :]
        ug_ref[r, n, H:, :] = _dot(xn, wg_ref[:, c:c + sub])
        uv_ref[r, n, H:, :] = _dot(xn, wv_ref[:, c:c + sub])

    def conv(u_ref, r, n, h_ref, cw_ref, cb_ref):
        c = n * sub
        u_ref[r, n, 0:H, :] = h_ref[f, :, c:c + sub] if r == 0 else u_ref[r - 1, n, rb:rb + H, :]
        if r == nrb - 1:
            h_ref[f, :, c:c + sub] = u_ref[r, n, rb:rb + H, :]
        return (cb_ref[:, c:c + sub] + cw_ref[2:3, c:c + sub] * u_ref[r, n, H:H + rb, :]
                + cw_ref[1:2, c:c + sub] * u_ref[r, n, H - 1:H - 1 + rb, :]
                + cw_ref[0:1, c:c + sub] * u_ref[r, n, H - 2:H - 2 + rb, :])

    def down(r, n):
        gate = conv(ug_ref, r, n, hg_ref, cwg_ref, cbg_ref)
        val = conv(uv_ref, r, n, hv_ref, cwv_ref, cbv_ref)
        act = (jax.nn.silu(gate) * val).astype(BF16)
        o_ref[r * rb:(r + 1) * rb, :] += _dot(act, wd_ref[n * sub:(n + 1) * sub, :])

    units = [(r, n) for n in range(fc // sub) for r in range(nrb)]
    up(*units[0])
    for k, unit in enumerate(units):
        if k + 1 < len(units):
            up(*units[k + 1])
        down(*unit)

    if final:
        @pl.when(f == pl.num_programs(1) - 1)
        def _():
            o_ref[...] = _rms(o_ref[...], fg_ref[...])


def conv_ffn(x, g, w_up, conv_w, conv_b, w_down, layer, final_g, *, tm, rb, fc, final):
    T, D = x.shape
    L, dff, _ = w_down.shape
    nf = dff // fc
    cb = conv_b.reshape(L, 1, 2 * dff)
    return pl.pallas_call(
        functools.partial(_ffn_kernel, tm=tm, rb=rb, sub=MXU_COLS, final=final),
        out_shape=jax.ShapeDtypeStruct((T, D), F32),
        grid=(T // tm, nf),
        in_specs=[
            pl.BlockSpec((tm, D), lambda i, f: (i, 0), pipeline_mode=pl.Buffered(1)),
            pl.BlockSpec((1, D), lambda i, f: (0, 0)),
            pl.BlockSpec((None, D, fc), lambda i, f: (layer, 0, f)),
            pl.BlockSpec((None, D, fc), lambda i, f: (layer, 0, nf + f)),
            pl.BlockSpec((None, CONV_WIDTH, fc), lambda i, f: (layer, 0, f)),
            pl.BlockSpec((None, CONV_WIDTH, fc), lambda i, f: (layer, 0, nf + f)),
            pl.BlockSpec((None, 1, fc), lambda i, f: (layer, 0, f)),
            pl.BlockSpec((None, 1, fc), lambda i, f: (layer, 0, nf + f)),
            pl.BlockSpec((None, fc, D), lambda i, f: (layer, f, 0)),
            pl.BlockSpec((1, D), lambda i, f: (0, 0)),
        ],
        out_specs=pl.BlockSpec((tm, D), lambda i, f: (i, 0)),
        scratch_shapes=[pltpu.VMEM((tm, D), BF16), pltpu.VMEM((nf, SUBLANES, fc), F32),
                        pltpu.VMEM((nf, SUBLANES, fc), F32),
                        pltpu.VMEM((tm // rb, fc // MXU_COLS, rb + SUBLANES, MXU_COLS), F32),
                        pltpu.VMEM((tm // rb, fc // MXU_COLS, rb + SUBLANES, MXU_COLS), F32)],
        compiler_params=_cparams(("arbitrary", "arbitrary")),
        name="conv_ffn",
    )(x, g.reshape(1, D), w_up, w_up, conv_w, conv_w, cb, cb, w_down, final_g.reshape(1, D))


TR_PREP, TN_PREP = 512, 512
TM_PROJ, TN_PROJ = 1024, 1536
TB_GATE = 512
TQ_FOX, TK_FOX, NH_FOX = 512, 512, 2
TB_SWA = 512
TB_GLA = 512
TM_MIX = 512
TM_FFN, RB_FFN, FC_FFN = 1024, 512, 512


def _main_colscale():
    cs = np.ones((1, MAIN_COLS), np.float32)
    cs[0, C_FQ:C_FK] = HEAD_DIM ** -0.5 * LOG2E
    cs[0, C_SQ:C_SK] = HEAD_DIM ** -0.5
    cs[0, C_GQ:C_GK] = GLA_DK ** -0.5
    return jnp.asarray(cs)


def kernel(x, mem, w_in, b_fox_f, swa_sinks, t5_bias, w_gla_gate, b_gla_gate, gla_norm, w_mix_out, norm_mix,
           norm_xattn, norm_mem, wq_x, wkv_x, wo_x, norm_ffn, w_up, conv_w, conv_b, w_down, final_norm):
    depth = w_in.shape[0]
    _, T, D = x.shape
    M = mem.shape[1]
    h = x.reshape(T, D)
    memf = mem.reshape(M, D)
    colscale = _main_colscale()
    ones_kv = jnp.ones((1, wkv_x.shape[2]), F32)
    table = t5_table(t5_bias)
    pad_lanes = LANES - FOX_HEADS - GLA_GATE_RANK
    w_main, w_small = prep_w_in(w_in, tr=TR_PREP, tn=TN_PREP)
    w_out_b, wq_b, wkv_b, wo_b = (w.astype(BF16) for w in (w_mix_out, wq_x, wkv_x, wo_x))
    w_up_b, w_down_b = w_up.astype(BF16), w_down.astype(BF16)
    for l in range(depth):
        bvec = jnp.concatenate([b_fox_f[l], jnp.zeros((LANES - FOX_HEADS,), F32)]).reshape(1, LANES)
        wg_pad = jnp.concatenate([jnp.zeros((S_GLR, GLA_HEADS * GLA_DK), F32), w_gla_gate[l],
                                  jnp.zeros((pad_lanes, GLA_HEADS * GLA_DK), F32)], axis=0).astype(BF16)

        proj, ps = norm_proj(h, norm_mix[l], w_main, l, colscale, w_small, tm=TM_PROJ, tn=TN_PROJ)
        kx, row = fox_gate(ps, bvec, tb=TB_GATE)
        o_fox = fox_attention(proj, kx, row, tq=TQ_FOX, tk=TK_FOX, nh=NH_FOX)
        o_swa = swa_attention(proj, table, swa_sinks[l], tb=TB_SWA)
        o_gla = gla(proj, ps, wg_pad, b_gla_gate[l].reshape(1, -1), gla_norm[l].reshape(1, -1), tb=TB_GLA)
        kv = norm_proj(memf, norm_mem[l], wkv_b, l, ones_kv, tm=M, tn=wkv_x.shape[2])
        h = mix_xattn(h, o_fox, o_swa, o_gla, w_out_b, l, norm_xattn[l], wq_b, kv, wo_b, tm=TM_MIX)
        h = conv_ffn(h, norm_ffn[l], w_up_b, conv_w, conv_b, w_down_b, l, final_norm,
                     tm=TM_FFN, rb=RB_FFN, fc=FC_FFN, final=(l == depth - 1))
    return h.reshape(x.shape)
```

```python
import functools
import math

import numpy as np
import jax
import jax.numpy as jnp
from jax import lax
from jax.experimental import pallas as pl
from jax.experimental.pallas import tpu as pltpu

F32 = jnp.float32
BF16 = jnp.bfloat16

HEAD_DIM = 128
FOX_HEADS = 4
SWA_HEADS = 8
SWA_KV_HEADS = 2
SWA_GROUP = SWA_HEADS // SWA_KV_HEADS
GLA_HEADS = 4
GLA_DK = 64
GLA_DV = 128
GLA_GATE_RANK = 16
GLA_TAU = 16.0
GLA_CHUNK = 64
SWA_WINDOW = 128
NUM_BUCKETS = 32
T5_MAX_DISTANCE = 128
XATTN_HEADS = 4
CONV_WIDTH = 3
EPS = 1e-6
LOG2E = math.log2(math.e)

LANES = 128
SUBLANES = 8
BF16_ROWS = 16
MXU_COLS = 256
VMEM_LIMIT = 56 * 1024 * 1024

_FOX_W = FOX_HEADS * HEAD_DIM
_OFF_FF = 3 * _FOX_W
_OFF_SQ = _OFF_FF + FOX_HEADS
_MAIN_B = (SWA_HEADS + 2 * SWA_KV_HEADS) * HEAD_DIM + 2 * GLA_HEADS * GLA_DK + 2 * GLA_HEADS * GLA_DV
_OFF_GLR = _OFF_SQ + _MAIN_B
C_FQ, C_FK, C_FV = 0, 512, 1024
C_SQ, C_SK, C_SV = 1536, 2560, 2816
C_GQ, C_GK, C_GV, C_GR = 3072, 3328, 3584, 4096
MAIN_COLS = 4608
S_FF, S_GLR = 0, SUBLANES


def _cparams(sem, flags=None):
    return pltpu.CompilerParams(dimension_semantics=sem, vmem_limit_bytes=VMEM_LIMIT, flags=flags)


def _rms(x, g):
    return x * lax.rsqrt(jnp.mean(x * x, axis=-1, keepdims=True) + EPS) * g


N_SPLIT = 3


def _split3(x):
    hi = x.astype(BF16)
    r1 = x - hi.astype(F32)
    mid = r1.astype(BF16)
    lo = (r1 - mid.astype(F32)).astype(BF16)
    return hi, mid, lo


def _dot(a, b):
    return jnp.dot(a, b, preferred_element_type=F32)


def _dot_nt(a, b):
    return lax.dot_general(a, b, (((1,), (1,)), ((), ())), preferred_element_type=F32)


def _dot_tn(a, b):
    return lax.dot_general(a, b, (((0,), (0,)), ((), ())), preferred_element_type=F32)


def _sel_dot(sel, x):
    hi, mid, lo = _split3(x)
    return _dot(sel, hi) + _dot(sel, mid) + _dot(sel, lo)


def _prep_w_in_kernel(cur_ref, ff_ref, glr_ref, wm_ref, ws_ref):
    L = wm_ref.shape[0]
    for l in range(L):
        wm_ref[l] = cur_ref[:, l, :].astype(BF16)

    @pl.when(pl.program_id(1) == 0)
    def _():
        pad = jnp.zeros((LANES - S_GLR - GLA_GATE_RANK, ws_ref.shape[2]), F32)
        for l in range(L):
            ws_ref[l] = jnp.concatenate([ff_ref[:, l, :], glr_ref[:, l, :], pad], axis=0).astype(BF16)


def prep_w_in(w_in, *, tr, tc):
    L, D, _ = w_in.shape
    wt = jnp.transpose(w_in, (2, 0, 1))
    assert _OFF_FF % tc == 0 and MAIN_COLS % tc == 0 and S_GLR % SUBLANES == 0

    def src(j):
        return j * tc + jnp.where(j * tc >= _OFF_FF, _OFF_SQ - _OFF_FF, 0)

    return pl.pallas_call(
        _prep_w_in_kernel,
        out_shape=[jax.ShapeDtypeStruct((L, MAIN_COLS, D), BF16),
                   jax.ShapeDtypeStruct((L, LANES, D), BF16)],
        grid=(D // tr, MAIN_COLS // tc),
        in_specs=[
            pl.BlockSpec((pl.Element(tc), pl.Element(L), pl.Element(tr)), lambda r, j: (src(j), 0, r * tr)),
            pl.BlockSpec((pl.Element(S_GLR), pl.Element(L), pl.Element(tr)), lambda r, j: (_OFF_FF, 0, r * tr)),
            pl.BlockSpec((pl.Element(GLA_GATE_RANK), pl.Element(L), pl.Element(tr)),
                         lambda r, j: (_OFF_GLR, 0, r * tr)),
        ],
        out_specs=[pl.BlockSpec((L, tc, tr), lambda r, j: (0, j, r)),
                   pl.BlockSpec((L, LANES, tr), lambda r, j: (0, 0, r))],
        compiler_params=_cparams(("arbitrary", "arbitrary")),
        name="prep_w_in",
    )(wt, wt, wt)


def _norm_proj_kernel(*refs, has_small, w_t):
    if has_small:
        x_ref, g_ref, w_ref, cs_ref, ws_ref, o_ref, os_ref, xn_ref = refs
    else:
        x_ref, g_ref, w_ref, cs_ref, o_ref, xn_ref = refs
    dot = _dot_nt if w_t else _dot

    @pl.when(pl.program_id(1) == 0)
    def _():
        xn = _rms(x_ref[...], g_ref[...]).astype(BF16)
        xn_ref[...] = xn
        if has_small:
            os_ref[...] = dot(xn, ws_ref[...])

    o_ref[...] = (dot(xn_ref[...], w_ref[...]) * cs_ref[...]).astype(o_ref.dtype)


def norm_proj(x, g, w, layer, colscale, w_small=None, *, tm, tn, w_t=False):
    T, D = x.shape
    N = w.shape[1] if w_t else w.shape[2]
    has_small = w_small is not None

    def wspec(n):
        if w_t:
            return pl.BlockSpec((None, n, D), lambda i, j: (layer, j, 0))
        return pl.BlockSpec((None, D, n), lambda i, j: (layer, 0, j))

    in_specs = [
        pl.BlockSpec((tm, D), lambda i, j: (i, 0)),
        pl.BlockSpec((1, D), lambda i, j: (0, 0)),
        wspec(tn),
        pl.BlockSpec((1, tn), lambda i, j: (0, j)),
    ]
    args = [x, g.reshape(1, D), w, colscale]
    out_shape = [jax.ShapeDtypeStruct((T, N), BF16)]
    out_specs = [pl.BlockSpec((tm, tn), lambda i, j: (i, j))]
    if has_small:
        in_specs.append(pl.BlockSpec((None,) + w_small.shape[1:], lambda i, j: (layer, 0, 0)))
        args.append(w_small)
        out_shape.append(jax.ShapeDtypeStruct((T, LANES), F32))
        out_specs.append(pl.BlockSpec((tm, LANES), lambda i, j: (i, 0)))
    outs = pl.pallas_call(
        functools.partial(_norm_proj_kernel, has_small=has_small, w_t=w_t),
        out_shape=out_shape,
        grid=(T // tm, N // tn),
        in_specs=in_specs,
        out_specs=out_specs,
        scratch_shapes=[pltpu.VMEM((tm, D), BF16)],
        compiler_params=_cparams(("arbitrary", "arbitrary")),
        name="norm_proj_small" if has_small else "norm_proj",
    )(*args)
    return outs if has_small else outs[0]


def _fox_gate_kernel(ps_ref, b_ref, kx_ref, row_ref, carry_ref, *, tb):
    @pl.when(pl.program_id(0) == 0)
    def _():
        carry_ref[...] = jnp.zeros_like(carry_ref)

    x = jax.nn.log_sigmoid(ps_ref[...] + b_ref[...])
    r = lax.broadcasted_iota(jnp.int32, (tb, tb), 0)
    c = lax.broadcasted_iota(jnp.int32, (tb, tb), 1)
    tril = jnp.where(c <= r, 1.0, 0.0).astype(BF16)
    csum = _sel_dot(tril, x) + carry_ref[0:1, :]
    carry_ref[...] = jnp.broadcast_to(csum[tb - 1:tb, :], carry_ref.shape)
    pieces = _split3(csum * LOG2E)
    neg = [-p for p in pieces]
    lane_r = lax.broadcasted_iota(jnp.int32, (LANES, LANES), 0)
    lane_c = lax.broadcasted_iota(jnp.int32, (LANES, LANES), 1)
    lane_s = lax.broadcasted_iota(jnp.int32, (SUBLANES, LANES), 1)
    for h in range(FOX_HEADS):
        kx = sum(_dot(neg[n], jnp.where((lane_r == S_FF + h) & (lane_c == n), 1.0, 0.0).astype(BF16))
                 for n in range(len(neg)))
        kx_ref[h] = kx.astype(BF16)
        pick = jnp.where(lane_s == S_FF + h, 1.0, 0.0).astype(BF16)
        row_ref[h, 0] = sum(_dot_nt(pick, p) for p in pieces)


def fox_gate(ps, bvec, *, tb):
    T = ps.shape[0]
    nb = T // tb
    return pl.pallas_call(
        functools.partial(_fox_gate_kernel, tb=tb),
        out_shape=[jax.ShapeDtypeStruct((FOX_HEADS, T, LANES), BF16),
                   jax.ShapeDtypeStruct((FOX_HEADS, nb, SUBLANES, tb), F32)],
        grid=(nb,),
        in_specs=[pl.BlockSpec((tb, LANES), lambda i: (i, 0)),
                  pl.BlockSpec((1, LANES), lambda i: (0, 0))],
        out_specs=[pl.BlockSpec((FOX_HEADS, tb, LANES), lambda i: (0, i, 0)),
                   pl.BlockSpec((FOX_HEADS, 1, SUBLANES, tb), lambda i: (0, i, 0, 0))],
        scratch_shapes=[pltpu.VMEM((SUBLANES, LANES), F32)],
        compiler_params=_cparams(("arbitrary",)),
        name="fox_gate",
    )(ps, bvec)


def _fox_attn_kernel(q_ref, k_ref, v_ref, kx_ref, cq_ref, o_ref, vt_ref, ya_ref, yb_ref, m_ref, l_ref, acc_ref,
                     *, tq, tk, nk, nh):
    i = pl.program_id(1)
    D = HEAD_DIM
    heads = range(nh)
    n_full = i // (tk // tq)

    @pl.when(i == 0)
    def _():
        for hh in heads:
            for n in range(nk):
                vt_ref[hh, n] = v_ref[n * tk:(n + 1) * tk, hh * D:(hh + 1) * D].T

    lane = lax.broadcasted_iota(jnp.int32, (tq, LANES), 1)
    ones = jnp.where(lane < N_SPLIT, 1.0, 0.0).astype(BF16)
    q_aug = [jnp.concatenate([q_ref[:, hh * D:(hh + 1) * D], ones], axis=1) for hh in heads]
    cq = [cq_ref[hh, 0, 0:1, :] for hh in heads]
    m_ref[...] = jnp.full(m_ref.shape, -jnp.inf, F32)
    l_ref[...] = jnp.zeros(l_ref.shape, F32)
    acc_ref[...] = jnp.zeros(acc_ref.shape, F32)

    def scores(hh, j):
        start = pl.multiple_of(j * tk, tk)
        k_aug = jnp.concatenate([k_ref[pl.ds(start, tk), hh * D:(hh + 1) * D], kx_ref[hh, pl.ds(start, tk), :]],
                                axis=1)
        return _dot_nt(k_aug, q_aug[hh])

    def update(hh, j, y):
        m_old = m_ref[hh]
        m_new = jnp.maximum(m_old, jnp.max(y, axis=0, keepdims=True) + cq[hh])
        alpha = jnp.exp2(m_old - m_new)
        p = jnp.exp2(y + (cq[hh] - m_new))
        l_ref[hh] = alpha * l_ref[hh] + jnp.sum(p, axis=0, keepdims=True)
        acc_ref[hh] = alpha * acc_ref[hh] + _dot(vt_ref[hh, j], p.astype(BF16))
        m_ref[hh] = m_new

    for hh in heads:
        ya_ref[hh] = scores(hh, 0)

    def step(j, cur_ref, nxt_ref):
        for hh in heads:
            nxt_ref[hh] = scores(hh, j + 1)
        for hh in heads:
            update(hh, j, cur_ref[hh])

    def body(j, carry):
        pl.when(j % 2 == 0)(lambda: step(j, ya_ref, yb_ref))
        pl.when(j % 2 == 1)(lambda: step(j, yb_ref, ya_ref))
        return carry

    lax.fori_loop(0, n_full, body, 0)
    key = n_full * tk + lax.broadcasted_iota(jnp.int32, (tk, tq), 0)
    qry = i * tq + lax.broadcasted_iota(jnp.int32, (tk, tq), 1)

    def diagonal(cur_ref):
        for hh in heads:
            update(hh, n_full, jnp.where(key <= qry, cur_ref[hh], -jnp.inf))

    pl.when(n_full % 2 == 0)(lambda: diagonal(ya_ref))
    pl.when(n_full % 2 == 1)(lambda: diagonal(yb_ref))
    for hh in heads:
        o_ref[:, hh * D:(hh + 1) * D] = (acc_ref[hh] / l_ref[hh]).T.astype(o_ref.dtype)


def fox_attention(proj, kx, row, *, tq, tk, nh):
    T = proj.shape[0]
    nq = T // tq
    gw = nh * HEAD_DIM
    qb, kb, vb = C_FQ // gw, C_FK // gw, C_FV // gw
    assert tk % tq == 0 and T % tk == 0
    return pl.pallas_call(
        functools.partial(_fox_attn_kernel, tq=tq, tk=tk, nk=T // tk, nh=nh),
        out_shape=jax.ShapeDtypeStruct((T, FOX_HEADS * HEAD_DIM), BF16),
        grid=(FOX_HEADS // nh, nq),
        in_specs=[
            pl.BlockSpec((tq, gw), lambda g, i: (i, qb + g)),
            pl.BlockSpec((T, gw), lambda g, i: (0, kb + g)),
            pl.BlockSpec((T, gw), lambda g, i: (0, vb + g)),
            pl.BlockSpec((nh, T, LANES), lambda g, i: (g, 0, 0)),
            pl.BlockSpec((nh, 1, SUBLANES, tq), lambda g, i: (g, i, 0, 0)),
        ],
        out_specs=pl.BlockSpec((tq, gw), lambda g, i: (i, g)),
        scratch_shapes=[pltpu.VMEM((nh, T // tk, HEAD_DIM, tk), BF16),
                        pltpu.VMEM((nh, tk, tq), F32), pltpu.VMEM((nh, tk, tq), F32),
                        pltpu.VMEM((nh, 1, tq), F32), pltpu.VMEM((nh, 1, tq), F32),
                        pltpu.VMEM((nh, HEAD_DIM, tq), F32)],
        compiler_params=_cparams(("arbitrary", "arbitrary")),
        name="fox_attn",
    )(proj, proj, proj, kx, row)


def _t5_table_kernel(t5_ref, o_ref):
    h = pl.program_id(0)
    W = SWA_WINDOW
    i = lax.broadcasted_iota(jnp.int32, (W, 2 * W), 0)
    j = lax.broadcasted_iota(jnp.int32, (W, 2 * W), 1)
    rel = (W + i) - j
    n = jnp.maximum(rel, 0)
    max_exact = NUM_BUCKETS // 2
    nf = jnp.maximum(n, 1).astype(F32)
    large = max_exact + (jnp.log(nf / max_exact) / math.log(T5_MAX_DISTANCE / max_exact)
                         * (NUM_BUCKETS - max_exact)).astype(jnp.int32)
    large = jnp.minimum(large, NUM_BUCKETS - 1)
    bucket = jnp.where(n < max_exact, n, large)
    bias = jnp.zeros((W, 2 * W), F32)
    for b in range(NUM_BUCKETS):
        bias = jnp.where(bucket == b, t5_ref[b, h], bias)
    in_window = (rel >= 0) & (rel < W)
    o_ref[...] = jnp.where(in_window, bias, -jnp.inf)


def t5_table(t5_bias):
    W = SWA_WINDOW
    return pl.pallas_call(
        _t5_table_kernel,
        out_shape=jax.ShapeDtypeStruct((SWA_HEADS, W, 2 * W), F32),
        grid=(SWA_HEADS,),
        in_specs=[pl.BlockSpec(memory_space=pltpu.SMEM)],
        out_specs=pl.BlockSpec((None, W, 2 * W), lambda h: (h, 0, 0)),
        compiler_params=_cparams(("arbitrary",)),
        name="t5_table",
    )(t5_bias)


def _swa_kernel(sink_ref, q_ref, kc_ref, kp_ref, vc_ref, vp_ref, tab_ref, o_ref, *, nsub):
    g = pl.program_id(0)
    n = pl.program_id(1)
    W = SWA_WINDOW
    first_prev = lax.broadcasted_iota(jnp.int32, (W, 2 * W), 1) < W
    for sub in range(nsub):
        if sub == 0:
            kp, vp = kp_ref[...], vp_ref[...]
        else:
            kp, vp = kc_ref[(sub - 1) * W:sub * W, :], vc_ref[(sub - 1) * W:sub * W, :]
        kk = jnp.concatenate([kp, kc_ref[sub * W:(sub + 1) * W, :]], axis=0)
        vv = jnp.concatenate([vp, vc_ref[sub * W:(sub + 1) * W, :]], axis=0)
        for hh in range(SWA_GROUP):
            q = q_ref[sub * W:(sub + 1) * W, hh * HEAD_DIM:(hh + 1) * HEAD_DIM]
            s = _dot_nt(q, kk) + tab_ref[hh]
            if sub == 0:
                s = jnp.where(jnp.logical_and(first_prev, n == 0), -jnp.inf, s)
            sink = sink_ref[0, g * SWA_GROUP + hh]
            m = jnp.maximum(jnp.max(s, axis=1, keepdims=True), sink)
            p = jnp.exp(s - m)
            l = jnp.sum(p, axis=1, keepdims=True) + jnp.exp(sink - m)
            o = _dot(p.astype(BF16), vv) / l
            o_ref[sub * W:(sub + 1) * W, hh * HEAD_DIM:(hh + 1) * HEAD_DIM] = o.astype(o_ref.dtype)


def swa_attention(proj, table, sinks, *, tb):
    T = proj.shape[0]
    W = SWA_WINDOW
    nsub = tb // W
    gw = SWA_GROUP * HEAD_DIM
    qb, kb, vb = C_SQ // gw, C_SK // HEAD_DIM, C_SV // HEAD_DIM
    prev = lambda n: jnp.maximum(n * nsub - 1, 0)
    return pl.pallas_call(
        functools.partial(_swa_kernel, nsub=nsub),
        out_shape=jax.ShapeDtypeStruct((T, SWA_HEADS * HEAD_DIM), BF16),
        grid=(SWA_KV_HEADS, T // tb),
        in_specs=[
            pl.BlockSpec(memory_space=pltpu.SMEM),
            pl.BlockSpec((tb, gw), lambda g, n: (n, qb + g)),
            pl.BlockSpec((tb, HEAD_DIM), lambda g, n: (n, kb + g)),
            pl.BlockSpec((W, HEAD_DIM), lambda g, n: (prev(n), kb + g)),
            pl.BlockSpec((tb, HEAD_DIM), lambda g, n: (n, vb + g)),
            pl.BlockSpec((W, HEAD_DIM), lambda g, n: (prev(n), vb + g)),
            pl.BlockSpec((SWA_GROUP, W, 2 * W), lambda g, n: (g, 0, 0)),
        ],
        out_specs=pl.BlockSpec((tb, gw), lambda g, n: (n, g)),
        compiler_params=_cparams(("arbitrary", "arbitrary")),
        name="swa_attn",
    )(sinks.reshape(1, SWA_HEADS), proj, proj, proj, proj, proj, table)


def _gla_kernel(q_ref, k_ref, v_ref, r_ref, ps_ref, wg_ref, bg_ref, gn_ref, o_ref, st_ref, oc_ref, *, tb):
    C = GLA_CHUNK

    @pl.when(pl.program_id(0) == 0)
    def _():
        st_ref[...] = jnp.zeros_like(st_ref)

    glr = ps_ref[...].astype(BF16)
    g = jax.nn.log_sigmoid(_dot(glr, wg_ref[...]) + bg_ref[...]) / GLA_TAU
    r = lax.broadcasted_iota(jnp.int32, (tb, tb), 0)
    c = lax.broadcasted_iota(jnp.int32, (tb, tb), 1)
    same = (r // C) == (c // C)
    tril = jnp.where(jnp.logical_and(same, c <= r), 1.0, 0.0).astype(BF16)
    whole = jnp.where(same, 1.0, 0.0).astype(BF16)
    gh, gm, gl = _split3(g)
    b = _dot(tril, gh) + _dot(tril, gm) + _dot(tril, gl)
    b_last = _dot(whole, gh) + _dot(whole, gm) + _dot(whole, gl)
    q_t = (q_ref[...].astype(F32) * jnp.exp(b)).astype(BF16)
    kf = k_ref[...].astype(F32)
    k_t = kf * jnp.exp(-b)
    k_end = kf * jnp.exp(b_last - b)
    decay = jnp.exp(b_last)
    lane = lax.broadcasted_iota(jnp.int32, (1, LANES), 1)
    causal = (lax.broadcasted_iota(jnp.int32, (C, C), 1) <= lax.broadcasted_iota(jnp.int32, (C, C), 0))
    for h in range(GLA_HEADS):
        pair = slice((h // 2) * LANES, (h // 2 + 1) * LANES)
        mine = (lane // GLA_DK) == (h % 2)
        ktm = jnp.where(mine, k_t[:, pair], 0.0).astype(BF16)
        kem = jnp.where(mine, k_end[:, pair], 0.0).astype(BF16)
        qh = q_t[:, pair]
        dec = decay[:, pair]
        vh = v_ref[:, h * GLA_DV:(h + 1) * GLA_DV]
        st = st_ref[h]
        for n in range(tb // C):
            rows = slice(n * C, (n + 1) * C)
            a = jnp.where(causal, _dot_nt(qh[rows], ktm[rows]), 0.0)
            o = _dot(a.astype(BF16), vh[rows]) + _dot_nt(qh[rows], st.astype(BF16))
            oc_ref[rows, h * GLA_DV:(h + 1) * GLA_DV] = o
            kv_t = _dot_tn(vh[rows], kem[rows])
            st = dec[n * C:n * C + 1, :] * st + kv_t
        st_ref[h] = st
    for h in range(GLA_HEADS):
        cols = slice(h * GLA_DV, (h + 1) * GLA_DV)
        o = _rms(oc_ref[:, cols], gn_ref[...])
        o_ref[:, cols] = (o * jax.nn.silu(r_ref[:, cols].astype(F32))).astype(o_ref.dtype)


def gla(proj, ps, wg_pad, bg, gnorm, *, tb):
    T = proj.shape[0]
    qw = GLA_HEADS * GLA_DK
    vw = GLA_HEADS * GLA_DV
    return pl.pallas_call(
        functools.partial(_gla_kernel, tb=tb),
        out_shape=jax.ShapeDtypeStruct((T, vw), BF16),
        grid=(T // tb,),
        in_specs=[
            pl.BlockSpec((tb, qw), lambda n: (n, C_GQ // qw)),
            pl.BlockSpec((tb, qw), lambda n: (n, C_GK // qw)),
            pl.BlockSpec((tb, vw), lambda n: (n, C_GV // vw)),
            pl.BlockSpec((tb, vw), lambda n: (n, C_GR // vw)),
            pl.BlockSpec((tb, LANES), lambda n: (n, 0)),
            pl.BlockSpec((LANES, qw), lambda n: (0, 0)),
            pl.BlockSpec((1, qw), lambda n: (0, 0)),
            pl.BlockSpec((1, GLA_DV), lambda n: (0, 0)),
        ],
        out_specs=pl.BlockSpec((tb, vw), lambda n: (n, 0)),
        scratch_shapes=[pltpu.VMEM((GLA_HEADS, GLA_DV, LANES), F32),
                        pltpu.VMEM((tb, vw), F32)],
        compiler_params=_cparams(("arbitrary",)),
        name="gla",
    )(proj, proj, proj, proj, ps, wg_pad, bg, gnorm)


def _mix_xattn_kernel(h_ref, of_ref, os_ref, og_ref, wo_ref, gx_ref, wq_ref, kv_ref, wx_ref, o_ref):
    nf, ns = of_ref.shape[1], os_ref.shape[1]
    mix = (_dot(of_ref[...], wo_ref[0:nf, :]) + _dot(os_ref[...], wo_ref[nf:nf + ns, :])
           + _dot(og_ref[...], wo_ref[nf + ns:, :]))
    h1 = h_ref[...] + mix
    xn = _rms(h1, gx_ref[...]).astype(BF16)
    q = (_dot(xn, wq_ref[...]) * (HEAD_DIM ** -0.5)).astype(BF16)
    xw = XATTN_HEADS * HEAD_DIM
    outs = []
    for hh in range(XATTN_HEADS):
        cols = slice(hh * HEAD_DIM, (hh + 1) * HEAD_DIM)
        k = kv_ref[:, cols]
        v = kv_ref[:, xw + hh * HEAD_DIM:xw + (hh + 1) * HEAD_DIM]
        s = _dot_nt(q[:, cols], k)
        p = jnp.exp(s - jnp.max(s, axis=1, keepdims=True))
        l = jnp.sum(p, axis=1, keepdims=True)
        outs.append((_dot(p.astype(BF16), v) / l).astype(BF16))
    o = jnp.concatenate(outs, axis=1)
    o_ref[...] = h1 + _dot(o, wx_ref[...])


def mix_xattn(h, o_fox, o_swa, o_gla, w_out, layer, gx, wq, kv, wo, *, tm):
    T, D = h.shape
    const = lambda shape: pl.BlockSpec(shape, lambda i: (0, 0))
    stacked = lambda w: pl.BlockSpec((None,) + w.shape[1:], lambda i: (layer, 0, 0))
    row = lambda w: pl.BlockSpec((tm, w), lambda i: (i, 0))
    return pl.pallas_call(
        _mix_xattn_kernel,
        out_shape=jax.ShapeDtypeStruct((T, D), F32),
        grid=(T // tm,),
        in_specs=[row(D), row(o_fox.shape[1]), row(o_swa.shape[1]), row(o_gla.shape[1]),
                  stacked(w_out), const((1, D)), stacked(wq), const(kv.shape), stacked(wo)],
        out_specs=row(D),
        compiler_params=_cparams(("arbitrary",)),
        name="mix_xattn",
    )(h, o_fox, o_swa, o_gla, w_out, gx.reshape(1, D), wq, kv, wo)


def _ffn_kernel(x_ref, g_ref, wg_ref, wv_ref, cwg_ref, cwv_ref, cbg_ref, cbv_ref, wd_ref, fg_ref,
                o_ref, xn_ref, hg_ref, hv_ref, ug_ref, uv_ref, *, tm, rb, sub, final):
    i = pl.program_id(0)
    f = pl.program_id(1)
    H = SUBLANES
    fc = wd_ref.shape[0]
    nrb = tm // rb

    @pl.when(f == 0)
    def _():
        x = x_ref[...]
        xn_ref[...] = _rms(x, g_ref[...]).astype(BF16)
        o_ref[...] = x

    @pl.when(i == 0)
    def _():
        hg_ref[f] = jnp.zeros(hg_ref.shape[1:], F32)
        hv_ref[f] = jnp.zeros(hv_ref.shape[1:], F32)

    def up(r, n):
        c = n * sub
        xn = xn_ref[r * rb:(r + 1) * rb, :]
        ug_ref[r, n, H:, :] = _dot(xn, wg_ref[:, c:c + sub])
        uv_ref[r, n, H:, :] = _dot(xn, wv_ref[:, c:c + sub])

    def conv(u_ref, r, n, h_ref, cw_ref, cb_ref):
        c = n * sub
        u_ref[r, n, 0:H, :] = h_ref[f, :, c:c + sub] if r == 0 else u_ref[r - 1, n, rb:rb + H, :]
        if r == nrb - 1:
            h_ref[f, :, c:c + sub] = u_ref[r, n, rb:rb + H, :]
        return (cb_ref[:, c:c + sub] + cw_ref[2:3, c:c + sub] * u_ref[r, n, H:H + rb, :]
                + cw_ref[1:2, c:c + sub] * u_ref[r, n, H - 1:H - 1 + rb, :]
                + cw_ref[0:1, c:c + sub] * u_ref[r, n, H - 2:H - 2 + rb, :])

    def down(r, n):
        gate = conv(ug_ref, r, n, hg_ref, cwg_ref, cbg_ref)
        val = conv(uv_ref, r, n, hv_ref, cwv_ref, cbv_ref)
        act = (jax.nn.silu(gate) * val).astype(BF16)
        o_ref[r * rb:(r + 1) * rb, :] += _dot(act, wd_ref[n * sub:(n + 1) * sub, :])

    units = [(r, n) for n in range(fc // sub) for r in range(nrb)]
    up(*units[0])
    for k, unit in enumerate(units):
        if k + 1 < len(units):
            up(*units[k + 1])
        down(*unit)

    if final:
        @pl.when(f == pl.num_programs(1) - 1)
        def _():
            o_ref[...] = _rms(o_ref[...], fg_ref[...])


def conv_ffn(x, g, w_up, conv_w, conv_b, w_down, layer, final_g, *, tm, rb, fc, final):
    T, D = x.shape
    L, dff, _ = w_down.shape
    nf = dff // fc
    cb = conv_b.reshape(L, 1, 2 * dff)
    return pl.pallas_call(
        functools.partial(_ffn_kernel, tm=tm, rb=rb, sub=MXU_COLS, final=final),
        out_shape=jax.ShapeDtypeStruct((T, D), F32),
        grid=(T // tm, nf),
        in_specs=[
            pl.BlockSpec((tm, D), lambda i, f: (i, 0)),
            pl.BlockSpec((1, D), lambda i, f: (0, 0)),
            pl.BlockSpec((None, D, fc), lambda i, f: (layer, 0, f)),
            pl.BlockSpec((None, D, fc), lambda i, f: (layer, 0, nf + f)),
            pl.BlockSpec((None, CONV_WIDTH, fc), lambda i, f: (layer, 0, f)),
            pl.BlockSpec((None, CONV_WIDTH, fc), lambda i, f: (layer, 0, nf + f)),
            pl.BlockSpec((None, 1, fc), lambda i, f: (layer, 0, f)),
            pl.BlockSpec((None, 1, fc), lambda i, f: (layer, 0, nf + f)),
            pl.BlockSpec((None, fc, D), lambda i, f: (layer, f, 0)),
            pl.BlockSpec((1, D), lambda i, f: (0, 0)),
        ],
        out_specs=pl.BlockSpec((tm, D), lambda i, f: (i, 0)),
        scratch_shapes=[pltpu.VMEM((tm, D), BF16), pltpu.VMEM((nf, SUBLANES, fc), F32),
                        pltpu.VMEM((nf, SUBLANES, fc), F32),
                        pltpu.VMEM((tm // rb, fc // MXU_COLS, rb + SUBLANES, MXU_COLS), F32),
                        pltpu.VMEM((tm // rb, fc // MXU_COLS, rb + SUBLANES, MXU_COLS), F32)],
        compiler_params=_cparams(("arbitrary", "arbitrary")),
        name="conv_ffn",
    )(x, g.reshape(1, D), w_up, w_up, conv_w, conv_w, cb, cb, w_down, final_g.reshape(1, D))


TR_PREP, TC_PREP = 512, 512
TM_PROJ, TN_PROJ = 1024, 1536
TB_GATE = 512
TQ_FOX, TK_FOX, NH_FOX = 512, 512, 2
TB_SWA = 512
TB_GLA = 512
TM_MIX = 512
TM_FFN, RB_FFN, FC_FFN = 512, 512, 512


def _main_colscale():
    cs = np.ones((1, MAIN_COLS), np.float32)
    cs[0, C_FQ:C_FK] = HEAD_DIM ** -0.5 * LOG2E
    cs[0, C_SQ:C_SK] = HEAD_DIM ** -0.5
    cs[0, C_GQ:C_GK] = GLA_DK ** -0.5
    return jnp.asarray(cs)


def kernel(x, mem, w_in, b_fox_f, swa_sinks, t5_bias, w_gla_gate, b_gla_gate, gla_norm, w_mix_out, norm_mix,
           norm_xattn, norm_mem, wq_x, wkv_x, wo_x, norm_ffn, w_up, conv_w, conv_b, w_down, final_norm):
    depth = w_in.shape[0]
    _, T, D = x.shape
    M = mem.shape[1]
    h = x.reshape(T, D)
    memf = mem.reshape(M, D)
    colscale = _main_colscale()
    ones_kv = jnp.ones((1, wkv_x.shape[2]), F32)
    table = t5_table(t5_bias)
    pad_lanes = LANES - S_GLR - GLA_GATE_RANK
    w_main, w_small = prep_w_in(w_in, tr=TR_PREP, tc=TC_PREP)
    w_out_b, wq_b, wkv_b, wo_b = (w.astype(BF16) for w in (w_mix_out, wq_x, wkv_x, wo_x))
    w_up_b, w_down_b = w_up.astype(BF16), w_down.astype(BF16)
    for l in range(depth):
        bvec = jnp.concatenate([b_fox_f[l], jnp.zeros((LANES - FOX_HEADS,), F32)]).reshape(1, LANES)
        wg_pad = jnp.concatenate([jnp.zeros((S_GLR, GLA_HEADS * GLA_DK), F32), w_gla_gate[l],
                                  jnp.zeros((pad_lanes, GLA_HEADS * GLA_DK), F32)], axis=0).astype(BF16)

        proj, ps = norm_proj(h, norm_mix[l], w_main, l, colscale, w_small, tm=TM_PROJ, tn=TN_PROJ, w_t=True)
        kx, row = fox_gate(ps, bvec, tb=TB_GATE)
        o_fox = fox_attention(proj, kx, row, tq=TQ_FOX, tk=TK_FOX, nh=NH_FOX)
        o_swa = swa_attention(proj, table, swa_sinks[l], tb=TB_SWA)
        o_gla = gla(proj, ps, wg_pad, b_gla_gate[l].reshape(1, -1), gla_norm[l].reshape(1, -1), tb=TB_GLA)
        kv = norm_proj(memf, norm_mem[l], wkv_b, l, ones_kv, tm=M, tn=wkv_x.shape[2])
        h = mix_xattn(h, o_fox, o_swa, o_gla, w_out_b, l, norm_xattn[l], wq_b, kv, wo_b, tm=TM_MIX)
        h = conv_ffn(h, norm_ffn[l], w_up_b, conv_w, conv_b, w_down_b, l, final_norm,
                     tm=TM_FFN, rb=RB_FFN, fc=FC_FFN, final=(l == depth - 1))
    return h.reshape(x.shape)
```

```python
import functools
import math

import numpy as np
import jax
import jax.numpy as jnp
from jax import lax
from jax.experimental import pallas as pl
from jax.experimental.pallas import tpu as pltpu

F32 = jnp.float32
BF16 = jnp.bfloat16

HEAD_DIM = 128
FOX_HEADS = 4
SWA_HEADS = 8
SWA_KV_HEADS = 2
SWA_GROUP = SWA_HEADS // SWA_KV_HEADS
GLA_HEADS = 4
GLA_DK = 64
GLA_DV = 128
GLA_GATE_RANK = 16
GLA_TAU = 16.0
GLA_CHUNK = 64
SWA_WINDOW = 128
NUM_BUCKETS = 32
T5_MAX_DISTANCE = 128
XATTN_HEADS = 4
CONV_WIDTH = 3
EPS = 1e-6
LOG2E = math.log2(math.e)

LANES = 128
SUBLANES = 8
BF16_ROWS = 16
MXU_COLS = 256
VMEM_LIMIT = 56 * 1024 * 1024

_FOX_W = FOX_HEADS * HEAD_DIM
_OFF_FF = 3 * _FOX_W
_OFF_SQ = _OFF_FF + FOX_HEADS
_MAIN_B = (SWA_HEADS + 2 * SWA_KV_HEADS) * HEAD_DIM + 2 * GLA_HEADS * GLA_DK + 2 * GLA_HEADS * GLA_DV
_OFF_GLR = _OFF_SQ + _MAIN_B
C_FQ, C_FK, C_FV = 0, 512, 1024
C_SQ, C_SK, C_SV = 1536, 2560, 2816
C_GQ, C_GK, C_GV, C_GR = 3072, 3328, 3584, 4096
MAIN_COLS = 4608
S_FF, S_GLR = 0, SUBLANES


def _cparams(sem, flags=None):
    return pltpu.CompilerParams(dimension_semantics=sem, vmem_limit_bytes=VMEM_LIMIT, flags=flags)


def _rms(x, g):
    return x * lax.rsqrt(jnp.mean(x * x, axis=-1, keepdims=True) + EPS) * g


N_SPLIT = 3


def _split3(x):
    hi = x.astype(BF16)
    r1 = x - hi.astype(F32)
    mid = r1.astype(BF16)
    lo = (r1 - mid.astype(F32)).astype(BF16)
    return hi, mid, lo


def _dot(a, b):
    return jnp.dot(a, b, preferred_element_type=F32)


def _dot_nt(a, b):
    return lax.dot_general(a, b, (((1,), (1,)), ((), ())), preferred_element_type=F32)


def _dot_tn(a, b):
    return lax.dot_general(a, b, (((0,), (0,)), ((), ())), preferred_element_type=F32)


def _sel_dot(sel, x):
    hi, mid, lo = _split3(x)
    return _dot(sel, hi) + _dot(sel, mid) + _dot(sel, lo)


def _prep_w_in_kernel(cur_ref, ff_ref, glr_ref, wm_ref, ws_ref):
    L = wm_ref.shape[0]
    for l in range(L):
        wm_ref[l] = cur_ref[:, l, :].astype(BF16)

    @pl.when(pl.program_id(1) == 0)
    def _():
        pad = jnp.zeros((LANES - S_GLR - GLA_GATE_RANK, ws_ref.shape[2]), F32)
        for l in range(L):
            ws_ref[l] = jnp.concatenate([ff_ref[:, l, :], glr_ref[:, l, :], pad], axis=0).astype(BF16)


def prep_w_in(w_in, *, tr, tc):
    L, D, _ = w_in.shape
    wt = jnp.transpose(w_in, (2, 0, 1))
    assert _OFF_FF % tc == 0 and MAIN_COLS % tc == 0 and S_GLR % SUBLANES == 0

    def src(j):
        return j * tc + jnp.where(j * tc >= _OFF_FF, _OFF_SQ - _OFF_FF, 0)

    return pl.pallas_call(
        _prep_w_in_kernel,
        out_shape=[jax.ShapeDtypeStruct((L, MAIN_COLS, D), BF16),
                   jax.ShapeDtypeStruct((L, LANES, D), BF16)],
        grid=(D // tr, MAIN_COLS // tc),
        in_specs=[
            pl.BlockSpec((pl.Element(tc), pl.Element(L), pl.Element(tr)), lambda r, j: (src(j), 0, r * tr)),
            pl.BlockSpec((pl.Element(S_GLR), pl.Element(L), pl.Element(tr)), lambda r, j: (_OFF_FF, 0, r * tr)),
            pl.BlockSpec((pl.Element(GLA_GATE_RANK), pl.Element(L), pl.Element(tr)),
                         lambda r, j: (_OFF_GLR, 0, r * tr)),
        ],
        out_specs=[pl.BlockSpec((L, tc, tr), lambda r, j: (0, j, r)),
                   pl.BlockSpec((L, LANES, tr), lambda r, j: (0, 0, r))],
        compiler_params=_cparams(("arbitrary", "arbitrary")),
        name="prep_w_in",
    )(wt, wt, wt)


def _norm_proj_kernel(*refs, has_small, w_t):
    if has_small:
        x_ref, g_ref, w_ref, cs_ref, ws_ref, o_ref, os_ref, xn_ref = refs
    else:
        x_ref, g_ref, w_ref, cs_ref, o_ref, xn_ref = refs
    dot = _dot_nt if w_t else _dot

    @pl.when(pl.program_id(1) == 0)
    def _():
        xn = _rms(x_ref[...], g_ref[...]).astype(BF16)
        xn_ref[...] = xn
        if has_small:
            os_ref[...] = dot(xn, ws_ref[...])

    o_ref[...] = (dot(xn_ref[...], w_ref[...]) * cs_ref[...]).astype(o_ref.dtype)


def norm_proj(x, g, w, layer, colscale, w_small=None, *, tm, tn, w_t=False):
    T, D = x.shape
    N = w.shape[1] if w_t else w.shape[2]
    has_small = w_small is not None

    def wspec(n):
        if w_t:
            return pl.BlockSpec((None, n, D), lambda i, j: (layer, j, 0))
        return pl.BlockSpec((None, D, n), lambda i, j: (layer, 0, j))

    in_specs = [
        pl.BlockSpec((tm, D), lambda i, j: (i, 0)),
        pl.BlockSpec((1, D), lambda i, j: (0, 0)),
        wspec(tn),
        pl.BlockSpec((1, tn), lambda i, j: (0, j)),
    ]
    args = [x, g.reshape(1, D), w, colscale]
    out_shape = [jax.ShapeDtypeStruct((T, N), BF16)]
    out_specs = [pl.BlockSpec((tm, tn), lambda i, j: (i, j))]
    if has_small:
        in_specs.append(pl.BlockSpec((None,) + w_small.shape[1:], lambda i, j: (layer, 0, 0)))
        args.append(w_small)
        out_shape.append(jax.ShapeDtypeStruct((T, LANES), F32))
        out_specs.append(pl.BlockSpec((tm, LANES), lambda i, j: (i, 0)))
    outs = pl.pallas_call(
        functools.partial(_norm_proj_kernel, has_small=has_small, w_t=w_t),
        out_shape=out_shape,
        grid=(T // tm, N // tn),
        in_specs=in_specs,
        out_specs=out_specs,
        scratch_shapes=[pltpu.VMEM((tm, D), BF16)],
        compiler_params=_cparams(("arbitrary", "arbitrary")),
        name="norm_proj_small" if has_small else "norm_proj",
    )(*args)
    return outs if has_small else outs[0]


def _fox_gate_kernel(ps_ref, b_ref, kx_ref, row_ref, carry_ref, *, tb):
    @pl.when(pl.program_id(0) == 0)
    def _():
        carry_ref[...] = jnp.zeros_like(carry_ref)

    x = jax.nn.log_sigmoid(ps_ref[...] + b_ref[...])
    r = lax.broadcasted_iota(jnp.int32, (tb, tb), 0)
    c = lax.broadcasted_iota(jnp.int32, (tb, tb), 1)
    tril = jnp.where(c <= r, 1.0, 0.0).astype(BF16)
    csum = _sel_dot(tril, x) + carry_ref[0:1, :]
    carry_ref[...] = jnp.broadcast_to(csum[tb - 1:tb, :], carry_ref.shape)
    pieces = _split3(csum * LOG2E)
    neg = [-p for p in pieces]
    src = lax.broadcasted_iota(jnp.int32, (LANES, FOX_HEADS * LANES), 0)
    dst = lax.broadcasted_iota(jnp.int32, (LANES, FOX_HEADS * LANES), 1)
    kx = sum(_dot(neg[n], jnp.where((dst // LANES == src - S_FF) & (dst % LANES == n), 1.0, 0.0).astype(BF16))
             for n in range(len(neg))).astype(BF16)
    sub = lax.broadcasted_iota(jnp.int32, (FOX_HEADS * SUBLANES, LANES), 0)
    lane = lax.broadcasted_iota(jnp.int32, (FOX_HEADS * SUBLANES, LANES), 1)
    pick = jnp.where(lane - S_FF == sub // SUBLANES, 1.0, 0.0).astype(BF16)
    rows = sum(_dot_nt(pick, p) for p in pieces)
    for h in range(FOX_HEADS):
        kx_ref[h] = kx[:, h * LANES:(h + 1) * LANES]
        row_ref[h, 0] = rows[h * SUBLANES:(h + 1) * SUBLANES, :]


def fox_gate(ps, bvec, *, tb):
    T = ps.shape[0]
    nb = T // tb
    return pl.pallas_call(
        functools.partial(_fox_gate_kernel, tb=tb),
        out_shape=[jax.ShapeDtypeStruct((FOX_HEADS, T, LANES), BF16),
                   jax.ShapeDtypeStruct((FOX_HEADS, nb, SUBLANES, tb), F32)],
        grid=(nb,),
        in_specs=[pl.BlockSpec((tb, LANES), lambda i: (i, 0)),
                  pl.BlockSpec((1, LANES), lambda i: (0, 0))],
        out_specs=[pl.BlockSpec((FOX_HEADS, tb, LANES), lambda i: (0, i, 0)),
                   pl.BlockSpec((FOX_HEADS, 1, SUBLANES, tb), lambda i: (0, i, 0, 0))],
        scratch_shapes=[pltpu.VMEM((SUBLANES, LANES), F32)],
        compiler_params=_cparams(("arbitrary",)),
        name="fox_gate",
    )(ps, bvec)


def _fox_attn_kernel(q_ref, k_ref, v_ref, kx_ref, cq_ref, o_ref, vt_ref, ya_ref, yb_ref, m_ref, l_ref, acc_ref,
                     *, tq, tk, nk, nh):
    i = pl.program_id(1)
    D = HEAD_DIM
    heads = range(nh)
    n_full = i // (tk // tq)

    @pl.when(i == 0)
    def _():
        for hh in heads:
            for n in range(nk):
                vt_ref[hh, n] = v_ref[n * tk:(n + 1) * tk, hh * D:(hh + 1) * D].T

    lane = lax.broadcasted_iota(jnp.int32, (tq, LANES), 1)
    ones = jnp.where(lane < N_SPLIT, 1.0, 0.0).astype(BF16)
    q_aug = [jnp.concatenate([q_ref[:, hh * D:(hh + 1) * D], ones], axis=1) for hh in heads]
    cq = [cq_ref[hh, 0, 0:1, :] for hh in heads]
    m_ref[...] = jnp.full(m_ref.shape, -jnp.inf, F32)
    l_ref[...] = jnp.zeros(l_ref.shape, F32)
    acc_ref[...] = jnp.zeros(acc_ref.shape, F32)

    def scores(hh, j):
        start = pl.multiple_of(j * tk, tk)
        k_aug = jnp.concatenate([k_ref[pl.ds(start, tk), hh * D:(hh + 1) * D], kx_ref[hh, pl.ds(start, tk), :]],
                                axis=1)
        return _dot_nt(k_aug, q_aug[hh])

    def update(hh, j, y):
        m_old = m_ref[hh]
        m_new = jnp.maximum(m_old, jnp.max(y, axis=0, keepdims=True) + cq[hh])
        alpha = jnp.exp2(m_old - m_new)
        p = jnp.exp2(y + (cq[hh] - m_new))
        l_ref[hh] = alpha * l_ref[hh] + jnp.sum(p, axis=0, keepdims=True)
        acc_ref[hh] = alpha * acc_ref[hh] + _dot(vt_ref[hh, j], p.astype(BF16))
        m_ref[hh] = m_new

    for hh in heads:
        ya_ref[hh] = scores(hh, 0)

    def step(j, cur_ref, nxt_ref):
        for hh in heads:
            nxt_ref[hh] = scores(hh, j + 1)
        for hh in heads:
            update(hh, j, cur_ref[hh])

    def body(j, carry):
        pl.when(j % 2 == 0)(lambda: step(j, ya_ref, yb_ref))
        pl.when(j % 2 == 1)(lambda: step(j, yb_ref, ya_ref))
        return carry

    lax.fori_loop(0, n_full, body, 0)
    key = n_full * tk + lax.broadcasted_iota(jnp.int32, (tk, tq), 0)
    qry = i * tq + lax.broadcasted_iota(jnp.int32, (tk, tq), 1)

    def diagonal(cur_ref):
        for hh in heads:
            update(hh, n_full, jnp.where(key <= qry, cur_ref[hh], -jnp.inf))

    pl.when(n_full % 2 == 0)(lambda: diagonal(ya_ref))
    pl.when(n_full % 2 == 1)(lambda: diagonal(yb_ref))
    for hh in heads:
        o_ref[:, hh * D:(hh + 1) * D] = (acc_ref[hh] / l_ref[hh]).T.astype(o_ref.dtype)


def fox_attention(proj, kx, row, *, tq, tk, nh):
    T = proj.shape[0]
    nq = T // tq
    gw = nh * HEAD_DIM
    qb, kb, vb = C_FQ // gw, C_FK // gw, C_FV // gw
    assert tk % tq == 0 and T % tk == 0
    return pl.pallas_call(
        functools.partial(_fox_attn_kernel, tq=tq, tk=tk, nk=T // tk, nh=nh),
        out_shape=jax.ShapeDtypeStruct((T, FOX_HEADS * HEAD_DIM), BF16),
        grid=(FOX_HEADS // nh, nq),
        in_specs=[
            pl.BlockSpec((tq, gw), lambda g, i: (i, qb + g)),
            pl.BlockSpec((T, gw), lambda g, i: (0, kb + g), pipeline_mode=pl.Buffered(1)),
            pl.BlockSpec((T, gw), lambda g, i: (0, vb + g), pipeline_mode=pl.Buffered(1)),
            pl.BlockSpec((nh, T, LANES), lambda g, i: (g, 0, 0), pipeline_mode=pl.Buffered(1)),
            pl.BlockSpec((nh, 1, SUBLANES, tq), lambda g, i: (g, i, 0, 0)),
        ],
        out_specs=pl.BlockSpec((tq, gw), lambda g, i: (i, g)),
        scratch_shapes=[pltpu.VMEM((nh, T // tk, HEAD_DIM, tk), BF16),
                        pltpu.VMEM((nh, tk, tq), F32), pltpu.VMEM((nh, tk, tq), F32),
                        pltpu.VMEM((nh, 1, tq), F32), pltpu.VMEM((nh, 1, tq), F32),
                        pltpu.VMEM((nh, HEAD_DIM, tq), F32)],
        compiler_params=_cparams(("arbitrary", "arbitrary")),
        name="fox_attn",
    )(proj, proj, proj, kx, row)


def _t5_table_kernel(t5_ref, o_ref):
    h = pl.program_id(0)
    W = SWA_WINDOW
    i = lax.broadcasted_iota(jnp.int32, (W, 2 * W), 0)
    j = lax.broadcasted_iota(jnp.int32, (W, 2 * W), 1)
    rel = (W + i) - j
    n = jnp.maximum(rel, 0)
    max_exact = NUM_BUCKETS // 2
    nf = jnp.maximum(n, 1).astype(F32)
    large = max_exact + (jnp.log(nf / max_exact) / math.log(T5_MAX_DISTANCE / max_exact)
                         * (NUM_BUCKETS - max_exact)).astype(jnp.int32)
    large = jnp.minimum(large, NUM_BUCKETS - 1)
    bucket = jnp.where(n < max_exact, n, large)
    bias = jnp.zeros((W, 2 * W), F32)
    for b in range(NUM_BUCKETS):
        bias = jnp.where(bucket == b, t5_ref[b, h], bias)
    in_window = (rel >= 0) & (rel < W)
    o_ref[...] = jnp.where(in_window, bias, -jnp.inf)


def t5_table(t5_bias):
    W = SWA_WINDOW
    return pl.pallas_call(
        _t5_table_kernel,
        out_shape=jax.ShapeDtypeStruct((SWA_HEADS, W, 2 * W), F32),
        grid=(SWA_HEADS,),
        in_specs=[pl.BlockSpec(memory_space=pltpu.SMEM)],
        out_specs=pl.BlockSpec((None, W, 2 * W), lambda h: (h, 0, 0)),
        compiler_params=_cparams(("arbitrary",)),
        name="t5_table",
    )(t5_bias)


def _swa_kernel(sink_ref, q_ref, kc_ref, kp_ref, vc_ref, vp_ref, tab_ref, o_ref, *, nsub):
    g = pl.program_id(0)
    n = pl.program_id(1)
    W = SWA_WINDOW
    first_prev = lax.broadcasted_iota(jnp.int32, (W, 2 * W), 1) < W
    for sub in range(nsub):
        if sub == 0:
            kp, vp = kp_ref[...], vp_ref[...]
        else:
            kp, vp = kc_ref[(sub - 1) * W:sub * W, :], vc_ref[(sub - 1) * W:sub * W, :]
        kk = jnp.concatenate([kp, kc_ref[sub * W:(sub + 1) * W, :]], axis=0)
        vv = jnp.concatenate([vp, vc_ref[sub * W:(sub + 1) * W, :]], axis=0)
        for hh in range(SWA_GROUP):
            q = q_ref[sub * W:(sub + 1) * W, hh * HEAD_DIM:(hh + 1) * HEAD_DIM]
            s = _dot_nt(q, kk) + tab_ref[hh]
            if sub == 0:
                s = jnp.where(jnp.logical_and(first_prev, n == 0), -jnp.inf, s)
            sink = sink_ref[0, g * SWA_GROUP + hh]
            m = jnp.maximum(jnp.max(s, axis=1, keepdims=True), sink)
            p = jnp.exp(s - m)
            l = jnp.sum(p, axis=1, keepdims=True) + jnp.exp(sink - m)
            o = _dot(p.astype(BF16), vv) / l
            o_ref[sub * W:(sub + 1) * W, hh * HEAD_DIM:(hh + 1) * HEAD_DIM] = o.astype(o_ref.dtype)


def swa_attention(proj, table, sinks, *, tb):
    T = proj.shape[0]
    W = SWA_WINDOW
    nsub = tb // W
    gw = SWA_GROUP * HEAD_DIM
    qb, kb, vb = C_SQ // gw, C_SK // HEAD_DIM, C_SV // HEAD_DIM
    prev = lambda n: jnp.maximum(n * nsub - 1, 0)
    return pl.pallas_call(
        functools.partial(_swa_kernel, nsub=nsub),
        out_shape=jax.ShapeDtypeStruct((T, SWA_HEADS * HEAD_DIM), BF16),
        grid=(SWA_KV_HEADS, T // tb),
        in_specs=[
            pl.BlockSpec(memory_space=pltpu.SMEM),
            pl.BlockSpec((tb, gw), lambda g, n: (n, qb + g)),
            pl.BlockSpec((tb, HEAD_DIM), lambda g, n: (n, kb + g)),
            pl.BlockSpec((W, HEAD_DIM), lambda g, n: (prev(n), kb + g)),
            pl.BlockSpec((tb, HEAD_DIM), lambda g, n: (n, vb + g)),
            pl.BlockSpec((W, HEAD_DIM), lambda g, n: (prev(n), vb + g)),
            pl.BlockSpec((SWA_GROUP, W, 2 * W), lambda g, n: (g, 0, 0)),
        ],
        out_specs=pl.BlockSpec((tb, gw), lambda g, n: (n, g)),
        compiler_params=_cparams(("arbitrary", "arbitrary")),
        name="swa_attn",
    )(sinks.reshape(1, SWA_HEADS), proj, proj, proj, proj, proj, table)


def _gla_kernel(q_ref, k_ref, v_ref, r_ref, ps_ref, wg_ref, bg_ref, gn_ref, o_ref, st_ref, oc_ref, *, tb):
    C = GLA_CHUNK

    @pl.when(pl.program_id(0) == 0)
    def _():
        st_ref[...] = jnp.zeros_like(st_ref)

    glr = ps_ref[...].astype(BF16)
    g = jax.nn.log_sigmoid(_dot(glr, wg_ref[...]) + bg_ref[...]) / GLA_TAU
    r = lax.broadcasted_iota(jnp.int32, (tb, tb), 0)
    c = lax.broadcasted_iota(jnp.int32, (tb, tb), 1)
    same = (r // C) == (c // C)
    tril = jnp.where(jnp.logical_and(same, c <= r), 1.0, 0.0).astype(BF16)
    whole = jnp.where(same, 1.0, 0.0).astype(BF16)
    gh, gm, gl = _split3(g)
    b = _dot(tril, gh) + _dot(tril, gm) + _dot(tril, gl)
    b_last = _dot(whole, gh) + _dot(whole, gm) + _dot(whole, gl)
    q_t = (q_ref[...].astype(F32) * jnp.exp(b)).astype(BF16)
    kf = k_ref[...].astype(F32)
    k_t = kf * jnp.exp(-b)
    k_end = kf * jnp.exp(b_last - b)
    decay = jnp.exp(b_last)
    lane = lax.broadcasted_iota(jnp.int32, (1, LANES), 1)
    causal = (lax.broadcasted_iota(jnp.int32, (C, C), 1) <= lax.broadcasted_iota(jnp.int32, (C, C), 0))
    for h in range(GLA_HEADS):
        pair = slice((h // 2) * LANES, (h // 2 + 1) * LANES)
        mine = (lane // GLA_DK) == (h % 2)
        ktm = jnp.where(mine, k_t[:, pair], 0.0).astype(BF16)
        kem = jnp.where(mine, k_end[:, pair], 0.0).astype(BF16)
        qh = q_t[:, pair]
        dec = decay[:, pair]
        vh = v_ref[:, h * GLA_DV:(h + 1) * GLA_DV]
        st = st_ref[h]
        for n in range(tb // C):
            rows = slice(n * C, (n + 1) * C)
            a = jnp.where(causal, _dot_nt(qh[rows], ktm[rows]), 0.0)
            o = _dot(a.astype(BF16), vh[rows]) + _dot_nt(qh[rows], st.astype(BF16))
            oc_ref[rows, h * GLA_DV:(h + 1) * GLA_DV] = o
            kv_t = _dot_tn(vh[rows], kem[rows])
            st = dec[n * C:n * C + 1, :] * st + kv_t
        st_ref[h] = st
    for h in range(GLA_HEADS):
        cols = slice(h * GLA_DV, (h + 1) * GLA_DV)
        o = _rms(oc_ref[:, cols], gn_ref[...])
        o_ref[:, cols] = (o * jax.nn.silu(r_ref[:, cols].astype(F32))).astype(o_ref.dtype)


def gla(proj, ps, wg_pad, bg, gnorm, *, tb):
    T = proj.shape[0]
    qw = GLA_HEADS * GLA_DK
    vw = GLA_HEADS * GLA_DV
    return pl.pallas_call(
        functools.partial(_gla_kernel, tb=tb),
        out_shape=jax.ShapeDtypeStruct((T, vw), BF16),
        grid=(T // tb,),
        in_specs=[
            pl.BlockSpec((tb, qw), lambda n: (n, C_GQ // qw)),
            pl.BlockSpec((tb, qw), lambda n: (n, C_GK // qw)),
            pl.BlockSpec((tb, vw), lambda n: (n, C_GV // vw)),
            pl.BlockSpec((tb, vw), lambda n: (n, C_GR // vw)),
            pl.BlockSpec((tb, LANES), lambda n: (n, 0)),
            pl.BlockSpec((LANES, qw), lambda n: (0, 0)),
            pl.BlockSpec((1, qw), lambda n: (0, 0)),
            pl.BlockSpec((1, GLA_DV), lambda n: (0, 0)),
        ],
        out_specs=pl.BlockSpec((tb, vw), lambda n: (n, 0)),
        scratch_shapes=[pltpu.VMEM((GLA_HEADS, GLA_DV, LANES), F32),
                        pltpu.VMEM((tb, vw), F32)],
        compiler_params=_cparams(("arbitrary",)),
        name="gla",
    )(proj, proj, proj, proj, ps, wg_pad, bg, gnorm)


def _mix_xattn_kernel(h_ref, of_ref, os_ref, og_ref, wo_ref, gx_ref, wq_ref, kv_ref, wx_ref, o_ref):
    nf, ns = of_ref.shape[1], os_ref.shape[1]
    mix = (_dot(of_ref[...], wo_ref[0:nf, :]) + _dot(os_ref[...], wo_ref[nf:nf + ns, :])
           + _dot(og_ref[...], wo_ref[nf + ns:, :]))
    h1 = h_ref[...] + mix
    xn = _rms(h1, gx_ref[...]).astype(BF16)
    q = (_dot(xn, wq_ref[...]) * (HEAD_DIM ** -0.5)).astype(BF16)
    xw = XATTN_HEADS * HEAD_DIM
    outs = []
    for hh in range(XATTN_HEADS):
        cols = slice(hh * HEAD_DIM, (hh + 1) * HEAD_DIM)
        k = kv_ref[:, cols]
        v = kv_ref[:, xw + hh * HEAD_DIM:xw + (hh + 1) * HEAD_DIM]
        s = _dot_nt(q[:, cols], k)
        p = jnp.exp(s - jnp.max(s, axis=1, keepdims=True))
        l = jnp.sum(p, axis=1, keepdims=True)
        outs.append((_dot(p.astype(BF16), v) / l).astype(BF16))
    o = jnp.concatenate(outs, axis=1)
    o_ref[...] = h1 + _dot(o, wx_ref[...])


def mix_xattn(h, o_fox, o_swa, o_gla, w_out, layer, gx, wq, kv, wo, *, tm):
    T, D = h.shape
    const = lambda shape: pl.BlockSpec(shape, lambda i: (0, 0))
    stacked = lambda w: pl.BlockSpec((None,) + w.shape[1:], lambda i: (layer, 0, 0))
    row = lambda w: pl.BlockSpec((tm, w), lambda i: (i, 0))
    return pl.pallas_call(
        _mix_xattn_kernel,
        out_shape=jax.ShapeDtypeStruct((T, D), F32),
        grid=(T // tm,),
        in_specs=[row(D), row(o_fox.shape[1]), row(o_swa.shape[1]), row(o_gla.shape[1]),
                  stacked(w_out), const((1, D)), stacked(wq), const(kv.shape), stacked(wo)],
        out_specs=row(D),
        compiler_params=_cparams(("arbitrary",)),
        name="mix_xattn",
    )(h, o_fox, o_swa, o_gla, w_out, gx.reshape(1, D), wq, kv, wo)


def _ffn_kernel(x_ref, g_ref, wg_ref, wv_ref, cwg_ref, cwv_ref, cbg_ref, cbv_ref, wd_ref, fg_ref,
                o_ref, xn_ref, hg_ref, hv_ref, ug_ref, uv_ref, *, tm, sub, final):
    i = pl.program_id(0)
    f = pl.program_id(1)
    H = SUBLANES
    fc = wd_ref.shape[0]

    @pl.when(f == 0)
    def _():
        x = x_ref[...]
        xn_ref[...] = _rms(x, g_ref[...]).astype(BF16)
        o_ref[...] = x

    @pl.when(i == 0)
    def _():
        hg_ref[f] = jnp.zeros(hg_ref.shape[1:], F32)
        hv_ref[f] = jnp.zeros(hv_ref.shape[1:], F32)

    xn = xn_ref[...]

    def up(c, slot):
        ug_ref[slot, H:, :] = _dot(xn, wg_ref[:, c:c + sub])
        uv_ref[slot, H:, :] = _dot(xn, wv_ref[:, c:c + sub])

    def conv(u_ref, slot, h_ref, cw_ref, cb_ref, c):
        u_ref[slot, 0:H, :] = h_ref[f, :, c:c + sub]
        h_ref[f, :, c:c + sub] = u_ref[slot, tm:tm + H, :]
        return (cb_ref[:, c:c + sub] + cw_ref[2:3, c:c + sub] * u_ref[slot, H:H + tm, :]
                + cw_ref[1:2, c:c + sub] * u_ref[slot, H - 1:H - 1 + tm, :]
                + cw_ref[0:1, c:c + sub] * u_ref[slot, H - 2:H - 2 + tm, :])

    def down(c, slot):
        gate = conv(ug_ref, slot, hg_ref, cwg_ref, cbg_ref, c)
        val = conv(uv_ref, slot, hv_ref, cwv_ref, cbv_ref, c)
        act = (jax.nn.silu(gate) * val).astype(BF16)
        o_ref[...] += _dot(act, wd_ref[c:c + sub, :])

    up(0, 0)
    for n, c in enumerate(range(0, fc, sub)):
        if c + sub < fc:
            up(c + sub, (n + 1) % 2)
        down(c, n % 2)

    if final:
        @pl.when(f == pl.num_programs(1) - 1)
        def _():
            o_ref[...] = _rms(o_ref[...], fg_ref[...])


def conv_ffn(x, g, w_up, conv_w, conv_b, w_down, layer, final_g, *, tm, fc, final):
    T, D = x.shape
    L, dff, _ = w_down.shape
    nf = dff // fc
    cb = conv_b.reshape(L, 1, 2 * dff)
    return pl.pallas_call(
        functools.partial(_ffn_kernel, tm=tm, sub=MXU_COLS, final=final),
        out_shape=jax.ShapeDtypeStruct((T, D), F32),
        grid=(T // tm, nf),
        in_specs=[
            pl.BlockSpec((tm, D), lambda i, f: (i, 0)),
            pl.BlockSpec((1, D), lambda i, f: (0, 0)),
            pl.BlockSpec((None, D, fc), lambda i, f: (layer, 0, f)),
            pl.BlockSpec((None, D, fc), lambda i, f: (layer, 0, nf + f)),
            pl.BlockSpec((None, CONV_WIDTH, fc), lambda i, f: (layer, 0, f)),
            pl.BlockSpec((None, CONV_WIDTH, fc), lambda i, f: (layer, 0, nf + f)),
            pl.BlockSpec((None, 1, fc), lambda i, f: (layer, 0, f)),
            pl.BlockSpec((None, 1, fc), lambda i, f: (layer, 0, nf + f)),
            pl.BlockSpec((None, fc, D), lambda i, f: (layer, f, 0)),
            pl.BlockSpec((1, D), lambda i, f: (0, 0)),
        ],
        out_specs=pl.BlockSpec((tm, D), lambda i, f: (i, 0)),
        scratch_shapes=[pltpu.VMEM((tm, D), BF16), pltpu.VMEM((nf, SUBLANES, fc), F32),
                        pltpu.VMEM((nf, SUBLANES, fc), F32),
                        pltpu.VMEM((2, tm + SUBLANES, MXU_COLS), F32),
                        pltpu.VMEM((2, tm + SUBLANES, MXU_COLS), F32)],
        compiler_params=_cparams(("arbitrary", "arbitrary")),
        name="conv_ffn",
    )(x, g.reshape(1, D), w_up, w_up, conv_w, conv_w, cb, cb, w_down, final_g.reshape(1, D))


TR_PREP, TC_PREP = 512, 512
TM_PROJ, TN_PROJ = 1024, 1536
TB_GATE = 512
TQ_FOX, TK_FOX, NH_FOX = 512, 512, 4
TB_SWA = 512
TB_GLA = 512
TM_MIX = 512
TM_FFN, FC_FFN = 512, 512


def _main_colscale():
    cs = np.ones((1, MAIN_COLS), np.float32)
    cs[0, C_FQ:C_FK] = HEAD_DIM ** -0.5 * LOG2E
    cs[0, C_SQ:C_SK] = HEAD_DIM ** -0.5
    cs[0, C_GQ:C_GK] = GLA_DK ** -0.5
    return jnp.asarray(cs)


def kernel(x, mem, w_in, b_fox_f, swa_sinks, t5_bias, w_gla_gate, b_gla_gate, gla_norm, w_mix_out, norm_mix,
           norm_xattn, norm_mem, wq_x, wkv_x, wo_x, norm_ffn, w_up, conv_w, conv_b, w_down, final_norm):
    depth = w_in.shape[0]
    _, T, D = x.shape
    M = mem.shape[1]
    h = x.reshape(T, D)
    memf = mem.reshape(M, D)
    colscale = _main_colscale()
    ones_kv = jnp.ones((1, wkv_x.shape[2]), F32)
    table = t5_table(t5_bias)
    pad_lanes = LANES - S_GLR - GLA_GATE_RANK
    w_main, w_small = prep_w_in(w_in, tr=TR_PREP, tc=TC_PREP)
    w_out_b, wq_b, wkv_b, wo_b = (w.astype(BF16) for w in (w_mix_out, wq_x, wkv_x, wo_x))
    w_up_b, w_down_b = w_up.astype(BF16), w_down.astype(BF16)
    for l in range(depth):
        bvec = jnp.concatenate([b_fox_f[l], jnp.zeros((LANES - FOX_HEADS,), F32)]).reshape(1, LANES)
        wg_pad = jnp.concatenate([jnp.zeros((S_GLR, GLA_HEADS * GLA_DK), F32), w_gla_gate[l],
                                  jnp.zeros((pad_lanes, GLA_HEADS * GLA_DK), F32)], axis=0).astype(BF16)

        proj, ps = norm_proj(h, norm_mix[l], w_main, l, colscale, w_small, tm=TM_PROJ, tn=TN_PROJ, w_t=True)
        kx, row = fox_gate(ps, bvec, tb=TB_GATE)
        o_fox = fox_attention(proj, kx, row, tq=TQ_FOX, tk=TK_FOX, nh=NH_FOX)
        o_swa = swa_attention(proj, table, swa_sinks[l], tb=TB_SWA)
        o_gla = gla(proj, ps, wg_pad, b_gla_gate[l].reshape(1, -1), gla_norm[l].reshape(1, -1), tb=TB_GLA)
        kv = norm_proj(memf, norm_mem[l], wkv_b, l, ones_kv, tm=M, tn=wkv_x.shape[2])
        h = mix_xattn(h, o_fox, o_swa, o_gla, w_out_b, l, norm_xattn[l], wq_b, kv, wo_b, tm=TM_MIX)
        h = conv_ffn(h, norm_ffn[l], w_up_b, conv_w, conv_b, w_down_b, l, final_norm,
                     tm=TM_FFN, fc=FC_FFN, final=(l == depth - 1))
    return h.reshape(x.shape)
```

```python
import functools
import math

import numpy as np
import jax
import jax.numpy as jnp
from jax import lax
from jax.experimental import pallas as pl
from jax.experimental.pallas import tpu as pltpu

F32 = jnp.float32
BF16 = jnp.bfloat16

HEAD_DIM = 128
FOX_HEADS = 4
SWA_HEADS = 8
SWA_KV_HEADS = 2
SWA_GROUP = SWA_HEADS // SWA_KV_HEADS
GLA_HEADS = 4
GLA_DK = 64
GLA_DV = 128
GLA_GATE_RANK = 16
GLA_TAU = 16.0
GLA_CHUNK = 64
SWA_WINDOW = 128
NUM_BUCKETS = 32
T5_MAX_DISTANCE = 128
XATTN_HEADS = 4
CONV_WIDTH = 3
EPS = 1e-6
LOG2E = math.log2(math.e)

LANES = 128
SUBLANES = 8
BF16_ROWS = 16
MXU_COLS = 256
VMEM_LIMIT = 56 * 1024 * 1024

_FOX_W = FOX_HEADS * HEAD_DIM
_OFF_FF = 3 * _FOX_W
_OFF_SQ = _OFF_FF + FOX_HEADS
_MAIN_B = (SWA_HEADS + 2 * SWA_KV_HEADS) * HEAD_DIM + 2 * GLA_HEADS * GLA_DK + 2 * GLA_HEADS * GLA_DV
_OFF_GLR = _OFF_SQ + _MAIN_B
C_FQ, C_FK, C_FV = 0, 512, 1024
C_SQ, C_SK, C_SV = 1536, 2560, 2816
C_GQ, C_GK, C_GV, C_GR = 3072, 3328, 3584, 4096
MAIN_COLS = 4608
S_FF, S_GLR = 0, SUBLANES


def _cparams(sem, flags=None):
    return pltpu.CompilerParams(dimension_semantics=sem, vmem_limit_bytes=VMEM_LIMIT, flags=flags)


def _rms(x, g):
    return x * lax.rsqrt(jnp.mean(x * x, axis=-1, keepdims=True) + EPS) * g


N_SPLIT = 3


def _split3(x):
    hi = x.astype(BF16)
    r1 = x - hi.astype(F32)
    mid = r1.astype(BF16)
    lo = (r1 - mid.astype(F32)).astype(BF16)
    return hi, mid, lo


def _dot(a, b):
    return jnp.dot(a, b, preferred_element_type=F32)


def _dot_nt(a, b):
    return lax.dot_general(a, b, (((1,), (1,)), ((), ())), preferred_element_type=F32)


def _dot_tn(a, b):
    return lax.dot_general(a, b, (((0,), (0,)), ((), ())), preferred_element_type=F32)


def _sel_dot(sel, x):
    hi, mid, lo = _split3(x)
    return _dot(sel, hi) + _dot(sel, mid) + _dot(sel, lo)


def _prep_w_in_kernel(cur_ref, ff_ref, glr_ref, wm_ref, ws_ref):
    L = wm_ref.shape[0]
    for l in range(L):
        wm_ref[l] = cur_ref[:, l, :].astype(BF16)

    @pl.when(pl.program_id(1) == 0)
    def _():
        pad = jnp.zeros((LANES - S_GLR - GLA_GATE_RANK, ws_ref.shape[2]), F32)
        for l in range(L):
            ws_ref[l] = jnp.concatenate([ff_ref[:, l, :], glr_ref[:, l, :], pad], axis=0).astype(BF16)


def prep_w_in(w_in, *, tr, tc):
    L, D, _ = w_in.shape
    wt = jnp.transpose(w_in, (2, 0, 1))
    assert _OFF_FF % tc == 0 and MAIN_COLS % tc == 0 and S_GLR % SUBLANES == 0

    def src(j):
        return j * tc + jnp.where(j * tc >= _OFF_FF, _OFF_SQ - _OFF_FF, 0)

    return pl.pallas_call(
        _prep_w_in_kernel,
        out_shape=[jax.ShapeDtypeStruct((L, MAIN_COLS, D), BF16),
                   jax.ShapeDtypeStruct((L, LANES, D), BF16)],
        grid=(D // tr, MAIN_COLS // tc),
        in_specs=[
            pl.BlockSpec((pl.Element(tc), pl.Element(L), pl.Element(tr)), lambda r, j: (src(j), 0, r * tr)),
            pl.BlockSpec((pl.Element(S_GLR), pl.Element(L), pl.Element(tr)), lambda r, j: (_OFF_FF, 0, r * tr)),
            pl.BlockSpec((pl.Element(GLA_GATE_RANK), pl.Element(L), pl.Element(tr)),
                         lambda r, j: (_OFF_GLR, 0, r * tr)),
        ],
        out_specs=[pl.BlockSpec((L, tc, tr), lambda r, j: (0, j, r)),
                   pl.BlockSpec((L, LANES, tr), lambda r, j: (0, 0, r))],
        compiler_params=_cparams(("arbitrary", "arbitrary")),
        name="prep_w_in",
    )(wt, wt, wt)


def _norm_proj_kernel(*refs, has_small, w_t):
    if has_small:
        x_ref, g_ref, w_ref, cs_ref, ws_ref, o_ref, os_ref, xn_ref = refs
    else:
        x_ref, g_ref, w_ref, cs_ref, o_ref, xn_ref = refs
    dot = _dot_nt if w_t else _dot

    @pl.when(pl.program_id(1) == 0)
    def _():
        xn = _rms(x_ref[...], g_ref[...]).astype(BF16)
        xn_ref[...] = xn
        if has_small:
            os_ref[...] = dot(xn, ws_ref[...])

    o_ref[...] = (dot(xn_ref[...], w_ref[...]) * cs_ref[...]).astype(o_ref.dtype)


def norm_proj(x, g, w, layer, colscale, w_small=None, *, tm, tn, w_t=False):
    T, D = x.shape
    N = w.shape[1] if w_t else w.shape[2]
    has_small = w_small is not None

    def wspec(n):
        if w_t:
            return pl.BlockSpec((None, n, D), lambda i, j: (layer, j, 0))
        return pl.BlockSpec((None, D, n), lambda i, j: (layer, 0, j))

    in_specs = [
        pl.BlockSpec((tm, D), lambda i, j: (i, 0)),
        pl.BlockSpec((1, D), lambda i, j: (0, 0)),
        wspec(tn),
        pl.BlockSpec((1, tn), lambda i, j: (0, j)),
    ]
    args = [x, g.reshape(1, D), w, colscale]
    out_shape = [jax.ShapeDtypeStruct((T, N), BF16)]
    out_specs = [pl.BlockSpec((tm, tn), lambda i, j: (i, j))]
    if has_small:
        in_specs.append(pl.BlockSpec((None,) + w_small.shape[1:], lambda i, j: (layer, 0, 0)))
        args.append(w_small)
        out_shape.append(jax.ShapeDtypeStruct((T, LANES), F32))
        out_specs.append(pl.BlockSpec((tm, LANES), lambda i, j: (i, 0)))
    outs = pl.pallas_call(
        functools.partial(_norm_proj_kernel, has_small=has_small, w_t=w_t),
        out_shape=out_shape,
        grid=(T // tm, N // tn),
        in_specs=in_specs,
        out_specs=out_specs,
        scratch_shapes=[pltpu.VMEM((tm, D), BF16)],
        compiler_params=_cparams(("arbitrary", "arbitrary")),
        name="norm_proj_small" if has_small else "norm_proj",
    )(*args)
    return outs if has_small else outs[0]


def _fox_gate_kernel(ps_ref, b_ref, kx_ref, row_ref, carry_ref, *, tb):
    @pl.when(pl.program_id(0) == 0)
    def _():
        carry_ref[...] = jnp.zeros_like(carry_ref)

    x = jax.nn.log_sigmoid(ps_ref[...] + b_ref[...])
    r = lax.broadcasted_iota(jnp.int32, (tb, tb), 0)
    c = lax.broadcasted_iota(jnp.int32, (tb, tb), 1)
    tril = jnp.where(c <= r, 1.0, 0.0).astype(BF16)
    csum = _sel_dot(tril, x) + carry_ref[0:1, :]
    carry_ref[...] = jnp.broadcast_to(csum[tb - 1:tb, :], carry_ref.shape)
    pieces = _split3(csum * LOG2E)
    neg = [-p for p in pieces]
    src = lax.broadcasted_iota(jnp.int32, (LANES, FOX_HEADS * LANES), 0)
    dst = lax.broadcasted_iota(jnp.int32, (LANES, FOX_HEADS * LANES), 1)
    kx = sum(_dot(neg[n], jnp.where((dst // LANES == src - S_FF) & (dst % LANES == n), 1.0, 0.0).astype(BF16))
             for n in range(len(neg))).astype(BF16)
    sub = lax.broadcasted_iota(jnp.int32, (FOX_HEADS * SUBLANES, LANES), 0)
    lane = lax.broadcasted_iota(jnp.int32, (FOX_HEADS * SUBLANES, LANES), 1)
    pick = jnp.where(lane - S_FF == sub // SUBLANES, 1.0, 0.0).astype(BF16)
    rows = sum(_dot_nt(pick, p) for p in pieces)
    for h in range(FOX_HEADS):
        kx_ref[h] = kx[:, h * LANES:(h + 1) * LANES]
        row_ref[h, 0] = rows[h * SUBLANES:(h + 1) * SUBLANES, :]


def fox_gate(ps, bvec, *, tb):
    T = ps.shape[0]
    nb = T // tb
    return pl.pallas_call(
        functools.partial(_fox_gate_kernel, tb=tb),
        out_shape=[jax.ShapeDtypeStruct((FOX_HEADS, T, LANES), BF16),
                   jax.ShapeDtypeStruct((FOX_HEADS, nb, SUBLANES, tb), F32)],
        grid=(nb,),
        in_specs=[pl.BlockSpec((tb, LANES), lambda i: (i, 0)),
                  pl.BlockSpec((1, LANES), lambda i: (0, 0))],
        out_specs=[pl.BlockSpec((FOX_HEADS, tb, LANES), lambda i: (0, i, 0)),
                   pl.BlockSpec((FOX_HEADS, 1, SUBLANES, tb), lambda i: (0, i, 0, 0))],
        scratch_shapes=[pltpu.VMEM((SUBLANES, LANES), F32)],
        compiler_params=_cparams(("arbitrary",)),
        name="fox_gate",
    )(ps, bvec)


def _fox_attn_kernel(q_ref, k_ref, v_ref, kx_ref, cq_ref, o_ref, vt_ref, ya_ref, yb_ref, m_ref, l_ref, acc_ref,
                     *, tq, tk, nk, nh):
    i = pl.program_id(1)
    D = HEAD_DIM
    heads = range(nh)
    n_full = i // (tk // tq)

    @pl.when(i == 0)
    def _():
        for hh in heads:
            for n in range(nk):
                vt_ref[hh, n] = v_ref[n * tk:(n + 1) * tk, hh * D:(hh + 1) * D].T

    lane = lax.broadcasted_iota(jnp.int32, (tq, LANES), 1)
    ones = jnp.where(lane < N_SPLIT, 1.0, 0.0).astype(BF16)
    q_aug = [jnp.concatenate([q_ref[:, hh * D:(hh + 1) * D], ones], axis=1) for hh in heads]
    cq = [cq_ref[hh, 0, 0:1, :] for hh in heads]
    m_ref[...] = jnp.full(m_ref.shape, -jnp.inf, F32)
    l_ref[...] = jnp.zeros(l_ref.shape, F32)
    acc_ref[...] = jnp.zeros(acc_ref.shape, F32)

    def scores(hh, j):
        start = pl.multiple_of(j * tk, tk)
        k_aug = jnp.concatenate([k_ref[pl.ds(start, tk), hh * D:(hh + 1) * D], kx_ref[hh, pl.ds(start, tk), :]],
                                axis=1)
        return _dot_nt(k_aug, q_aug[hh])

    def update(hh, j, y):
        m_old = m_ref[hh]
        m_new = jnp.maximum(m_old, jnp.max(y, axis=0, keepdims=True) + cq[hh])
        alpha = jnp.exp2(m_old - m_new)
        p = jnp.exp2(y + (cq[hh] - m_new))
        l_ref[hh] = alpha * l_ref[hh] + jnp.sum(p, axis=0, keepdims=True)
        acc_ref[hh] = alpha * acc_ref[hh] + _dot(vt_ref[hh, j], p.astype(BF16))
        m_ref[hh] = m_new

    for hh in heads:
        ya_ref[hh] = scores(hh, 0)

    def step(j, cur_ref, nxt_ref):
        for hh in heads:
            nxt_ref[hh] = scores(hh, j + 1)
        for hh in heads:
            update(hh, j, cur_ref[hh])

    def body(j, carry):
        pl.when(j % 2 == 0)(lambda: step(j, ya_ref, yb_ref))
        pl.when(j % 2 == 1)(lambda: step(j, yb_ref, ya_ref))
        return carry

    lax.fori_loop(0, n_full, body, 0)
    key = n_full * tk + lax.broadcasted_iota(jnp.int32, (tk, tq), 0)
    qry = i * tq + lax.broadcasted_iota(jnp.int32, (tk, tq), 1)

    def diagonal(cur_ref):
        for hh in heads:
            update(hh, n_full, jnp.where(key <= qry, cur_ref[hh], -jnp.inf))

    pl.when(n_full % 2 == 0)(lambda: diagonal(ya_ref))
    pl.when(n_full % 2 == 1)(lambda: diagonal(yb_ref))
    for hh in heads:
        o_ref[:, hh * D:(hh + 1) * D] = (acc_ref[hh] / l_ref[hh]).T.astype(o_ref.dtype)


def fox_attention(proj, kx, row, *, tq, tk, nh):
    T = proj.shape[0]
    nq = T // tq
    gw = nh * HEAD_DIM
    qb, kb, vb = C_FQ // gw, C_FK // gw, C_FV // gw
    assert tk % tq == 0 and T % tk == 0
    return pl.pallas_call(
        functools.partial(_fox_attn_kernel, tq=tq, tk=tk, nk=T // tk, nh=nh),
        out_shape=jax.ShapeDtypeStruct((T, FOX_HEADS * HEAD_DIM), BF16),
        grid=(FOX_HEADS // nh, nq),
        in_specs=[
            pl.BlockSpec((tq, gw), lambda g, i: (i, qb + g)),
            pl.BlockSpec((T, gw), lambda g, i: (0, kb + g), pipeline_mode=pl.Buffered(1)),
            pl.BlockSpec((T, gw), lambda g, i: (0, vb + g), pipeline_mode=pl.Buffered(1)),
            pl.BlockSpec((nh, T, LANES), lambda g, i: (g, 0, 0), pipeline_mode=pl.Buffered(1)),
            pl.BlockSpec((nh, 1, SUBLANES, tq), lambda g, i: (g, i, 0, 0)),
        ],
        out_specs=pl.BlockSpec((tq, gw), lambda g, i: (i, g)),
        scratch_shapes=[pltpu.VMEM((nh, T // tk, HEAD_DIM, tk), BF16),
                        pltpu.VMEM((nh, tk, tq), F32), pltpu.VMEM((nh, tk, tq), F32),
                        pltpu.VMEM((nh, 1, tq), F32), pltpu.VMEM((nh, 1, tq), F32),
                        pltpu.VMEM((nh, HEAD_DIM, tq), F32)],
        compiler_params=_cparams(("arbitrary", "arbitrary")),
        name="fox_attn",
    )(proj, proj, proj, kx, row)


def _t5_table_kernel(t5_ref, o_ref):
    h = pl.program_id(0)
    W = SWA_WINDOW
    i = lax.broadcasted_iota(jnp.int32, (W, 2 * W), 0)
    j = lax.broadcasted_iota(jnp.int32, (W, 2 * W), 1)
    rel = (W + i) - j
    n = jnp.maximum(rel, 0)
    max_exact = NUM_BUCKETS // 2
    nf = jnp.maximum(n, 1).astype(F32)
    large = max_exact + (jnp.log(nf / max_exact) / math.log(T5_MAX_DISTANCE / max_exact)
                         * (NUM_BUCKETS - max_exact)).astype(jnp.int32)
    large = jnp.minimum(large, NUM_BUCKETS - 1)
    bucket = jnp.where(n < max_exact, n, large)
    bias = jnp.zeros((W, 2 * W), F32)
    for b in range(NUM_BUCKETS):
        bias = jnp.where(bucket == b, t5_ref[b, h], bias)
    in_window = (rel >= 0) & (rel < W)
    o_ref[...] = jnp.where(in_window, bias, -jnp.inf)


def t5_table(t5_bias):
    W = SWA_WINDOW
    return pl.pallas_call(
        _t5_table_kernel,
        out_shape=jax.ShapeDtypeStruct((SWA_HEADS, W, 2 * W), F32),
        grid=(SWA_HEADS,),
        in_specs=[pl.BlockSpec(memory_space=pltpu.SMEM)],
        out_specs=pl.BlockSpec((None, W, 2 * W), lambda h: (h, 0, 0)),
        compiler_params=_cparams(("arbitrary",)),
        name="t5_table",
    )(t5_bias)


def _swa_kernel(sink_ref, q_ref, kc_ref, kp_ref, vc_ref, vp_ref, tab_ref, o_ref, *, nsub):
    g = pl.program_id(0)
    n = pl.program_id(1)
    W = SWA_WINDOW
    first_prev = lax.broadcasted_iota(jnp.int32, (W, 2 * W), 1) < W
    for sub in range(nsub):
        if sub == 0:
            kp, vp = kp_ref[...], vp_ref[...]
        else:
            kp, vp = kc_ref[(sub - 1) * W:sub * W, :], vc_ref[(sub - 1) * W:sub * W, :]
        kk = jnp.concatenate([kp, kc_ref[sub * W:(sub + 1) * W, :]], axis=0)
        vv = jnp.concatenate([vp, vc_ref[sub * W:(sub + 1) * W, :]], axis=0)
        for hh in range(SWA_GROUP):
            q = q_ref[sub * W:(sub + 1) * W, hh * HEAD_DIM:(hh + 1) * HEAD_DIM]
            s = _dot_nt(q, kk) + tab_ref[hh]
            if sub == 0:
                s = jnp.where(jnp.logical_and(first_prev, n == 0), -jnp.inf, s)
            sink = sink_ref[0, g * SWA_GROUP + hh]
            m = jnp.maximum(jnp.max(s, axis=1, keepdims=True), sink)
            p = jnp.exp(s - m)
            l = jnp.sum(p, axis=1, keepdims=True) + jnp.exp(sink - m)
            o = _dot(p.astype(BF16), vv) / l
            o_ref[sub * W:(sub + 1) * W, hh * HEAD_DIM:(hh + 1) * HEAD_DIM] = o.astype(o_ref.dtype)


def swa_attention(proj, table, sinks, *, tb):
    T = proj.shape[0]
    W = SWA_WINDOW
    nsub = tb // W
    gw = SWA_GROUP * HEAD_DIM
    qb, kb, vb = C_SQ // gw, C_SK // HEAD_DIM, C_SV // HEAD_DIM
    prev = lambda n: jnp.maximum(n * nsub - 1, 0)
    return pl.pallas_call(
        functools.partial(_swa_kernel, nsub=nsub),
        out_shape=jax.ShapeDtypeStruct((T, SWA_HEADS * HEAD_DIM), BF16),
        grid=(SWA_KV_HEADS, T // tb),
        in_specs=[
            pl.BlockSpec(memory_space=pltpu.SMEM),
            pl.BlockSpec((tb, gw), lambda g, n: (n, qb + g)),
            pl.BlockSpec((tb, HEAD_DIM), lambda g, n: (n, kb + g)),
            pl.BlockSpec((W, HEAD_DIM), lambda g, n: (prev(n), kb + g)),
            pl.BlockSpec((tb, HEAD_DIM), lambda g, n: (n, vb + g)),
            pl.BlockSpec((W, HEAD_DIM), lambda g, n: (prev(n), vb + g)),
            pl.BlockSpec((SWA_GROUP, W, 2 * W), lambda g, n: (g, 0, 0)),
        ],
        out_specs=pl.BlockSpec((tb, gw), lambda g, n: (n, g)),
        compiler_params=_cparams(("arbitrary", "arbitrary")),
        name="swa_attn",
    )(sinks.reshape(1, SWA_HEADS), proj, proj, proj, proj, proj, table)


def _gla_kernel(q_ref, k_ref, v_ref, r_ref, ps_ref, wg_ref, bg_ref, gn_ref, o_ref, st_ref, oc_ref, *, tb):
    C = GLA_CHUNK

    @pl.when(pl.program_id(0) == 0)
    def _():
        st_ref[...] = jnp.zeros_like(st_ref)

    glr = ps_ref[...].astype(BF16)
    g = jax.nn.log_sigmoid(_dot(glr, wg_ref[...]) + bg_ref[...]) / GLA_TAU
    r = lax.broadcasted_iota(jnp.int32, (tb, tb), 0)
    c = lax.broadcasted_iota(jnp.int32, (tb, tb), 1)
    same = (r // C) == (c // C)
    tril = jnp.where(jnp.logical_and(same, c <= r), 1.0, 0.0).astype(BF16)
    whole = jnp.where(same, 1.0, 0.0).astype(BF16)
    gh, gm, gl = _split3(g)
    b = _dot(tril, gh) + _dot(tril, gm) + _dot(tril, gl)
    b_last = _dot(whole, gh) + _dot(whole, gm) + _dot(whole, gl)
    q_t = (q_ref[...].astype(F32) * jnp.exp(b)).astype(BF16)
    kf = k_ref[...].astype(F32)
    k_t = kf * jnp.exp(-b)
    k_end = kf * jnp.exp(b_last - b)
    decay = jnp.exp(b_last)
    lane = lax.broadcasted_iota(jnp.int32, (1, LANES), 1)
    causal = (lax.broadcasted_iota(jnp.int32, (C, C), 1) <= lax.broadcasted_iota(jnp.int32, (C, C), 0))
    for h in range(GLA_HEADS):
        pair = slice((h // 2) * LANES, (h // 2 + 1) * LANES)
        mine = (lane // GLA_DK) == (h % 2)
        ktm = jnp.where(mine, k_t[:, pair], 0.0).astype(BF16)
        kem = jnp.where(mine, k_end[:, pair], 0.0).astype(BF16)
        qh = q_t[:, pair]
        dec = decay[:, pair]
        vh = v_ref[:, h * GLA_DV:(h + 1) * GLA_DV]
        st = st_ref[h]
        for n in range(tb // C):
            rows = slice(n * C, (n + 1) * C)
            a = jnp.where(causal, _dot_nt(qh[rows], ktm[rows]), 0.0)
            o = _dot(a.astype(BF16), vh[rows]) + _dot_nt(qh[rows], st.astype(BF16))
            oc_ref[rows, h * GLA_DV:(h + 1) * GLA_DV] = o
            kv_t = _dot_tn(vh[rows], kem[rows])
            st = dec[n * C:n * C + 1, :] * st + kv_t
        st_ref[h] = st
    for h in range(GLA_HEADS):
        cols = slice(h * GLA_DV, (h + 1) * GLA_DV)
        o = _rms(oc_ref[:, cols], gn_ref[...])
        o_ref[:, cols] = (o * jax.nn.silu(r_ref[:, cols].astype(F32))).astype(o_ref.dtype)


def gla(proj, ps, wg_pad, bg, gnorm, *, tb):
    T = proj.shape[0]
    qw = GLA_HEADS * GLA_DK
    vw = GLA_HEADS * GLA_DV
    return pl.pallas_call(
        functools.partial(_gla_kernel, tb=tb),
        out_shape=jax.ShapeDtypeStruct((T, vw), BF16),
        grid=(T // tb,),
        in_specs=[
            pl.BlockSpec((tb, qw), lambda n: (n, C_GQ // qw)),
            pl.BlockSpec((tb, qw), lambda n: (n, C_GK // qw)),
            pl.BlockSpec((tb, vw), lambda n: (n, C_GV // vw)),
            pl.BlockSpec((tb, vw), lambda n: (n, C_GR // vw)),
            pl.BlockSpec((tb, LANES), lambda n: (n, 0)),
            pl.BlockSpec((LANES, qw), lambda n: (0, 0)),
            pl.BlockSpec((1, qw), lambda n: (0, 0)),
            pl.BlockSpec((1, GLA_DV), lambda n: (0, 0)),
        ],
        out_specs=pl.BlockSpec((tb, vw), lambda n: (n, 0)),
        scratch_shapes=[pltpu.VMEM((GLA_HEADS, GLA_DV, LANES), F32),
                        pltpu.VMEM((tb, vw), F32)],
        compiler_params=_cparams(("arbitrary",)),
        name="gla",
    )(proj, proj, proj, proj, ps, wg_pad, bg, gnorm)


def _mix_xattn_kernel(h_ref, of_ref, os_ref, og_ref, wo_ref, gx_ref, wq_ref, kv_ref, wx_ref, o_ref):
    nf, ns = of_ref.shape[1], os_ref.shape[1]
    mix = (_dot(of_ref[...], wo_ref[0:nf, :]) + _dot(os_ref[...], wo_ref[nf:nf + ns, :])
           + _dot(og_ref[...], wo_ref[nf + ns:, :]))
    h1 = h_ref[...] + mix
    xn = _rms(h1, gx_ref[...]).astype(BF16)
    q = (_dot(xn, wq_ref[...]) * (HEAD_DIM ** -0.5)).astype(BF16)
    xw = XATTN_HEADS * HEAD_DIM
    outs = []
    for hh in range(XATTN_HEADS):
        cols = slice(hh * HEAD_DIM, (hh + 1) * HEAD_DIM)
        k = kv_ref[:, cols]
        v = kv_ref[:, xw + hh * HEAD_DIM:xw + (hh + 1) * HEAD_DIM]
        s = _dot_nt(q[:, cols], k)
        p = jnp.exp(s - jnp.max(s, axis=1, keepdims=True))
        l = jnp.sum(p, axis=1, keepdims=True)
        outs.append((_dot(p.astype(BF16), v) / l).astype(BF16))
    o = jnp.concatenate(outs, axis=1)
    o_ref[...] = h1 + _dot(o, wx_ref[...])


def mix_xattn(h, o_fox, o_swa, o_gla, w_out, layer, gx, wq, kv, wo, *, tm):
    T, D = h.shape
    const = lambda shape: pl.BlockSpec(shape, lambda i: (0, 0))
    stacked = lambda w: pl.BlockSpec((None,) + w.shape[1:], lambda i: (layer, 0, 0))
    row = lambda w: pl.BlockSpec((tm, w), lambda i: (i, 0))
    return pl.pallas_call(
        _mix_xattn_kernel,
        out_shape=jax.ShapeDtypeStruct((T, D), F32),
        grid=(T // tm,),
        in_specs=[row(D), row(o_fox.shape[1]), row(o_swa.shape[1]), row(o_gla.shape[1]),
                  stacked(w_out), const((1, D)), stacked(wq), const(kv.shape), stacked(wo)],
        out_specs=row(D),
        compiler_params=_cparams(("arbitrary",)),
        name="mix_xattn",
    )(h, o_fox, o_swa, o_gla, w_out, gx.reshape(1, D), wq, kv, wo)


def _ffn_kernel(x_ref, g_ref, wg_ref, wv_ref, cwg_ref, cwv_ref, cbg_ref, cbv_ref, wd_ref, fg_ref,
                o_ref, xn_ref, hg_ref, hv_ref, ug_ref, uv_ref, *, tm, sub, final):
    i = pl.program_id(0)
    f = pl.program_id(1)
    H = SUBLANES
    fc = wd_ref.shape[0]

    @pl.when(f == 0)
    def _():
        x = x_ref[...]
        xn_ref[...] = _rms(x, g_ref[...]).astype(BF16)
        o_ref[...] = x

    @pl.when(i == 0)
    def _():
        hg_ref[f] = jnp.zeros(hg_ref.shape[1:], F32)
        hv_ref[f] = jnp.zeros(hv_ref.shape[1:], F32)

    xn = xn_ref[...]

    def up(c, slot):
        ug_ref[slot, H:, :] = _dot(xn, wg_ref[:, c:c + sub])
        uv_ref[slot, H:, :] = _dot(xn, wv_ref[:, c:c + sub])

    def conv(u_ref, slot, h_ref, cw_ref, cb_ref, c):
        u_ref[slot, 0:H, :] = h_ref[f, :, c:c + sub]
        h_ref[f, :, c:c + sub] = u_ref[slot, tm:tm + H, :]
        return (cb_ref[:, c:c + sub] + cw_ref[2:3, c:c + sub] * u_ref[slot, H:H + tm, :]
                + cw_ref[1:2, c:c + sub] * u_ref[slot, H - 1:H - 1 + tm, :]
                + cw_ref[0:1, c:c + sub] * u_ref[slot, H - 2:H - 2 + tm, :])

    def down(c, slot):
        gate = conv(ug_ref, slot, hg_ref, cwg_ref, cbg_ref, c)
        val = conv(uv_ref, slot, hv_ref, cwv_ref, cbv_ref, c)
        act = (jax.nn.silu(gate) * val).astype(BF16)
        o_ref[...] += _dot(act, wd_ref[c:c + sub, :])

    up(0, 0)
    for n, c in enumerate(range(0, fc, sub)):
        if c + sub < fc:
            up(c + sub, (n + 1) % 2)
        down(c, n % 2)

    if final:
        @pl.when(f == pl.num_programs(1) - 1)
        def _():
            o_ref[...] = _rms(o_ref[...], fg_ref[...])


def conv_ffn(x, g, w_up, conv_w, conv_b, w_down, layer, final_g, *, tm, fc, final):
    T, D = x.shape
    L, dff, _ = w_down.shape
    nf = dff // fc
    cb = conv_b.reshape(L, 1, 2 * dff)
    return pl.pallas_call(
        functools.partial(_ffn_kernel, tm=tm, sub=MXU_COLS, final=final),
        out_shape=jax.ShapeDtypeStruct((T, D), F32),
        grid=(T // tm, nf),
        in_specs=[
            pl.BlockSpec((tm, D), lambda i, f: (i, 0)),
            pl.BlockSpec((1, D), lambda i, f: (0, 0)),
            pl.BlockSpec((None, D, fc), lambda i, f: (layer, 0, f)),
            pl.BlockSpec((None, D, fc), lambda i, f: (layer, 0, nf + f)),
            pl.BlockSpec((None, CONV_WIDTH, fc), lambda i, f: (layer, 0, f)),
            pl.BlockSpec((None, CONV_WIDTH, fc), lambda i, f: (layer, 0, nf + f)),
            pl.BlockSpec((None, 1, fc), lambda i, f: (layer, 0, f)),
            pl.BlockSpec((None, 1, fc), lambda i, f: (layer, 0, nf + f)),
            pl.BlockSpec((None, fc, D), lambda i, f: (layer, f, 0)),
            pl.BlockSpec((1, D), lambda i, f: (0, 0)),
        ],
        out_specs=pl.BlockSpec((tm, D), lambda i, f: (i, 0)),
        scratch_shapes=[pltpu.VMEM((tm, D), BF16), pltpu.VMEM((nf, SUBLANES, fc), F32),
                        pltpu.VMEM((nf, SUBLANES, fc), F32),
                        pltpu.VMEM((2, tm + SUBLANES, MXU_COLS), F32),
                        pltpu.VMEM((2, tm + SUBLANES, MXU_COLS), F32)],
        compiler_params=_cparams(("arbitrary", "arbitrary")),
        name="conv_ffn",
    )(x, g.reshape(1, D), w_up, w_up, conv_w, conv_w, cb, cb, w_down, final_g.reshape(1, D))


def _ffn_up_kernel(x_ref, g_ref, wg_ref, wv_ref, cwg_ref, cwv_ref, cbg_ref, cbv_ref,
                   a_ref, xn_ref, hg_ref, hv_ref, ug_ref, uv_ref, *, rb, sub):
    i = pl.program_id(0)
    f = pl.program_id(1)
    H = SUBLANES
    tm, fc = a_ref.shape
    nrb = tm // rb

    @pl.when(f == 0)
    def _():
        xn_ref[...] = _rms(x_ref[...], g_ref[...]).astype(BF16)

    @pl.when(i == 0)
    def _():
        hg_ref[f] = jnp.zeros(hg_ref.shape[1:], F32)
        hv_ref[f] = jnp.zeros(hv_ref.shape[1:], F32)

    def up(r, n):
        xn = xn_ref[r * rb:(r + 1) * rb, :]
        ug_ref[r, n, H:, :] = _dot(xn, wg_ref[:, n * sub:(n + 1) * sub])
        uv_ref[r, n, H:, :] = _dot(xn, wv_ref[:, n * sub:(n + 1) * sub])

    def conv(u_ref, r, n, h_ref, cw_ref, cb_ref):
        cols = slice(n * sub, (n + 1) * sub)
        u_ref[r, n, 0:H, :] = h_ref[f, :, cols] if r == 0 else u_ref[r - 1, n, rb:rb + H, :]
        if r == nrb - 1:
            h_ref[f, :, cols] = u_ref[r, n, rb:rb + H, :]
        return (cb_ref[:, cols] + cw_ref[2:3, cols] * u_ref[r, n, H:H + rb, :]
                + cw_ref[1:2, cols] * u_ref[r, n, H - 1:H - 1 + rb, :]
                + cw_ref[0:1, cols] * u_ref[r, n, H - 2:H - 2 + rb, :])

    def gate(r, n):
        g = conv(ug_ref, r, n, hg_ref, cwg_ref, cbg_ref)
        v = conv(uv_ref, r, n, hv_ref, cwv_ref, cbv_ref)
        a_ref[r * rb:(r + 1) * rb, n * sub:(n + 1) * sub] = (jax.nn.silu(g) * v).astype(a_ref.dtype)

    units = [(r, n) for n in range(fc // sub) for r in range(nrb)]
    up(*units[0])
    for k, unit in enumerate(units):
        if k + 1 < len(units):
            up(*units[k + 1])
        gate(*unit)


def ffn_up(x, g, w_up, conv_w, conv_b, layer, *, tm, rb, fc):
    T, D = x.shape
    L = w_up.shape[0]
    dff = w_up.shape[2] // 2
    nf = dff // fc
    cb = conv_b.reshape(L, 1, 2 * dff)
    stage = pltpu.VMEM((tm // rb, fc // MXU_COLS, rb + SUBLANES, MXU_COLS), F32)
    return pl.pallas_call(
        functools.partial(_ffn_up_kernel, rb=rb, sub=MXU_COLS),
        out_shape=jax.ShapeDtypeStruct((T, dff), BF16),
        grid=(T // tm, nf),
        in_specs=[
            pl.BlockSpec((tm, D), lambda i, f: (i, 0)),
            pl.BlockSpec((1, D), lambda i, f: (0, 0)),
            pl.BlockSpec((None, D, fc), lambda i, f: (layer, 0, f)),
            pl.BlockSpec((None, D, fc), lambda i, f: (layer, 0, nf + f)),
            pl.BlockSpec((None, CONV_WIDTH, fc), lambda i, f: (layer, 0, f)),
            pl.BlockSpec((None, CONV_WIDTH, fc), lambda i, f: (layer, 0, nf + f)),
            pl.BlockSpec((None, 1, fc), lambda i, f: (layer, 0, f)),
            pl.BlockSpec((None, 1, fc), lambda i, f: (layer, 0, nf + f)),
        ],
        out_specs=pl.BlockSpec((tm, fc), lambda i, f: (i, f)),
        scratch_shapes=[pltpu.VMEM((tm, D), BF16), pltpu.VMEM((nf, SUBLANES, fc), F32),
                        pltpu.VMEM((nf, SUBLANES, fc), F32), stage, stage],
        compiler_params=_cparams(("arbitrary", "arbitrary")),
        name="ffn_up",
    )(x, g.reshape(1, D), w_up, w_up, conv_w, conv_w, cb, cb)


def _ffn_down_kernel(a_ref, w_ref, x_ref, o_ref):
    o_ref[...] = x_ref[...] + _dot(a_ref[...], w_ref[...])


def ffn_down(act, w_down, layer, x, *, tm, tn):
    T, D = x.shape
    dff = act.shape[1]
    return pl.pallas_call(
        _ffn_down_kernel,
        out_shape=jax.ShapeDtypeStruct((T, D), F32),
        grid=(T // tm, D // tn),
        in_specs=[
            pl.BlockSpec((tm, dff), lambda i, j: (i, 0)),
            pl.BlockSpec((None, dff, tn), lambda i, j: (layer, 0, j)),
            pl.BlockSpec((tm, tn), lambda i, j: (i, j)),
        ],
        out_specs=pl.BlockSpec((tm, tn), lambda i, j: (i, j)),
        compiler_params=_cparams(("arbitrary", "arbitrary")),
        name="ffn_down",
    )(act, w_down, x)


def _rmsnorm_kernel(x_ref, g_ref, o_ref):
    o_ref[...] = _rms(x_ref[...], g_ref[...])


def rmsnorm(x, g, *, tm):
    T, D = x.shape
    return pl.pallas_call(
        _rmsnorm_kernel,
        out_shape=jax.ShapeDtypeStruct((T, D), x.dtype),
        grid=(T // tm,),
        in_specs=[pl.BlockSpec((tm, D), lambda i: (i, 0)), pl.BlockSpec((1, D), lambda i: (0, 0))],
        out_specs=pl.BlockSpec((tm, D), lambda i: (i, 0)),
        compiler_params=_cparams(("arbitrary",)),
        name="final_rmsnorm",
    )(x, g.reshape(1, D))


TR_PREP, TC_PREP = 512, 512
TM_PROJ, TN_PROJ = 1024, 1536
TB_GATE = 512
TQ_FOX, TK_FOX, NH_FOX = 512, 512, 4
TB_SWA = 512
TB_GLA = 512
TM_MIX = 512
TM_FFN, RB_FFN, FC_FFN = 1024, 512, 512
TM_DOWN, TN_DOWN = 1024, 512


def _main_colscale():
    cs = np.ones((1, MAIN_COLS), np.float32)
    cs[0, C_FQ:C_FK] = HEAD_DIM ** -0.5 * LOG2E
    cs[0, C_SQ:C_SK] = HEAD_DIM ** -0.5
    cs[0, C_GQ:C_GK] = GLA_DK ** -0.5
    return jnp.asarray(cs)


def kernel(x, mem, w_in, b_fox_f, swa_sinks, t5_bias, w_gla_gate, b_gla_gate, gla_norm, w_mix_out, norm_mix,
           norm_xattn, norm_mem, wq_x, wkv_x, wo_x, norm_ffn, w_up, conv_w, conv_b, w_down, final_norm):
    depth = w_in.shape[0]
    _, T, D = x.shape
    M = mem.shape[1]
    h = x.reshape(T, D)
    memf = mem.reshape(M, D)
    colscale = _main_colscale()
    ones_kv = jnp.ones((1, wkv_x.shape[2]), F32)
    table = t5_table(t5_bias)
    pad_lanes = LANES - S_GLR - GLA_GATE_RANK
    w_main, w_small = prep_w_in(w_in, tr=TR_PREP, tc=TC_PREP)
    w_out_b, wq_b, wkv_b, wo_b = (w.astype(BF16) for w in (w_mix_out, wq_x, wkv_x, wo_x))
    w_up_b, w_down_b = w_up.astype(BF16), w_down.astype(BF16)
    for l in range(depth):
        bvec = jnp.concatenate([b_fox_f[l], jnp.zeros((LANES - FOX_HEADS,), F32)]).reshape(1, LANES)
        wg_pad = jnp.concatenate([jnp.zeros((S_GLR, GLA_HEADS * GLA_DK), F32), w_gla_gate[l],
                                  jnp.zeros((pad_lanes, GLA_HEADS * GLA_DK), F32)], axis=0).astype(BF16)

        proj, ps = norm_proj(h, norm_mix[l], w_main, l, colscale, w_small, tm=TM_PROJ, tn=TN_PROJ, w_t=True)
        kx, row = fox_gate(ps, bvec, tb=TB_GATE)
        o_fox = fox_attention(proj, kx, row, tq=TQ_FOX, tk=TK_FOX, nh=NH_FOX)
        o_swa = swa_attention(proj, table, swa_sinks[l], tb=TB_SWA)
        o_gla = gla(proj, ps, wg_pad, b_gla_gate[l].reshape(1, -1), gla_norm[l].reshape(1, -1), tb=TB_GLA)
        kv = norm_proj(memf, norm_mem[l], wkv_b, l, ones_kv, tm=M, tn=wkv_x.shape[2])
        h = mix_xattn(h, o_fox, o_swa, o_gla, w_out_b, l, norm_xattn[l], wq_b, kv, wo_b, tm=TM_MIX)
        act = ffn_up(h, norm_ffn[l], w_up_b, conv_w, conv_b, l, tm=TM_FFN, rb=RB_FFN, fc=FC_FFN)
        h = ffn_down(act, w_down_b, l, h, tm=TM_DOWN, tn=TN_DOWN)
    return rmsnorm(h, final_norm, tm=TM_MIX).reshape(x.shape)
```

```python
import functools
import math

import numpy as np
import jax
import jax.numpy as jnp
from jax import lax
from jax.experimental import pallas as pl
from jax.experimental.pallas import tpu as pltpu

F32 = jnp.float32
BF16 = jnp.bfloat16

HEAD_DIM = 128
FOX_HEADS = 4
SWA_HEADS = 8
SWA_KV_HEADS = 2
SWA_GROUP = SWA_HEADS // SWA_KV_HEADS
GLA_HEADS = 4
GLA_DK = 64
GLA_DV = 128
GLA_GATE_RANK = 16
GLA_TAU = 16.0
GLA_CHUNK = 64
SWA_WINDOW = 128
NUM_BUCKETS = 32
T5_MAX_DISTANCE = 128
XATTN_HEADS = 4
CONV_WIDTH = 3
EPS = 1e-6
LOG2E = math.log2(math.e)

LANES = 128
SUBLANES = 8
BF16_ROWS = 16
MXU_COLS = 256
VMEM_LIMIT = 56 * 1024 * 1024

_FOX_W = FOX_HEADS * HEAD_DIM
_OFF_FF = 3 * _FOX_W
_OFF_SQ = _OFF_FF + FOX_HEADS
_MAIN_B = (SWA_HEADS + 2 * SWA_KV_HEADS) * HEAD_DIM + 2 * GLA_HEADS * GLA_DK + 2 * GLA_HEADS * GLA_DV
_OFF_GLR = _OFF_SQ + _MAIN_B
C_FQ, C_FK, C_FV = 0, 512, 1024
C_SQ, C_SK, C_SV = 1536, 2560, 2816
C_GQ, C_GK, C_GV, C_GR = 3072, 3328, 3584, 4096
MAIN_COLS = 4608
S_FF, S_GLR = 0, SUBLANES


def _cparams(sem, flags=None):
    return pltpu.CompilerParams(dimension_semantics=sem, vmem_limit_bytes=VMEM_LIMIT, flags=flags)


def _rms(x, g):
    return x * lax.rsqrt(jnp.mean(x * x, axis=-1, keepdims=True) + EPS) * g


N_SPLIT = 3


def _split3(x):
    hi = x.astype(BF16)
    r1 = x - hi.astype(F32)
    mid = r1.astype(BF16)
    lo = (r1 - mid.astype(F32)).astype(BF16)
    return hi, mid, lo


def _dot(a, b):
    return jnp.dot(a, b, preferred_element_type=F32)


def _dot_nt(a, b):
    return lax.dot_general(a, b, (((1,), (1,)), ((), ())), preferred_element_type=F32)


def _dot_tn(a, b):
    return lax.dot_general(a, b, (((0,), (0,)), ((), ())), preferred_element_type=F32)


def _sel_dot(sel, x):
    hi, mid, lo = _split3(x)
    return _dot(sel, hi) + _dot(sel, mid) + _dot(sel, lo)


def _prep_w_in_kernel(cur_ref, ff_ref, glr_ref, wm_ref, ws_ref):
    L = wm_ref.shape[0]
    for l in range(L):
        wm_ref[l] = cur_ref[:, l, :].astype(BF16)

    @pl.when(pl.program_id(1) == 0)
    def _():
        pad = jnp.zeros((LANES - S_GLR - GLA_GATE_RANK, ws_ref.shape[2]), F32)
        for l in range(L):
            ws_ref[l] = jnp.concatenate([ff_ref[:, l, :], glr_ref[:, l, :], pad], axis=0).astype(BF16)


def prep_w_in(w_in, *, tr, tc):
    L, D, _ = w_in.shape
    wt = jnp.transpose(w_in, (2, 0, 1))
    assert _OFF_FF % tc == 0 and MAIN_COLS % tc == 0 and S_GLR % SUBLANES == 0

    def src(j):
        return j * tc + jnp.where(j * tc >= _OFF_FF, _OFF_SQ - _OFF_FF, 0)

    return pl.pallas_call(
        _prep_w_in_kernel,
        out_shape=[jax.ShapeDtypeStruct((L, MAIN_COLS, D), BF16),
                   jax.ShapeDtypeStruct((L, LANES, D), BF16)],
        grid=(D // tr, MAIN_COLS // tc),
        in_specs=[
            pl.BlockSpec((pl.Element(tc), pl.Element(L), pl.Element(tr)), lambda r, j: (src(j), 0, r * tr)),
            pl.BlockSpec((pl.Element(S_GLR), pl.Element(L), pl.Element(tr)), lambda r, j: (_OFF_FF, 0, r * tr)),
            pl.BlockSpec((pl.Element(GLA_GATE_RANK), pl.Element(L), pl.Element(tr)),
                         lambda r, j: (_OFF_GLR, 0, r * tr)),
        ],
        out_specs=[pl.BlockSpec((L, tc, tr), lambda r, j: (0, j, r)),
                   pl.BlockSpec((L, LANES, tr), lambda r, j: (0, 0, r))],
        compiler_params=_cparams(("arbitrary", "arbitrary")),
        name="prep_w_in",
    )(wt, wt, wt)


def _norm_proj_kernel(*refs, has_small, w_t):
    if has_small:
        x_ref, g_ref, w_ref, cs_ref, ws_ref, o_ref, os_ref, xn_ref = refs
    else:
        x_ref, g_ref, w_ref, cs_ref, o_ref, xn_ref = refs
    dot = _dot_nt if w_t else _dot

    @pl.when(pl.program_id(1) == 0)
    def _():
        xn = _rms(x_ref[...], g_ref[...]).astype(BF16)
        xn_ref[...] = xn
        if has_small:
            os_ref[...] = dot(xn, ws_ref[...])

    o_ref[...] = (dot(xn_ref[...], w_ref[...]) * cs_ref[...]).astype(o_ref.dtype)


def norm_proj(x, g, w, layer, colscale, w_small=None, *, tm, tn, w_t=False):
    T, D = x.shape
    N = w.shape[1] if w_t else w.shape[2]
    has_small = w_small is not None

    def wspec(n):
        if w_t:
            return pl.BlockSpec((None, n, D), lambda i, j: (layer, j, 0))
        return pl.BlockSpec((None, D, n), lambda i, j: (layer, 0, j))

    in_specs = [
        pl.BlockSpec((tm, D), lambda i, j: (i, 0)),
        pl.BlockSpec((1, D), lambda i, j: (0, 0)),
        wspec(tn),
        pl.BlockSpec((1, tn), lambda i, j: (0, j)),
    ]
    args = [x, g.reshape(1, D), w, colscale]
    out_shape = [jax.ShapeDtypeStruct((T, N), BF16)]
    out_specs = [pl.BlockSpec((tm, tn), lambda i, j: (i, j))]
    if has_small:
        in_specs.append(pl.BlockSpec((None,) + w_small.shape[1:], lambda i, j: (layer, 0, 0)))
        args.append(w_small)
        out_shape.append(jax.ShapeDtypeStruct((T, LANES), F32))
        out_specs.append(pl.BlockSpec((tm, LANES), lambda i, j: (i, 0)))
    outs = pl.pallas_call(
        functools.partial(_norm_proj_kernel, has_small=has_small, w_t=w_t),
        out_shape=out_shape,
        grid=(T // tm, N // tn),
        in_specs=in_specs,
        out_specs=out_specs,
        scratch_shapes=[pltpu.VMEM((tm, D), BF16)],
        compiler_params=_cparams(("arbitrary", "arbitrary")),
        name="norm_proj_small" if has_small else "norm_proj",
    )(*args)
    return outs if has_small else outs[0]


def _fox_gate_kernel(ps_ref, b_ref, kx_ref, row_ref, carry_ref, *, tb):
    @pl.when(pl.program_id(0) == 0)
    def _():
        carry_ref[...] = jnp.zeros_like(carry_ref)

    x = jax.nn.log_sigmoid(ps_ref[...] + b_ref[...])
    r = lax.broadcasted_iota(jnp.int32, (tb, tb), 0)
    c = lax.broadcasted_iota(jnp.int32, (tb, tb), 1)
    tril = jnp.where(c <= r, 1.0, 0.0).astype(BF16)
    csum = _sel_dot(tril, x) + carry_ref[0:1, :]
    carry_ref[...] = jnp.broadcast_to(csum[tb - 1:tb, :], carry_ref.shape)
    pieces = _split3(csum * LOG2E)
    neg = [-p for p in pieces]
    src = lax.broadcasted_iota(jnp.int32, (LANES, FOX_HEADS * LANES), 0)
    dst = lax.broadcasted_iota(jnp.int32, (LANES, FOX_HEADS * LANES), 1)
    kx = sum(_dot(neg[n], jnp.where((dst // LANES == src - S_FF) & (dst % LANES == n), 1.0, 0.0).astype(BF16))
             for n in range(len(neg))).astype(BF16)
    sub = lax.broadcasted_iota(jnp.int32, (FOX_HEADS * SUBLANES, LANES), 0)
    lane = lax.broadcasted_iota(jnp.int32, (FOX_HEADS * SUBLANES, LANES), 1)
    pick = jnp.where(lane - S_FF == sub // SUBLANES, 1.0, 0.0).astype(BF16)
    rows = sum(_dot_nt(pick, p) for p in pieces)
    for h in range(FOX_HEADS):
        kx_ref[h] = kx[:, h * LANES:(h + 1) * LANES]
        row_ref[h, 0] = rows[h * SUBLANES:(h + 1) * SUBLANES, :]


def fox_gate(ps, bvec, *, tb):
    T = ps.shape[0]
    nb = T // tb
    return pl.pallas_call(
        functools.partial(_fox_gate_kernel, tb=tb),
        out_shape=[jax.ShapeDtypeStruct((FOX_HEADS, T, LANES), BF16),
                   jax.ShapeDtypeStruct((FOX_HEADS, nb, SUBLANES, tb), F32)],
        grid=(nb,),
        in_specs=[pl.BlockSpec((tb, LANES), lambda i: (i, 0)),
                  pl.BlockSpec((1, LANES), lambda i: (0, 0))],
        out_specs=[pl.BlockSpec((FOX_HEADS, tb, LANES), lambda i: (0, i, 0)),
                   pl.BlockSpec((FOX_HEADS, 1, SUBLANES, tb), lambda i: (0, i, 0, 0))],
        scratch_shapes=[pltpu.VMEM((SUBLANES, LANES), F32)],
        compiler_params=_cparams(("arbitrary",)),
        name="fox_gate",
    )(ps, bvec)


def _fox_attn_kernel(q_ref, k_ref, v_ref, kx_ref, cq_ref, o_ref, vt_ref, ya_ref, yb_ref, m_ref, l_ref, acc_ref,
                     *, tq, tk, nk, nh):
    i = pl.program_id(1)
    D = HEAD_DIM
    heads = range(nh)
    n_full = i // (tk // tq)

    @pl.when(i == 0)
    def _():
        for hh in heads:
            for n in range(nk):
                vt_ref[hh, n] = v_ref[n * tk:(n + 1) * tk, hh * D:(hh + 1) * D].T

    lane = lax.broadcasted_iota(jnp.int32, (tq, LANES), 1)
    ones = jnp.where(lane < N_SPLIT, 1.0, 0.0).astype(BF16)
    q_aug = [jnp.concatenate([q_ref[:, hh * D:(hh + 1) * D], ones], axis=1) for hh in heads]
    cq = [cq_ref[hh, 0, 0:1, :] for hh in heads]
    m_ref[...] = jnp.full(m_ref.shape, -jnp.inf, F32)
    l_ref[...] = jnp.zeros(l_ref.shape, F32)
    acc_ref[...] = jnp.zeros(acc_ref.shape, F32)

    def scores(hh, j):
        start = pl.multiple_of(j * tk, tk)
        k_aug = jnp.concatenate([k_ref[pl.ds(start, tk), hh * D:(hh + 1) * D], kx_ref[hh, pl.ds(start, tk), :]],
                                axis=1)
        return _dot_nt(k_aug, q_aug[hh])

    def update(hh, j, y):
        m_old = m_ref[hh]
        m_new = jnp.maximum(m_old, jnp.max(y, axis=0, keepdims=True) + cq[hh])
        alpha = jnp.exp2(m_old - m_new)
        p = jnp.exp2(y + (cq[hh] - m_new))
        l_ref[hh] = alpha * l_ref[hh] + jnp.sum(p, axis=0, keepdims=True)
        acc_ref[hh] = alpha * acc_ref[hh] + _dot(vt_ref[hh, j], p.astype(BF16))
        m_ref[hh] = m_new

    for hh in heads:
        ya_ref[hh] = scores(hh, 0)

    def step(j, cur_ref, nxt_ref):
        for hh in heads:
            nxt_ref[hh] = scores(hh, j + 1)
        for hh in heads:
            update(hh, j, cur_ref[hh])

    def body(j, carry):
        pl.when(j % 2 == 0)(lambda: step(j, ya_ref, yb_ref))
        pl.when(j % 2 == 1)(lambda: step(j, yb_ref, ya_ref))
        return carry

    lax.fori_loop(0, n_full, body, 0)
    key = n_full * tk + lax.broadcasted_iota(jnp.int32, (tk, tq), 0)
    qry = i * tq + lax.broadcasted_iota(jnp.int32, (tk, tq), 1)

    def diagonal(cur_ref):
        for hh in heads:
            update(hh, n_full, jnp.where(key <= qry, cur_ref[hh], -jnp.inf))

    pl.when(n_full % 2 == 0)(lambda: diagonal(ya_ref))
    pl.when(n_full % 2 == 1)(lambda: diagonal(yb_ref))
    for hh in heads:
        o_ref[:, hh * D:(hh + 1) * D] = (acc_ref[hh] / l_ref[hh]).T.astype(o_ref.dtype)


def fox_attention(proj, kx, row, *, tq, tk, nh):
    T = proj.shape[0]
    nq = T // tq
    gw = nh * HEAD_DIM
    qb, kb, vb = C_FQ // gw, C_FK // gw, C_FV // gw
    assert tk % tq == 0 and T % tk == 0
    return pl.pallas_call(
        functools.partial(_fox_attn_kernel, tq=tq, tk=tk, nk=T // tk, nh=nh),
        out_shape=jax.ShapeDtypeStruct((T, FOX_HEADS * HEAD_DIM), BF16),
        grid=(FOX_HEADS // nh, nq),
        in_specs=[
            pl.BlockSpec((tq, gw), lambda g, i: (i, qb + g)),
            pl.BlockSpec((T, gw), lambda g, i: (0, kb + g), pipeline_mode=pl.Buffered(1)),
            pl.BlockSpec((T, gw), lambda g, i: (0, vb + g), pipeline_mode=pl.Buffered(1)),
            pl.BlockSpec((nh, T, LANES), lambda g, i: (g, 0, 0), pipeline_mode=pl.Buffered(1)),
            pl.BlockSpec((nh, 1, SUBLANES, tq), lambda g, i: (g, i, 0, 0)),
        ],
        out_specs=pl.BlockSpec((tq, gw), lambda g, i: (i, g)),
        scratch_shapes=[pltpu.VMEM((nh, T // tk, HEAD_DIM, tk), BF16),
                        pltpu.VMEM((nh, tk, tq), F32), pltpu.VMEM((nh, tk, tq), F32),
                        pltpu.VMEM((nh, 1, tq), F32), pltpu.VMEM((nh, 1, tq), F32),
                        pltpu.VMEM((nh, HEAD_DIM, tq), F32)],
        compiler_params=_cparams(("arbitrary", "arbitrary")),
        name="fox_attn",
    )(proj, proj, proj, kx, row)


def _t5_table_kernel(t5_ref, o_ref):
    h = pl.program_id(0)
    W = SWA_WINDOW
    i = lax.broadcasted_iota(jnp.int32, (W, 2 * W), 0)
    j = lax.broadcasted_iota(jnp.int32, (W, 2 * W), 1)
    rel = (W + i) - j
    n = jnp.maximum(rel, 0)
    max_exact = NUM_BUCKETS // 2
    nf = jnp.maximum(n, 1).astype(F32)
    large = max_exact + (jnp.log(nf / max_exact) / math.log(T5_MAX_DISTANCE / max_exact)
                         * (NUM_BUCKETS - max_exact)).astype(jnp.int32)
    large = jnp.minimum(large, NUM_BUCKETS - 1)
    bucket = jnp.where(n < max_exact, n, large)
    bias = jnp.zeros((W, 2 * W), F32)
    for b in range(NUM_BUCKETS):
        bias = jnp.where(bucket == b, t5_ref[b, h], bias)
    in_window = (rel >= 0) & (rel < W)
    o_ref[...] = jnp.where(in_window, bias, -jnp.inf)


def t5_table(t5_bias):
    W = SWA_WINDOW
    return pl.pallas_call(
        _t5_table_kernel,
        out_shape=jax.ShapeDtypeStruct((SWA_HEADS, W, 2 * W), F32),
        grid=(SWA_HEADS,),
        in_specs=[pl.BlockSpec(memory_space=pltpu.SMEM)],
        out_specs=pl.BlockSpec((None, W, 2 * W), lambda h: (h, 0, 0)),
        compiler_params=_cparams(("arbitrary",)),
        name="t5_table",
    )(t5_bias)


def _cast_slab(w, layer, nsteps, step):
    _, rows, cols = w.shape
    slab = rows // nsteps
    assert slab * nsteps == rows and slab % BF16_ROWS == 0
    in_spec = pl.BlockSpec((None, slab, cols), lambda *ids: (layer, step(*ids), 0))
    out_spec = pl.BlockSpec((slab, cols), lambda *ids: (step(*ids), 0))
    return in_spec, jax.ShapeDtypeStruct((rows, cols), BF16), out_spec


def _swa_kernel(sink_ref, q_ref, kc_ref, kp_ref, vc_ref, vp_ref, tab_ref, wi_ref, o_ref, wo_ref, *, nsub):
    g = pl.program_id(0)
    n = pl.program_id(1)
    W = SWA_WINDOW
    wo_ref[...] = wi_ref[...].astype(wo_ref.dtype)

    first_prev = lax.broadcasted_iota(jnp.int32, (W, 2 * W), 1) < W
    for sub in range(nsub):
        if sub == 0:
            kp, vp = kp_ref[...], vp_ref[...]
        else:
            kp, vp = kc_ref[(sub - 1) * W:sub * W, :], vc_ref[(sub - 1) * W:sub * W, :]
        kk = jnp.concatenate([kp, kc_ref[sub * W:(sub + 1) * W, :]], axis=0)
        vv = jnp.concatenate([vp, vc_ref[sub * W:(sub + 1) * W, :]], axis=0)
        for hh in range(SWA_GROUP):
            q = q_ref[sub * W:(sub + 1) * W, hh * HEAD_DIM:(hh + 1) * HEAD_DIM]
            s = _dot_nt(q, kk) + tab_ref[hh]
            if sub == 0:
                s = jnp.where(jnp.logical_and(first_prev, n == 0), -jnp.inf, s)
            sink = sink_ref[0, g * SWA_GROUP + hh]
            m = jnp.maximum(jnp.max(s, axis=1, keepdims=True), sink)
            p = jnp.exp(s - m)
            l = jnp.sum(p, axis=1, keepdims=True) + jnp.exp(sink - m)
            o = _dot(p.astype(BF16), vv) / l
            o_ref[sub * W:(sub + 1) * W, hh * HEAD_DIM:(hh + 1) * HEAD_DIM] = o.astype(o_ref.dtype)


def swa_attention(proj, table, sinks, w_cast, layer, *, tb):
    T = proj.shape[0]
    W = SWA_WINDOW
    nsub = tb // W
    nt = T // tb
    gw = SWA_GROUP * HEAD_DIM
    qb, kb, vb = C_SQ // gw, C_SK // HEAD_DIM, C_SV // HEAD_DIM
    prev = lambda n: jnp.maximum(n * nsub - 1, 0)
    w_in_spec, w_shape, w_out_spec = _cast_slab(w_cast, layer, SWA_KV_HEADS * nt, lambda g, n: g * nt + n)
    return pl.pallas_call(
        functools.partial(_swa_kernel, nsub=nsub),
        out_shape=[jax.ShapeDtypeStruct((T, SWA_HEADS * HEAD_DIM), BF16), w_shape],
        grid=(SWA_KV_HEADS, nt),
        in_specs=[
            pl.BlockSpec(memory_space=pltpu.SMEM),
            pl.BlockSpec((tb, gw), lambda g, n: (n, qb + g)),
            pl.BlockSpec((tb, HEAD_DIM), lambda g, n: (n, kb + g)),
            pl.BlockSpec((W, HEAD_DIM), lambda g, n: (prev(n), kb + g)),
            pl.BlockSpec((tb, HEAD_DIM), lambda g, n: (n, vb + g)),
            pl.BlockSpec((W, HEAD_DIM), lambda g, n: (prev(n), vb + g)),
            pl.BlockSpec((SWA_GROUP, W, 2 * W), lambda g, n: (g, 0, 0)),
            w_in_spec,
        ],
        out_specs=[pl.BlockSpec((tb, gw), lambda g, n: (n, g)), w_out_spec],
        compiler_params=_cparams(("arbitrary", "arbitrary")),
        name="swa_attn",
    )(sinks.reshape(1, SWA_HEADS), proj, proj, proj, proj, proj, table, w_cast)


def _gla_kernel(q_ref, k_ref, v_ref, r_ref, ps_ref, wg_ref, bg_ref, gn_ref, wi_ref, o_ref, wo_ref,
                st_ref, oc_ref, *, tb):
    C = GLA_CHUNK
    wo_ref[...] = wi_ref[...].astype(wo_ref.dtype)

    @pl.when(pl.program_id(0) == 0)
    def _():
        st_ref[...] = jnp.zeros_like(st_ref)

    glr = ps_ref[...].astype(BF16)
    g = jax.nn.log_sigmoid(_dot(glr, wg_ref[...]) + bg_ref[...]) / GLA_TAU
    r = lax.broadcasted_iota(jnp.int32, (tb, tb), 0)
    c = lax.broadcasted_iota(jnp.int32, (tb, tb), 1)
    same = (r // C) == (c // C)
    tril = jnp.where(jnp.logical_and(same, c <= r), 1.0, 0.0).astype(BF16)
    whole = jnp.where(same, 1.0, 0.0).astype(BF16)
    gh, gm, gl = _split3(g)
    b = _dot(tril, gh) + _dot(tril, gm) + _dot(tril, gl)
    b_last = _dot(whole, gh) + _dot(whole, gm) + _dot(whole, gl)
    q_t = (q_ref[...].astype(F32) * jnp.exp(b)).astype(BF16)
    kf = k_ref[...].astype(F32)
    k_t = kf * jnp.exp(-b)
    k_end = kf * jnp.exp(b_last - b)
    decay = jnp.exp(b_last)
    lane = lax.broadcasted_iota(jnp.int32, (1, LANES), 1)
    causal = (lax.broadcasted_iota(jnp.int32, (C, C), 1) <= lax.broadcasted_iota(jnp.int32, (C, C), 0))
    for h in range(GLA_HEADS):
        pair = slice((h // 2) * LANES, (h // 2 + 1) * LANES)
        mine = (lane // GLA_DK) == (h % 2)
        ktm = jnp.where(mine, k_t[:, pair], 0.0).astype(BF16)
        kem = jnp.where(mine, k_end[:, pair], 0.0).astype(BF16)
        qh = q_t[:, pair]
        dec = decay[:, pair]
        vh = v_ref[:, h * GLA_DV:(h + 1) * GLA_DV]
        st = st_ref[h]
        for n in range(tb // C):
            rows = slice(n * C, (n + 1) * C)
            a = jnp.where(causal, _dot_nt(qh[rows], ktm[rows]), 0.0)
            o = _dot(a.astype(BF16), vh[rows]) + _dot_nt(qh[rows], st.astype(BF16))
            oc_ref[rows, h * GLA_DV:(h + 1) * GLA_DV] = o
            kv_t = _dot_tn(vh[rows], kem[rows])
            st = dec[n * C:n * C + 1, :] * st + kv_t
        st_ref[h] = st
    for h in range(GLA_HEADS):
        cols = slice(h * GLA_DV, (h + 1) * GLA_DV)
        o = _rms(oc_ref[:, cols], gn_ref[...])
        o_ref[:, cols] = (o * jax.nn.silu(r_ref[:, cols].astype(F32))).astype(o_ref.dtype)


def gla(proj, ps, wg_pad, bg, gnorm, w_cast, layer, *, tb):
    T = proj.shape[0]
    qw = GLA_HEADS * GLA_DK
    vw = GLA_HEADS * GLA_DV
    w_in_spec, w_shape, w_out_spec = _cast_slab(w_cast, layer, T // tb, lambda n: n)
    return pl.pallas_call(
        functools.partial(_gla_kernel, tb=tb),
        out_shape=[jax.ShapeDtypeStruct((T, vw), BF16), w_shape],
        grid=(T // tb,),
        in_specs=[
            pl.BlockSpec((tb, qw), lambda n: (n, C_GQ // qw)),
            pl.BlockSpec((tb, qw), lambda n: (n, C_GK // qw)),
            pl.BlockSpec((tb, vw), lambda n: (n, C_GV // vw)),
            pl.BlockSpec((tb, vw), lambda n: (n, C_GR // vw)),
            pl.BlockSpec((tb, LANES), lambda n: (n, 0)),
            pl.BlockSpec((LANES, qw), lambda n: (0, 0)),
            pl.BlockSpec((1, qw), lambda n: (0, 0)),
            pl.BlockSpec((1, GLA_DV), lambda n: (0, 0)),
            w_in_spec,
        ],
        out_specs=[pl.BlockSpec((tb, vw), lambda n: (n, 0)), w_out_spec],
        scratch_shapes=[pltpu.VMEM((GLA_HEADS, GLA_DV, LANES), F32),
                        pltpu.VMEM((tb, vw), F32)],
        compiler_params=_cparams(("arbitrary",)),
        name="gla",
    )(proj, proj, proj, proj, ps, wg_pad, bg, gnorm, w_cast)


def _mix_xattn_kernel(h_ref, of_ref, os_ref, og_ref, wo_ref, gx_ref, wq_ref, kv_ref, wx_ref, o_ref):
    nf, ns = of_ref.shape[1], os_ref.shape[1]
    mix = (_dot(of_ref[...], wo_ref[0:nf, :]) + _dot(os_ref[...], wo_ref[nf:nf + ns, :])
           + _dot(og_ref[...], wo_ref[nf + ns:, :]))
    h1 = h_ref[...] + mix
    xn = _rms(h1, gx_ref[...]).astype(BF16)
    q = (_dot(xn, wq_ref[...]) * (HEAD_DIM ** -0.5)).astype(BF16)
    xw = XATTN_HEADS * HEAD_DIM
    outs = []
    for hh in range(XATTN_HEADS):
        cols = slice(hh * HEAD_DIM, (hh + 1) * HEAD_DIM)
        k = kv_ref[:, cols]
        v = kv_ref[:, xw + hh * HEAD_DIM:xw + (hh + 1) * HEAD_DIM]
        s = _dot_nt(q[:, cols], k)
        p = jnp.exp(s - jnp.max(s, axis=1, keepdims=True))
        l = jnp.sum(p, axis=1, keepdims=True)
        outs.append((_dot(p.astype(BF16), v) / l).astype(BF16))
    o = jnp.concatenate(outs, axis=1)
    o_ref[...] = h1 + _dot(o, wx_ref[...])


def mix_xattn(h, o_fox, o_swa, o_gla, w_out, layer, gx, wq, kv, wo, *, tm):
    T, D = h.shape
    const = lambda shape: pl.BlockSpec(shape, lambda i: (0, 0))
    stacked = lambda w: pl.BlockSpec((None,) + w.shape[1:], lambda i: (layer, 0, 0))
    row = lambda w: pl.BlockSpec((tm, w), lambda i: (i, 0))
    return pl.pallas_call(
        _mix_xattn_kernel,
        out_shape=jax.ShapeDtypeStruct((T, D), F32),
        grid=(T // tm,),
        in_specs=[row(D), row(o_fox.shape[1]), row(o_swa.shape[1]), row(o_gla.shape[1]),
                  stacked(w_out), const((1, D)), stacked(wq), const(kv.shape), stacked(wo)],
        out_specs=row(D),
        compiler_params=_cparams(("arbitrary",)),
        name="mix_xattn",
    )(h, o_fox, o_swa, o_gla, w_out, gx.reshape(1, D), wq, kv, wo)


def _ffn_kernel(x_ref, g_ref, wg_ref, wv_ref, cwg_ref, cwv_ref, cbg_ref, cbv_ref, wd_ref, fg_ref,
                o_ref, xn_ref, hg_ref, hv_ref, ug_ref, uv_ref, *, tm, sub, final):
    i = pl.program_id(0)
    f = pl.program_id(1)
    H = SUBLANES
    fc = wd_ref.shape[0]

    @pl.when(f == 0)
    def _():
        x = x_ref[...]
        xn_ref[...] = _rms(x, g_ref[...]).astype(BF16)
        o_ref[...] = x

    @pl.when(i == 0)
    def _():
        hg_ref[f] = jnp.zeros(hg_ref.shape[1:], F32)
        hv_ref[f] = jnp.zeros(hv_ref.shape[1:], F32)

    xn = xn_ref[...]

    def up(c, slot):
        ug_ref[slot, H:, :] = _dot(xn, wg_ref[:, c:c + sub])
        uv_ref[slot, H:, :] = _dot(xn, wv_ref[:, c:c + sub])

    def conv(u_ref, slot, h_ref, cw_ref, cb_ref, c):
        u_ref[slot, 0:H, :] = h_ref[f, :, c:c + sub]
        h_ref[f, :, c:c + sub] = u_ref[slot, tm:tm + H, :]
        return (cb_ref[:, c:c + sub] + cw_ref[2:3, c:c + sub] * u_ref[slot, H:H + tm, :]
                + cw_ref[1:2, c:c + sub] * u_ref[slot, H - 1:H - 1 + tm, :]
                + cw_ref[0:1, c:c + sub] * u_ref[slot, H - 2:H - 2 + tm, :])

    def down(c, slot):
        gate = conv(ug_ref, slot, hg_ref, cwg_ref, cbg_ref, c)
        val = conv(uv_ref, slot, hv_ref, cwv_ref, cbv_ref, c)
        act = (jax.nn.silu(gate) * val).astype(BF16)
        o_ref[...] += _dot(act, wd_ref[c:c + sub, :])

    up(0, 0)
    for n, c in enumerate(range(0, fc, sub)):
        if c + sub < fc:
            up(c + sub, (n + 1) % 2)
        down(c, n % 2)

    if final:
        @pl.when(f == pl.num_programs(1) - 1)
        def _():
            o_ref[...] = _rms(o_ref[...], fg_ref[...])


def conv_ffn(x, g, w_up, conv_w, conv_b, w_down, layer, final_g, *, tm, fc, final):
    T, D = x.shape
    dff = w_down.shape[0]
    L = conv_w.shape[0]
    nf = dff // fc
    cb = conv_b.reshape(L, 1, 2 * dff)
    return pl.pallas_call(
        functools.partial(_ffn_kernel, tm=tm, sub=MXU_COLS, final=final),
        out_shape=jax.ShapeDtypeStruct((T, D), F32),
        grid=(T // tm, nf),
        in_specs=[
            pl.BlockSpec((tm, D), lambda i, f: (i, 0)),
            pl.BlockSpec((1, D), lambda i, f: (0, 0)),
            pl.BlockSpec((D, fc), lambda i, f: (0, f)),
            pl.BlockSpec((D, fc), lambda i, f: (0, nf + f)),
            pl.BlockSpec((None, CONV_WIDTH, fc), lambda i, f: (layer, 0, f)),
            pl.BlockSpec((None, CONV_WIDTH, fc), lambda i, f: (layer, 0, nf + f)),
            pl.BlockSpec((None, 1, fc), lambda i, f: (layer, 0, f)),
            pl.BlockSpec((None, 1, fc), lambda i, f: (layer, 0, nf + f)),
            pl.BlockSpec((fc, D), lambda i, f: (f, 0)),
            pl.BlockSpec((1, D), lambda i, f: (0, 0)),
        ],
        out_specs=pl.BlockSpec((tm, D), lambda i, f: (i, 0)),
        scratch_shapes=[pltpu.VMEM((tm, D), BF16), pltpu.VMEM((nf, SUBLANES, fc), F32),
                        pltpu.VMEM((nf, SUBLANES, fc), F32),
                        pltpu.VMEM((2, tm + SUBLANES, MXU_COLS), F32),
                        pltpu.VMEM((2, tm + SUBLANES, MXU_COLS), F32)],
        compiler_params=_cparams(("arbitrary", "arbitrary")),
        name="conv_ffn",
    )(x, g.reshape(1, D), w_up, w_up, conv_w, conv_w, cb, cb, w_down, final_g.reshape(1, D))


TR_PREP, TC_PREP = 512, 512
TM_PROJ, TN_PROJ = 1024, 1536
TB_GATE = 512
TQ_FOX, TK_FOX, NH_FOX = 512, 512, 4
TB_SWA = 512
TB_GLA = 512
TM_MIX = 512
TM_FFN, FC_FFN = 512, 512


def _main_colscale():
    cs = np.ones((1, MAIN_COLS), np.float32)
    cs[0, C_FQ:C_FK] = HEAD_DIM ** -0.5 * LOG2E
    cs[0, C_SQ:C_SK] = HEAD_DIM ** -0.5
    cs[0, C_GQ:C_GK] = GLA_DK ** -0.5
    return jnp.asarray(cs)


def kernel(x, mem, w_in, b_fox_f, swa_sinks, t5_bias, w_gla_gate, b_gla_gate, gla_norm, w_mix_out, norm_mix,
           norm_xattn, norm_mem, wq_x, wkv_x, wo_x, norm_ffn, w_up, conv_w, conv_b, w_down, final_norm):
    depth = w_in.shape[0]
    _, T, D = x.shape
    M = mem.shape[1]
    h = x.reshape(T, D)
    memf = mem.reshape(M, D)
    colscale = _main_colscale()
    ones_kv = jnp.ones((1, wkv_x.shape[2]), F32)
    table = t5_table(t5_bias)
    pad_lanes = LANES - S_GLR - GLA_GATE_RANK
    w_main, w_small = prep_w_in(w_in, tr=TR_PREP, tc=TC_PREP)
    w_out_b, wq_b, wkv_b, wo_b = (w.astype(BF16) for w in (w_mix_out, wq_x, wkv_x, wo_x))
    for l in range(depth):
        bvec = jnp.concatenate([b_fox_f[l], jnp.zeros((LANES - FOX_HEADS,), F32)]).reshape(1, LANES)
        wg_pad = jnp.concatenate([jnp.zeros((S_GLR, GLA_HEADS * GLA_DK), F32), w_gla_gate[l],
                                  jnp.zeros((pad_lanes, GLA_HEADS * GLA_DK), F32)], axis=0).astype(BF16)

        proj, ps = norm_proj(h, norm_mix[l], w_main, l, colscale, w_small, tm=TM_PROJ, tn=TN_PROJ, w_t=True)
        kx, row = fox_gate(ps, bvec, tb=TB_GATE)
        o_fox = fox_attention(proj, kx, row, tq=TQ_FOX, tk=TK_FOX, nh=NH_FOX)
        o_swa, w_down_b = swa_attention(proj, table, swa_sinks[l], w_down, l, tb=TB_SWA)
        o_gla, w_up_b = gla(proj, ps, wg_pad, b_gla_gate[l].reshape(1, -1), gla_norm[l].reshape(1, -1),
                            w_up, l, tb=TB_GLA)
        kv = norm_proj(memf, norm_mem[l], wkv_b, l, ones_kv, tm=M, tn=wkv_x.shape[2])
        h = mix_xattn(h, o_fox, o_swa, o_gla, w_out_b, l, norm_xattn[l], wq_b, kv, wo_b, tm=TM_MIX)
        h = conv_ffn(h, norm_ffn[l], w_up_b, conv_w, conv_b, w_down_b, l, final_norm,
                     tm=TM_FFN, fc=FC_FFN, final=(l == depth - 1))
    return h.reshape(x.shape)
```

```python
import functools
import math

import numpy as np
import jax
import jax.numpy as jnp
from jax import lax
from jax.experimental import pallas as pl
from jax.experimental.pallas import tpu as pltpu

F32 = jnp.float32
BF16 = jnp.bfloat16

HEAD_DIM = 128
FOX_HEADS = 4
SWA_HEADS = 8
SWA_KV_HEADS = 2
SWA_GROUP = SWA_HEADS // SWA_KV_HEADS
GLA_HEADS = 4
GLA_DK = 64
GLA_DV = 128
GLA_GATE_RANK = 16
GLA_TAU = 16.0
GLA_CHUNK = 64
SWA_WINDOW = 128
NUM_BUCKETS = 32
T5_MAX_DISTANCE = 128
XATTN_HEADS = 4
CONV_WIDTH = 3
EPS = 1e-6
LOG2E = math.log2(math.e)

LANES = 128
SUBLANES = 8
BF16_ROWS = 16
MXU_COLS = 256
VMEM_LIMIT = 56 * 1024 * 1024

_FOX_W = FOX_HEADS * HEAD_DIM
_OFF_FF = 3 * _FOX_W
_OFF_SQ = _OFF_FF + FOX_HEADS
_MAIN_B = (SWA_HEADS + 2 * SWA_KV_HEADS) * HEAD_DIM + 2 * GLA_HEADS * GLA_DK + 2 * GLA_HEADS * GLA_DV
_OFF_GLR = _OFF_SQ + _MAIN_B
C_FQ, C_FK, C_FV = 0, 512, 1024
C_SQ, C_SK, C_SV = 1536, 2560, 2816
C_GQ, C_GK, C_GV, C_GR = 3072, 3328, 3584, 4096
MAIN_COLS = 4608
S_FF, S_GLR = 0, SUBLANES


def _cparams(sem, flags=None):
    return pltpu.CompilerParams(dimension_semantics=sem, vmem_limit_bytes=VMEM_LIMIT, flags=flags)


def _rms(x, g):
    return x * lax.rsqrt(jnp.mean(x * x, axis=-1, keepdims=True) + EPS) * g


N_SPLIT = 3


def _split3(x):
    hi = x.astype(BF16)
    r1 = x - hi.astype(F32)
    mid = r1.astype(BF16)
    lo = (r1 - mid.astype(F32)).astype(BF16)
    return hi, mid, lo


def _dot(a, b):
    return jnp.dot(a, b, preferred_element_type=F32)


def _dot_nt(a, b):
    return lax.dot_general(a, b, (((1,), (1,)), ((), ())), preferred_element_type=F32)


def _dot_tn(a, b):
    return lax.dot_general(a, b, (((0,), (0,)), ((), ())), preferred_element_type=F32)


def _sel_dot(sel, x):
    hi, mid, lo = _split3(x)
    return _dot(sel, hi) + _dot(sel, mid) + _dot(sel, lo)


def _prep_w_in_kernel(cur_ref, ff_ref, glr_ref, wm_ref, ws_ref):
    L = wm_ref.shape[0]
    for l in range(L):
        wm_ref[l] = cur_ref[:, l, :].astype(BF16)

    @pl.when(pl.program_id(1) == 0)
    def _():
        pad = jnp.zeros((LANES - S_GLR - GLA_GATE_RANK, ws_ref.shape[2]), F32)
        for l in range(L):
            ws_ref[l] = jnp.concatenate([ff_ref[:, l, :], glr_ref[:, l, :], pad], axis=0).astype(BF16)


def prep_w_in(w_in, *, tr, tc):
    L, D, _ = w_in.shape
    wt = jnp.transpose(w_in, (2, 0, 1))
    assert _OFF_FF % tc == 0 and MAIN_COLS % tc == 0 and S_GLR % SUBLANES == 0

    def src(j):
        return j * tc + jnp.where(j * tc >= _OFF_FF, _OFF_SQ - _OFF_FF, 0)

    return pl.pallas_call(
        _prep_w_in_kernel,
        out_shape=[jax.ShapeDtypeStruct((L, MAIN_COLS, D), BF16),
                   jax.ShapeDtypeStruct((L, LANES, D), BF16)],
        grid=(D // tr, MAIN_COLS // tc),
        in_specs=[
            pl.BlockSpec((pl.Element(tc), pl.Element(L), pl.Element(tr)), lambda r, j: (src(j), 0, r * tr)),
            pl.BlockSpec((pl.Element(S_GLR), pl.Element(L), pl.Element(tr)), lambda r, j: (_OFF_FF, 0, r * tr)),
            pl.BlockSpec((pl.Element(GLA_GATE_RANK), pl.Element(L), pl.Element(tr)),
                         lambda r, j: (_OFF_GLR, 0, r * tr)),
        ],
        out_specs=[pl.BlockSpec((L, tc, tr), lambda r, j: (0, j, r)),
                   pl.BlockSpec((L, LANES, tr), lambda r, j: (0, 0, r))],
        compiler_params=_cparams(("arbitrary", "arbitrary")),
        name="prep_w_in",
    )(wt, wt, wt)


def _norm_proj_kernel(*refs, has_small, w_t):
    if has_small:
        x_ref, g_ref, w_ref, cs_ref, ws_ref, o_ref, os_ref, xn_ref = refs
    else:
        x_ref, g_ref, w_ref, cs_ref, o_ref, xn_ref = refs
    dot = _dot_nt if w_t else _dot

    @pl.when(pl.program_id(1) == 0)
    def _():
        xn = _rms(x_ref[...], g_ref[...]).astype(BF16)
        xn_ref[...] = xn
        if has_small:
            os_ref[...] = dot(xn, ws_ref[...])

    o_ref[...] = (dot(xn_ref[...], w_ref[...]) * cs_ref[...]).astype(o_ref.dtype)


def norm_proj(x, g, w, layer, colscale, w_small=None, *, tm, tn, w_t=False):
    T, D = x.shape
    N = w.shape[1] if w_t else w.shape[2]
    has_small = w_small is not None

    def wspec(n):
        if w_t:
            return pl.BlockSpec((None, n, D), lambda i, j: (layer, j, 0))
        return pl.BlockSpec((None, D, n), lambda i, j: (layer, 0, j))

    in_specs = [
        pl.BlockSpec((tm, D), lambda i, j: (i, 0)),
        pl.BlockSpec((1, D), lambda i, j: (0, 0)),
        wspec(tn),
        pl.BlockSpec((1, tn), lambda i, j: (0, j)),
    ]
    args = [x, g.reshape(1, D), w, colscale]
    out_shape = [jax.ShapeDtypeStruct((T, N), BF16)]
    out_specs = [pl.BlockSpec((tm, tn), lambda i, j: (i, j))]
    if has_small:
        in_specs.append(pl.BlockSpec((None,) + w_small.shape[1:], lambda i, j: (layer, 0, 0)))
        args.append(w_small)
        out_shape.append(jax.ShapeDtypeStruct((T, LANES), F32))
        out_specs.append(pl.BlockSpec((tm, LANES), lambda i, j: (i, 0)))
    outs = pl.pallas_call(
        functools.partial(_norm_proj_kernel, has_small=has_small, w_t=w_t),
        out_shape=out_shape,
        grid=(T // tm, N // tn),
        in_specs=in_specs,
        out_specs=out_specs,
        scratch_shapes=[pltpu.VMEM((tm, D), BF16)],
        compiler_params=_cparams(("arbitrary", "arbitrary")),
        name="norm_proj_small" if has_small else "norm_proj",
    )(*args)
    return outs if has_small else outs[0]


def _fox_gate_kernel(ps_ref, b_ref, kx_ref, row_ref, carry_ref, *, tb):
    @pl.when(pl.program_id(0) == 0)
    def _():
        carry_ref[...] = jnp.zeros_like(carry_ref)

    x = jax.nn.log_sigmoid(ps_ref[...] + b_ref[...])
    r = lax.broadcasted_iota(jnp.int32, (tb, tb), 0)
    c = lax.broadcasted_iota(jnp.int32, (tb, tb), 1)
    tril = jnp.where(c <= r, 1.0, 0.0).astype(BF16)
    csum = _sel_dot(tril, x) + carry_ref[0:1, :]
    carry_ref[...] = jnp.broadcast_to(csum[tb - 1:tb, :], carry_ref.shape)
    pieces = _split3(csum * LOG2E)
    neg = [-p for p in pieces]
    src = lax.broadcasted_iota(jnp.int32, (LANES, FOX_HEADS * LANES), 0)
    dst = lax.broadcasted_iota(jnp.int32, (LANES, FOX_HEADS * LANES), 1)
    kx = sum(_dot(neg[n], jnp.where((dst // LANES == src - S_FF) & (dst % LANES == n), 1.0, 0.0).astype(BF16))
             for n in range(len(neg))).astype(BF16)
    sub = lax.broadcasted_iota(jnp.int32, (FOX_HEADS * SUBLANES, LANES), 0)
    lane = lax.broadcasted_iota(jnp.int32, (FOX_HEADS * SUBLANES, LANES), 1)
    pick = jnp.where(lane - S_FF == sub // SUBLANES, 1.0, 0.0).astype(BF16)
    rows = sum(_dot_nt(pick, p) for p in pieces)
    for h in range(FOX_HEADS):
        kx_ref[h] = kx[:, h * LANES:(h + 1) * LANES]
        row_ref[h, 0] = rows[h * SUBLANES:(h + 1) * SUBLANES, :]


def fox_gate(ps, bvec, *, tb):
    T = ps.shape[0]
    nb = T // tb
    return pl.pallas_call(
        functools.partial(_fox_gate_kernel, tb=tb),
        out_shape=[jax.ShapeDtypeStruct((FOX_HEADS, T, LANES), BF16),
                   jax.ShapeDtypeStruct((FOX_HEADS, nb, SUBLANES, tb), F32)],
        grid=(nb,),
        in_specs=[pl.BlockSpec((tb, LANES), lambda i: (i, 0)),
                  pl.BlockSpec((1, LANES), lambda i: (0, 0))],
        out_specs=[pl.BlockSpec((FOX_HEADS, tb, LANES), lambda i: (0, i, 0)),
                   pl.BlockSpec((FOX_HEADS, 1, SUBLANES, tb), lambda i: (0, i, 0, 0))],
        scratch_shapes=[pltpu.VMEM((SUBLANES, LANES), F32)],
        compiler_params=_cparams(("arbitrary",)),
        name="fox_gate",
    )(ps, bvec)


def _fox_attn_kernel(q_ref, k_ref, v_ref, kx_ref, cq_ref, wi_ref, o_ref, wo_ref,
                     vt_ref, ya_ref, yb_ref, m_ref, l_ref, acc_ref, *, tq, tk, nk, nh):
    i = pl.program_id(1)
    D = HEAD_DIM
    heads = range(nh)
    n_full = i // (tk // tq)
    _cast_along([wi_ref], [wo_ref])

    @pl.when(i == 0)
    def _():
        for hh in heads:
            for n in range(nk):
                vt_ref[hh, n] = v_ref[n * tk:(n + 1) * tk, hh * D:(hh + 1) * D].T

    lane = lax.broadcasted_iota(jnp.int32, (tq, LANES), 1)
    ones = jnp.where(lane < N_SPLIT, 1.0, 0.0).astype(BF16)
    q_aug = [jnp.concatenate([q_ref[:, hh * D:(hh + 1) * D], ones], axis=1) for hh in heads]
    cq = [cq_ref[hh, 0, 0:1, :] for hh in heads]
    m_ref[...] = jnp.full(m_ref.shape, -jnp.inf, F32)
    l_ref[...] = jnp.zeros(l_ref.shape, F32)
    acc_ref[...] = jnp.zeros(acc_ref.shape, F32)

    def scores(hh, j):
        start = pl.multiple_of(j * tk, tk)
        k_aug = jnp.concatenate([k_ref[pl.ds(start, tk), hh * D:(hh + 1) * D], kx_ref[hh, pl.ds(start, tk), :]],
                                axis=1)
        return _dot_nt(k_aug, q_aug[hh])

    def update(hh, j, y):
        m_old = m_ref[hh]
        m_new = jnp.maximum(m_old, jnp.max(y, axis=0, keepdims=True) + cq[hh])
        alpha = jnp.exp2(m_old - m_new)
        p = jnp.exp2(y + (cq[hh] - m_new))
        l_ref[hh] = alpha * l_ref[hh] + jnp.sum(p, axis=0, keepdims=True)
        acc_ref[hh] = alpha * acc_ref[hh] + _dot(vt_ref[hh, j], p.astype(BF16))
        m_ref[hh] = m_new

    for hh in heads:
        ya_ref[hh] = scores(hh, 0)

    def step(j, cur_ref, nxt_ref):
        for hh in heads:
            nxt_ref[hh] = scores(hh, j + 1)
        for hh in heads:
            update(hh, j, cur_ref[hh])

    def body(j, carry):
        pl.when(j % 2 == 0)(lambda: step(j, ya_ref, yb_ref))
        pl.when(j % 2 == 1)(lambda: step(j, yb_ref, ya_ref))
        return carry

    lax.fori_loop(0, n_full, body, 0)
    key = n_full * tk + lax.broadcasted_iota(jnp.int32, (tk, tq), 0)
    qry = i * tq + lax.broadcasted_iota(jnp.int32, (tk, tq), 1)

    def diagonal(cur_ref):
        for hh in heads:
            update(hh, n_full, jnp.where(key <= qry, cur_ref[hh], -jnp.inf))

    pl.when(n_full % 2 == 0)(lambda: diagonal(ya_ref))
    pl.when(n_full % 2 == 1)(lambda: diagonal(yb_ref))
    for hh in heads:
        o_ref[:, hh * D:(hh + 1) * D] = (acc_ref[hh] / l_ref[hh]).T.astype(o_ref.dtype)


def fox_attention(proj, kx, row, w_cast, layer, *, tq, tk, nh):
    T = proj.shape[0]
    nq = T // tq
    ng = FOX_HEADS // nh
    gw = nh * HEAD_DIM
    qb, kb, vb = C_FQ // gw, C_FK // gw, C_FV // gw
    assert tk % tq == 0 and T % tk == 0
    (w_in_spec,), (w_shape,), (w_out_spec,) = _cast_slabs([w_cast], layer, ng * nq, lambda g, i: g * nq + i)
    return pl.pallas_call(
        functools.partial(_fox_attn_kernel, tq=tq, tk=tk, nk=T // tk, nh=nh),
        out_shape=[jax.ShapeDtypeStruct((T, FOX_HEADS * HEAD_DIM), BF16), w_shape],
        grid=(ng, nq),
        in_specs=[
            pl.BlockSpec((tq, gw), lambda g, i: (i, qb + g)),
            pl.BlockSpec((T, gw), lambda g, i: (0, kb + g), pipeline_mode=pl.Buffered(1)),
            pl.BlockSpec((T, gw), lambda g, i: (0, vb + g), pipeline_mode=pl.Buffered(1)),
            pl.BlockSpec((nh, T, LANES), lambda g, i: (g, 0, 0), pipeline_mode=pl.Buffered(1)),
            pl.BlockSpec((nh, 1, SUBLANES, tq), lambda g, i: (g, i, 0, 0)),
            w_in_spec,
        ],
        out_specs=[pl.BlockSpec((tq, gw), lambda g, i: (i, g)), w_out_spec],
        scratch_shapes=[pltpu.VMEM((nh, T // tk, HEAD_DIM, tk), BF16),
                        pltpu.VMEM((nh, tk, tq), F32), pltpu.VMEM((nh, tk, tq), F32),
                        pltpu.VMEM((nh, 1, tq), F32), pltpu.VMEM((nh, 1, tq), F32),
                        pltpu.VMEM((nh, HEAD_DIM, tq), F32)],
        compiler_params=_cparams(("arbitrary", "arbitrary")),
        name="fox_attn",
    )(proj, proj, proj, kx, row, w_cast)


def _t5_table_kernel(t5_ref, o_ref):
    h = pl.program_id(0)
    W = SWA_WINDOW
    i = lax.broadcasted_iota(jnp.int32, (W, 2 * W), 0)
    j = lax.broadcasted_iota(jnp.int32, (W, 2 * W), 1)
    rel = (W + i) - j
    n = jnp.maximum(rel, 0)
    max_exact = NUM_BUCKETS // 2
    nf = jnp.maximum(n, 1).astype(F32)
    large = max_exact + (jnp.log(nf / max_exact) / math.log(T5_MAX_DISTANCE / max_exact)
                         * (NUM_BUCKETS - max_exact)).astype(jnp.int32)
    large = jnp.minimum(large, NUM_BUCKETS - 1)
    bucket = jnp.where(n < max_exact, n, large)
    bias = jnp.zeros((W, 2 * W), F32)
    for b in range(NUM_BUCKETS):
        bias = jnp.where(bucket == b, t5_ref[b, h], bias)
    in_window = (rel >= 0) & (rel < W)
    o_ref[...] = jnp.where(in_window, bias, -jnp.inf)


def t5_table(t5_bias):
    W = SWA_WINDOW
    return pl.pallas_call(
        _t5_table_kernel,
        out_shape=jax.ShapeDtypeStruct((SWA_HEADS, W, 2 * W), F32),
        grid=(SWA_HEADS,),
        in_specs=[pl.BlockSpec(memory_space=pltpu.SMEM)],
        out_specs=pl.BlockSpec((None, W, 2 * W), lambda h: (h, 0, 0)),
        compiler_params=_cparams(("arbitrary",)),
        name="t5_table",
    )(t5_bias)


def _cast_slabs(ws, layer, nsteps, step):
    in_specs, shapes, out_specs = [], [], []
    for w in ws:
        _, rows, cols = w.shape
        slab = rows // nsteps
        assert slab * nsteps == rows and slab % BF16_ROWS == 0
        in_specs.append(pl.BlockSpec((None, slab, cols), lambda *ids: (layer, step(*ids), 0)))
        out_specs.append(pl.BlockSpec((slab, cols), lambda *ids: (step(*ids), 0)))
        shapes.append(jax.ShapeDtypeStruct((rows, cols), BF16))
    return in_specs, shapes, out_specs


def _cast_along(wi_refs, wo_refs):
    for wi_ref, wo_ref in zip(wi_refs, wo_refs, strict=True):
        wo_ref[...] = wi_ref[...].astype(wo_ref.dtype)


def _swa_kernel(sink_ref, q_ref, kc_ref, kp_ref, vc_ref, vp_ref, tab_ref, *refs, nsub, ncast):
    wi_refs, o_ref, wo_refs = refs[:ncast], refs[ncast], refs[ncast + 1:]
    g = pl.program_id(0)
    n = pl.program_id(1)
    W = SWA_WINDOW
    _cast_along(wi_refs, wo_refs)

    first_prev = lax.broadcasted_iota(jnp.int32, (W, 2 * W), 1) < W
    for sub in range(nsub):
        if sub == 0:
            kp, vp = kp_ref[...], vp_ref[...]
        else:
            kp, vp = kc_ref[(sub - 1) * W:sub * W, :], vc_ref[(sub - 1) * W:sub * W, :]
        kk = jnp.concatenate([kp, kc_ref[sub * W:(sub + 1) * W, :]], axis=0)
        vv = jnp.concatenate([vp, vc_ref[sub * W:(sub + 1) * W, :]], axis=0)
        for hh in range(SWA_GROUP):
            q = q_ref[sub * W:(sub + 1) * W, hh * HEAD_DIM:(hh + 1) * HEAD_DIM]
            s = _dot_nt(q, kk) + tab_ref[hh]
            if sub == 0:
                s = jnp.where(jnp.logical_and(first_prev, n == 0), -jnp.inf, s)
            sink = sink_ref[0, g * SWA_GROUP + hh]
            m = jnp.maximum(jnp.max(s, axis=1, keepdims=True), sink)
            p = jnp.exp(s - m)
            l = jnp.sum(p, axis=1, keepdims=True) + jnp.exp(sink - m)
            o = _dot(p.astype(BF16), vv) / l
            o_ref[sub * W:(sub + 1) * W, hh * HEAD_DIM:(hh + 1) * HEAD_DIM] = o.astype(o_ref.dtype)


def swa_attention(proj, table, sinks, w_cast, layer, *, tb):
    T = proj.shape[0]
    W = SWA_WINDOW
    nsub = tb // W
    nt = T // tb
    gw = SWA_GROUP * HEAD_DIM
    qb, kb, vb = C_SQ // gw, C_SK // HEAD_DIM, C_SV // HEAD_DIM
    prev = lambda n: jnp.maximum(n * nsub - 1, 0)
    w_in_specs, w_shapes, w_out_specs = _cast_slabs(w_cast, layer, SWA_KV_HEADS * nt, lambda g, n: g * nt + n)
    o, *w_bf16 = pl.pallas_call(
        functools.partial(_swa_kernel, nsub=nsub, ncast=len(w_cast)),
        out_shape=[jax.ShapeDtypeStruct((T, SWA_HEADS * HEAD_DIM), BF16), *w_shapes],
        grid=(SWA_KV_HEADS, nt),
        in_specs=[
            pl.BlockSpec(memory_space=pltpu.SMEM),
            pl.BlockSpec((tb, gw), lambda g, n: (n, qb + g)),
            pl.BlockSpec((tb, HEAD_DIM), lambda g, n: (n, kb + g)),
            pl.BlockSpec((W, HEAD_DIM), lambda g, n: (prev(n), kb + g)),
            pl.BlockSpec((tb, HEAD_DIM), lambda g, n: (n, vb + g)),
            pl.BlockSpec((W, HEAD_DIM), lambda g, n: (prev(n), vb + g)),
            pl.BlockSpec((SWA_GROUP, W, 2 * W), lambda g, n: (g, 0, 0)),
            *w_in_specs,
        ],
        out_specs=[pl.BlockSpec((tb, gw), lambda g, n: (n, g)), *w_out_specs],
        compiler_params=_cparams(("arbitrary", "arbitrary")),
        name="swa_attn",
    )(sinks.reshape(1, SWA_HEADS), proj, proj, proj, proj, proj, table, *w_cast)
    return o, w_bf16


def _gla_kernel(q_ref, k_ref, v_ref, r_ref, ps_ref, wg_ref, bg_ref, gn_ref, wi_ref, o_ref, wo_ref,
                st_ref, oc_ref, *, tb):
    C = GLA_CHUNK
    _cast_along([wi_ref], [wo_ref])

    @pl.when(pl.program_id(0) == 0)
    def _():
        st_ref[...] = jnp.zeros_like(st_ref)

    glr = ps_ref[...].astype(BF16)
    g = jax.nn.log_sigmoid(_dot(glr, wg_ref[...]) + bg_ref[...]) / GLA_TAU
    r = lax.broadcasted_iota(jnp.int32, (tb, tb), 0)
    c = lax.broadcasted_iota(jnp.int32, (tb, tb), 1)
    same = (r // C) == (c // C)
    tril = jnp.where(jnp.logical_and(same, c <= r), 1.0, 0.0).astype(BF16)
    whole = jnp.where(same, 1.0, 0.0).astype(BF16)
    gh, gm, gl = _split3(g)
    b = _dot(tril, gh) + _dot(tril, gm) + _dot(tril, gl)
    b_last = _dot(whole, gh) + _dot(whole, gm) + _dot(whole, gl)
    q_t = (q_ref[...].astype(F32) * jnp.exp(b)).astype(BF16)
    kf = k_ref[...].astype(F32)
    k_t = kf * jnp.exp(-b)
    k_end = kf * jnp.exp(b_last - b)
    decay = jnp.exp(b_last)
    lane = lax.broadcasted_iota(jnp.int32, (1, LANES), 1)
    causal = (lax.broadcasted_iota(jnp.int32, (C, C), 1) <= lax.broadcasted_iota(jnp.int32, (C, C), 0))
    for h in range(GLA_HEADS):
        pair = slice((h // 2) * LANES, (h // 2 + 1) * LANES)
        mine = (lane // GLA_DK) == (h % 2)
        ktm = jnp.where(mine, k_t[:, pair], 0.0).astype(BF16)
        kem = jnp.where(mine, k_end[:, pair], 0.0).astype(BF16)
        qh = q_t[:, pair]
        dec = decay[:, pair]
        vh = v_ref[:, h * GLA_DV:(h + 1) * GLA_DV]
        st = st_ref[h]
        for n in range(tb // C):
            rows = slice(n * C, (n + 1) * C)
            a = jnp.where(causal, _dot_nt(qh[rows], ktm[rows]), 0.0)
            o = _dot(a.astype(BF16), vh[rows]) + _dot_nt(qh[rows], st.astype(BF16))
            oc_ref[rows, h * GLA_DV:(h + 1) * GLA_DV] = o
            kv_t = _dot_tn(vh[rows], kem[rows])
            st = dec[n * C:n * C + 1, :] * st + kv_t
        st_ref[h] = st
    for h in range(GLA_HEADS):
        cols = slice(h * GLA_DV, (h + 1) * GLA_DV)
        o = _rms(oc_ref[:, cols], gn_ref[...])
        o_ref[:, cols] = (o * jax.nn.silu(r_ref[:, cols].astype(F32))).astype(o_ref.dtype)


def gla(proj, ps, wg_pad, bg, gnorm, w_cast, layer, *, tb):
    T = proj.shape[0]
    qw = GLA_HEADS * GLA_DK
    vw = GLA_HEADS * GLA_DV
    (w_in_spec,), (w_shape,), (w_out_spec,) = _cast_slabs([w_cast], layer, T // tb, lambda n: n)
    return pl.pallas_call(
        functools.partial(_gla_kernel, tb=tb),
        out_shape=[jax.ShapeDtypeStruct((T, vw), BF16), w_shape],
        grid=(T // tb,),
        in_specs=[
            pl.BlockSpec((tb, qw), lambda n: (n, C_GQ // qw)),
            pl.BlockSpec((tb, qw), lambda n: (n, C_GK // qw)),
            pl.BlockSpec((tb, vw), lambda n: (n, C_GV // vw)),
            pl.BlockSpec((tb, vw), lambda n: (n, C_GR // vw)),
            pl.BlockSpec((tb, LANES), lambda n: (n, 0)),
            pl.BlockSpec((LANES, qw), lambda n: (0, 0)),
            pl.BlockSpec((1, qw), lambda n: (0, 0)),
            pl.BlockSpec((1, GLA_DV), lambda n: (0, 0)),
            w_in_spec,
        ],
        out_specs=[pl.BlockSpec((tb, vw), lambda n: (n, 0)), w_out_spec],
        scratch_shapes=[pltpu.VMEM((GLA_HEADS, GLA_DV, LANES), F32),
                        pltpu.VMEM((tb, vw), F32)],
        compiler_params=_cparams(("arbitrary",)),
        name="gla",
    )(proj, proj, proj, proj, ps, wg_pad, bg, gnorm, w_cast)


def _mix_xattn_kernel(h_ref, of_ref, os_ref, og_ref, wo_ref, gx_ref, wq_ref, kv_ref, wx_ref, o_ref):
    nf, ns = of_ref.shape[1], os_ref.shape[1]
    mix = (_dot(of_ref[...], wo_ref[0:nf, :]) + _dot(os_ref[...], wo_ref[nf:nf + ns, :])
           + _dot(og_ref[...], wo_ref[nf + ns:, :]))
    h1 = h_ref[...] + mix
    xn = _rms(h1, gx_ref[...]).astype(BF16)
    q = (_dot(xn, wq_ref[...]) * (HEAD_DIM ** -0.5)).astype(BF16)
    xw = XATTN_HEADS * HEAD_DIM
    outs = []
    for hh in range(XATTN_HEADS):
        cols = slice(hh * HEAD_DIM, (hh + 1) * HEAD_DIM)
        k = kv_ref[:, cols]
        v = kv_ref[:, xw + hh * HEAD_DIM:xw + (hh + 1) * HEAD_DIM]
        s = _dot_nt(q[:, cols], k)
        p = jnp.exp(s - jnp.max(s, axis=1, keepdims=True))
        l = jnp.sum(p, axis=1, keepdims=True)
        outs.append((_dot(p.astype(BF16), v) / l).astype(BF16))
    o = jnp.concatenate(outs, axis=1)
    o_ref[...] = h1 + _dot(o, wx_ref[...])


def mix_xattn(h, o_fox, o_swa, o_gla, w_out, gx, wq, kv, wo, *, tm):
    T, D = h.shape
    const = lambda shape: pl.BlockSpec(shape, lambda i: (0, 0))
    row = lambda w: pl.BlockSpec((tm, w), lambda i: (i, 0))
    return pl.pallas_call(
        _mix_xattn_kernel,
        out_shape=jax.ShapeDtypeStruct((T, D), F32),
        grid=(T // tm,),
        in_specs=[row(D), row(o_fox.shape[1]), row(o_swa.shape[1]), row(o_gla.shape[1]),
                  const(w_out.shape), const((1, D)), const(wq.shape), const(kv.shape), const(wo.shape)],
        out_specs=row(D),
        compiler_params=_cparams(("arbitrary",)),
        name="mix_xattn",
    )(h, o_fox, o_swa, o_gla, w_out, gx.reshape(1, D), wq, kv, wo)


def _ffn_kernel(x_ref, g_ref, wg_ref, wv_ref, cwg_ref, cwv_ref, cbg_ref, cbv_ref, wd_ref, fg_ref,
                o_ref, xn_ref, hg_ref, hv_ref, ug_ref, uv_ref, *, tm, sub, final):
    i = pl.program_id(0)
    f = pl.program_id(1)
    H = SUBLANES
    fc = wd_ref.shape[0]

    @pl.when(f == 0)
    def _():
        x = x_ref[...]
        xn_ref[...] = _rms(x, g_ref[...]).astype(BF16)
        o_ref[...] = x

    @pl.when(i == 0)
    def _():
        hg_ref[f] = jnp.zeros(hg_ref.shape[1:], F32)
        hv_ref[f] = jnp.zeros(hv_ref.shape[1:], F32)

    xn = xn_ref[...]

    def up(c, slot):
        ug_ref[slot, H:, :] = _dot(xn, wg_ref[:, c:c + sub])
        uv_ref[slot, H:, :] = _dot(xn, wv_ref[:, c:c + sub])

    def conv(u_ref, slot, h_ref, cw_ref, cb_ref, c):
        u_ref[slot, 0:H, :] = h_ref[f, :, c:c + sub]
        h_ref[f, :, c:c + sub] = u_ref[slot, tm:tm + H, :]
        return (cb_ref[:, c:c + sub] + cw_ref[2:3, c:c + sub] * u_ref[slot, H:H + tm, :]
                + cw_ref[1:2, c:c + sub] * u_ref[slot, H - 1:H - 1 + tm, :]
                + cw_ref[0:1, c:c + sub] * u_ref[slot, H - 2:H - 2 + tm, :])

    def down(c, slot):
        gate = conv(ug_ref, slot, hg_ref, cwg_ref, cbg_ref, c)
        val = conv(uv_ref, slot, hv_ref, cwv_ref, cbv_ref, c)
        act = (jax.nn.silu(gate) * val).astype(BF16)
        o_ref[...] += _dot(act, wd_ref[c:c + sub, :])

    up(0, 0)
    for n, c in enumerate(range(0, fc, sub)):
        if c + sub < fc:
            up(c + sub, (n + 1) % 2)
        down(c, n % 2)

    if final:
        @pl.when(f == pl.num_programs(1) - 1)
        def _():
            o_ref[...] = _rms(o_ref[...], fg_ref[...])


def conv_ffn(x, g, w_up, conv_w, conv_b, w_down, layer, final_g, *, tm, fc, final):
    T, D = x.shape
    dff = w_down.shape[0]
    L = conv_w.shape[0]
    nf = dff // fc
    cb = conv_b.reshape(L, 1, 2 * dff)
    return pl.pallas_call(
        functools.partial(_ffn_kernel, tm=tm, sub=MXU_COLS, final=final),
        out_shape=jax.ShapeDtypeStruct((T, D), F32),
        grid=(T // tm, nf),
        in_specs=[
            pl.BlockSpec((tm, D), lambda i, f: (i, 0)),
            pl.BlockSpec((1, D), lambda i, f: (0, 0)),
            pl.BlockSpec((D, fc), lambda i, f: (0, f)),
            pl.BlockSpec((D, fc), lambda i, f: (0, nf + f)),
            pl.BlockSpec((None, CONV_WIDTH, fc), lambda i, f: (layer, 0, f)),
            pl.BlockSpec((None, CONV_WIDTH, fc), lambda i, f: (layer, 0, nf + f)),
            pl.BlockSpec((None, 1, fc), lambda i, f: (layer, 0, f)),
            pl.BlockSpec((None, 1, fc), lambda i, f: (layer, 0, nf + f)),
            pl.BlockSpec((fc, D), lambda i, f: (f, 0)),
            pl.BlockSpec((1, D), lambda i, f: (0, 0)),
        ],
        out_specs=pl.BlockSpec((tm, D), lambda i, f: (i, 0)),
        scratch_shapes=[pltpu.VMEM((tm, D), BF16), pltpu.VMEM((nf, SUBLANES, fc), F32),
                        pltpu.VMEM((nf, SUBLANES, fc), F32),
                        pltpu.VMEM((2, tm + SUBLANES, MXU_COLS), F32),
                        pltpu.VMEM((2, tm + SUBLANES, MXU_COLS), F32)],
        compiler_params=_cparams(("arbitrary", "arbitrary")),
        name="conv_ffn",
    )(x, g.reshape(1, D), w_up, w_up, conv_w, conv_w, cb, cb, w_down, final_g.reshape(1, D))


TR_PREP, TC_PREP = 512, 512
TM_PROJ, TN_PROJ = 1024, 1536
TB_GATE = 512
TQ_FOX, TK_FOX, NH_FOX = 512, 512, 4
TB_SWA = 512
TB_GLA = 512
TM_MIX = 512
TM_FFN, FC_FFN = 512, 512


def _main_colscale():
    cs = np.ones((1, MAIN_COLS), np.float32)
    cs[0, C_FQ:C_FK] = HEAD_DIM ** -0.5 * LOG2E
    cs[0, C_SQ:C_SK] = HEAD_DIM ** -0.5
    cs[0, C_GQ:C_GK] = GLA_DK ** -0.5
    return jnp.asarray(cs)


def kernel(x, mem, w_in, b_fox_f, swa_sinks, t5_bias, w_gla_gate, b_gla_gate, gla_norm, w_mix_out, norm_mix,
           norm_xattn, norm_mem, wq_x, wkv_x, wo_x, norm_ffn, w_up, conv_w, conv_b, w_down, final_norm):
    depth = w_in.shape[0]
    _, T, D = x.shape
    M = mem.shape[1]
    h = x.reshape(T, D)
    memf = mem.reshape(M, D)
    colscale = _main_colscale()
    ones_kv = jnp.ones((1, wkv_x.shape[2]), F32)
    table = t5_table(t5_bias)
    pad_lanes = LANES - S_GLR - GLA_GATE_RANK
    w_main, w_small = prep_w_in(w_in, tr=TR_PREP, tc=TC_PREP)
    for l in range(depth):
        bvec = jnp.concatenate([b_fox_f[l], jnp.zeros((LANES - FOX_HEADS,), F32)]).reshape(1, LANES)
        wg_pad = jnp.concatenate([jnp.zeros((S_GLR, GLA_HEADS * GLA_DK), F32), w_gla_gate[l],
                                  jnp.zeros((pad_lanes, GLA_HEADS * GLA_DK), F32)], axis=0).astype(BF16)

        proj, ps = norm_proj(h, norm_mix[l], w_main, l, colscale, w_small, tm=TM_PROJ, tn=TN_PROJ, w_t=True)
        kx, row = fox_gate(ps, bvec, tb=TB_GATE)
        o_fox, w_down_b = fox_attention(proj, kx, row, w_down, l, tq=TQ_FOX, tk=TK_FOX, nh=NH_FOX)
        o_swa, (w_out_b, wq_b, wkv_b, wo_b) = swa_attention(proj, table, swa_sinks[l],
                                                           (w_mix_out, wq_x, wkv_x, wo_x), l, tb=TB_SWA)
        o_gla, w_up_b = gla(proj, ps, wg_pad, b_gla_gate[l].reshape(1, -1), gla_norm[l].reshape(1, -1),
                            w_up, l, tb=TB_GLA)
        kv = norm_proj(memf, norm_mem[l], wkv_b[None], 0, ones_kv, tm=M, tn=wkv_x.shape[2])
        h = mix_xattn(h, o_fox, o_swa, o_gla, w_out_b, norm_xattn[l], wq_b, kv, wo_b, tm=TM_MIX)
        h = conv_ffn(h, norm_ffn[l], w_up_b, conv_w, conv_b, w_down_b, l, final_norm,
                     tm=TM_FFN, fc=FC_FFN, final=(l == depth - 1))
    return h.reshape(x.shape)
```

```python
import functools
import math

import numpy as np
import jax
import jax.numpy as jnp
from jax import lax
from jax.experimental import pallas as pl
from jax.experimental.pallas import tpu as pltpu

F32 = jnp.float32
BF16 = jnp.bfloat16

HEAD_DIM = 128
FOX_HEADS = 4
SWA_HEADS = 8
SWA_KV_HEADS = 2
SWA_GROUP = SWA_HEADS // SWA_KV_HEADS
GLA_HEADS = 4
GLA_DK = 64
GLA_DV = 128
GLA_GATE_RANK = 16
GLA_TAU = 16.0
GLA_CHUNK = 64
SWA_WINDOW = 128
NUM_BUCKETS = 32
T5_MAX_DISTANCE = 128
XATTN_HEADS = 4
CONV_WIDTH = 3
EPS = 1e-6
LOG2E = math.log2(math.e)

LANES = 128
SUBLANES = 8
BF16_ROWS = 16
MXU_COLS = 256
VMEM_LIMIT = 56 * 1024 * 1024

_FOX_W = FOX_HEADS * HEAD_DIM
_OFF_FF = 3 * _FOX_W
_OFF_SQ = _OFF_FF + FOX_HEADS
_MAIN_B = (SWA_HEADS + 2 * SWA_KV_HEADS) * HEAD_DIM + 2 * GLA_HEADS * GLA_DK + 2 * GLA_HEADS * GLA_DV
_OFF_GLR = _OFF_SQ + _MAIN_B
C_FQ, C_FK, C_FV = 0, 512, 1024
C_SQ, C_SK, C_SV = 1536, 2560, 2816
C_GQ, C_GK, C_GV, C_GR = 3072, 3328, 3584, 4096
MAIN_COLS = 4608
S_FF, S_GLR = 0, SUBLANES


def _cparams(sem, flags=None):
    return pltpu.CompilerParams(dimension_semantics=sem, vmem_limit_bytes=VMEM_LIMIT, flags=flags)


def _rms(x, g):
    return x * lax.rsqrt(jnp.mean(x * x, axis=-1, keepdims=True) + EPS) * g


N_SPLIT = 3


def _split3(x):
    hi = x.astype(BF16)
    r1 = x - hi.astype(F32)
    mid = r1.astype(BF16)
    lo = (r1 - mid.astype(F32)).astype(BF16)
    return hi, mid, lo


def _dot(a, b):
    return jnp.dot(a, b, preferred_element_type=F32)


def _dot_nt(a, b):
    return lax.dot_general(a, b, (((1,), (1,)), ((), ())), preferred_element_type=F32)


def _dot_tn(a, b):
    return lax.dot_general(a, b, (((0,), (0,)), ((), ())), preferred_element_type=F32)


def _sel_dot(sel, x):
    hi, mid, lo = _split3(x)
    return _dot(sel, hi) + _dot(sel, mid) + _dot(sel, lo)


def _prep_w_in_kernel(cur_ref, ff_ref, glr_ref, wm_ref, ws_ref):
    L = wm_ref.shape[0]
    wm_ref[...] = pltpu.einshape("cld->lcd", cur_ref[...]).astype(BF16)

    @pl.when(pl.program_id(1) == 0)
    def _():
        pad = jnp.zeros((LANES - S_GLR - GLA_GATE_RANK, ws_ref.shape[2]), F32)
        for l in range(L):
            ws_ref[l] = jnp.concatenate([ff_ref[:, l, :], glr_ref[:, l, :], pad], axis=0).astype(BF16)


def prep_w_in(w_in, *, tr, tc):
    L, D, _ = w_in.shape
    wt = jnp.transpose(w_in, (2, 0, 1))
    assert _OFF_FF % tc == 0 and MAIN_COLS % tc == 0 and S_GLR % SUBLANES == 0

    def src(j):
        return j * tc + jnp.where(j * tc >= _OFF_FF, _OFF_SQ - _OFF_FF, 0)

    return pl.pallas_call(
        _prep_w_in_kernel,
        out_shape=[jax.ShapeDtypeStruct((L, MAIN_COLS, D), BF16),
                   jax.ShapeDtypeStruct((L, LANES, D), BF16)],
        grid=(D // tr, MAIN_COLS // tc),
        in_specs=[
            pl.BlockSpec((pl.Element(tc), pl.Element(L), pl.Element(tr)), lambda r, j: (src(j), 0, r * tr)),
            pl.BlockSpec((pl.Element(S_GLR), pl.Element(L), pl.Element(tr)), lambda r, j: (_OFF_FF, 0, r * tr)),
            pl.BlockSpec((pl.Element(GLA_GATE_RANK), pl.Element(L), pl.Element(tr)),
                         lambda r, j: (_OFF_GLR, 0, r * tr)),
        ],
        out_specs=[pl.BlockSpec((L, tc, tr), lambda r, j: (0, j, r)),
                   pl.BlockSpec((L, LANES, tr), lambda r, j: (0, 0, r))],
        compiler_params=_cparams(("arbitrary", "arbitrary")),
        name="prep_w_in",
    )(wt, wt, wt)


def _norm_proj_kernel(*refs, has_small, w_t):
    if has_small:
        x_ref, g_ref, w_ref, cs_ref, ws_ref, o_ref, os_ref, xn_ref = refs
    else:
        x_ref, g_ref, w_ref, cs_ref, o_ref, xn_ref = refs
    dot = _dot_nt if w_t else _dot

    @pl.when(pl.program_id(1) == 0)
    def _():
        xn = _rms(x_ref[...], g_ref[...]).astype(BF16)
        xn_ref[...] = xn
        if has_small:
            os_ref[...] = dot(xn, ws_ref[...])

    o_ref[...] = (dot(xn_ref[...], w_ref[...]) * cs_ref[...]).astype(o_ref.dtype)


def norm_proj(x, g, w, layer, colscale, w_small=None, *, tm, tn, w_t=False):
    T, D = x.shape
    N = w.shape[1] if w_t else w.shape[2]
    has_small = w_small is not None

    def wspec(n):
        if w_t:
            return pl.BlockSpec((None, n, D), lambda i, j: (layer, j, 0))
        return pl.BlockSpec((None, D, n), lambda i, j: (layer, 0, j))

    in_specs = [
        pl.BlockSpec((tm, D), lambda i, j: (i, 0)),
        pl.BlockSpec((1, D), lambda i, j: (0, 0)),
        wspec(tn),
        pl.BlockSpec((1, tn), lambda i, j: (0, j)),
    ]
    args = [x, g.reshape(1, D), w, colscale]
    out_shape = [jax.ShapeDtypeStruct((T, N), BF16)]
    out_specs = [pl.BlockSpec((tm, tn), lambda i, j: (i, j))]
    if has_small:
        in_specs.append(pl.BlockSpec((None,) + w_small.shape[1:], lambda i, j: (layer, 0, 0)))
        args.append(w_small)
        out_shape.append(jax.ShapeDtypeStruct((T, LANES), F32))
        out_specs.append(pl.BlockSpec((tm, LANES), lambda i, j: (i, 0)))
    outs = pl.pallas_call(
        functools.partial(_norm_proj_kernel, has_small=has_small, w_t=w_t),
        out_shape=out_shape,
        grid=(T // tm, N // tn),
        in_specs=in_specs,
        out_specs=out_specs,
        scratch_shapes=[pltpu.VMEM((tm, D), BF16)],
        compiler_params=_cparams(("arbitrary", "arbitrary")),
        name="norm_proj_small" if has_small else "norm_proj",
    )(*args)
    return outs if has_small else outs[0]


def _fox_gate_kernel(ps_ref, b_ref, kx_ref, row_ref, carry_ref, *, tb):
    @pl.when(pl.program_id(0) == 0)
    def _():
        carry_ref[...] = jnp.zeros_like(carry_ref)

    x = jax.nn.log_sigmoid(ps_ref[...] + b_ref[...])
    r = lax.broadcasted_iota(jnp.int32, (tb, tb), 0)
    c = lax.broadcasted_iota(jnp.int32, (tb, tb), 1)
    tril = jnp.where(c <= r, 1.0, 0.0).astype(BF16)
    csum = _sel_dot(tril, x) + carry_ref[0:1, :]
    carry_ref[...] = jnp.broadcast_to(csum[tb - 1:tb, :], carry_ref.shape)
    pieces = _split3(csum * LOG2E)
    neg = [-p for p in pieces]
    src = lax.broadcasted_iota(jnp.int32, (LANES, FOX_HEADS * LANES), 0)
    dst = lax.broadcasted_iota(jnp.int32, (LANES, FOX_HEADS * LANES), 1)
    kx = sum(_dot(neg[n], jnp.where((dst // LANES == src - S_FF) & (dst % LANES == n), 1.0, 0.0).astype(BF16))
             for n in range(len(neg))).astype(BF16)
    sub = lax.broadcasted_iota(jnp.int32, (FOX_HEADS * SUBLANES, LANES), 0)
    lane = lax.broadcasted_iota(jnp.int32, (FOX_HEADS * SUBLANES, LANES), 1)
    pick = jnp.where(lane - S_FF == sub // SUBLANES, 1.0, 0.0).astype(BF16)
    rows = sum(_dot_nt(pick, p) for p in pieces)
    for h in range(FOX_HEADS):
        kx_ref[h] = kx[:, h * LANES:(h + 1) * LANES]
        row_ref[h, 0] = rows[h * SUBLANES:(h + 1) * SUBLANES, :]


def fox_gate(ps, bvec, *, tb):
    T = ps.shape[0]
    nb = T // tb
    return pl.pallas_call(
        functools.partial(_fox_gate_kernel, tb=tb),
        out_shape=[jax.ShapeDtypeStruct((FOX_HEADS, T, LANES), BF16),
                   jax.ShapeDtypeStruct((FOX_HEADS, nb, SUBLANES, tb), F32)],
        grid=(nb,),
        in_specs=[pl.BlockSpec((tb, LANES), lambda i: (i, 0)),
                  pl.BlockSpec((1, LANES), lambda i: (0, 0))],
        out_specs=[pl.BlockSpec((FOX_HEADS, tb, LANES), lambda i: (0, i, 0)),
                   pl.BlockSpec((FOX_HEADS, 1, SUBLANES, tb), lambda i: (0, i, 0, 0))],
        scratch_shapes=[pltpu.VMEM((SUBLANES, LANES), F32)],
        compiler_params=_cparams(("arbitrary",)),
        name="fox_gate",
    )(ps, bvec)


def _fox_attn_kernel(q_ref, k_ref, v_ref, kx_ref, cq_ref, wi_ref, o_ref, wo_ref,
                     vt_ref, ya_ref, yb_ref, m_ref, l_ref, acc_ref, *, tq, tk, nk, nh):
    i = pl.program_id(1)
    D = HEAD_DIM
    heads = range(nh)
    n_full = i // (tk // tq)
    _cast_along([wi_ref], [wo_ref])

    @pl.when(i == 0)
    def _():
        for hh in heads:
            for n in range(nk):
                vt_ref[hh, n] = v_ref[n * tk:(n + 1) * tk, hh * D:(hh + 1) * D].T

    lane = lax.broadcasted_iota(jnp.int32, (tq, LANES), 1)
    ones = jnp.where(lane < N_SPLIT, 1.0, 0.0).astype(BF16)
    q_aug = [jnp.concatenate([q_ref[:, hh * D:(hh + 1) * D], ones], axis=1) for hh in heads]
    cq = [cq_ref[hh, 0, 0:1, :] for hh in heads]
    m_ref[...] = jnp.full(m_ref.shape, -jnp.inf, F32)
    l_ref[...] = jnp.zeros(l_ref.shape, F32)
    acc_ref[...] = jnp.zeros(acc_ref.shape, F32)

    def scores(hh, j):
        start = pl.multiple_of(j * tk, tk)
        k_aug = jnp.concatenate([k_ref[pl.ds(start, tk), hh * D:(hh + 1) * D], kx_ref[hh, pl.ds(start, tk), :]],
                                axis=1)
        return _dot_nt(k_aug, q_aug[hh])

    def update(hh, j, y):
        m_old = m_ref[hh]
        m_new = jnp.maximum(m_old, jnp.max(y, axis=0, keepdims=True) + cq[hh])
        alpha = jnp.exp2(m_old - m_new)
        p = jnp.exp2(y + (cq[hh] - m_new))
        l_ref[hh] = alpha * l_ref[hh] + jnp.sum(p, axis=0, keepdims=True)
        acc_ref[hh] = alpha * acc_ref[hh] + _dot(vt_ref[hh, j], p.astype(BF16))
        m_ref[hh] = m_new

    for hh in heads:
        ya_ref[hh] = scores(hh, 0)

    def step(j, cur_ref, nxt_ref):
        for hh in heads:
            nxt_ref[hh] = scores(hh, j + 1)
        for hh in heads:
            update(hh, j, cur_ref[hh])

    def body(j, carry):
        pl.when(j % 2 == 0)(lambda: step(j, ya_ref, yb_ref))
        pl.when(j % 2 == 1)(lambda: step(j, yb_ref, ya_ref))
        return carry

    lax.fori_loop(0, n_full, body, 0)
    key = n_full * tk + lax.broadcasted_iota(jnp.int32, (tk, tq), 0)
    qry = i * tq + lax.broadcasted_iota(jnp.int32, (tk, tq), 1)

    def diagonal(cur_ref):
        for hh in heads:
            update(hh, n_full, jnp.where(key <= qry, cur_ref[hh], -jnp.inf))

    pl.when(n_full % 2 == 0)(lambda: diagonal(ya_ref))
    pl.when(n_full % 2 == 1)(lambda: diagonal(yb_ref))
    for hh in heads:
        o_ref[:, hh * D:(hh + 1) * D] = (acc_ref[hh] / l_ref[hh]).T.astype(o_ref.dtype)


def fox_attention(proj, kx, row, w_cast, layer, *, tq, tk, nh):
    T = proj.shape[0]
    nq = T // tq
    ng = FOX_HEADS // nh
    gw = nh * HEAD_DIM
    qb, kb, vb = C_FQ // gw, C_FK // gw, C_FV // gw
    assert tk % tq == 0 and T % tk == 0
    (w_in_spec,), (w_shape,), (w_out_spec,) = _cast_slabs([w_cast], layer, ng * nq, lambda g, i: g * nq + i)
    return pl.pallas_call(
        functools.partial(_fox_attn_kernel, tq=tq, tk=tk, nk=T // tk, nh=nh),
        out_shape=[jax.ShapeDtypeStruct((T, FOX_HEADS * HEAD_DIM), BF16), w_shape],
        grid=(ng, nq),
        in_specs=[
            pl.BlockSpec((tq, gw), lambda g, i: (i, qb + g)),
            pl.BlockSpec((T, gw), lambda g, i: (0, kb + g), pipeline_mode=pl.Buffered(1)),
            pl.BlockSpec((T, gw), lambda g, i: (0, vb + g), pipeline_mode=pl.Buffered(1)),
            pl.BlockSpec((nh, T, LANES), lambda g, i: (g, 0, 0), pipeline_mode=pl.Buffered(1)),
            pl.BlockSpec((nh, 1, SUBLANES, tq), lambda g, i: (g, i, 0, 0)),
            w_in_spec,
        ],
        out_specs=[pl.BlockSpec((tq, gw), lambda g, i: (i, g)), w_out_spec],
        scratch_shapes=[pltpu.VMEM((nh, T // tk, HEAD_DIM, tk), BF16),
                        pltpu.VMEM((nh, tk, tq), F32), pltpu.VMEM((nh, tk, tq), F32),
                        pltpu.VMEM((nh, 1, tq), F32), pltpu.VMEM((nh, 1, tq), F32),
                        pltpu.VMEM((nh, HEAD_DIM, tq), F32)],
        compiler_params=_cparams(("arbitrary", "arbitrary")),
        name="fox_attn",
    )(proj, proj, proj, kx, row, w_cast)


def _t5_table_kernel(t5_ref, o_ref):
    h = pl.program_id(0)
    W = SWA_WINDOW
    i = lax.broadcasted_iota(jnp.int32, (W, 2 * W), 0)
    j = lax.broadcasted_iota(jnp.int32, (W, 2 * W), 1)
    rel = (W + i) - j
    n = jnp.maximum(rel, 0)
    max_exact = NUM_BUCKETS // 2
    nf = jnp.maximum(n, 1).astype(F32)
    large = max_exact + (jnp.log(nf / max_exact) / math.log(T5_MAX_DISTANCE / max_exact)
                         * (NUM_BUCKETS - max_exact)).astype(jnp.int32)
    large = jnp.minimum(large, NUM_BUCKETS - 1)
    bucket = jnp.where(n < max_exact, n, large)
    bias = jnp.zeros((W, 2 * W), F32)
    for b in range(NUM_BUCKETS):
        bias = jnp.where(bucket == b, t5_ref[b, h], bias)
    in_window = (rel >= 0) & (rel < W)
    o_ref[...] = jnp.where(in_window, bias, -jnp.inf)


def t5_table(t5_bias):
    W = SWA_WINDOW
    return pl.pallas_call(
        _t5_table_kernel,
        out_shape=jax.ShapeDtypeStruct((SWA_HEADS, W, 2 * W), F32),
        grid=(SWA_HEADS,),
        in_specs=[pl.BlockSpec(memory_space=pltpu.SMEM)],
        out_specs=pl.BlockSpec((None, W, 2 * W), lambda h: (h, 0, 0)),
        compiler_params=_cparams(("arbitrary",)),
        name="t5_table",
    )(t5_bias)


def _cast_slabs(ws, layer, nsteps, step):
    in_specs, shapes, out_specs = [], [], []
    for w in ws:
        _, rows, cols = w.shape
        slab = rows // nsteps
        assert slab * nsteps == rows and slab % BF16_ROWS == 0
        in_specs.append(pl.BlockSpec((None, slab, cols), lambda *ids: (layer, step(*ids), 0)))
        out_specs.append(pl.BlockSpec((slab, cols), lambda *ids: (step(*ids), 0)))
        shapes.append(jax.ShapeDtypeStruct((rows, cols), BF16))
    return in_specs, shapes, out_specs


def _cast_along(wi_refs, wo_refs):
    for wi_ref, wo_ref in zip(wi_refs, wo_refs, strict=True):
        wo_ref[...] = wi_ref[...].astype(wo_ref.dtype)


def _swa_kernel(sink_ref, q_ref, kc_ref, kp_ref, vc_ref, vp_ref, tab_ref, *refs, nsub, ncast):
    wi_refs, o_ref, wo_refs = refs[:ncast], refs[ncast], refs[ncast + 1:]
    g = pl.program_id(0)
    n = pl.program_id(1)
    W = SWA_WINDOW
    _cast_along(wi_refs, wo_refs)

    first_prev = lax.broadcasted_iota(jnp.int32, (W, 2 * W), 1) < W
    for sub in range(nsub):
        if sub == 0:
            kp, vp = kp_ref[...], vp_ref[...]
        else:
            kp, vp = kc_ref[(sub - 1) * W:sub * W, :], vc_ref[(sub - 1) * W:sub * W, :]
        kk = jnp.concatenate([kp, kc_ref[sub * W:(sub + 1) * W, :]], axis=0)
        vv = jnp.concatenate([vp, vc_ref[sub * W:(sub + 1) * W, :]], axis=0)
        for hh in range(SWA_GROUP):
            q = q_ref[sub * W:(sub + 1) * W, hh * HEAD_DIM:(hh + 1) * HEAD_DIM]
            s = _dot_nt(q, kk) + tab_ref[hh]
            if sub == 0:
                s = jnp.where(jnp.logical_and(first_prev, n == 0), -jnp.inf, s)
            sink = sink_ref[0, g * SWA_GROUP + hh]
            m = jnp.maximum(jnp.max(s, axis=1, keepdims=True), sink)
            p = jnp.exp(s - m)
            l = jnp.sum(p, axis=1, keepdims=True) + jnp.exp(sink - m)
            o = _dot(p.astype(BF16), vv) / l
            o_ref[sub * W:(sub + 1) * W, hh * HEAD_DIM:(hh + 1) * HEAD_DIM] = o.astype(o_ref.dtype)


def swa_attention(proj, table, sinks, w_cast, layer, *, tb):
    T = proj.shape[0]
    W = SWA_WINDOW
    nsub = tb // W
    nt = T // tb
    gw = SWA_GROUP * HEAD_DIM
    qb, kb, vb = C_SQ // gw, C_SK // HEAD_DIM, C_SV // HEAD_DIM
    prev = lambda n: jnp.maximum(n * nsub - 1, 0)
    w_in_specs, w_shapes, w_out_specs = _cast_slabs(w_cast, layer, SWA_KV_HEADS * nt, lambda g, n: g * nt + n)
    o, *w_bf16 = pl.pallas_call(
        functools.partial(_swa_kernel, nsub=nsub, ncast=len(w_cast)),
        out_shape=[jax.ShapeDtypeStruct((T, SWA_HEADS * HEAD_DIM), BF16), *w_shapes],
        grid=(SWA_KV_HEADS, nt),
        in_specs=[
            pl.BlockSpec(memory_space=pltpu.SMEM),
            pl.BlockSpec((tb, gw), lambda g, n: (n, qb + g)),
            pl.BlockSpec((tb, HEAD_DIM), lambda g, n: (n, kb + g)),
            pl.BlockSpec((W, HEAD_DIM), lambda g, n: (prev(n), kb + g)),
            pl.BlockSpec((tb, HEAD_DIM), lambda g, n: (n, vb + g)),
            pl.BlockSpec((W, HEAD_DIM), lambda g, n: (prev(n), vb + g)),
            pl.BlockSpec((SWA_GROUP, W, 2 * W), lambda g, n: (g, 0, 0)),
            *w_in_specs,
        ],
        out_specs=[pl.BlockSpec((tb, gw), lambda g, n: (n, g)), *w_out_specs],
        compiler_params=_cparams(("arbitrary", "arbitrary")),
        name="swa_attn",
    )(sinks.reshape(1, SWA_HEADS), proj, proj, proj, proj, proj, table, *w_cast)
    return o, w_bf16


def _gla_kernel(q_ref, k_ref, v_ref, r_ref, ps_ref, wg_ref, bg_ref, gn_ref, wi_ref, o_ref, wo_ref,
                st_ref, oc_ref, *, tb):
    C = GLA_CHUNK
    _cast_along([wi_ref], [wo_ref])

    @pl.when(pl.program_id(0) == 0)
    def _():
        st_ref[...] = jnp.zeros_like(st_ref)

    glr = ps_ref[...].astype(BF16)
    g = jax.nn.log_sigmoid(_dot(glr, wg_ref[...]) + bg_ref[...]) / GLA_TAU
    r = lax.broadcasted_iota(jnp.int32, (tb, tb), 0)
    c = lax.broadcasted_iota(jnp.int32, (tb, tb), 1)
    same = (r // C) == (c // C)
    tril = jnp.where(jnp.logical_and(same, c <= r), 1.0, 0.0).astype(BF16)
    whole = jnp.where(same, 1.0, 0.0).astype(BF16)
    gh, gm, gl = _split3(g)
    b = _dot(tril, gh) + _dot(tril, gm) + _dot(tril, gl)
    b_last = _dot(whole, gh) + _dot(whole, gm) + _dot(whole, gl)
    q_t = (q_ref[...].astype(F32) * jnp.exp(b)).astype(BF16)
    kf = k_ref[...].astype(F32)
    k_t = kf * jnp.exp(-b)
    k_end = kf * jnp.exp(b_last - b)
    decay = jnp.exp(b_last)
    lane = lax.broadcasted_iota(jnp.int32, (1, LANES), 1)
    causal = (lax.broadcasted_iota(jnp.int32, (C, C), 1) <= lax.broadcasted_iota(jnp.int32, (C, C), 0))
    for h in range(GLA_HEADS):
        pair = slice((h // 2) * LANES, (h // 2 + 1) * LANES)
        mine = (lane // GLA_DK) == (h % 2)
        ktm = jnp.where(mine, k_t[:, pair], 0.0).astype(BF16)
        kem = jnp.where(mine, k_end[:, pair], 0.0).astype(BF16)
        qh = q_t[:, pair]
        dec = decay[:, pair]
        vh = v_ref[:, h * GLA_DV:(h + 1) * GLA_DV]
        st = st_ref[h]
        for n in range(tb // C):
            rows = slice(n * C, (n + 1) * C)
            a = jnp.where(causal, _dot_nt(qh[rows], ktm[rows]), 0.0)
            o = _dot(a.astype(BF16), vh[rows]) + _dot_nt(qh[rows], st.astype(BF16))
            oc_ref[rows, h * GLA_DV:(h + 1) * GLA_DV] = o
            kv_t = _dot_tn(vh[rows], kem[rows])
            st = dec[n * C:n * C + 1, :] * st + kv_t
        st_ref[h] = st
    for h in range(GLA_HEADS):
        cols = slice(h * GLA_DV, (h + 1) * GLA_DV)
        o = _rms(oc_ref[:, cols], gn_ref[...])
        o_ref[:, cols] = (o * jax.nn.silu(r_ref[:, cols].astype(F32))).astype(o_ref.dtype)


def gla(proj, ps, wg_pad, bg, gnorm, w_cast, layer, *, tb):
    T = proj.shape[0]
    qw = GLA_HEADS * GLA_DK
    vw = GLA_HEADS * GLA_DV
    (w_in_spec,), (w_shape,), (w_out_spec,) = _cast_slabs([w_cast], layer, T // tb, lambda n: n)
    return pl.pallas_call(
        functools.partial(_gla_kernel, tb=tb),
        out_shape=[jax.ShapeDtypeStruct((T, vw), BF16), w_shape],
        grid=(T // tb,),
        in_specs=[
            pl.BlockSpec((tb, qw), lambda n: (n, C_GQ // qw)),
            pl.BlockSpec((tb, qw), lambda n: (n, C_GK // qw)),
            pl.BlockSpec((tb, vw), lambda n: (n, C_GV // vw)),
            pl.BlockSpec((tb, vw), lambda n: (n, C_GR // vw)),
            pl.BlockSpec((tb, LANES), lambda n: (n, 0)),
            pl.BlockSpec((LANES, qw), lambda n: (0, 0)),
            pl.BlockSpec((1, qw), lambda n: (0, 0)),
            pl.BlockSpec((1, GLA_DV), lambda n: (0, 0)),
            w_in_spec,
        ],
        out_specs=[pl.BlockSpec((tb, vw), lambda n: (n, 0)), w_out_spec],
        scratch_shapes=[pltpu.VMEM((GLA_HEADS, GLA_DV, LANES), F32),
                        pltpu.VMEM((tb, vw), F32)],
        compiler_params=_cparams(("arbitrary",)),
        name="gla",
    )(proj, proj, proj, proj, ps, wg_pad, bg, gnorm, w_cast)


def _mix_xattn_kernel(h_ref, of_ref, os_ref, og_ref, wo_ref, gx_ref, wq_ref, kv_ref, wx_ref, o_ref):
    nf, ns = of_ref.shape[1], os_ref.shape[1]
    mix = (_dot(of_ref[...], wo_ref[0:nf, :]) + _dot(os_ref[...], wo_ref[nf:nf + ns, :])
           + _dot(og_ref[...], wo_ref[nf + ns:, :]))
    h1 = h_ref[...] + mix
    xn = _rms(h1, gx_ref[...]).astype(BF16)
    q = (_dot(xn, wq_ref[...]) * (HEAD_DIM ** -0.5)).astype(BF16)
    xw = XATTN_HEADS * HEAD_DIM
    outs = []
    for hh in range(XATTN_HEADS):
        cols = slice(hh * HEAD_DIM, (hh + 1) * HEAD_DIM)
        k = kv_ref[:, cols]
        v = kv_ref[:, xw + hh * HEAD_DIM:xw + (hh + 1) * HEAD_DIM]
        s = _dot_nt(q[:, cols], k)
        p = jnp.exp(s - jnp.max(s, axis=1, keepdims=True))
        l = jnp.sum(p, axis=1, keepdims=True)
        outs.append((_dot(p.astype(BF16), v) / l).astype(BF16))
    o = jnp.concatenate(outs, axis=1)
    o_ref[...] = h1 + _dot(o, wx_ref[...])


def mix_xattn(h, o_fox, o_swa, o_gla, w_out, gx, wq, kv, wo, *, tm):
    T, D = h.shape
    const = lambda shape: pl.BlockSpec(shape, lambda i: (0, 0))
    row = lambda w: pl.BlockSpec((tm, w), lambda i: (i, 0))
    return pl.pallas_call(
        _mix_xattn_kernel,
        out_shape=jax.ShapeDtypeStruct((T, D), F32),
        grid=(T // tm,),
        in_specs=[row(D), row(o_fox.shape[1]), row(o_swa.shape[1]), row(o_gla.shape[1]),
                  const(w_out.shape), const((1, D)), const(wq.shape), const(kv.shape), const(wo.shape)],
        out_specs=row(D),
        compiler_params=_cparams(("arbitrary",)),
        name="mix_xattn",
    )(h, o_fox, o_swa, o_gla, w_out, gx.reshape(1, D), wq, kv, wo)


def _ffn_kernel(x_ref, g_ref, wg_ref, wv_ref, cwg_ref, cwv_ref, cbg_ref, cbv_ref, wd_ref, fg_ref,
                o_ref, xn_ref, hg_ref, hv_ref, ug_ref, uv_ref, *, tm, sub, final):
    i = pl.program_id(0)
    f = pl.program_id(1)
    H = SUBLANES
    fc = wd_ref.shape[0]

    @pl.when(f == 0)
    def _():
        x = x_ref[...]
        xn_ref[...] = _rms(x, g_ref[...]).astype(BF16)
        o_ref[...] = x

    @pl.when(i == 0)
    def _():
        hg_ref[f] = jnp.zeros(hg_ref.shape[1:], F32)
        hv_ref[f] = jnp.zeros(hv_ref.shape[1:], F32)

    xn = xn_ref[...]

    def up(c, slot):
        ug_ref[slot, H:, :] = _dot(xn, wg_ref[:, c:c + sub])
        uv_ref[slot, H:, :] = _dot(xn, wv_ref[:, c:c + sub])

    def conv(u_ref, slot, h_ref, cw_ref, cb_ref, c):
        u_ref[slot, 0:H, :] = h_ref[f, :, c:c + sub]
        h_ref[f, :, c:c + sub] = u_ref[slot, tm:tm + H, :]
        return (cb_ref[:, c:c + sub] + cw_ref[2:3, c:c + sub] * u_ref[slot, H:H + tm, :]
                + cw_ref[1:2, c:c + sub] * u_ref[slot, H - 1:H - 1 + tm, :]
                + cw_ref[0:1, c:c + sub] * u_ref[slot, H - 2:H - 2 + tm, :])

    def down(c, slot):
        gate = conv(ug_ref, slot, hg_ref, cwg_ref, cbg_ref, c)
        val = conv(uv_ref, slot, hv_ref, cwv_ref, cbv_ref, c)
        act = (jax.nn.silu(gate) * val).astype(BF16)
        o_ref[...] += _dot(act, wd_ref[c:c + sub, :])

    up(0, 0)
    for n, c in enumerate(range(0, fc, sub)):
        if c + sub < fc:
            up(c + sub, (n + 1) % 2)
        down(c, n % 2)

    if final:
        @pl.when(f == pl.num_programs(1) - 1)
        def _():
            o_ref[...] = _rms(o_ref[...], fg_ref[...])


def conv_ffn(x, g, w_up, conv_w, conv_b, w_down, layer, final_g, *, tm, fc, final):
    T, D = x.shape
    dff = w_down.shape[0]
    L = conv_w.shape[0]
    nf = dff // fc
    cb = conv_b.reshape(L, 1, 2 * dff)
    return pl.pallas_call(
        functools.partial(_ffn_kernel, tm=tm, sub=MXU_COLS, final=final),
        out_shape=jax.ShapeDtypeStruct((T, D), F32),
        grid=(T // tm, nf),
        in_specs=[
            pl.BlockSpec((tm, D), lambda i, f: (i, 0)),
            pl.BlockSpec((1, D), lambda i, f: (0, 0)),
            pl.BlockSpec((D, fc), lambda i, f: (0, f)),
            pl.BlockSpec((D, fc), lambda i, f: (0, nf + f)),
            pl.BlockSpec((None, CONV_WIDTH, fc), lambda i, f: (layer, 0, f)),
            pl.BlockSpec((None, CONV_WIDTH, fc), lambda i, f: (layer, 0, nf + f)),
            pl.BlockSpec((None, 1, fc), lambda i, f: (layer, 0, f)),
            pl.BlockSpec((None, 1, fc), lambda i, f: (layer, 0, nf + f)),
            pl.BlockSpec((fc, D), lambda i, f: (f, 0)),
            pl.BlockSpec((1, D), lambda i, f: (0, 0)),
        ],
        out_specs=pl.BlockSpec((tm, D), lambda i, f: (i, 0)),
        scratch_shapes=[pltpu.VMEM((tm, D), BF16), pltpu.VMEM((nf, SUBLANES, fc), F32),
                        pltpu.VMEM((nf, SUBLANES, fc), F32),
                        pltpu.VMEM((2, tm + SUBLANES, MXU_COLS), F32),
                        pltpu.VMEM((2, tm + SUBLANES, MXU_COLS), F32)],
        compiler_params=_cparams(("arbitrary", "arbitrary")),
        name="conv_ffn",
    )(x, g.reshape(1, D), w_up, w_up, conv_w, conv_w, cb, cb, w_down, final_g.reshape(1, D))


TR_PREP, TC_PREP = 512, 512
TM_PROJ, TN_PROJ = 1024, 1536
TB_GATE = 512
TQ_FOX, TK_FOX, NH_FOX = 512, 512, 4
TB_SWA = 512
TB_GLA = 512
TM_MIX = 512
TM_FFN, FC_FFN = 512, 512


def _main_colscale():
    cs = np.ones((1, MAIN_COLS), np.float32)
    cs[0, C_FQ:C_FK] = HEAD_DIM ** -0.5 * LOG2E
    cs[0, C_SQ:C_SK] = HEAD_DIM ** -0.5
    cs[0, C_GQ:C_GK] = GLA_DK ** -0.5
    return jnp.asarray(cs)


def kernel(x, mem, w_in, b_fox_f, swa_sinks, t5_bias, w_gla_gate, b_gla_gate, gla_norm, w_mix_out, norm_mix,
           norm_xattn, norm_mem, wq_x, wkv_x, wo_x, norm_ffn, w_up, conv_w, conv_b, w_down, final_norm):
    depth = w_in.shape[0]
    _, T, D = x.shape
    M = mem.shape[1]
    h = x.reshape(T, D)
    memf = mem.reshape(M, D)
    colscale = _main_colscale()
    ones_kv = jnp.ones((1, wkv_x.shape[2]), F32)
    table = t5_table(t5_bias)
    pad_lanes = LANES - S_GLR - GLA_GATE_RANK
    w_main, w_small = prep_w_in(w_in, tr=TR_PREP, tc=TC_PREP)
    for l in range(depth):
        bvec = jnp.concatenate([b_fox_f[l], jnp.zeros((LANES - FOX_HEADS,), F32)]).reshape(1, LANES)
        wg_pad = jnp.concatenate([jnp.zeros((S_GLR, GLA_HEADS * GLA_DK), F32), w_gla_gate[l],
                                  jnp.zeros((pad_lanes, GLA_HEADS * GLA_DK), F32)], axis=0).astype(BF16)

        proj, ps = norm_proj(h, norm_mix[l], w_main, l, colscale, w_small, tm=TM_PROJ, tn=TN_PROJ, w_t=True)
        kx, row = fox_gate(ps, bvec, tb=TB_GATE)
        o_fox, w_down_b = fox_attention(proj, kx, row, w_down, l, tq=TQ_FOX, tk=TK_FOX, nh=NH_FOX)
        o_swa, (w_out_b, wq_b, wkv_b, wo_b) = swa_attention(proj, table, swa_sinks[l],
                                                           (w_mix_out, wq_x, wkv_x, wo_x), l, tb=TB_SWA)
        o_gla, w_up_b = gla(proj, ps, wg_pad, b_gla_gate[l].reshape(1, -1), gla_norm[l].reshape(1, -1),
                            w_up, l, tb=TB_GLA)
        kv = norm_proj(memf, norm_mem[l], wkv_b[None], 0, ones_kv, tm=M, tn=wkv_x.shape[2])
        h = mix_xattn(h, o_fox, o_swa, o_gla, w_out_b, norm_xattn[l], wq_b, kv, wo_b, tm=TM_MIX)
        h = conv_ffn(h, norm_ffn[l], w_up_b, conv_w, conv_b, w_down_b, l, final_norm,
                     tm=TM_FFN, fc=FC_FFN, final=(l == depth - 1))
    return h.reshape(x.shape)
```

```python
import functools
import math

import numpy as np
import jax
import jax.numpy as jnp
from jax import lax
from jax.experimental import pallas as pl
from jax.experimental.pallas import tpu as pltpu

F32 = jnp.float32
BF16 = jnp.bfloat16

HEAD_DIM = 128
FOX_HEADS = 4
SWA_HEADS = 8
SWA_KV_HEADS = 2
SWA_GROUP = SWA_HEADS // SWA_KV_HEADS
GLA_HEADS = 4
GLA_DK = 64
GLA_DV = 128
GLA_GATE_RANK = 16
GLA_TAU = 16.0
GLA_CHUNK = 64
SWA_WINDOW = 128
NUM_BUCKETS = 32
T5_MAX_DISTANCE = 128
XATTN_HEADS = 4
CONV_WIDTH = 3
EPS = 1e-6
LOG2E = math.log2(math.e)

LANES = 128
SUBLANES = 8
BF16_ROWS = 16
MXU_COLS = 256
VMEM_LIMIT = 56 * 1024 * 1024

_FOX_W = FOX_HEADS * HEAD_DIM
_OFF_FF = 3 * _FOX_W
_OFF_SQ = _OFF_FF + FOX_HEADS
_MAIN_B = (SWA_HEADS + 2 * SWA_KV_HEADS) * HEAD_DIM + 2 * GLA_HEADS * GLA_DK + 2 * GLA_HEADS * GLA_DV
_OFF_GLR = _OFF_SQ + _MAIN_B
C_FQ, C_FK, C_FV = 0, 512, 1024
C_SQ, C_SK, C_SV = 1536, 2560, 2816
C_GQ, C_GK, C_GV, C_GR = 3072, 3328, 3584, 4096
MAIN_COLS = 4608
S_FF, S_GLR = 0, SUBLANES


def _cparams(sem):
    return pltpu.CompilerParams(dimension_semantics=sem, vmem_limit_bytes=VMEM_LIMIT)


def _rms(x, g):
    return x * lax.rsqrt(jnp.mean(x * x, axis=-1, keepdims=True) + EPS) * g


N_SPLIT = 3


def _split3(x):
    hi = x.astype(BF16)
    r1 = x - hi.astype(F32)
    mid = r1.astype(BF16)
    lo = (r1 - mid.astype(F32)).astype(BF16)
    return hi, mid, lo


def _dot(a, b):
    return jnp.dot(a, b, preferred_element_type=F32)


def _dot_nt(a, b):
    return lax.dot_general(a, b, (((1,), (1,)), ((), ())), preferred_element_type=F32)


def _dot_tn(a, b):
    return lax.dot_general(a, b, (((0,), (0,)), ((), ())), preferred_element_type=F32)


def _sel_dot(sel, x):
    hi, mid, lo = _split3(x)
    return _dot(sel, hi) + _dot(sel, mid) + _dot(sel, lo)


def _prep_w_in_kernel(cur_ref, ff_ref, glr_ref, wm_ref, ws_ref):
    L = wm_ref.shape[0]
    wm_ref[...] = pltpu.einshape("cld->lcd", cur_ref[...]).astype(BF16)

    @pl.when(pl.program_id(1) == 0)
    def _():
        pad = jnp.zeros((LANES - S_GLR - GLA_GATE_RANK, ws_ref.shape[2]), F32)
        for l in range(L):
            ws_ref[l] = jnp.concatenate([ff_ref[:, l, :], glr_ref[:, l, :], pad], axis=0).astype(BF16)


def prep_w_in(w_in, *, tr, tc):
    L, D, _ = w_in.shape
    wt = jnp.transpose(w_in, (2, 0, 1))
    assert _OFF_FF % tc == 0 and MAIN_COLS % tc == 0 and S_GLR % SUBLANES == 0

    def src(j):
        return j * tc + jnp.where(j * tc >= _OFF_FF, _OFF_SQ - _OFF_FF, 0)

    return pl.pallas_call(
        _prep_w_in_kernel,
        out_shape=[jax.ShapeDtypeStruct((L, MAIN_COLS, D), BF16),
                   jax.ShapeDtypeStruct((L, LANES, D), BF16)],
        grid=(D // tr, MAIN_COLS // tc),
        in_specs=[
            pl.BlockSpec((pl.Element(tc), pl.Element(L), pl.Element(tr)), lambda r, j: (src(j), 0, r * tr)),
            pl.BlockSpec((pl.Element(S_GLR), pl.Element(L), pl.Element(tr)), lambda r, j: (_OFF_FF, 0, r * tr)),
            pl.BlockSpec((pl.Element(GLA_GATE_RANK), pl.Element(L), pl.Element(tr)),
                         lambda r, j: (_OFF_GLR, 0, r * tr)),
        ],
        out_specs=[pl.BlockSpec((L, tc, tr), lambda r, j: (0, j, r)),
                   pl.BlockSpec((L, LANES, tr), lambda r, j: (0, 0, r))],
        compiler_params=_cparams(("arbitrary", "arbitrary")),
        name="prep_w_in",
    )(wt, wt, wt)


def _norm_proj_kernel(*refs, has_small, w_t):
    if has_small:
        x_ref, g_ref, w_ref, cs_ref, ws_ref, o_ref, os_ref, xn_ref = refs
    else:
        x_ref, g_ref, w_ref, cs_ref, o_ref, xn_ref = refs
    dot = _dot_nt if w_t else _dot

    @pl.when(pl.program_id(1) == 0)
    def _():
        xn = _rms(x_ref[...], g_ref[...]).astype(BF16)
        xn_ref[...] = xn
        if has_small:
            os_ref[...] = dot(xn, ws_ref[...])

    o_ref[...] = (dot(xn_ref[...], w_ref[...]) * cs_ref[...]).astype(o_ref.dtype)


def norm_proj(x, g, w, layer, colscale, w_small=None, *, tm, tn, w_t=False):
    T, D = x.shape
    N = w.shape[1] if w_t else w.shape[2]
    has_small = w_small is not None

    def wspec(n):
        if w_t:
            return pl.BlockSpec((None, n, D), lambda i, j: (layer, j, 0))
        return pl.BlockSpec((None, D, n), lambda i, j: (layer, 0, j))

    in_specs = [
        pl.BlockSpec((tm, D), lambda i, j: (i, 0)),
        pl.BlockSpec((1, D), lambda i, j: (0, 0)),
        wspec(tn),
        pl.BlockSpec((1, tn), lambda i, j: (0, j)),
    ]
    args = [x, g.reshape(1, D), w, colscale]
    out_shape = [jax.ShapeDtypeStruct((T, N), BF16)]
    out_specs = [pl.BlockSpec((tm, tn), lambda i, j: (i, j))]
    if has_small:
        in_specs.append(pl.BlockSpec((None,) + w_small.shape[1:], lambda i, j: (layer, 0, 0)))
        args.append(w_small)
        out_shape.append(jax.ShapeDtypeStruct((T, LANES), F32))
        out_specs.append(pl.BlockSpec((tm, LANES), lambda i, j: (i, 0)))
    outs = pl.pallas_call(
        functools.partial(_norm_proj_kernel, has_small=has_small, w_t=w_t),
        out_shape=out_shape,
        grid=(T // tm, N // tn),
        in_specs=in_specs,
        out_specs=out_specs,
        scratch_shapes=[pltpu.VMEM((tm, D), BF16)],
        compiler_params=_cparams(("arbitrary", "arbitrary")),
        name="norm_proj_small" if has_small else "norm_proj",
    )(*args)
    return outs if has_small else outs[0]


def _fox_gate_kernel(ps_ref, b_ref, kx_ref, row_ref, carry_ref, *, tb):
    @pl.when(pl.program_id(0) == 0)
    def _():
        carry_ref[...] = jnp.zeros_like(carry_ref)

    x = jax.nn.log_sigmoid(ps_ref[...] + b_ref[...])
    r = lax.broadcasted_iota(jnp.int32, (tb, tb), 0)
    c = lax.broadcasted_iota(jnp.int32, (tb, tb), 1)
    tril = jnp.where(c <= r, 1.0, 0.0).astype(BF16)
    csum = _sel_dot(tril, x) + carry_ref[0:1, :]
    carry_ref[...] = jnp.broadcast_to(csum[tb - 1:tb, :], carry_ref.shape)
    pieces = _split3(csum * LOG2E)
    neg = [-p for p in pieces]
    src = lax.broadcasted_iota(jnp.int32, (LANES, FOX_HEADS * LANES), 0)
    dst = lax.broadcasted_iota(jnp.int32, (LANES, FOX_HEADS * LANES), 1)
    kx = sum(_dot(neg[n], jnp.where((dst // LANES == src - S_FF) & (dst % LANES == n), 1.0, 0.0).astype(BF16))
             for n in range(len(neg))).astype(BF16)
    sub = lax.broadcasted_iota(jnp.int32, (FOX_HEADS * SUBLANES, LANES), 0)
    lane = lax.broadcasted_iota(jnp.int32, (FOX_HEADS * SUBLANES, LANES), 1)
    pick = jnp.where(lane - S_FF == sub // SUBLANES, 1.0, 0.0).astype(BF16)
    rows = sum(_dot_nt(pick, p) for p in pieces)
    for h in range(FOX_HEADS):
        kx_ref[h] = kx[:, h * LANES:(h + 1) * LANES]
        row_ref[h, 0] = rows[h * SUBLANES:(h + 1) * SUBLANES, :]


def fox_gate(ps, bvec, *, tb):
    T = ps.shape[0]
    nb = T // tb
    return pl.pallas_call(
        functools.partial(_fox_gate_kernel, tb=tb),
        out_shape=[jax.ShapeDtypeStruct((FOX_HEADS, T, LANES), BF16),
                   jax.ShapeDtypeStruct((FOX_HEADS, nb, SUBLANES, tb), F32)],
        grid=(nb,),
        in_specs=[pl.BlockSpec((tb, LANES), lambda i: (i, 0)),
                  pl.BlockSpec((1, LANES), lambda i: (0, 0))],
        out_specs=[pl.BlockSpec((FOX_HEADS, tb, LANES), lambda i: (0, i, 0)),
                   pl.BlockSpec((FOX_HEADS, 1, SUBLANES, tb), lambda i: (0, i, 0, 0))],
        scratch_shapes=[pltpu.VMEM((SUBLANES, LANES), F32)],
        compiler_params=_cparams(("arbitrary",)),
        name="fox_gate",
    )(ps, bvec)


def _fox_attn_kernel(q_ref, k_ref, v_ref, kx_ref, cq_ref, wi_ref, o_ref, wo_ref,
                     vt_ref, ya_ref, yb_ref, xa_ref, xb_ref, m_ref, acc_ref, *, tq, tk, nk, nh):
    i = pl.program_id(1)
    D = HEAD_DIM
    heads = range(nh)
    n_full = i // (tk // tq)
    _cast_along([wi_ref], [wo_ref])

    @pl.when(i == 0)
    def _():
        for hh in heads:
            for n in range(nk):
                vt_ref[hh, n, 0:D, :] = v_ref[n * tk:(n + 1) * tk, hh * D:(hh + 1) * D].T
                vt_ref[hh, n, D:, :] = jnp.ones((BF16_ROWS, tk), BF16)

    lane = lax.broadcasted_iota(jnp.int32, (tq, LANES), 1)
    ones = jnp.where(lane < N_SPLIT, 1.0, 0.0).astype(BF16)
    q_aug = [jnp.concatenate([q_ref[:, hh * D:(hh + 1) * D], ones], axis=1) for hh in heads]
    cq = [cq_ref[hh, 0, 0:1, :] for hh in heads]
    m_ref[...] = jnp.full(m_ref.shape, -jnp.inf, F32)
    acc_ref[...] = jnp.zeros(acc_ref.shape, F32)

    def scores(hh, j):
        start = pl.multiple_of(j * tk, tk)
        k_aug = jnp.concatenate([k_ref[pl.ds(start, tk), hh * D:(hh + 1) * D], kx_ref[hh, pl.ds(start, tk), :]],
                                axis=1)
        return _dot_nt(k_aug, q_aug[hh])

    def update(hh, j, y, ymax):
        m_old = m_ref[hh]
        m_new = jnp.maximum(m_old, ymax + cq[hh])
        alpha = jnp.exp2(m_old - m_new)
        p = jnp.exp2(y + (cq[hh] - m_new))
        acc_ref[hh] = alpha * acc_ref[hh] + _dot(vt_ref[hh, j], p.astype(BF16))
        m_ref[hh] = m_new

    def produce(j, y_ref, ymax_ref):
        for hh in heads:
            y = scores(hh, j)
            y_ref[hh] = y
            ymax_ref[hh] = jnp.max(y, axis=0, keepdims=True)

    produce(0, ya_ref, xa_ref)

    def step(j, cur_ref, cmax_ref, nxt_ref, nmax_ref):
        produce(j + 1, nxt_ref, nmax_ref)
        for hh in heads:
            update(hh, j, cur_ref[hh], cmax_ref[hh])

    def body(j, carry):
        pl.when(j % 2 == 0)(lambda: step(j, ya_ref, xa_ref, yb_ref, xb_ref))
        pl.when(j % 2 == 1)(lambda: step(j, yb_ref, xb_ref, ya_ref, xa_ref))
        return carry

    lax.fori_loop(0, n_full, body, 0)
    key = n_full * tk + lax.broadcasted_iota(jnp.int32, (tk, tq), 0)
    qry = i * tq + lax.broadcasted_iota(jnp.int32, (tk, tq), 1)

    def diagonal(cur_ref):
        for hh in heads:
            y = jnp.where(key <= qry, cur_ref[hh], -jnp.inf)
            update(hh, n_full, y, jnp.max(y, axis=0, keepdims=True))

    pl.when(n_full % 2 == 0)(lambda: diagonal(ya_ref))
    pl.when(n_full % 2 == 1)(lambda: diagonal(yb_ref))
    for hh in heads:
        o_ref[:, hh * D:(hh + 1) * D] = (acc_ref[hh, 0:D, :] / acc_ref[hh, D:D + 1, :]).T.astype(o_ref.dtype)


def fox_attention(proj, kx, row, w_cast, layer, *, tq, tk, nh):
    T = proj.shape[0]
    nq = T // tq
    ng = FOX_HEADS // nh
    gw = nh * HEAD_DIM
    qb, kb, vb = C_FQ // gw, C_FK // gw, C_FV // gw
    assert tk % tq == 0 and T % tk == 0
    (w_in_spec,), (w_shape,), (w_out_spec,) = _cast_slabs([w_cast], layer, ng * nq, lambda g, i: g * nq + i)
    return pl.pallas_call(
        functools.partial(_fox_attn_kernel, tq=tq, tk=tk, nk=T // tk, nh=nh),
        out_shape=[jax.ShapeDtypeStruct((T, FOX_HEADS * HEAD_DIM), BF16), w_shape],
        grid=(ng, nq),
        in_specs=[
            pl.BlockSpec((tq, gw), lambda g, i: (i, qb + g)),
            pl.BlockSpec((T, gw), lambda g, i: (0, kb + g), pipeline_mode=pl.Buffered(1)),
            pl.BlockSpec((T, gw), lambda g, i: (0, vb + g), pipeline_mode=pl.Buffered(1)),
            pl.BlockSpec((nh, T, LANES), lambda g, i: (g, 0, 0), pipeline_mode=pl.Buffered(1)),
            pl.BlockSpec((nh, 1, SUBLANES, tq), lambda g, i: (g, i, 0, 0)),
            w_in_spec,
        ],
        out_specs=[pl.BlockSpec((tq, gw), lambda g, i: (i, g)), w_out_spec],
        scratch_shapes=[pltpu.VMEM((nh, T // tk, HEAD_DIM + BF16_ROWS, tk), BF16),
                        pltpu.VMEM((nh, tk, tq), F32), pltpu.VMEM((nh, tk, tq), F32),
                        pltpu.VMEM((nh, 1, tq), F32), pltpu.VMEM((nh, 1, tq), F32),
                        pltpu.VMEM((nh, 1, tq), F32),
                        pltpu.VMEM((nh, HEAD_DIM + BF16_ROWS, tq), F32)],
        compiler_params=_cparams(("arbitrary", "arbitrary")),
        name="fox_attn",
    )(proj, proj, proj, kx, row, w_cast)


def _t5_table_kernel(t5_ref, o_ref):
    h = pl.program_id(0)
    W = SWA_WINDOW
    i = lax.broadcasted_iota(jnp.int32, (W, 2 * W), 0)
    j = lax.broadcasted_iota(jnp.int32, (W, 2 * W), 1)
    rel = (W + i) - j
    n = jnp.maximum(rel, 0)
    max_exact = NUM_BUCKETS // 2
    nf = jnp.maximum(n, 1).astype(F32)
    large = max_exact + (jnp.log(nf / max_exact) / math.log(T5_MAX_DISTANCE / max_exact)
                         * (NUM_BUCKETS - max_exact)).astype(jnp.int32)
    large = jnp.minimum(large, NUM_BUCKETS - 1)
    bucket = jnp.where(n < max_exact, n, large)
    bias = jnp.zeros((W, 2 * W), F32)
    for b in range(NUM_BUCKETS):
        bias = jnp.where(bucket == b, t5_ref[b, h], bias)
    in_window = (rel >= 0) & (rel < W)
    o_ref[...] = jnp.where(in_window, bias, -jnp.inf)


def t5_table(t5_bias):
    W = SWA_WINDOW
    return pl.pallas_call(
        _t5_table_kernel,
        out_shape=jax.ShapeDtypeStruct((SWA_HEADS, W, 2 * W), F32),
        grid=(SWA_HEADS,),
        in_specs=[pl.BlockSpec(memory_space=pltpu.SMEM)],
        out_specs=pl.BlockSpec((None, W, 2 * W), lambda h: (h, 0, 0)),
        compiler_params=_cparams(("arbitrary",)),
        name="t5_table",
    )(t5_bias)


def _cast_slabs(ws, layer, nsteps, step):
    in_specs, shapes, out_specs = [], [], []
    for w in ws:
        _, rows, cols = w.shape
        slab = rows // nsteps
        assert slab * nsteps == rows and slab % BF16_ROWS == 0
        in_specs.append(pl.BlockSpec((None, slab, cols), lambda *ids: (layer, step(*ids), 0)))
        out_specs.append(pl.BlockSpec((slab, cols), lambda *ids: (step(*ids), 0)))
        shapes.append(jax.ShapeDtypeStruct((rows, cols), BF16))
    return in_specs, shapes, out_specs


def _cast_along(wi_refs, wo_refs):
    for wi_ref, wo_ref in zip(wi_refs, wo_refs, strict=True):
        wo_ref[...] = wi_ref[...].astype(wo_ref.dtype)


def _swa_kernel(sink_ref, q_ref, kc_ref, kp_ref, vc_ref, vp_ref, tab_ref, *refs, nsub, ncast):
    wi_refs, o_ref, wo_refs = refs[:ncast], refs[ncast], refs[ncast + 1:]
    g = pl.program_id(0)
    n = pl.program_id(1)
    W = SWA_WINDOW
    _cast_along(wi_refs, wo_refs)

    first_prev = lax.broadcasted_iota(jnp.int32, (W, 2 * W), 1) < W
    for sub in range(nsub):
        if sub == 0:
            kp, vp = kp_ref[...], vp_ref[...]
        else:
            kp, vp = kc_ref[(sub - 1) * W:sub * W, :], vc_ref[(sub - 1) * W:sub * W, :]
        kk = jnp.concatenate([kp, kc_ref[sub * W:(sub + 1) * W, :]], axis=0)
        vv = jnp.concatenate([vp, vc_ref[sub * W:(sub + 1) * W, :]], axis=0)
        for hh in range(SWA_GROUP):
            q = q_ref[sub * W:(sub + 1) * W, hh * HEAD_DIM:(hh + 1) * HEAD_DIM]
            s = _dot_nt(q, kk) + tab_ref[hh]
            if sub == 0:
                s = jnp.where(jnp.logical_and(first_prev, n == 0), -jnp.inf, s)
            sink = sink_ref[0, g * SWA_GROUP + hh]
            m = jnp.maximum(jnp.max(s, axis=1, keepdims=True), sink)
            p = jnp.exp(s - m)
            l = jnp.sum(p, axis=1, keepdims=True) + jnp.exp(sink - m)
            o = _dot(p.astype(BF16), vv) / l
            o_ref[sub * W:(sub + 1) * W, hh * HEAD_DIM:(hh + 1) * HEAD_DIM] = o.astype(o_ref.dtype)


def swa_attention(proj, table, sinks, w_cast, layer, *, tb):
    T = proj.shape[0]
    W = SWA_WINDOW
    nsub = tb // W
    nt = T // tb
    gw = SWA_GROUP * HEAD_DIM
    qb, kb, vb = C_SQ // gw, C_SK // HEAD_DIM, C_SV // HEAD_DIM
    prev = lambda n: jnp.maximum(n * nsub - 1, 0)
    w_in_specs, w_shapes, w_out_specs = _cast_slabs(w_cast, layer, SWA_KV_HEADS * nt, lambda g, n: g * nt + n)
    o, *w_bf16 = pl.pallas_call(
        functools.partial(_swa_kernel, nsub=nsub, ncast=len(w_cast)),
        out_shape=[jax.ShapeDtypeStruct((T, SWA_HEADS * HEAD_DIM), BF16), *w_shapes],
        grid=(SWA_KV_HEADS, nt),
        in_specs=[
            pl.BlockSpec(memory_space=pltpu.SMEM),
            pl.BlockSpec((tb, gw), lambda g, n: (n, qb + g)),
            pl.BlockSpec((tb, HEAD_DIM), lambda g, n: (n, kb + g)),
            pl.BlockSpec((W, HEAD_DIM), lambda g, n: (prev(n), kb + g)),
            pl.BlockSpec((tb, HEAD_DIM), lambda g, n: (n, vb + g)),
            pl.BlockSpec((W, HEAD_DIM), lambda g, n: (prev(n), vb + g)),
            pl.BlockSpec((SWA_GROUP, W, 2 * W), lambda g, n: (g, 0, 0)),
            *w_in_specs,
        ],
        out_specs=[pl.BlockSpec((tb, gw), lambda g, n: (n, g)), *w_out_specs],
        compiler_params=_cparams(("arbitrary", "arbitrary")),
        name="swa_attn",
    )(sinks.reshape(1, SWA_HEADS), proj, proj, proj, proj, proj, table, *w_cast)
    return o, w_bf16


def _gla_kernel(q_ref, k_ref, v_ref, r_ref, ps_ref, wg_ref, bg_ref, gn_ref, wi_ref, o_ref, wo_ref,
                st_ref, oc_ref, *, tb):
    C = GLA_CHUNK
    _cast_along([wi_ref], [wo_ref])

    @pl.when(pl.program_id(0) == 0)
    def _():
        st_ref[...] = jnp.zeros_like(st_ref)

    glr = ps_ref[...].astype(BF16)
    g = jax.nn.log_sigmoid(_dot(glr, wg_ref[...]) + bg_ref[...]) / GLA_TAU
    r = lax.broadcasted_iota(jnp.int32, (tb, tb), 0)
    c = lax.broadcasted_iota(jnp.int32, (tb, tb), 1)
    same = (r // C) == (c // C)
    tril = jnp.where(jnp.logical_and(same, c <= r), 1.0, 0.0).astype(BF16)
    whole = jnp.where(same, 1.0, 0.0).astype(BF16)
    gh, gm, gl = _split3(g)
    b = _dot(tril, gh) + _dot(tril, gm) + _dot(tril, gl)
    b_last = _dot(whole, gh) + _dot(whole, gm) + _dot(whole, gl)
    q_t = (q_ref[...].astype(F32) * jnp.exp(b)).astype(BF16)
    kf = k_ref[...].astype(F32)
    k_t = kf * jnp.exp(-b)
    k_end = kf * jnp.exp(b_last - b)
    decay = jnp.exp(b_last)
    lane = lax.broadcasted_iota(jnp.int32, (1, LANES), 1)
    causal = (lax.broadcasted_iota(jnp.int32, (C, C), 1) <= lax.broadcasted_iota(jnp.int32, (C, C), 0))
    for h in range(GLA_HEADS):
        pair = slice((h // 2) * LANES, (h // 2 + 1) * LANES)
        mine = (lane // GLA_DK) == (h % 2)
        ktm = jnp.where(mine, k_t[:, pair], 0.0).astype(BF16)
        kem = jnp.where(mine, k_end[:, pair], 0.0).astype(BF16)
        qh = q_t[:, pair]
        dec = decay[:, pair]
        vh = v_ref[:, h * GLA_DV:(h + 1) * GLA_DV]
        st = st_ref[h]
        for n in range(tb // C):
            rows = slice(n * C, (n + 1) * C)
            a = jnp.where(causal, _dot_nt(qh[rows], ktm[rows]), 0.0)
            o = _dot(a.astype(BF16), vh[rows]) + _dot_nt(qh[rows], st.astype(BF16))
            oc_ref[rows, h * GLA_DV:(h + 1) * GLA_DV] = o
            kv_t = _dot_tn(vh[rows], kem[rows])
            st = dec[n * C:n * C + 1, :] * st + kv_t
        st_ref[h] = st
    for h in range(GLA_HEADS):
        cols = slice(h * GLA_DV, (h + 1) * GLA_DV)
        o = _rms(oc_ref[:, cols], gn_ref[...])
        o_ref[:, cols] = (o * jax.nn.silu(r_ref[:, cols].astype(F32))).astype(o_ref.dtype)


def gla(proj, ps, wg_pad, bg, gnorm, w_cast, layer, *, tb):
    T = proj.shape[0]
    qw = GLA_HEADS * GLA_DK
    vw = GLA_HEADS * GLA_DV
    (w_in_spec,), (w_shape,), (w_out_spec,) = _cast_slabs([w_cast], layer, T // tb, lambda n: n)
    return pl.pallas_call(
        functools.partial(_gla_kernel, tb=tb),
        out_shape=[jax.ShapeDtypeStruct((T, vw), BF16), w_shape],
        grid=(T // tb,),
        in_specs=[
            pl.BlockSpec((tb, qw), lambda n: (n, C_GQ // qw)),
            pl.BlockSpec((tb, qw), lambda n: (n, C_GK // qw)),
            pl.BlockSpec((tb, vw), lambda n: (n, C_GV // vw)),
            pl.BlockSpec((tb, vw), lambda n: (n, C_GR // vw)),
            pl.BlockSpec((tb, LANES), lambda n: (n, 0)),
            pl.BlockSpec((LANES, qw), lambda n: (0, 0)),
            pl.BlockSpec((1, qw), lambda n: (0, 0)),
            pl.BlockSpec((1, GLA_DV), lambda n: (0, 0)),
            w_in_spec,
        ],
        out_specs=[pl.BlockSpec((tb, vw), lambda n: (n, 0)), w_out_spec],
        scratch_shapes=[pltpu.VMEM((GLA_HEADS, GLA_DV, LANES), F32),
                        pltpu.VMEM((tb, vw), F32)],
        compiler_params=_cparams(("arbitrary",)),
        name="gla",
    )(proj, proj, proj, proj, ps, wg_pad, bg, gnorm, w_cast)


def _mix_xattn_kernel(h_ref, of_ref, os_ref, og_ref, wo_ref, gx_ref, wq_ref, kv_ref, wx_ref, o_ref):
    nf, ns = of_ref.shape[1], os_ref.shape[1]
    mix = (_dot(of_ref[...], wo_ref[0:nf, :]) + _dot(os_ref[...], wo_ref[nf:nf + ns, :])
           + _dot(og_ref[...], wo_ref[nf + ns:, :]))
    h1 = h_ref[...] + mix
    xn = _rms(h1, gx_ref[...]).astype(BF16)
    q = (_dot(xn, wq_ref[...]) * (HEAD_DIM ** -0.5)).astype(BF16)
    xw = XATTN_HEADS * HEAD_DIM
    outs = []
    for hh in range(XATTN_HEADS):
        cols = slice(hh * HEAD_DIM, (hh + 1) * HEAD_DIM)
        k = kv_ref[:, cols]
        v = kv_ref[:, xw + hh * HEAD_DIM:xw + (hh + 1) * HEAD_DIM]
        s = _dot_nt(q[:, cols], k)
        p = jnp.exp(s - jnp.max(s, axis=1, keepdims=True))
        l = jnp.sum(p, axis=1, keepdims=True)
        outs.append((_dot(p.astype(BF16), v) / l).astype(BF16))
    o = jnp.concatenate(outs, axis=1)
    o_ref[...] = h1 + _dot(o, wx_ref[...])


def mix_xattn(h, o_fox, o_swa, o_gla, w_out, gx, wq, kv, wo, *, tm):
    T, D = h.shape
    const = lambda shape: pl.BlockSpec(shape, lambda i: (0, 0))
    row = lambda w: pl.BlockSpec((tm, w), lambda i: (i, 0))
    return pl.pallas_call(
        _mix_xattn_kernel,
        out_shape=jax.ShapeDtypeStruct((T, D), F32),
        grid=(T // tm,),
        in_specs=[row(D), row(o_fox.shape[1]), row(o_swa.shape[1]), row(o_gla.shape[1]),
                  const(w_out.shape), const((1, D)), const(wq.shape), const(kv.shape), const(wo.shape)],
        out_specs=row(D),
        compiler_params=_cparams(("arbitrary",)),
        name="mix_xattn",
    )(h, o_fox, o_swa, o_gla, w_out, gx.reshape(1, D), wq, kv, wo)


def _ffn_kernel(x_ref, g_ref, wg_ref, wv_ref, cwg_ref, cwv_ref, cbg_ref, cbv_ref, wd_ref, fg_ref,
                o_ref, xn_ref, hg_ref, hv_ref, ug_ref, uv_ref, *, tm, sub, final):
    i = pl.program_id(0)
    f = pl.program_id(1)
    H = SUBLANES
    fc = wd_ref.shape[0]

    @pl.when(f == 0)
    def _():
        x = x_ref[...]
        xn_ref[...] = _rms(x, g_ref[...]).astype(BF16)
        o_ref[...] = x

    @pl.when(i == 0)
    def _():
        hg_ref[f] = jnp.zeros(hg_ref.shape[1:], F32)
        hv_ref[f] = jnp.zeros(hv_ref.shape[1:], F32)

    xn = xn_ref[...]

    def up(c, slot):
        ug_ref[slot, H:, :] = _dot(xn, wg_ref[:, c:c + sub])
        uv_ref[slot, H:, :] = _dot(xn, wv_ref[:, c:c + sub])

    def conv(u_ref, slot, h_ref, cw_ref, cb_ref, c):
        u_ref[slot, 0:H, :] = h_ref[f, :, c:c + sub]
        h_ref[f, :, c:c + sub] = u_ref[slot, tm:tm + H, :]
        return (cb_ref[:, c:c + sub] + cw_ref[2:3, c:c + sub] * u_ref[slot, H:H + tm, :]
                + cw_ref[1:2, c:c + sub] * u_ref[slot, H - 1:H - 1 + tm, :]
                + cw_ref[0:1, c:c + sub] * u_ref[slot, H - 2:H - 2 + tm, :])

    def down(c, slot):
        gate = conv(ug_ref, slot, hg_ref, cwg_ref, cbg_ref, c)
        val = conv(uv_ref, slot, hv_ref, cwv_ref, cbv_ref, c)
        act = (jax.nn.silu(gate) * val).astype(BF16)
        o_ref[...] += _dot(act, wd_ref[c:c + sub, :])

    up(0, 0)
    for n, c in enumerate(range(0, fc, sub)):
        if c + sub < fc:
            up(c + sub, (n + 1) % 2)
        down(c, n % 2)

    if final:
        @pl.when(f == pl.num_programs(1) - 1)
        def _():
            o_ref[...] = _rms(o_ref[...], fg_ref[...])


def conv_ffn(x, g, w_up, conv_w, conv_b, w_down, layer, final_g, *, tm, fc, final):
    T, D = x.shape
    dff = w_down.shape[0]
    L = conv_w.shape[0]
    nf = dff // fc
    cb = conv_b.reshape(L, 1, 2 * dff)
    return pl.pallas_call(
        functools.partial(_ffn_kernel, tm=tm, sub=MXU_COLS, final=final),
        out_shape=jax.ShapeDtypeStruct((T, D), F32),
        grid=(T // tm, nf),
        in_specs=[
            pl.BlockSpec((tm, D), lambda i, f: (i, 0)),
            pl.BlockSpec((1, D), lambda i, f: (0, 0)),
            pl.BlockSpec((D, fc), lambda i, f: (0, f)),
            pl.BlockSpec((D, fc), lambda i, f: (0, nf + f)),
            pl.BlockSpec((None, CONV_WIDTH, fc), lambda i, f: (layer, 0, f)),
            pl.BlockSpec((None, CONV_WIDTH, fc), lambda i, f: (layer, 0, nf + f)),
            pl.BlockSpec((None, 1, fc), lambda i, f: (layer, 0, f)),
            pl.BlockSpec((None, 1, fc), lambda i, f: (layer, 0, nf + f)),
            pl.BlockSpec((fc, D), lambda i, f: (f, 0)),
            pl.BlockSpec((1, D), lambda i, f: (0, 0)),
        ],
        out_specs=pl.BlockSpec((tm, D), lambda i, f: (i, 0)),
        scratch_shapes=[pltpu.VMEM((tm, D), BF16), pltpu.VMEM((nf, SUBLANES, fc), F32),
                        pltpu.VMEM((nf, SUBLANES, fc), F32),
                        pltpu.VMEM((2, tm + SUBLANES, MXU_COLS), F32),
                        pltpu.VMEM((2, tm + SUBLANES, MXU_COLS), F32)],
        compiler_params=_cparams(("arbitrary", "arbitrary")),
        name="conv_ffn",
    )(x, g.reshape(1, D), w_up, w_up, conv_w, conv_w, cb, cb, w_down, final_g.reshape(1, D))


TR_PREP, TC_PREP = 512, 512
TM_PROJ, TN_PROJ = 1024, 1536
TB_GATE = 512
TQ_FOX, TK_FOX, NH_FOX = 512, 512, 4
TB_SWA = 512
TB_GLA = 512
TM_MIX = 512
TM_FFN, FC_FFN = 512, 512


def _main_colscale():
    cs = np.ones((1, MAIN_COLS), np.float32)
    cs[0, C_FQ:C_FK] = HEAD_DIM ** -0.5 * LOG2E
    cs[0, C_SQ:C_SK] = HEAD_DIM ** -0.5
    cs[0, C_GQ:C_GK] = GLA_DK ** -0.5
    return jnp.asarray(cs)


def kernel(x, mem, w_in, b_fox_f, swa_sinks, t5_bias, w_gla_gate, b_gla_gate, gla_norm, w_mix_out, norm_mix,
           norm_xattn, norm_mem, wq_x, wkv_x, wo_x, norm_ffn, w_up, conv_w, conv_b, w_down, final_norm):
    depth = w_in.shape[0]
    _, T, D = x.shape
    M = mem.shape[1]
    h = x.reshape(T, D)
    memf = mem.reshape(M, D)
    colscale = _main_colscale()
    ones_kv = jnp.ones((1, wkv_x.shape[2]), F32)
    table = t5_table(t5_bias)
    pad_lanes = LANES - S_GLR - GLA_GATE_RANK
    w_main, w_small = prep_w_in(w_in, tr=TR_PREP, tc=TC_PREP)
    for l in range(depth):
        bvec = jnp.concatenate([b_fox_f[l], jnp.zeros((LANES - FOX_HEADS,), F32)]).reshape(1, LANES)
        wg_pad = jnp.concatenate([jnp.zeros((S_GLR, GLA_HEADS * GLA_DK), F32), w_gla_gate[l],
                                  jnp.zeros((pad_lanes, GLA_HEADS * GLA_DK), F32)], axis=0).astype(BF16)

        proj, ps = norm_proj(h, norm_mix[l], w_main, l, colscale, w_small, tm=TM_PROJ, tn=TN_PROJ, w_t=True)
        kx, row = fox_gate(ps, bvec, tb=TB_GATE)
        o_fox, w_down_b = fox_attention(proj, kx, row, w_down, l, tq=TQ_FOX, tk=TK_FOX, nh=NH_FOX)
        o_swa, (w_out_b, wq_b, wkv_b, wo_b) = swa_attention(proj, table, swa_sinks[l],
                                                           (w_mix_out, wq_x, wkv_x, wo_x), l, tb=TB_SWA)
        o_gla, w_up_b = gla(proj, ps, wg_pad, b_gla_gate[l].reshape(1, -1), gla_norm[l].reshape(1, -1),
                            w_up, l, tb=TB_GLA)
        kv = norm_proj(memf, norm_mem[l], wkv_b[None], 0, ones_kv, tm=M, tn=wkv_x.shape[2])
        h = mix_xattn(h, o_fox, o_swa, o_gla, w_out_b, norm_xattn[l], wq_b, kv, wo_b, tm=TM_MIX)
        h = conv_ffn(h, norm_ffn[l], w_up_b, conv_w, conv_b, w_down_b, l, final_norm,
                     tm=TM_FFN, fc=FC_FFN, final=(l == depth - 1))
    return h.reshape(x.shape)
```

```python
import functools
import math

import numpy as np
import jax
import jax.numpy as jnp
from jax import lax
from jax.experimental import pallas as pl
from jax.experimental.pallas import tpu as pltpu

F32 = jnp.float32
BF16 = jnp.bfloat16

HEAD_DIM = 128
FOX_HEADS = 4
SWA_HEADS = 8
SWA_KV_HEADS = 2
SWA_GROUP = SWA_HEADS // SWA_KV_HEADS
GLA_HEADS = 4
GLA_DK = 64
GLA_DV = 128
GLA_GATE_RANK = 16
GLA_TAU = 16.0
GLA_CHUNK = 64
SWA_WINDOW = 128
NUM_BUCKETS = 32
T5_MAX_DISTANCE = 128
XATTN_HEADS = 4
CONV_WIDTH = 3
EPS = 1e-6
LOG2E = math.log2(math.e)

LANES = 128
SUBLANES = 8
BF16_ROWS = 16
MXU_COLS = 256
VMEM_LIMIT = 56 * 1024 * 1024

_FOX_W = FOX_HEADS * HEAD_DIM
_OFF_FF = 3 * _FOX_W
_OFF_SQ = _OFF_FF + FOX_HEADS
_MAIN_B = (SWA_HEADS + 2 * SWA_KV_HEADS) * HEAD_DIM + 2 * GLA_HEADS * GLA_DK + 2 * GLA_HEADS * GLA_DV
_OFF_GLR = _OFF_SQ + _MAIN_B
C_FQ, C_FK, C_FV = 0, 512, 1024
C_SQ, C_SK, C_SV = 1536, 2560, 2816
C_GQ, C_GK, C_GV, C_GR = 3072, 3328, 3584, 4096
MAIN_COLS = 4608
S_FF, S_GLR = 0, SUBLANES


def _cparams(sem):
    return pltpu.CompilerParams(dimension_semantics=sem, vmem_limit_bytes=VMEM_LIMIT)


def _rms(x, g):
    return x * lax.rsqrt(jnp.mean(x * x, axis=-1, keepdims=True) + EPS) * g


N_SPLIT = 3


def _split3(x):
    hi = x.astype(BF16)
    r1 = x - hi.astype(F32)
    mid = r1.astype(BF16)
    lo = (r1 - mid.astype(F32)).astype(BF16)
    return hi, mid, lo


def _dot(a, b):
    return jnp.dot(a, b, preferred_element_type=F32)


def _dot_nt(a, b):
    return lax.dot_general(a, b, (((1,), (1,)), ((), ())), preferred_element_type=F32)


def _dot_tn(a, b):
    return lax.dot_general(a, b, (((0,), (0,)), ((), ())), preferred_element_type=F32)


def _sel_dot(sel, x):
    hi, mid, lo = _split3(x)
    return _dot(sel, hi) + _dot(sel, mid) + _dot(sel, lo)


def _prep_w_in_kernel(cur_ref, ff_ref, glr_ref, wm_ref, ws_ref):
    L = wm_ref.shape[0]
    wm_ref[...] = pltpu.einshape("cld->lcd", cur_ref[...]).astype(BF16)

    @pl.when(pl.program_id(1) == 0)
    def _():
        pad = jnp.zeros((LANES - S_GLR - GLA_GATE_RANK, ws_ref.shape[2]), F32)
        for l in range(L):
            ws_ref[l] = jnp.concatenate([ff_ref[:, l, :], glr_ref[:, l, :], pad], axis=0).astype(BF16)


def prep_w_in(w_in, *, tr, tc):
    L, D, _ = w_in.shape
    wt = jnp.transpose(w_in, (2, 0, 1))
    assert _OFF_FF % tc == 0 and MAIN_COLS % tc == 0 and S_GLR % SUBLANES == 0

    def src(j):
        return j * tc + jnp.where(j * tc >= _OFF_FF, _OFF_SQ - _OFF_FF, 0)

    return pl.pallas_call(
        _prep_w_in_kernel,
        out_shape=[jax.ShapeDtypeStruct((L, MAIN_COLS, D), BF16),
                   jax.ShapeDtypeStruct((L, LANES, D), BF16)],
        grid=(D // tr, MAIN_COLS // tc),
        in_specs=[
            pl.BlockSpec((pl.Element(tc), pl.Element(L), pl.Element(tr)), lambda r, j: (src(j), 0, r * tr)),
            pl.BlockSpec((pl.Element(S_GLR), pl.Element(L), pl.Element(tr)), lambda r, j: (_OFF_FF, 0, r * tr)),
            pl.BlockSpec((pl.Element(GLA_GATE_RANK), pl.Element(L), pl.Element(tr)),
                         lambda r, j: (_OFF_GLR, 0, r * tr)),
        ],
        out_specs=[pl.BlockSpec((L, tc, tr), lambda r, j: (0, j, r)),
                   pl.BlockSpec((L, LANES, tr), lambda r, j: (0, 0, r))],
        compiler_params=_cparams(("arbitrary", "arbitrary")),
        name="prep_w_in",
    )(wt, wt, wt)


def _norm_proj_kernel(*refs, has_small, w_t):
    if has_small:
        x_ref, g_ref, w_ref, cs_ref, ws_ref, o_ref, os_ref, xn_ref = refs
    else:
        x_ref, g_ref, w_ref, cs_ref, o_ref, xn_ref = refs
    dot = _dot_nt if w_t else _dot

    @pl.when(pl.program_id(1) == 0)
    def _():
        xn = _rms(x_ref[...], g_ref[...]).astype(BF16)
        xn_ref[...] = xn
        if has_small:
            os_ref[...] = dot(xn, ws_ref[...])

    o_ref[...] = (dot(xn_ref[...], w_ref[...]) * cs_ref[...]).astype(o_ref.dtype)


def norm_proj(x, g, w, layer, colscale, w_small=None, *, tm, tn, w_t=False):
    T, D = x.shape
    N = w.shape[1] if w_t else w.shape[2]
    has_small = w_small is not None

    def wspec(n):
        if w_t:
            return pl.BlockSpec((None, n, D), lambda i, j: (layer, j, 0))
        return pl.BlockSpec((None, D, n), lambda i, j: (layer, 0, j))

    in_specs = [
        pl.BlockSpec((tm, D), lambda i, j: (i, 0)),
        pl.BlockSpec((1, D), lambda i, j: (0, 0)),
        wspec(tn),
        pl.BlockSpec((1, tn), lambda i, j: (0, j)),
    ]
    args = [x, g.reshape(1, D), w, colscale]
    out_shape = [jax.ShapeDtypeStruct((T, N), BF16)]
    out_specs = [pl.BlockSpec((tm, tn), lambda i, j: (i, j))]
    if has_small:
        in_specs.append(pl.BlockSpec((None,) + w_small.shape[1:], lambda i, j: (layer, 0, 0)))
        args.append(w_small)
        out_shape.append(jax.ShapeDtypeStruct((T, LANES), F32))
        out_specs.append(pl.BlockSpec((tm, LANES), lambda i, j: (i, 0)))
    outs = pl.pallas_call(
        functools.partial(_norm_proj_kernel, has_small=has_small, w_t=w_t),
        out_shape=out_shape,
        grid=(T // tm, N // tn),
        in_specs=in_specs,
        out_specs=out_specs,
        scratch_shapes=[pltpu.VMEM((tm, D), BF16)],
        compiler_params=_cparams(("arbitrary", "arbitrary")),
        name="norm_proj_small" if has_small else "norm_proj",
    )(*args)
    return outs if has_small else outs[0]


def _fox_gate_kernel(ps_ref, b_ref, kx_ref, row_ref, carry_ref, *, tb):
    @pl.when(pl.program_id(0) == 0)
    def _():
        carry_ref[...] = jnp.zeros_like(carry_ref)

    x = jax.nn.log_sigmoid(ps_ref[...] + b_ref[...])
    r = lax.broadcasted_iota(jnp.int32, (tb, tb), 0)
    c = lax.broadcasted_iota(jnp.int32, (tb, tb), 1)
    tril = jnp.where(c <= r, 1.0, 0.0).astype(BF16)
    csum = _sel_dot(tril, x) + carry_ref[0:1, :]
    carry_ref[...] = jnp.broadcast_to(csum[tb - 1:tb, :], carry_ref.shape)
    pieces = _split3(csum * LOG2E)
    neg = [-p for p in pieces]
    src = lax.broadcasted_iota(jnp.int32, (LANES, FOX_HEADS * LANES), 0)
    dst = lax.broadcasted_iota(jnp.int32, (LANES, FOX_HEADS * LANES), 1)
    kx = sum(_dot(neg[n], jnp.where((dst // LANES == src - S_FF) & (dst % LANES == n), 1.0, 0.0).astype(BF16))
             for n in range(len(neg))).astype(BF16)
    sub = lax.broadcasted_iota(jnp.int32, (FOX_HEADS * SUBLANES, LANES), 0)
    lane = lax.broadcasted_iota(jnp.int32, (FOX_HEADS * SUBLANES, LANES), 1)
    pick = jnp.where(lane - S_FF == sub // SUBLANES, 1.0, 0.0).astype(BF16)
    rows = sum(_dot_nt(pick, p) for p in pieces)
    for h in range(FOX_HEADS):
        kx_ref[h] = kx[:, h * LANES:(h + 1) * LANES]
        row_ref[h, 0] = rows[h * SUBLANES:(h + 1) * SUBLANES, :]


def fox_gate(ps, bvec, *, tb):
    T = ps.shape[0]
    nb = T // tb
    return pl.pallas_call(
        functools.partial(_fox_gate_kernel, tb=tb),
        out_shape=[jax.ShapeDtypeStruct((FOX_HEADS, T, LANES), BF16),
                   jax.ShapeDtypeStruct((FOX_HEADS, nb, SUBLANES, tb), F32)],
        grid=(nb,),
        in_specs=[pl.BlockSpec((tb, LANES), lambda i: (i, 0)),
                  pl.BlockSpec((1, LANES), lambda i: (0, 0))],
        out_specs=[pl.BlockSpec((FOX_HEADS, tb, LANES), lambda i: (0, i, 0)),
                   pl.BlockSpec((FOX_HEADS, 1, SUBLANES, tb), lambda i: (0, i, 0, 0))],
        scratch_shapes=[pltpu.VMEM((SUBLANES, LANES), F32)],
        compiler_params=_cparams(("arbitrary",)),
        name="fox_gate",
    )(ps, bvec)


def _fox_attn_kernel(q_ref, k_ref, v_ref, kx_ref, cq_ref, wi_ref, o_ref, wo_ref,
                     vt_ref, ya_ref, yb_ref, xa_ref, xb_ref, m_ref, acc_ref, *, tq, tk, nk, nh):
    i = pl.program_id(1)
    D = HEAD_DIM
    heads = range(nh)
    n_full = i // (tk // tq)
    _cast_along([wi_ref], [wo_ref])

    @pl.when(i == 0)
    def _():
        for hh in heads:
            for n in range(nk):
                vt_ref[hh, n, 0:D, :] = v_ref[n * tk:(n + 1) * tk, hh * D:(hh + 1) * D].T
                vt_ref[hh, n, D:, :] = jnp.ones((BF16_ROWS, tk), BF16)

    lane = lax.broadcasted_iota(jnp.int32, (tq, LANES), 1)
    ones = jnp.where(lane < N_SPLIT, 1.0, 0.0).astype(BF16)
    q_aug = [jnp.concatenate([q_ref[:, hh * D:(hh + 1) * D], ones], axis=1) for hh in heads]
    cq = [cq_ref[hh, 0, 0:1, :] for hh in heads]
    m_ref[...] = jnp.full(m_ref.shape, -jnp.inf, F32)
    acc_ref[...] = jnp.zeros(acc_ref.shape, F32)

    def scores(hh, j):
        start = pl.multiple_of(j * tk, tk)
        k_aug = jnp.concatenate([k_ref[pl.ds(start, tk), hh * D:(hh + 1) * D], kx_ref[hh, pl.ds(start, tk), :]],
                                axis=1)
        return _dot_nt(k_aug, q_aug[hh])

    def update(hh, j, y, ymax):
        m_old = m_ref[hh]
        m_new = jnp.maximum(m_old, ymax + cq[hh])
        alpha = jnp.exp2(m_old - m_new)
        p = jnp.exp2(y + (cq[hh] - m_new))
        acc_ref[hh] = alpha * acc_ref[hh] + _dot(vt_ref[hh, j], p.astype(BF16))
        m_ref[hh] = m_new

    def produce(hh, j, y_ref, ymax_ref):
        y = scores(hh, j)
        y_ref[hh] = y
        ymax_ref[hh] = jnp.max(y, axis=0, keepdims=True)

    for hh in heads:
        produce(hh, 0, ya_ref, xa_ref)

    def step(j, cur_ref, cmax_ref, nxt_ref, nmax_ref):
        for hh in heads:
            produce(hh, j + 1, nxt_ref, nmax_ref)
            update(hh, j, cur_ref[hh], cmax_ref[hh])

    def body(j, carry):
        pl.when(j % 2 == 0)(lambda: step(j, ya_ref, xa_ref, yb_ref, xb_ref))
        pl.when(j % 2 == 1)(lambda: step(j, yb_ref, xb_ref, ya_ref, xa_ref))
        return carry

    lax.fori_loop(0, n_full, body, 0)
    key = n_full * tk + lax.broadcasted_iota(jnp.int32, (tk, tq), 0)
    qry = i * tq + lax.broadcasted_iota(jnp.int32, (tk, tq), 1)

    def diagonal(cur_ref):
        for hh in heads:
            y = jnp.where(key <= qry, cur_ref[hh], -jnp.inf)
            update(hh, n_full, y, jnp.max(y, axis=0, keepdims=True))

    pl.when(n_full % 2 == 0)(lambda: diagonal(ya_ref))
    pl.when(n_full % 2 == 1)(lambda: diagonal(yb_ref))
    for hh in heads:
        o_ref[:, hh * D:(hh + 1) * D] = (acc_ref[hh, 0:D, :] / acc_ref[hh, D:D + 1, :]).T.astype(o_ref.dtype)


def fox_attention(proj, kx, row, w_cast, layer, *, tq, tk, nh):
    T = proj.shape[0]
    nq = T // tq
    ng = FOX_HEADS // nh
    gw = nh * HEAD_DIM
    qb, kb, vb = C_FQ // gw, C_FK // gw, C_FV // gw
    assert tk % tq == 0 and T % tk == 0
    (w_in_spec,), (w_shape,), (w_out_spec,) = _cast_slabs([w_cast], layer, ng * nq, lambda g, i: g * nq + i)
    return pl.pallas_call(
        functools.partial(_fox_attn_kernel, tq=tq, tk=tk, nk=T // tk, nh=nh),
        out_shape=[jax.ShapeDtypeStruct((T, FOX_HEADS * HEAD_DIM), BF16), w_shape],
        grid=(ng, nq),
        in_specs=[
            pl.BlockSpec((tq, gw), lambda g, i: (i, qb + g)),
            pl.BlockSpec((T, gw), lambda g, i: (0, kb + g), pipeline_mode=pl.Buffered(1)),
            pl.BlockSpec((T, gw), lambda g, i: (0, vb + g), pipeline_mode=pl.Buffered(1)),
            pl.BlockSpec((nh, T, LANES), lambda g, i: (g, 0, 0), pipeline_mode=pl.Buffered(1)),
            pl.BlockSpec((nh, 1, SUBLANES, tq), lambda g, i: (g, i, 0, 0)),
            w_in_spec,
        ],
        out_specs=[pl.BlockSpec((tq, gw), lambda g, i: (i, g)), w_out_spec],
        scratch_shapes=[pltpu.VMEM((nh, T // tk, HEAD_DIM + BF16_ROWS, tk), BF16),
                        pltpu.VMEM((nh, tk, tq), F32), pltpu.VMEM((nh, tk, tq), F32),
                        pltpu.VMEM((nh, 1, tq), F32), pltpu.VMEM((nh, 1, tq), F32),
                        pltpu.VMEM((nh, 1, tq), F32),
                        pltpu.VMEM((nh, HEAD_DIM + BF16_ROWS, tq), F32)],
        compiler_params=_cparams(("arbitrary", "arbitrary")),
        name="fox_attn",
    )(proj, proj, proj, kx, row, w_cast)


def _t5_table_kernel(t5_ref, o_ref):
    h = pl.program_id(0)
    W = SWA_WINDOW
    i = lax.broadcasted_iota(jnp.int32, (W, 2 * W), 0)
    j = lax.broadcasted_iota(jnp.int32, (W, 2 * W), 1)
    rel = (W + i) - j
    n = jnp.maximum(rel, 0)
    max_exact = NUM_BUCKETS // 2
    nf = jnp.maximum(n, 1).astype(F32)
    large = max_exact + (jnp.log(nf / max_exact) / math.log(T5_MAX_DISTANCE / max_exact)
                         * (NUM_BUCKETS - max_exact)).astype(jnp.int32)
    large = jnp.minimum(large, NUM_BUCKETS - 1)
    bucket = jnp.where(n < max_exact, n, large)
    bias = jnp.zeros((W, 2 * W), F32)
    for b in range(NUM_BUCKETS):
        bias = jnp.where(bucket == b, t5_ref[b, h], bias)
    in_window = (rel >= 0) & (rel < W)
    o_ref[...] = jnp.where(in_window, bias, -jnp.inf)


def t5_table(t5_bias):
    W = SWA_WINDOW
    return pl.pallas_call(
        _t5_table_kernel,
        out_shape=jax.ShapeDtypeStruct((SWA_HEADS, W, 2 * W), F32),
        grid=(SWA_HEADS,),
        in_specs=[pl.BlockSpec(memory_space=pltpu.SMEM)],
        out_specs=pl.BlockSpec((None, W, 2 * W), lambda h: (h, 0, 0)),
        compiler_params=_cparams(("arbitrary",)),
        name="t5_table",
    )(t5_bias)


def _cast_slabs(ws, layer, nsteps, step):
    in_specs, shapes, out_specs = [], [], []
    for w in ws:
        _, rows, cols = w.shape
        slab = rows // nsteps
        assert slab * nsteps == rows and slab % BF16_ROWS == 0
        in_specs.append(pl.BlockSpec((None, slab, cols), lambda *ids: (layer, step(*ids), 0)))
        out_specs.append(pl.BlockSpec((slab, cols), lambda *ids: (step(*ids), 0)))
        shapes.append(jax.ShapeDtypeStruct((rows, cols), BF16))
    return in_specs, shapes, out_specs


def _cast_along(wi_refs, wo_refs):
    for wi_ref, wo_ref in zip(wi_refs, wo_refs, strict=True):
        wo_ref[...] = wi_ref[...].astype(wo_ref.dtype)


def _swa_kernel(sink_ref, q_ref, kc_ref, kp_ref, vc_ref, vp_ref, tab_ref, *refs, nsub, ncast):
    wi_refs, o_ref, wo_refs = refs[:ncast], refs[ncast], refs[ncast + 1:]
    g = pl.program_id(0)
    n = pl.program_id(1)
    W = SWA_WINDOW
    _cast_along(wi_refs, wo_refs)

    first_prev = lax.broadcasted_iota(jnp.int32, (W, 2 * W), 1) < W
    for sub in range(nsub):
        if sub == 0:
            kp, vp = kp_ref[...], vp_ref[...]
        else:
            kp, vp = kc_ref[(sub - 1) * W:sub * W, :], vc_ref[(sub - 1) * W:sub * W, :]
        kk = jnp.concatenate([kp, kc_ref[sub * W:(sub + 1) * W, :]], axis=0)
        vv = jnp.concatenate([vp, vc_ref[sub * W:(sub + 1) * W, :]], axis=0)
        for hh in range(SWA_GROUP):
            q = q_ref[sub * W:(sub + 1) * W, hh * HEAD_DIM:(hh + 1) * HEAD_DIM]
            s = _dot_nt(q, kk) + tab_ref[hh]
            if sub == 0:
                s = jnp.where(jnp.logical_and(first_prev, n == 0), -jnp.inf, s)
            sink = sink_ref[0, g * SWA_GROUP + hh]
            m = jnp.maximum(jnp.max(s, axis=1, keepdims=True), sink)
            p = jnp.exp(s - m)
            l = jnp.sum(p, axis=1, keepdims=True) + jnp.exp(sink - m)
            o = _dot(p.astype(BF16), vv) / l
            o_ref[sub * W:(sub + 1) * W, hh * HEAD_DIM:(hh + 1) * HEAD_DIM] = o.astype(o_ref.dtype)


def swa_attention(proj, table, sinks, w_cast, layer, *, tb):
    T = proj.shape[0]
    W = SWA_WINDOW
    nsub = tb // W
    nt = T // tb
    gw = SWA_GROUP * HEAD_DIM
    qb, kb, vb = C_SQ // gw, C_SK // HEAD_DIM, C_SV // HEAD_DIM
    prev = lambda n: jnp.maximum(n * nsub - 1, 0)
    w_in_specs, w_shapes, w_out_specs = _cast_slabs(w_cast, layer, SWA_KV_HEADS * nt, lambda g, n: g * nt + n)
    o, *w_bf16 = pl.pallas_call(
        functools.partial(_swa_kernel, nsub=nsub, ncast=len(w_cast)),
        out_shape=[jax.ShapeDtypeStruct((T, SWA_HEADS * HEAD_DIM), BF16), *w_shapes],
        grid=(SWA_KV_HEADS, nt),
        in_specs=[
            pl.BlockSpec(memory_space=pltpu.SMEM),
            pl.BlockSpec((tb, gw), lambda g, n: (n, qb + g)),
            pl.BlockSpec((tb, HEAD_DIM), lambda g, n: (n, kb + g)),
            pl.BlockSpec((W, HEAD_DIM), lambda g, n: (prev(n), kb + g)),
            pl.BlockSpec((tb, HEAD_DIM), lambda g, n: (n, vb + g)),
            pl.BlockSpec((W, HEAD_DIM), lambda g, n: (prev(n), vb + g)),
            pl.BlockSpec((SWA_GROUP, W, 2 * W), lambda g, n: (g, 0, 0)),
            *w_in_specs,
        ],
        out_specs=[pl.BlockSpec((tb, gw), lambda g, n: (n, g)), *w_out_specs],
        compiler_params=_cparams(("arbitrary", "arbitrary")),
        name="swa_attn",
    )(sinks.reshape(1, SWA_HEADS), proj, proj, proj, proj, proj, table, *w_cast)
    return o, w_bf16


def _gla_kernel(q_ref, k_ref, v_ref, r_ref, ps_ref, wg_ref, bg_ref, gn_ref, wi_ref, o_ref, wo_ref,
                st_ref, oc_ref, *, tb):
    C = GLA_CHUNK
    _cast_along([wi_ref], [wo_ref])

    @pl.when(pl.program_id(0) == 0)
    def _():
        st_ref[...] = jnp.zeros_like(st_ref)

    glr = ps_ref[...].astype(BF16)
    g = jax.nn.log_sigmoid(_dot(glr, wg_ref[...]) + bg_ref[...]) / GLA_TAU
    r = lax.broadcasted_iota(jnp.int32, (tb, tb), 0)
    c = lax.broadcasted_iota(jnp.int32, (tb, tb), 1)
    same = (r // C) == (c // C)
    tril = jnp.where(jnp.logical_and(same, c <= r), 1.0, 0.0).astype(BF16)
    whole = jnp.where(same, 1.0, 0.0).astype(BF16)
    gh, gm, gl = _split3(g)
    b = _dot(tril, gh) + _dot(tril, gm) + _dot(tril, gl)
    b_last = _dot(whole, gh) + _dot(whole, gm) + _dot(whole, gl)
    q_t = (q_ref[...].astype(F32) * jnp.exp(b)).astype(BF16)
    kf = k_ref[...].astype(F32)
    k_t = kf * jnp.exp(-b)
    k_end = kf * jnp.exp(b_last - b)
    decay = jnp.exp(b_last)
    lane = lax.broadcasted_iota(jnp.int32, (1, LANES), 1)
    causal = (lax.broadcasted_iota(jnp.int32, (C, C), 1) <= lax.broadcasted_iota(jnp.int32, (C, C), 0))
    heads = []
    for h in range(GLA_HEADS):
        pair = slice((h // 2) * LANES, (h // 2 + 1) * LANES)
        mine = (lane // GLA_DK) == (h % 2)
        ktm = jnp.where(mine, k_t[:, pair], 0.0).astype(BF16)
        kem = jnp.where(mine, k_end[:, pair], 0.0).astype(BF16)
        vh = v_ref[:, h * GLA_DV:(h + 1) * GLA_DV]
        heads.append((ktm, kem, q_t[:, pair], decay[:, pair], vh))
    sts = [st_ref[h] for h in range(GLA_HEADS)]
    chunks = [slice(n * C, (n + 1) * C) for n in range(tb // C)]
    intra = [[_dot(jnp.where(causal, _dot_nt(qh[rows], ktm[rows]), 0.0).astype(BF16), vh[rows])
              for (ktm, kem, qh, dec, vh) in heads] for rows in chunks]
    kv_t = [[_dot_tn(vh[rows], kem[rows]) for (ktm, kem, qh, dec, vh) in heads] for rows in chunks]
    for n, rows in enumerate(chunks):
        for h, (ktm, kem, qh, dec, vh) in enumerate(heads):
            oc_ref[rows, h * GLA_DV:(h + 1) * GLA_DV] = intra[n][h] + _dot_nt(qh[rows], sts[h].astype(BF16))
            sts[h] = dec[n * C:n * C + 1, :] * sts[h] + kv_t[n][h]
    for h in range(GLA_HEADS):
        st_ref[h] = sts[h]
    for h in range(GLA_HEADS):
        cols = slice(h * GLA_DV, (h + 1) * GLA_DV)
        o = _rms(oc_ref[:, cols], gn_ref[...])
        o_ref[:, cols] = (o * jax.nn.silu(r_ref[:, cols].astype(F32))).astype(o_ref.dtype)


def gla(proj, ps, wg_pad, bg, gnorm, w_cast, layer, *, tb):
    T = proj.shape[0]
    qw = GLA_HEADS * GLA_DK
    vw = GLA_HEADS * GLA_DV
    (w_in_spec,), (w_shape,), (w_out_spec,) = _cast_slabs([w_cast], layer, T // tb, lambda n: n)
    return pl.pallas_call(
        functools.partial(_gla_kernel, tb=tb),
        out_shape=[jax.ShapeDtypeStruct((T, vw), BF16), w_shape],
        grid=(T // tb,),
        in_specs=[
            pl.BlockSpec((tb, qw), lambda n: (n, C_GQ // qw)),
            pl.BlockSpec((tb, qw), lambda n: (n, C_GK // qw)),
            pl.BlockSpec((tb, vw), lambda n: (n, C_GV // vw)),
            pl.BlockSpec((tb, vw), lambda n: (n, C_GR // vw)),
            pl.BlockSpec((tb, LANES), lambda n: (n, 0)),
            pl.BlockSpec((LANES, qw), lambda n: (0, 0)),
            pl.BlockSpec((1, qw), lambda n: (0, 0)),
            pl.BlockSpec((1, GLA_DV), lambda n: (0, 0)),
            w_in_spec,
        ],
        out_specs=[pl.BlockSpec((tb, vw), lambda n: (n, 0)), w_out_spec],
        scratch_shapes=[pltpu.VMEM((GLA_HEADS, GLA_DV, LANES), F32),
                        pltpu.VMEM((tb, vw), F32)],
        compiler_params=_cparams(("arbitrary",)),
        name="gla",
    )(proj, proj, proj, proj, ps, wg_pad, bg, gnorm, w_cast)


def _mix_xattn_kernel(h_ref, of_ref, os_ref, og_ref, wo_ref, gx_ref, wq_ref, kv_ref, wx_ref, o_ref):
    nf, ns = of_ref.shape[1], os_ref.shape[1]
    mix = (_dot(of_ref[...], wo_ref[0:nf, :]) + _dot(os_ref[...], wo_ref[nf:nf + ns, :])
           + _dot(og_ref[...], wo_ref[nf + ns:, :]))
    h1 = h_ref[...] + mix
    xn = _rms(h1, gx_ref[...]).astype(BF16)
    q = (_dot(xn, wq_ref[...]) * (HEAD_DIM ** -0.5)).astype(BF16)
    xw = XATTN_HEADS * HEAD_DIM
    outs = []
    for hh in range(XATTN_HEADS):
        cols = slice(hh * HEAD_DIM, (hh + 1) * HEAD_DIM)
        k = kv_ref[:, cols]
        v = kv_ref[:, xw + hh * HEAD_DIM:xw + (hh + 1) * HEAD_DIM]
        s = _dot_nt(q[:, cols], k)
        p = jnp.exp(s - jnp.max(s, axis=1, keepdims=True))
        l = jnp.sum(p, axis=1, keepdims=True)
        outs.append((_dot(p.astype(BF16), v) / l).astype(BF16))
    o = jnp.concatenate(outs, axis=1)
    o_ref[...] = h1 + _dot(o, wx_ref[...])


def mix_xattn(h, o_fox, o_swa, o_gla, w_out, gx, wq, kv, wo, *, tm):
    T, D = h.shape
    const = lambda shape: pl.BlockSpec(shape, lambda i: (0, 0))
    row = lambda w: pl.BlockSpec((tm, w), lambda i: (i, 0))
    return pl.pallas_call(
        _mix_xattn_kernel,
        out_shape=jax.ShapeDtypeStruct((T, D), F32),
        grid=(T // tm,),
        in_specs=[row(D), row(o_fox.shape[1]), row(o_swa.shape[1]), row(o_gla.shape[1]),
                  const(w_out.shape), const((1, D)), const(wq.shape), const(kv.shape), const(wo.shape)],
        out_specs=row(D),
        compiler_params=_cparams(("arbitrary",)),
        name="mix_xattn",
    )(h, o_fox, o_swa, o_gla, w_out, gx.reshape(1, D), wq, kv, wo)


def _ffn_kernel(x_ref, g_ref, wg_ref, wv_ref, cwg_ref, cwv_ref, cbg_ref, cbv_ref, wd_ref, fg_ref,
                o_ref, xn_ref, hg_ref, hv_ref, ug_ref, uv_ref, *, tm, sub, final):
    i = pl.program_id(0)
    f = pl.program_id(1)
    H = SUBLANES
    fc = wd_ref.shape[0]

    @pl.when(f == 0)
    def _():
        x = x_ref[...]
        xn_ref[...] = _rms(x, g_ref[...]).astype(BF16)
        o_ref[...] = x

    @pl.when(i == 0)
    def _():
        hg_ref[f] = jnp.zeros(hg_ref.shape[1:], F32)
        hv_ref[f] = jnp.zeros(hv_ref.shape[1:], F32)

    xn = xn_ref[...]

    def up(c, slot):
        ug_ref[slot, H:, :] = _dot(xn, wg_ref[:, c:c + sub])
        uv_ref[slot, H:, :] = _dot(xn, wv_ref[:, c:c + sub])

    def conv(u_ref, slot, h_ref, cw_ref, cb_ref, c):
        u_ref[slot, 0:H, :] = h_ref[f, :, c:c + sub]
        h_ref[f, :, c:c + sub] = u_ref[slot, tm:tm + H, :]
        return (cb_ref[:, c:c + sub] + cw_ref[2:3, c:c + sub] * u_ref[slot, H:H + tm, :]
                + cw_ref[1:2, c:c + sub] * u_ref[slot, H - 1:H - 1 + tm, :]
                + cw_ref[0:1, c:c + sub] * u_ref[slot, H - 2:H - 2 + tm, :])

    def down(c, slot):
        gate = conv(ug_ref, slot, hg_ref, cwg_ref, cbg_ref, c)
        val = conv(uv_ref, slot, hv_ref, cwv_ref, cbv_ref, c)
        act = (jax.nn.silu(gate) * val).astype(BF16)
        o_ref[...] += _dot(act, wd_ref[c:c + sub, :])

    up(0, 0)
    for n, c in enumerate(range(0, fc, sub)):
        if c + sub < fc:
            up(c + sub, (n + 1) % 2)
        down(c, n % 2)

    if final:
        @pl.when(f == pl.num_programs(1) - 1)
        def _():
            o_ref[...] = _rms(o_ref[...], fg_ref[...])


def conv_ffn(x, g, w_up, conv_w, conv_b, w_down, layer, final_g, *, tm, fc, final):
    T, D = x.shape
    dff = w_down.shape[0]
    L = conv_w.shape[0]
    nf = dff // fc
    cb = conv_b.reshape(L, 1, 2 * dff)
    return pl.pallas_call(
        functools.partial(_ffn_kernel, tm=tm, sub=MXU_COLS, final=final),
        out_shape=jax.ShapeDtypeStruct((T, D), F32),
        grid=(T // tm, nf),
        in_specs=[
            pl.BlockSpec((tm, D), lambda i, f: (i, 0)),
            pl.BlockSpec((1, D), lambda i, f: (0, 0)),
            pl.BlockSpec((D, fc), lambda i, f: (0, f)),
            pl.BlockSpec((D, fc), lambda i, f: (0, nf + f)),
            pl.BlockSpec((None, CONV_WIDTH, fc), lambda i, f: (layer, 0, f)),
            pl.BlockSpec((None, CONV_WIDTH, fc), lambda i, f: (layer, 0, nf + f)),
            pl.BlockSpec((None, 1, fc), lambda i, f: (layer, 0, f)),
            pl.BlockSpec((None, 1, fc), lambda i, f: (layer, 0, nf + f)),
            pl.BlockSpec((fc, D), lambda i, f: (f, 0)),
            pl.BlockSpec((1, D), lambda i, f: (0, 0)),
        ],
        out_specs=pl.BlockSpec((tm, D), lambda i, f: (i, 0)),
        scratch_shapes=[pltpu.VMEM((tm, D), BF16), pltpu.VMEM((nf, SUBLANES, fc), F32),
                        pltpu.VMEM((nf, SUBLANES, fc), F32),
                        pltpu.VMEM((2, tm + SUBLANES, MXU_COLS), F32),
                        pltpu.VMEM((2, tm + SUBLANES, MXU_COLS), F32)],
        compiler_params=_cparams(("arbitrary", "arbitrary")),
        name="conv_ffn",
    )(x, g.reshape(1, D), w_up, w_up, conv_w, conv_w, cb, cb, w_down, final_g.reshape(1, D))


TR_PREP, TC_PREP = 512, 512
TM_PROJ, TN_PROJ = 1024, 1536
TB_GATE = 512
TQ_FOX, TK_FOX, NH_FOX = 512, 512, 4
TB_SWA = 512
TB_GLA = 512
TM_MIX = 512
TM_FFN, FC_FFN = 512, 512


def _main_colscale():
    cs = np.ones((1, MAIN_COLS), np.float32)
    cs[0, C_FQ:C_FK] = HEAD_DIM ** -0.5 * LOG2E
    cs[0, C_SQ:C_SK] = HEAD_DIM ** -0.5
    cs[0, C_GQ:C_GK] = GLA_DK ** -0.5
    return jnp.asarray(cs)


def kernel(x, mem, w_in, b_fox_f, swa_sinks, t5_bias, w_gla_gate, b_gla_gate, gla_norm, w_mix_out, norm_mix,
           norm_xattn, norm_mem, wq_x, wkv_x, wo_x, norm_ffn, w_up, conv_w, conv_b, w_down, final_norm):
    depth = w_in.shape[0]
    _, T, D = x.shape
    M = mem.shape[1]
    h = x.reshape(T, D)
    memf = mem.reshape(M, D)
    colscale = _main_colscale()
    ones_kv = jnp.ones((1, wkv_x.shape[2]), F32)
    table = t5_table(t5_bias)
    pad_lanes = LANES - S_GLR - GLA_GATE_RANK
    w_main, w_small = prep_w_in(w_in, tr=TR_PREP, tc=TC_PREP)
    for l in range(depth):
        bvec = jnp.concatenate([b_fox_f[l], jnp.zeros((LANES - FOX_HEADS,), F32)]).reshape(1, LANES)
        wg_pad = jnp.concatenate([jnp.zeros((S_GLR, GLA_HEADS * GLA_DK), F32), w_gla_gate[l],
                                  jnp.zeros((pad_lanes, GLA_HEADS * GLA_DK), F32)], axis=0).astype(BF16)

        proj, ps = norm_proj(h, norm_mix[l], w_main, l, colscale, w_small, tm=TM_PROJ, tn=TN_PROJ, w_t=True)
        kx, row = fox_gate(ps, bvec, tb=TB_GATE)
        o_fox, w_down_b = fox_attention(proj, kx, row, w_down, l, tq=TQ_FOX, tk=TK_FOX, nh=NH_FOX)
        o_swa, (w_out_b, wq_b, wkv_b, wo_b) = swa_attention(proj, table, swa_sinks[l],
                                                           (w_mix_out, wq_x, wkv_x, wo_x), l, tb=TB_SWA)
        o_gla, w_up_b = gla(proj, ps, wg_pad, b_gla_gate[l].reshape(1, -1), gla_norm[l].reshape(1, -1),
                            w_up, l, tb=TB_GLA)
        kv = norm_proj(memf, norm_mem[l], wkv_b[None], 0, ones_kv, tm=M, tn=wkv_x.shape[2])
        h = mix_xattn(h, o_fox, o_swa, o_gla, w_out_b, norm_xattn[l], wq_b, kv, wo_b, tm=TM_MIX)
        h = conv_ffn(h, norm_ffn[l], w_up_b, conv_w, conv_b, w_down_b, l, final_norm,
                     tm=TM_FFN, fc=FC_FFN, final=(l == depth - 1))
    return h.reshape(x.shape)
```

```python
import functools
import math

import numpy as np
import jax
import jax.numpy as jnp
from jax import lax
from jax.experimental import pallas as pl
from jax.experimental.pallas import tpu as pltpu

F32 = jnp.float32
BF16 = jnp.bfloat16

HEAD_DIM = 128
FOX_HEADS = 4
SWA_HEADS = 8
SWA_KV_HEADS = 2
SWA_GROUP = SWA_HEADS // SWA_KV_HEADS
GLA_HEADS = 4
GLA_DK = 64
GLA_DV = 128
GLA_GATE_RANK = 16
GLA_TAU = 16.0
GLA_CHUNK = 64
SWA_WINDOW = 128
NUM_BUCKETS = 32
T5_MAX_DISTANCE = 128
XATTN_HEADS = 4
CONV_WIDTH = 3
EPS = 1e-6
LOG2E = math.log2(math.e)

LANES = 128
SUBLANES = 8
BF16_ROWS = 16
MXU_COLS = 256
VMEM_LIMIT = 56 * 1024 * 1024

_FOX_W = FOX_HEADS * HEAD_DIM
_OFF_FF = 3 * _FOX_W
_OFF_SQ = _OFF_FF + FOX_HEADS
_MAIN_B = (SWA_HEADS + 2 * SWA_KV_HEADS) * HEAD_DIM + 2 * GLA_HEADS * GLA_DK + 2 * GLA_HEADS * GLA_DV
_OFF_GLR = _OFF_SQ + _MAIN_B
C_FQ, C_FK, C_FV = 0, 512, 1024
C_SQ, C_SK, C_SV = 1536, 2560, 2816
C_GQ, C_GK, C_GV, C_GR = 3072, 3328, 3584, 4096
MAIN_COLS = 4608
S_FF, S_GLR = 0, SUBLANES


def _cparams(sem):
    return pltpu.CompilerParams(dimension_semantics=sem, vmem_limit_bytes=VMEM_LIMIT)


def _rms(x, g):
    return x * lax.rsqrt(jnp.mean(x * x, axis=-1, keepdims=True) + EPS) * g


N_SPLIT = 3


def _split3(x):
    hi = x.astype(BF16)
    r1 = x - hi.astype(F32)
    mid = r1.astype(BF16)
    lo = (r1 - mid.astype(F32)).astype(BF16)
    return hi, mid, lo


def _dot(a, b):
    return jnp.dot(a, b, preferred_element_type=F32)


def _dot_nt(a, b):
    return lax.dot_general(a, b, (((1,), (1,)), ((), ())), preferred_element_type=F32)


def _dot_tn(a, b):
    return lax.dot_general(a, b, (((0,), (0,)), ((), ())), preferred_element_type=F32)


def _sel_dot(sel, x):
    hi, mid, lo = _split3(x)
    return _dot(sel, hi) + _dot(sel, mid) + _dot(sel, lo)


def _prep_w_in_kernel(cur_ref, ff_ref, glr_ref, wm_ref, ws_ref):
    L = wm_ref.shape[0]
    wm_ref[...] = pltpu.einshape("cld->lcd", cur_ref[...]).astype(BF16)

    @pl.when(pl.program_id(1) == 0)
    def _():
        pad = jnp.zeros((LANES - S_GLR - GLA_GATE_RANK, ws_ref.shape[2]), F32)
        for l in range(L):
            ws_ref[l] = jnp.concatenate([ff_ref[:, l, :], glr_ref[:, l, :], pad], axis=0).astype(BF16)


def prep_w_in(w_in, *, tr, tc):
    L, D, _ = w_in.shape
    wt = jnp.transpose(w_in, (2, 0, 1))
    assert _OFF_FF % tc == 0 and MAIN_COLS % tc == 0 and S_GLR % SUBLANES == 0

    def src(j):
        return j * tc + jnp.where(j * tc >= _OFF_FF, _OFF_SQ - _OFF_FF, 0)

    return pl.pallas_call(
        _prep_w_in_kernel,
        out_shape=[jax.ShapeDtypeStruct((L, MAIN_COLS, D), BF16),
                   jax.ShapeDtypeStruct((L, LANES, D), BF16)],
        grid=(D // tr, MAIN_COLS // tc),
        in_specs=[
            pl.BlockSpec((pl.Element(tc), pl.Element(L), pl.Element(tr)), lambda r, j: (src(j), 0, r * tr)),
            pl.BlockSpec((pl.Element(S_GLR), pl.Element(L), pl.Element(tr)), lambda r, j: (_OFF_FF, 0, r * tr)),
            pl.BlockSpec((pl.Element(GLA_GATE_RANK), pl.Element(L), pl.Element(tr)),
                         lambda r, j: (_OFF_GLR, 0, r * tr)),
        ],
        out_specs=[pl.BlockSpec((L, tc, tr), lambda r, j: (0, j, r)),
                   pl.BlockSpec((L, LANES, tr), lambda r, j: (0, 0, r))],
        compiler_params=_cparams(("arbitrary", "arbitrary")),
        name="prep_w_in",
    )(wt, wt, wt)


def _norm_proj_kernel(*refs, has_small, w_t):
    if has_small:
        x_ref, g_ref, w_ref, cs_ref, ws_ref, o_ref, os_ref, xn_ref = refs
    else:
        x_ref, g_ref, w_ref, cs_ref, o_ref, xn_ref = refs
    dot = _dot_nt if w_t else _dot

    @pl.when(pl.program_id(1) == 0)
    def _():
        xn = _rms(x_ref[...], g_ref[...]).astype(BF16)
        xn_ref[...] = xn
        if has_small:
            os_ref[...] = dot(xn, ws_ref[...])

    o_ref[...] = (dot(xn_ref[...], w_ref[...]) * cs_ref[...]).astype(o_ref.dtype)


def norm_proj(x, g, w, layer, colscale, w_small=None, *, tm, tn, w_t=False):
    T, D = x.shape
    N = w.shape[1] if w_t else w.shape[2]
    has_small = w_small is not None

    def wspec(n):
        if w_t:
            return pl.BlockSpec((None, n, D), lambda i, j: (layer, j, 0))
        return pl.BlockSpec((None, D, n), lambda i, j: (layer, 0, j))

    in_specs = [
        pl.BlockSpec((tm, D), lambda i, j: (i, 0)),
        pl.BlockSpec((1, D), lambda i, j: (0, 0)),
        wspec(tn),
        pl.BlockSpec((1, tn), lambda i, j: (0, j)),
    ]
    args = [x, g.reshape(1, D), w, colscale]
    out_shape = [jax.ShapeDtypeStruct((T, N), BF16)]
    out_specs = [pl.BlockSpec((tm, tn), lambda i, j: (i, j))]
    if has_small:
        in_specs.append(pl.BlockSpec((None,) + w_small.shape[1:], lambda i, j: (layer, 0, 0)))
        args.append(w_small)
        out_shape.append(jax.ShapeDtypeStruct((T, LANES), F32))
        out_specs.append(pl.BlockSpec((tm, LANES), lambda i, j: (i, 0)))
    outs = pl.pallas_call(
        functools.partial(_norm_proj_kernel, has_small=has_small, w_t=w_t),
        out_shape=out_shape,
        grid=(T // tm, N // tn),
        in_specs=in_specs,
        out_specs=out_specs,
        scratch_shapes=[pltpu.VMEM((tm, D), BF16)],
        compiler_params=_cparams(("arbitrary", "arbitrary")),
        name="norm_proj_small" if has_small else "norm_proj",
    )(*args)
    return outs if has_small else outs[0]


def _fox_gate_kernel(ps_ref, b_ref, kx_ref, row_ref, carry_ref, *, tb):
    @pl.when(pl.program_id(0) == 0)
    def _():
        carry_ref[...] = jnp.zeros_like(carry_ref)

    x = jax.nn.log_sigmoid(ps_ref[...] + b_ref[...])
    r = lax.broadcasted_iota(jnp.int32, (tb, tb), 0)
    c = lax.broadcasted_iota(jnp.int32, (tb, tb), 1)
    tril = jnp.where(c <= r, 1.0, 0.0).astype(BF16)
    csum = _sel_dot(tril, x) + carry_ref[0:1, :]
    carry_ref[...] = jnp.broadcast_to(csum[tb - 1:tb, :], carry_ref.shape)
    pieces = _split3(csum * LOG2E)
    neg = [-p for p in pieces]
    src = lax.broadcasted_iota(jnp.int32, (LANES, FOX_HEADS * LANES), 0)
    dst = lax.broadcasted_iota(jnp.int32, (LANES, FOX_HEADS * LANES), 1)
    kx = sum(_dot(neg[n], jnp.where((dst // LANES == src - S_FF) & (dst % LANES == n), 1.0, 0.0).astype(BF16))
             for n in range(len(neg))).astype(BF16)
    sub = lax.broadcasted_iota(jnp.int32, (FOX_HEADS * SUBLANES, LANES), 0)
    lane = lax.broadcasted_iota(jnp.int32, (FOX_HEADS * SUBLANES, LANES), 1)
    pick = jnp.where(lane - S_FF == sub // SUBLANES, 1.0, 0.0).astype(BF16)
    rows = sum(_dot_nt(pick, p) for p in pieces)
    for h in range(FOX_HEADS):
        kx_ref[h] = kx[:, h * LANES:(h + 1) * LANES]
        row_ref[h, 0] = rows[h * SUBLANES:(h + 1) * SUBLANES, :]


def fox_gate(ps, bvec, *, tb):
    T = ps.shape[0]
    nb = T // tb
    return pl.pallas_call(
        functools.partial(_fox_gate_kernel, tb=tb),
        out_shape=[jax.ShapeDtypeStruct((FOX_HEADS, T, LANES), BF16),
                   jax.ShapeDtypeStruct((FOX_HEADS, nb, SUBLANES, tb), F32)],
        grid=(nb,),
        in_specs=[pl.BlockSpec((tb, LANES), lambda i: (i, 0)),
                  pl.BlockSpec((1, LANES), lambda i: (0, 0))],
        out_specs=[pl.BlockSpec((FOX_HEADS, tb, LANES), lambda i: (0, i, 0)),
                   pl.BlockSpec((FOX_HEADS, 1, SUBLANES, tb), lambda i: (0, i, 0, 0))],
        scratch_shapes=[pltpu.VMEM((SUBLANES, LANES), F32)],
        compiler_params=_cparams(("arbitrary",)),
        name="fox_gate",
    )(ps, bvec)


def _fox_attn_kernel(q_ref, k_ref, v_ref, kx_ref, cq_ref, wi_ref, o_ref, wo_ref,
                     vt_ref, ya_ref, yb_ref, xa_ref, xb_ref, m_ref, acc_ref, *, tq, tk, nk, nh):
    i = pl.program_id(1)
    D = HEAD_DIM
    heads = range(nh)
    n_full = i // (tk // tq)
    _cast_along([wi_ref], [wo_ref])

    @pl.when(i == 0)
    def _():
        for hh in heads:
            for n in range(nk):
                vt_ref[hh, n, 0:D, :] = v_ref[n * tk:(n + 1) * tk, hh * D:(hh + 1) * D].T
                vt_ref[hh, n, D:, :] = jnp.ones((BF16_ROWS, tk), BF16)

    lane = lax.broadcasted_iota(jnp.int32, (tq, LANES), 1)
    ones = jnp.where(lane < N_SPLIT, 1.0, 0.0).astype(BF16)
    q_aug = [jnp.concatenate([q_ref[:, hh * D:(hh + 1) * D], ones], axis=1) for hh in heads]
    cq = [cq_ref[hh, 0, 0:1, :] for hh in heads]
    m_ref[...] = jnp.full(m_ref.shape, -jnp.inf, F32)
    acc_ref[...] = jnp.zeros(acc_ref.shape, F32)

    def scores(hh, j):
        start = pl.multiple_of(j * tk, tk)
        k_aug = jnp.concatenate([k_ref[pl.ds(start, tk), hh * D:(hh + 1) * D], kx_ref[hh, pl.ds(start, tk), :]],
                                axis=1)
        return _dot_nt(k_aug, q_aug[hh])

    def update(hh, j, y, ymax):
        m_old = m_ref[hh]
        m_new = jnp.maximum(m_old, ymax + cq[hh])
        alpha = jnp.exp2(m_old - m_new)
        p = jnp.exp2(y + (cq[hh] - m_new))
        acc_ref[hh] = alpha * acc_ref[hh] + _dot(vt_ref[hh, j], p.astype(BF16))
        m_ref[hh] = m_new

    def produce(hh, j, y_ref, ymax_ref):
        y = scores(hh, j)
        y_ref[hh] = y
        ymax_ref[hh] = jnp.max(y, axis=0, keepdims=True)

    for hh in heads:
        produce(hh, 0, ya_ref, xa_ref)

    def step(j, cur_ref, cmax_ref, nxt_ref, nmax_ref):
        for hh in heads:
            produce(hh, j + 1, nxt_ref, nmax_ref)
            update(hh, j, cur_ref[hh], cmax_ref[hh])

    def body(j, carry):
        pl.when(j % 2 == 0)(lambda: step(j, ya_ref, xa_ref, yb_ref, xb_ref))
        pl.when(j % 2 == 1)(lambda: step(j, yb_ref, xb_ref, ya_ref, xa_ref))
        return carry

    lax.fori_loop(0, n_full, body, 0)
    key = n_full * tk + lax.broadcasted_iota(jnp.int32, (tk, tq), 0)
    qry = i * tq + lax.broadcasted_iota(jnp.int32, (tk, tq), 1)

    def diagonal(cur_ref):
        for hh in heads:
            y = jnp.where(key <= qry, cur_ref[hh], -jnp.inf)
            update(hh, n_full, y, jnp.max(y, axis=0, keepdims=True))
            o_ref[:, hh * D:(hh + 1) * D] = (acc_ref[hh, 0:D, :] / acc_ref[hh, D:D + 1, :]).T.astype(o_ref.dtype)

    pl.when(n_full % 2 == 0)(lambda: diagonal(ya_ref))
    pl.when(n_full % 2 == 1)(lambda: diagonal(yb_ref))


def fox_attention(proj, kx, row, w_cast, layer, *, tq, tk, nh):
    T = proj.shape[0]
    nq = T // tq
    ng = FOX_HEADS // nh
    gw = nh * HEAD_DIM
    qb, kb, vb = C_FQ // gw, C_FK // gw, C_FV // gw
    assert tk % tq == 0 and T % tk == 0
    (w_in_spec,), (w_shape,), (w_out_spec,) = _cast_slabs([w_cast], layer, ng * nq, lambda g, i: g * nq + i)
    return pl.pallas_call(
        functools.partial(_fox_attn_kernel, tq=tq, tk=tk, nk=T // tk, nh=nh),
        out_shape=[jax.ShapeDtypeStruct((T, FOX_HEADS * HEAD_DIM), BF16), w_shape],
        grid=(ng, nq),
        in_specs=[
            pl.BlockSpec((tq, gw), lambda g, i: (i, qb + g)),
            pl.BlockSpec((T, gw), lambda g, i: (0, kb + g), pipeline_mode=pl.Buffered(1)),
            pl.BlockSpec((T, gw), lambda g, i: (0, vb + g), pipeline_mode=pl.Buffered(1)),
            pl.BlockSpec((nh, T, LANES), lambda g, i: (g, 0, 0), pipeline_mode=pl.Buffered(1)),
            pl.BlockSpec((nh, 1, SUBLANES, tq), lambda g, i: (g, i, 0, 0)),
            w_in_spec,
        ],
        out_specs=[pl.BlockSpec((tq, gw), lambda g, i: (i, g)), w_out_spec],
        scratch_shapes=[pltpu.VMEM((nh, T // tk, HEAD_DIM + BF16_ROWS, tk), BF16),
                        pltpu.VMEM((nh, tk, tq), F32), pltpu.VMEM((nh, tk, tq), F32),
                        pltpu.VMEM((nh, 1, tq), F32), pltpu.VMEM((nh, 1, tq), F32),
                        pltpu.VMEM((nh, 1, tq), F32),
                        pltpu.VMEM((nh, HEAD_DIM + BF16_ROWS, tq), F32)],
        compiler_params=_cparams(("arbitrary", "arbitrary")),
        name="fox_attn",
    )(proj, proj, proj, kx, row, w_cast)


def _t5_table_kernel(t5_ref, o_ref):
    h = pl.program_id(0)
    W = SWA_WINDOW
    i = lax.broadcasted_iota(jnp.int32, (W, 2 * W), 0)
    j = lax.broadcasted_iota(jnp.int32, (W, 2 * W), 1)
    rel = (W + i) - j
    n = jnp.maximum(rel, 0)
    max_exact = NUM_BUCKETS // 2
    nf = jnp.maximum(n, 1).astype(F32)
    large = max_exact + (jnp.log(nf / max_exact) / math.log(T5_MAX_DISTANCE / max_exact)
                         * (NUM_BUCKETS - max_exact)).astype(jnp.int32)
    large = jnp.minimum(large, NUM_BUCKETS - 1)
    bucket = jnp.where(n < max_exact, n, large)
    bias = jnp.zeros((W, 2 * W), F32)
    for b in range(NUM_BUCKETS):
        bias = jnp.where(bucket == b, t5_ref[b, h], bias)
    in_window = (rel >= 0) & (rel < W)
    o_ref[...] = jnp.where(in_window, bias, -jnp.inf)


def t5_table(t5_bias):
    W = SWA_WINDOW
    return pl.pallas_call(
        _t5_table_kernel,
        out_shape=jax.ShapeDtypeStruct((SWA_HEADS, W, 2 * W), F32),
        grid=(SWA_HEADS,),
        in_specs=[pl.BlockSpec(memory_space=pltpu.SMEM)],
        out_specs=pl.BlockSpec((None, W, 2 * W), lambda h: (h, 0, 0)),
        compiler_params=_cparams(("arbitrary",)),
        name="t5_table",
    )(t5_bias)


def _cast_slabs(ws, layer, nsteps, step):
    in_specs, shapes, out_specs = [], [], []
    for w in ws:
        _, rows, cols = w.shape
        slab = rows // nsteps
        assert slab * nsteps == rows and slab % BF16_ROWS == 0
        in_specs.append(pl.BlockSpec((None, slab, cols), lambda *ids: (layer, step(*ids), 0)))
        out_specs.append(pl.BlockSpec((slab, cols), lambda *ids: (step(*ids), 0)))
        shapes.append(jax.ShapeDtypeStruct((rows, cols), BF16))
    return in_specs, shapes, out_specs


def _cast_along(wi_refs, wo_refs):
    for wi_ref, wo_ref in zip(wi_refs, wo_refs, strict=True):
        wo_ref[...] = wi_ref[...].astype(wo_ref.dtype)


def _swa_kernel(sink_ref, q_ref, kc_ref, kp_ref, vc_ref, vp_ref, tab_ref, *refs, nsub, ncast):
    wi_refs, o_ref, wo_refs = refs[:ncast], refs[ncast], refs[ncast + 1:]
    g = pl.program_id(0)
    n = pl.program_id(1)
    W = SWA_WINDOW
    _cast_along(wi_refs, wo_refs)

    first_prev = lax.broadcasted_iota(jnp.int32, (W, 2 * W), 1) < W
    for sub in range(nsub):
        if sub == 0:
            kp, vp = kp_ref[...], vp_ref[...]
        else:
            kp, vp = kc_ref[(sub - 1) * W:sub * W, :], vc_ref[(sub - 1) * W:sub * W, :]
        kk = jnp.concatenate([kp, kc_ref[sub * W:(sub + 1) * W, :]], axis=0)
        vv = jnp.concatenate([vp, vc_ref[sub * W:(sub + 1) * W, :]], axis=0)
        for hh in range(SWA_GROUP):
            q = q_ref[sub * W:(sub + 1) * W, hh * HEAD_DIM:(hh + 1) * HEAD_DIM]
            s = _dot_nt(q, kk) + tab_ref[hh]
            if sub == 0:
                s = jnp.where(jnp.logical_and(first_prev, n == 0), -jnp.inf, s)
            sink = sink_ref[0, g * SWA_GROUP + hh]
            m = jnp.maximum(jnp.max(s, axis=1, keepdims=True), sink)
            p = jnp.exp(s - m)
            l = jnp.sum(p, axis=1, keepdims=True) + jnp.exp(sink - m)
            o = _dot(p.astype(BF16), vv) / l
            o_ref[sub * W:(sub + 1) * W, hh * HEAD_DIM:(hh + 1) * HEAD_DIM] = o.astype(o_ref.dtype)


def swa_attention(proj, table, sinks, w_cast, layer, *, tb):
    T = proj.shape[0]
    W = SWA_WINDOW
    nsub = tb // W
    nt = T // tb
    gw = SWA_GROUP * HEAD_DIM
    qb, kb, vb = C_SQ // gw, C_SK // HEAD_DIM, C_SV // HEAD_DIM
    prev = lambda n: jnp.maximum(n * nsub - 1, 0)
    w_in_specs, w_shapes, w_out_specs = _cast_slabs(w_cast, layer, SWA_KV_HEADS * nt, lambda g, n: g * nt + n)
    o, *w_bf16 = pl.pallas_call(
        functools.partial(_swa_kernel, nsub=nsub, ncast=len(w_cast)),
        out_shape=[jax.ShapeDtypeStruct((T, SWA_HEADS * HEAD_DIM), BF16), *w_shapes],
        grid=(SWA_KV_HEADS, nt),
        in_specs=[
            pl.BlockSpec(memory_space=pltpu.SMEM),
            pl.BlockSpec((tb, gw), lambda g, n: (n, qb + g)),
            pl.BlockSpec((tb, HEAD_DIM), lambda g, n: (n, kb + g)),
            pl.BlockSpec((W, HEAD_DIM), lambda g, n: (prev(n), kb + g)),
            pl.BlockSpec((tb, HEAD_DIM), lambda g, n: (n, vb + g)),
            pl.BlockSpec((W, HEAD_DIM), lambda g, n: (prev(n), vb + g)),
            pl.BlockSpec((SWA_GROUP, W, 2 * W), lambda g, n: (g, 0, 0)),
            *w_in_specs,
        ],
        out_specs=[pl.BlockSpec((tb, gw), lambda g, n: (n, g)), *w_out_specs],
        compiler_params=_cparams(("arbitrary", "arbitrary")),
        name="swa_attn",
    )(sinks.reshape(1, SWA_HEADS), proj, proj, proj, proj, proj, table, *w_cast)
    return o, w_bf16


def _gla_kernel(q_ref, k_ref, v_ref, r_ref, ps_ref, wg_ref, bg_ref, gn_ref, wi_ref, o_ref, wo_ref,
                st_ref, oc_ref, *, tb, rb):
    C = GLA_CHUNK
    _cast_along([wi_ref], [wo_ref])

    @pl.when(pl.program_id(0) == 0)
    def _():
        st_ref[...] = jnp.zeros_like(st_ref)

    r = lax.broadcasted_iota(jnp.int32, (rb, rb), 0)
    c = lax.broadcasted_iota(jnp.int32, (rb, rb), 1)
    same = (r // C) == (c // C)
    tril = jnp.where(jnp.logical_and(same, c <= r), 1.0, 0.0).astype(BF16)
    whole = jnp.where(same, 1.0, 0.0).astype(BF16)
    lane = lax.broadcasted_iota(jnp.int32, (1, LANES), 1)
    causal = (lax.broadcasted_iota(jnp.int32, (C, C), 1) <= lax.broadcasted_iota(jnp.int32, (C, C), 0))

    def prepare(blk):
        glr = ps_ref[blk, :].astype(BF16)
        g = jax.nn.log_sigmoid(_dot(glr, wg_ref[...]) + bg_ref[...]) / GLA_TAU
        gh, gm, gl = _split3(g)
        b = _dot(tril, gh) + _dot(tril, gm) + _dot(tril, gl)
        b_last = _dot(whole, gh) + _dot(whole, gm) + _dot(whole, gl)
        q_t = (q_ref[blk, :].astype(F32) * jnp.exp(b)).astype(BF16)
        kf = k_ref[blk, :].astype(F32)
        k_t = kf * jnp.exp(-b)
        k_end = kf * jnp.exp(b_last - b)
        decay = jnp.exp(b_last)
        units = []
        for n in range(rb // C):
            rows = slice(n * C, (n + 1) * C)
            per_head = []
            for h in range(GLA_HEADS):
                pair = slice((h // 2) * LANES, (h // 2 + 1) * LANES)
                mine = (lane // GLA_DK) == (h % 2)
                ktm = jnp.where(mine, k_t[rows, pair], 0.0).astype(BF16)
                kem = jnp.where(mine, k_end[rows, pair], 0.0).astype(BF16)
                qh = q_t[rows, pair]
                vh = v_ref[blk, h * GLA_DV:(h + 1) * GLA_DV][rows]
                intra = _dot(jnp.where(causal, _dot_nt(qh, ktm), 0.0).astype(BF16), vh)
                per_head.append((qh, intra, _dot_tn(vh, kem), decay[n * C:n * C + 1, pair]))
            units.append(per_head)
        return units

    blocks = [slice(s, s + rb) for s in range(0, tb, rb)]
    prepared = [prepare(blk) for blk in blocks]
    sts = [st_ref[h] for h in range(GLA_HEADS)]
    for blk, units in zip(blocks, prepared):
        for n, per_head in enumerate(units):
            rows = slice(blk.start + n * C, blk.start + (n + 1) * C)
            for h, (qh, intra, kv_t, dec) in enumerate(per_head):
                oc_ref[rows, h * GLA_DV:(h + 1) * GLA_DV] = intra + _dot_nt(qh, sts[h].astype(BF16))
                sts[h] = dec * sts[h] + kv_t
    for h in range(GLA_HEADS):
        st_ref[h] = sts[h]
    for h in range(GLA_HEADS):
        cols = slice(h * GLA_DV, (h + 1) * GLA_DV)
        o = _rms(oc_ref[:, cols], gn_ref[...])
        o_ref[:, cols] = (o * jax.nn.silu(r_ref[:, cols].astype(F32))).astype(o_ref.dtype)


def gla(proj, ps, wg_pad, bg, gnorm, w_cast, layer, *, tb, rb):
    T = proj.shape[0]
    qw = GLA_HEADS * GLA_DK
    vw = GLA_HEADS * GLA_DV
    (w_in_spec,), (w_shape,), (w_out_spec,) = _cast_slabs([w_cast], layer, T // tb, lambda n: n)
    return pl.pallas_call(
        functools.partial(_gla_kernel, tb=tb, rb=rb),
        out_shape=[jax.ShapeDtypeStruct((T, vw), BF16), w_shape],
        grid=(T // tb,),
        in_specs=[
            pl.BlockSpec((tb, qw), lambda n: (n, C_GQ // qw)),
            pl.BlockSpec((tb, qw), lambda n: (n, C_GK // qw)),
            pl.BlockSpec((tb, vw), lambda n: (n, C_GV // vw)),
            pl.BlockSpec((tb, vw), lambda n: (n, C_GR // vw)),
            pl.BlockSpec((tb, LANES), lambda n: (n, 0)),
            pl.BlockSpec((LANES, qw), lambda n: (0, 0)),
            pl.BlockSpec((1, qw), lambda n: (0, 0)),
            pl.BlockSpec((1, GLA_DV), lambda n: (0, 0)),
            w_in_spec,
        ],
        out_specs=[pl.BlockSpec((tb, vw), lambda n: (n, 0)), w_out_spec],
        scratch_shapes=[pltpu.VMEM((GLA_HEADS, GLA_DV, LANES), F32),
                        pltpu.VMEM((tb, vw), F32)],
        compiler_params=_cparams(("arbitrary",)),
        name="gla",
    )(proj, proj, proj, proj, ps, wg_pad, bg, gnorm, w_cast)


def _mix_xattn_kernel(h_ref, of_ref, os_ref, og_ref, wo_ref, gx_ref, wq_ref, kv_ref, wx_ref, o_ref):
    nf, ns = of_ref.shape[1], os_ref.shape[1]
    mix = (_dot(of_ref[...], wo_ref[0:nf, :]) + _dot(os_ref[...], wo_ref[nf:nf + ns, :])
           + _dot(og_ref[...], wo_ref[nf + ns:, :]))
    h1 = h_ref[...] + mix
    xn = _rms(h1, gx_ref[...]).astype(BF16)
    q = (_dot(xn, wq_ref[...]) * (HEAD_DIM ** -0.5)).astype(BF16)
    xw = XATTN_HEADS * HEAD_DIM
    outs = []
    for hh in range(XATTN_HEADS):
        cols = slice(hh * HEAD_DIM, (hh + 1) * HEAD_DIM)
        k = kv_ref[:, cols]
        v = kv_ref[:, xw + hh * HEAD_DIM:xw + (hh + 1) * HEAD_DIM]
        s = _dot_nt(q[:, cols], k)
        p = jnp.exp(s - jnp.max(s, axis=1, keepdims=True))
        l = jnp.sum(p, axis=1, keepdims=True)
        outs.append((_dot(p.astype(BF16), v) / l).astype(BF16))
    o = jnp.concatenate(outs, axis=1)
    o_ref[...] = h1 + _dot(o, wx_ref[...])


def mix_xattn(h, o_fox, o_swa, o_gla, w_out, gx, wq, kv, wo, *, tm):
    T, D = h.shape
    const = lambda shape: pl.BlockSpec(shape, lambda i: (0, 0))
    row = lambda w: pl.BlockSpec((tm, w), lambda i: (i, 0))
    return pl.pallas_call(
        _mix_xattn_kernel,
        out_shape=jax.ShapeDtypeStruct((T, D), F32),
        grid=(T // tm,),
        in_specs=[row(D), row(o_fox.shape[1]), row(o_swa.shape[1]), row(o_gla.shape[1]),
                  const(w_out.shape), const((1, D)), const(wq.shape), const(kv.shape), const(wo.shape)],
        out_specs=row(D),
        compiler_params=_cparams(("arbitrary",)),
        name="mix_xattn",
    )(h, o_fox, o_swa, o_gla, w_out, gx.reshape(1, D), wq, kv, wo)


def _ffn_kernel(x_ref, g_ref, wg_ref, wv_ref, cwg_ref, cwv_ref, cbg_ref, cbv_ref, wd_ref, fg_ref,
                o_ref, xn_ref, hg_ref, hv_ref, ug_ref, uv_ref, *, tm, sub, final):
    i = pl.program_id(0)
    f = pl.program_id(1)
    H = SUBLANES
    fc = wd_ref.shape[0]

    @pl.when(f == 0)
    def _():
        x = x_ref[...]
        xn_ref[...] = _rms(x, g_ref[...]).astype(BF16)
        o_ref[...] = x

    @pl.when(i == 0)
    def _():
        hg_ref[f] = jnp.zeros(hg_ref.shape[1:], F32)
        hv_ref[f] = jnp.zeros(hv_ref.shape[1:], F32)

    xn = xn_ref[...]

    def up(c, slot):
        ug_ref[slot, H:, :] = _dot(xn, wg_ref[:, c:c + sub])
        uv_ref[slot, H:, :] = _dot(xn, wv_ref[:, c:c + sub])

    def conv(u_ref, slot, h_ref, cw_ref, cb_ref, c):
        u_ref[slot, 0:H, :] = h_ref[f, :, c:c + sub]
        h_ref[f, :, c:c + sub] = u_ref[slot, tm:tm + H, :]
        return (cb_ref[:, c:c + sub] + cw_ref[2:3, c:c + sub] * u_ref[slot, H:H + tm, :]
                + cw_ref[1:2, c:c + sub] * u_ref[slot, H - 1:H - 1 + tm, :]
                + cw_ref[0:1, c:c + sub] * u_ref[slot, H - 2:H - 2 + tm, :])

    def down(c, slot):
        gate = conv(ug_ref, slot, hg_ref, cwg_ref, cbg_ref, c)
        val = conv(uv_ref, slot, hv_ref, cwv_ref, cbv_ref, c)
        act = (jax.nn.silu(gate) * val).astype(BF16)
        o_ref[...] += _dot(act, wd_ref[c:c + sub, :])

    up(0, 0)
    for n, c in enumerate(range(0, fc, sub)):
        if c + sub < fc:
            up(c + sub, (n + 1) % 2)
        down(c, n % 2)

    if final:
        @pl.when(f == pl.num_programs(1) - 1)
        def _():
            o_ref[...] = _rms(o_ref[...], fg_ref[...])


def conv_ffn(x, g, w_up, conv_w, conv_b, w_down, layer, final_g, *, tm, fc, final):
    T, D = x.shape
    dff = w_down.shape[0]
    L = conv_w.shape[0]
    nf = dff // fc
    cb = conv_b.reshape(L, 1, 2 * dff)
    return pl.pallas_call(
        functools.partial(_ffn_kernel, tm=tm, sub=MXU_COLS, final=final),
        out_shape=jax.ShapeDtypeStruct((T, D), F32),
        grid=(T // tm, nf),
        in_specs=[
            pl.BlockSpec((tm, D), lambda i, f: (i, 0)),
            pl.BlockSpec((1, D), lambda i, f: (0, 0)),
            pl.BlockSpec((D, fc), lambda i, f: (0, f)),
            pl.BlockSpec((D, fc), lambda i, f: (0, nf + f)),
            pl.BlockSpec((None, CONV_WIDTH, fc), lambda i, f: (layer, 0, f)),
            pl.BlockSpec((None, CONV_WIDTH, fc), lambda i, f: (layer, 0, nf + f)),
            pl.BlockSpec((None, 1, fc), lambda i, f: (layer, 0, f)),
            pl.BlockSpec((None, 1, fc), lambda i, f: (layer, 0, nf + f)),
            pl.BlockSpec((fc, D), lambda i, f: (f, 0)),
            pl.BlockSpec((1, D), lambda i, f: (0, 0)),
        ],
        out_specs=pl.BlockSpec((tm, D), lambda i, f: (i, 0)),
        scratch_shapes=[pltpu.VMEM((tm, D), BF16), pltpu.VMEM((nf, SUBLANES, fc), F32),
                        pltpu.VMEM((nf, SUBLANES, fc), F32),
                        pltpu.VMEM((2, tm + SUBLANES, MXU_COLS), F32),
                        pltpu.VMEM((2, tm + SUBLANES, MXU_COLS), F32)],
        compiler_params=_cparams(("arbitrary", "arbitrary")),
        name="conv_ffn",
    )(x, g.reshape(1, D), w_up, w_up, conv_w, conv_w, cb, cb, w_down, final_g.reshape(1, D))


TR_PREP, TC_PREP = 512, 512
TM_PROJ, TN_PROJ = 1024, 1536
TB_GATE = 512
TQ_FOX, TK_FOX, NH_FOX = 512, 512, 4
TB_SWA = 512
TB_GLA, RB_GLA = 512, 256
TM_MIX = 512
TM_FFN, FC_FFN = 512, 512


def _main_colscale():
    cs = np.ones((1, MAIN_COLS), np.float32)
    cs[0, C_FQ:C_FK] = HEAD_DIM ** -0.5 * LOG2E
    cs[0, C_SQ:C_SK] = HEAD_DIM ** -0.5
    cs[0, C_GQ:C_GK] = GLA_DK ** -0.5
    return jnp.asarray(cs)


def kernel(x, mem, w_in, b_fox_f, swa_sinks, t5_bias, w_gla_gate, b_gla_gate, gla_norm, w_mix_out, norm_mix,
           norm_xattn, norm_mem, wq_x, wkv_x, wo_x, norm_ffn, w_up, conv_w, conv_b, w_down, final_norm):
    depth = w_in.shape[0]
    _, T, D = x.shape
    M = mem.shape[1]
    h = x.reshape(T, D)
    memf = mem.reshape(M, D)
    colscale = _main_colscale()
    ones_kv = jnp.ones((1, wkv_x.shape[2]), F32)
    table = t5_table(t5_bias)
    pad_lanes = LANES - S_GLR - GLA_GATE_RANK
    w_main, w_small = prep_w_in(w_in, tr=TR_PREP, tc=TC_PREP)
    for l in range(depth):
        bvec = jnp.concatenate([b_fox_f[l], jnp.zeros((LANES - FOX_HEADS,), F32)]).reshape(1, LANES)
        wg_pad = jnp.concatenate([jnp.zeros((S_GLR, GLA_HEADS * GLA_DK), F32), w_gla_gate[l],
                                  jnp.zeros((pad_lanes, GLA_HEADS * GLA_DK), F32)], axis=0).astype(BF16)

        proj, ps = norm_proj(h, norm_mix[l], w_main, l, colscale, w_small, tm=TM_PROJ, tn=TN_PROJ, w_t=True)
        kx, row = fox_gate(ps, bvec, tb=TB_GATE)
        o_fox, w_down_b = fox_attention(proj, kx, row, w_down, l, tq=TQ_FOX, tk=TK_FOX, nh=NH_FOX)
        o_swa, (w_out_b, wq_b, wkv_b, wo_b) = swa_attention(proj, table, swa_sinks[l],
                                                           (w_mix_out, wq_x, wkv_x, wo_x), l, tb=TB_SWA)
        o_gla, w_up_b = gla(proj, ps, wg_pad, b_gla_gate[l].reshape(1, -1), gla_norm[l].reshape(1, -1),
                            w_up, l, tb=TB_GLA, rb=RB_GLA)
        kv = norm_proj(memf, norm_mem[l], wkv_b[None], 0, ones_kv, tm=M, tn=wkv_x.shape[2])
        h = mix_xattn(h, o_fox, o_swa, o_gla, w_out_b, norm_xattn[l], wq_b, kv, wo_b, tm=TM_MIX)
        h = conv_ffn(h, norm_ffn[l], w_up_b, conv_w, conv_b, w_down_b, l, final_norm,
                     tm=TM_FFN, fc=FC_FFN, final=(l == depth - 1))
    return h.reshape(x.shape)
```

```python
import functools
import math

import numpy as np
import jax
import jax.numpy as jnp
from jax import lax
from jax.experimental import pallas as pl
from jax.experimental.pallas import tpu as pltpu

F32 = jnp.float32
BF16 = jnp.bfloat16

HEAD_DIM = 128
FOX_HEADS = 4
SWA_HEADS = 8
SWA_KV_HEADS = 2
SWA_GROUP = SWA_HEADS // SWA_KV_HEADS
GLA_HEADS = 4
GLA_DK = 64
GLA_DV = 128
GLA_GATE_RANK = 16
GLA_TAU = 16.0
GLA_CHUNK = 64
SWA_WINDOW = 128
NUM_BUCKETS = 32
T5_MAX_DISTANCE = 128
XATTN_HEADS = 4
CONV_WIDTH = 3
EPS = 1e-6
LOG2E = math.log2(math.e)

LANES = 128
SUBLANES = 8
BF16_ROWS = 16
MXU_COLS = 256
VMEM_LIMIT = 56 * 1024 * 1024

_FOX_W = FOX_HEADS * HEAD_DIM
_OFF_FF = 3 * _FOX_W
_OFF_SQ = _OFF_FF + FOX_HEADS
_MAIN_B = (SWA_HEADS + 2 * SWA_KV_HEADS) * HEAD_DIM + 2 * GLA_HEADS * GLA_DK + 2 * GLA_HEADS * GLA_DV
_OFF_GLR = _OFF_SQ + _MAIN_B
C_FQ, C_FK, C_FV = 0, 512, 1024
C_SQ, C_SK, C_SV = 1536, 2560, 2816
C_GQ, C_GK, C_GV, C_GR = 3072, 3328, 3584, 4096
MAIN_COLS = 4608
S_FF, S_GLR = 0, SUBLANES


def _cparams(sem):
    return pltpu.CompilerParams(dimension_semantics=sem, vmem_limit_bytes=VMEM_LIMIT)


def _rms(x, g):
    return x * lax.rsqrt(jnp.mean(x * x, axis=-1, keepdims=True) + EPS) * g


N_SPLIT = 3


def _split3(x):
    hi = x.astype(BF16)
    r1 = x - hi.astype(F32)
    mid = r1.astype(BF16)
    lo = (r1 - mid.astype(F32)).astype(BF16)
    return hi, mid, lo


def _dot(a, b):
    return jnp.dot(a, b, preferred_element_type=F32)


def _dot_nt(a, b):
    return lax.dot_general(a, b, (((1,), (1,)), ((), ())), preferred_element_type=F32)


def _dot_tn(a, b):
    return lax.dot_general(a, b, (((0,), (0,)), ((), ())), preferred_element_type=F32)


def _sel_dot(sel, x):
    hi, mid, lo = _split3(x)
    return _dot(sel, hi) + _dot(sel, mid) + _dot(sel, lo)


def _prep_w_in_kernel(cur_ref, ff_ref, glr_ref, wm_ref, ws_ref):
    L = wm_ref.shape[0]
    wm_ref[...] = pltpu.einshape("cld->lcd", cur_ref[...]).astype(BF16)

    @pl.when(pl.program_id(1) == 0)
    def _():
        pad = jnp.zeros((LANES - S_GLR - GLA_GATE_RANK, ws_ref.shape[2]), F32)
        for l in range(L):
            ws_ref[l] = jnp.concatenate([ff_ref[:, l, :], glr_ref[:, l, :], pad], axis=0).astype(BF16)


def prep_w_in(w_in, *, tr, tc):
    L, D, _ = w_in.shape
    wt = jnp.transpose(w_in, (2, 0, 1))
    assert _OFF_FF % tc == 0 and MAIN_COLS % tc == 0 and S_GLR % SUBLANES == 0

    def src(j):
        return j * tc + jnp.where(j * tc >= _OFF_FF, _OFF_SQ - _OFF_FF, 0)

    return pl.pallas_call(
        _prep_w_in_kernel,
        out_shape=[jax.ShapeDtypeStruct((L, MAIN_COLS, D), BF16),
                   jax.ShapeDtypeStruct((L, LANES, D), BF16)],
        grid=(D // tr, MAIN_COLS // tc),
        in_specs=[
            pl.BlockSpec((pl.Element(tc), pl.Element(L), pl.Element(tr)), lambda r, j: (src(j), 0, r * tr)),
            pl.BlockSpec((pl.Element(S_GLR), pl.Element(L), pl.Element(tr)), lambda r, j: (_OFF_FF, 0, r * tr)),
            pl.BlockSpec((pl.Element(GLA_GATE_RANK), pl.Element(L), pl.Element(tr)),
                         lambda r, j: (_OFF_GLR, 0, r * tr)),
        ],
        out_specs=[pl.BlockSpec((L, tc, tr), lambda r, j: (0, j, r)),
                   pl.BlockSpec((L, LANES, tr), lambda r, j: (0, 0, r))],
        compiler_params=_cparams(("arbitrary", "arbitrary")),
        name="prep_w_in",
    )(wt, wt, wt)


def _norm_proj_kernel(*refs, has_small, w_t):
    if has_small:
        x_ref, g_ref, w_ref, cs_ref, ws_ref, o_ref, os_ref, xn_ref = refs
    else:
        x_ref, g_ref, w_ref, cs_ref, o_ref, xn_ref = refs
    dot = _dot_nt if w_t else _dot

    @pl.when(pl.program_id(1) == 0)
    def _():
        xn = _rms(x_ref[...], g_ref[...]).astype(BF16)
        xn_ref[...] = xn
        if has_small:
            os_ref[...] = dot(xn, ws_ref[...])

    o_ref[...] = (dot(xn_ref[...], w_ref[...]) * cs_ref[...]).astype(o_ref.dtype)


def norm_proj(x, g, w, layer, colscale, w_small=None, *, tm, tn, w_t=False):
    T, D = x.shape
    N = w.shape[1] if w_t else w.shape[2]
    has_small = w_small is not None

    def wspec(n):
        if w_t:
            return pl.BlockSpec((None, n, D), lambda i, j: (layer, j, 0))
        return pl.BlockSpec((None, D, n), lambda i, j: (layer, 0, j))

    in_specs = [
        pl.BlockSpec((tm, D), lambda i, j: (i, 0)),
        pl.BlockSpec((1, D), lambda i, j: (0, 0)),
        wspec(tn),
        pl.BlockSpec((1, tn), lambda i, j: (0, j)),
    ]
    args = [x, g.reshape(1, D), w, colscale]
    out_shape = [jax.ShapeDtypeStruct((T, N), BF16)]
    out_specs = [pl.BlockSpec((tm, tn), lambda i, j: (i, j))]
    if has_small:
        in_specs.append(pl.BlockSpec((None,) + w_small.shape[1:], lambda i, j: (layer, 0, 0)))
        args.append(w_small)
        out_shape.append(jax.ShapeDtypeStruct((T, LANES), F32))
        out_specs.append(pl.BlockSpec((tm, LANES), lambda i, j: (i, 0)))
    outs = pl.pallas_call(
        functools.partial(_norm_proj_kernel, has_small=has_small, w_t=w_t),
        out_shape=out_shape,
        grid=(T // tm, N // tn),
        in_specs=in_specs,
        out_specs=out_specs,
        scratch_shapes=[pltpu.VMEM((tm, D), BF16)],
        compiler_params=_cparams(("arbitrary", "arbitrary")),
        name="norm_proj_small" if has_small else "norm_proj",
    )(*args)
    return outs if has_small else outs[0]


def _fox_gate_kernel(ps_ref, b_ref, kx_ref, row_ref, carry_ref, *, tb):
    @pl.when(pl.program_id(0) == 0)
    def _():
        carry_ref[...] = jnp.zeros_like(carry_ref)

    x = jax.nn.log_sigmoid(ps_ref[...] + b_ref[...])
    r = lax.broadcasted_iota(jnp.int32, (tb, tb), 0)
    c = lax.broadcasted_iota(jnp.int32, (tb, tb), 1)
    tril = jnp.where(c <= r, 1.0, 0.0).astype(BF16)
    csum = _sel_dot(tril, x) + carry_ref[0:1, :]
    carry_ref[...] = jnp.broadcast_to(csum[tb - 1:tb, :], carry_ref.shape)
    pieces = _split3(csum * LOG2E)
    neg = [-p for p in pieces]
    src = lax.broadcasted_iota(jnp.int32, (LANES, FOX_HEADS * LANES), 0)
    dst = lax.broadcasted_iota(jnp.int32, (LANES, FOX_HEADS * LANES), 1)
    kx = sum(_dot(neg[n], jnp.where((dst // LANES == src - S_FF) & (dst % LANES == n), 1.0, 0.0).astype(BF16))
             for n in range(len(neg))).astype(BF16)
    sub = lax.broadcasted_iota(jnp.int32, (FOX_HEADS * SUBLANES, LANES), 0)
    lane = lax.broadcasted_iota(jnp.int32, (FOX_HEADS * SUBLANES, LANES), 1)
    pick = jnp.where(lane - S_FF == sub // SUBLANES, 1.0, 0.0).astype(BF16)
    rows = sum(_dot_nt(pick, p) for p in pieces)
    for h in range(FOX_HEADS):
        kx_ref[h] = kx[:, h * LANES:(h + 1) * LANES]
        row_ref[h, 0] = rows[h * SUBLANES:(h + 1) * SUBLANES, :]


def fox_gate(ps, bvec, *, tb):
    T = ps.shape[0]
    nb = T // tb
    return pl.pallas_call(
        functools.partial(_fox_gate_kernel, tb=tb),
        out_shape=[jax.ShapeDtypeStruct((FOX_HEADS, T, LANES), BF16),
                   jax.ShapeDtypeStruct((FOX_HEADS, nb, SUBLANES, tb), F32)],
        grid=(nb,),
        in_specs=[pl.BlockSpec((tb, LANES), lambda i: (i, 0)),
                  pl.BlockSpec((1, LANES), lambda i: (0, 0))],
        out_specs=[pl.BlockSpec((FOX_HEADS, tb, LANES), lambda i: (0, i, 0)),
                   pl.BlockSpec((FOX_HEADS, 1, SUBLANES, tb), lambda i: (0, i, 0, 0))],
        scratch_shapes=[pltpu.VMEM((SUBLANES, LANES), F32)],
        compiler_params=_cparams(("arbitrary",)),
        name="fox_gate",
    )(ps, bvec)


def _fox_attn_kernel(q_ref, k_ref, v_ref, kx_ref, cq_ref, wi_ref, o_ref, wo_ref,
                     vt_ref, ya_ref, yb_ref, xa_ref, xb_ref, m_ref, acc_ref, *, tq, tk, nk, nh):
    i = pl.program_id(1)
    D = HEAD_DIM
    heads = range(nh)
    n_full = i // (tk // tq)
    _cast_along([wi_ref], [wo_ref])

    @pl.when(i == 0)
    def _():
        for hh in heads:
            for n in range(nk):
                vt_ref[hh, n, 0:D, :] = v_ref[n * tk:(n + 1) * tk, hh * D:(hh + 1) * D].T
                vt_ref[hh, n, D:, :] = jnp.ones((BF16_ROWS, tk), BF16)

    lane = lax.broadcasted_iota(jnp.int32, (tq, LANES), 1)
    ones = jnp.where(lane < N_SPLIT, 1.0, 0.0).astype(BF16)
    q_aug = [jnp.concatenate([q_ref[:, hh * D:(hh + 1) * D], ones], axis=1) for hh in heads]
    cq = [cq_ref[hh, 0, 0:1, :] for hh in heads]
    m_ref[...] = jnp.full(m_ref.shape, -jnp.inf, F32)
    acc_ref[...] = jnp.zeros(acc_ref.shape, F32)

    def scores(hh, j):
        start = pl.multiple_of(j * tk, tk)
        k_aug = jnp.concatenate([k_ref[pl.ds(start, tk), hh * D:(hh + 1) * D], kx_ref[hh, pl.ds(start, tk), :]],
                                axis=1)
        return _dot_nt(k_aug, q_aug[hh])

    def update(hh, j, y, ymax):
        m_old = m_ref[hh]
        m_new = jnp.maximum(m_old, ymax + cq[hh])
        alpha = jnp.exp2(m_old - m_new)
        p = jnp.exp2(y + (cq[hh] - m_new))
        acc_ref[hh] = alpha * acc_ref[hh] + _dot(vt_ref[hh, j], p.astype(BF16))
        m_ref[hh] = m_new

    def produce(hh, j, y_ref, ymax_ref):
        y = scores(hh, j)
        y_ref[hh] = y
        ymax_ref[hh] = jnp.max(y, axis=0, keepdims=True)

    for hh in heads:
        produce(hh, 0, ya_ref, xa_ref)

    def step(j, cur_ref, cmax_ref, nxt_ref, nmax_ref):
        for hh in heads:
            produce(hh, j + 1, nxt_ref, nmax_ref)
            update(hh, j, cur_ref[hh], cmax_ref[hh])

    def body(j, carry):
        pl.when(j % 2 == 0)(lambda: step(j, ya_ref, xa_ref, yb_ref, xb_ref))
        pl.when(j % 2 == 1)(lambda: step(j, yb_ref, xb_ref, ya_ref, xa_ref))
        return carry

    lax.fori_loop(0, n_full, body, 0)
    key = n_full * tk + lax.broadcasted_iota(jnp.int32, (tk, tq), 0)
    qry = i * tq + lax.broadcasted_iota(jnp.int32, (tk, tq), 1)

    def diagonal(cur_ref):
        for hh in heads:
            y = jnp.where(key <= qry, cur_ref[hh], -jnp.inf)
            update(hh, n_full, y, jnp.max(y, axis=0, keepdims=True))
            o_ref[:, hh * D:(hh + 1) * D] = (acc_ref[hh, 0:D, :] / acc_ref[hh, D:D + 1, :]).T.astype(o_ref.dtype)

    pl.when(n_full % 2 == 0)(lambda: diagonal(ya_ref))
    pl.when(n_full % 2 == 1)(lambda: diagonal(yb_ref))


def fox_attention(proj, kx, row, w_cast, layer, *, tq, tk, nh):
    T = proj.shape[0]
    nq = T // tq
    ng = FOX_HEADS // nh
    gw = nh * HEAD_DIM
    qb, kb, vb = C_FQ // gw, C_FK // gw, C_FV // gw
    assert tk % tq == 0 and T % tk == 0
    (w_in_spec,), (w_shape,), (w_out_spec,) = _cast_slabs([w_cast], layer, ng * nq, lambda g, i: g * nq + i)
    return pl.pallas_call(
        functools.partial(_fox_attn_kernel, tq=tq, tk=tk, nk=T // tk, nh=nh),
        out_shape=[jax.ShapeDtypeStruct((T, FOX_HEADS * HEAD_DIM), BF16), w_shape],
        grid=(ng, nq),
        in_specs=[
            pl.BlockSpec((tq, gw), lambda g, i: (i, qb + g)),
            pl.BlockSpec((T, gw), lambda g, i: (0, kb + g), pipeline_mode=pl.Buffered(1)),
            pl.BlockSpec((T, gw), lambda g, i: (0, vb + g), pipeline_mode=pl.Buffered(1)),
            pl.BlockSpec((nh, T, LANES), lambda g, i: (g, 0, 0), pipeline_mode=pl.Buffered(1)),
            pl.BlockSpec((nh, 1, SUBLANES, tq), lambda g, i: (g, i, 0, 0)),
            w_in_spec,
        ],
        out_specs=[pl.BlockSpec((tq, gw), lambda g, i: (i, g)), w_out_spec],
        scratch_shapes=[pltpu.VMEM((nh, T // tk, HEAD_DIM + BF16_ROWS, tk), BF16),
                        pltpu.VMEM((nh, tk, tq), F32), pltpu.VMEM((nh, tk, tq), F32),
                        pltpu.VMEM((nh, 1, tq), F32), pltpu.VMEM((nh, 1, tq), F32),
                        pltpu.VMEM((nh, 1, tq), F32),
                        pltpu.VMEM((nh, HEAD_DIM + BF16_ROWS, tq), F32)],
        compiler_params=_cparams(("arbitrary", "arbitrary")),
        name="fox_attn",
    )(proj, proj, proj, kx, row, w_cast)


def _t5_table_kernel(t5_ref, o_ref):
    h = pl.program_id(0)
    W = SWA_WINDOW
    i = lax.broadcasted_iota(jnp.int32, (W, 2 * W), 0)
    j = lax.broadcasted_iota(jnp.int32, (W, 2 * W), 1)
    rel = (W + i) - j
    n = jnp.maximum(rel, 0)
    max_exact = NUM_BUCKETS // 2
    nf = jnp.maximum(n, 1).astype(F32)
    large = max_exact + (jnp.log(nf / max_exact) / math.log(T5_MAX_DISTANCE / max_exact)
                         * (NUM_BUCKETS - max_exact)).astype(jnp.int32)
    large = jnp.minimum(large, NUM_BUCKETS - 1)
    bucket = jnp.where(n < max_exact, n, large)
    bias = jnp.zeros((W, 2 * W), F32)
    for b in range(NUM_BUCKETS):
        bias = jnp.where(bucket == b, t5_ref[b, h], bias)
    in_window = (rel >= 0) & (rel < W)
    o_ref[...] = jnp.where(in_window, bias, -jnp.inf)


def t5_table(t5_bias):
    W = SWA_WINDOW
    return pl.pallas_call(
        _t5_table_kernel,
        out_shape=jax.ShapeDtypeStruct((SWA_HEADS, W, 2 * W), F32),
        grid=(SWA_HEADS,),
        in_specs=[pl.BlockSpec(memory_space=pltpu.SMEM)],
        out_specs=pl.BlockSpec((None, W, 2 * W), lambda h: (h, 0, 0)),
        compiler_params=_cparams(("arbitrary",)),
        name="t5_table",
    )(t5_bias)


def _cast_slabs(ws, layer, nsteps, step):
    in_specs, shapes, out_specs = [], [], []
    for w in ws:
        _, rows, cols = w.shape
        slab = rows // nsteps
        assert slab * nsteps == rows and slab % BF16_ROWS == 0
        in_specs.append(pl.BlockSpec((None, slab, cols), lambda *ids: (layer, step(*ids), 0)))
        out_specs.append(pl.BlockSpec((slab, cols), lambda *ids: (step(*ids), 0)))
        shapes.append(jax.ShapeDtypeStruct((rows, cols), BF16))
    return in_specs, shapes, out_specs


def _cast_along(wi_refs, wo_refs):
    for wi_ref, wo_ref in zip(wi_refs, wo_refs, strict=True):
        wo_ref[...] = wi_ref[...].astype(wo_ref.dtype)


def _swa_kernel(sink_ref, q_ref, kc_ref, kp_ref, vc_ref, vp_ref, tab_ref, *refs, nsub, ncast):
    wi_refs, o_ref, wo_refs = refs[:ncast], refs[ncast], refs[ncast + 1:]
    g = pl.program_id(0)
    n = pl.program_id(1)
    W = SWA_WINDOW
    _cast_along(wi_refs, wo_refs)

    first_prev = lax.broadcasted_iota(jnp.int32, (W, 2 * W), 1) < W
    for sub in range(nsub):
        if sub == 0:
            kp, vp = kp_ref[...], vp_ref[...]
        else:
            kp, vp = kc_ref[(sub - 1) * W:sub * W, :], vc_ref[(sub - 1) * W:sub * W, :]
        kk = jnp.concatenate([kp, kc_ref[sub * W:(sub + 1) * W, :]], axis=0)
        vv = jnp.concatenate([vp, vc_ref[sub * W:(sub + 1) * W, :]], axis=0)
        for hh in range(SWA_GROUP):
            q = q_ref[sub * W:(sub + 1) * W, hh * HEAD_DIM:(hh + 1) * HEAD_DIM]
            s = _dot_nt(q, kk) + tab_ref[hh]
            if sub == 0:
                s = jnp.where(jnp.logical_and(first_prev, n == 0), -jnp.inf, s)
            sink = sink_ref[0, g * SWA_GROUP + hh]
            m = jnp.maximum(jnp.max(s, axis=1, keepdims=True), sink)
            p = jnp.exp(s - m)
            l = jnp.sum(p, axis=1, keepdims=True) + jnp.exp(sink - m)
            o = _dot(p.astype(BF16), vv) / l
            o_ref[sub * W:(sub + 1) * W, hh * HEAD_DIM:(hh + 1) * HEAD_DIM] = o.astype(o_ref.dtype)


def swa_attention(proj, table, sinks, w_cast, layer, *, tb):
    T = proj.shape[0]
    W = SWA_WINDOW
    nsub = tb // W
    nt = T // tb
    gw = SWA_GROUP * HEAD_DIM
    qb, kb, vb = C_SQ // gw, C_SK // HEAD_DIM, C_SV // HEAD_DIM
    prev = lambda n: jnp.maximum(n * nsub - 1, 0)
    w_in_specs, w_shapes, w_out_specs = _cast_slabs(w_cast, layer, SWA_KV_HEADS * nt, lambda g, n: g * nt + n)
    o, *w_bf16 = pl.pallas_call(
        functools.partial(_swa_kernel, nsub=nsub, ncast=len(w_cast)),
        out_shape=[jax.ShapeDtypeStruct((T, SWA_HEADS * HEAD_DIM), BF16), *w_shapes],
        grid=(SWA_KV_HEADS, nt),
        in_specs=[
            pl.BlockSpec(memory_space=pltpu.SMEM),
            pl.BlockSpec((tb, gw), lambda g, n: (n, qb + g)),
            pl.BlockSpec((tb, HEAD_DIM), lambda g, n: (n, kb + g)),
            pl.BlockSpec((W, HEAD_DIM), lambda g, n: (prev(n), kb + g)),
            pl.BlockSpec((tb, HEAD_DIM), lambda g, n: (n, vb + g)),
            pl.BlockSpec((W, HEAD_DIM), lambda g, n: (prev(n), vb + g)),
            pl.BlockSpec((SWA_GROUP, W, 2 * W), lambda g, n: (g, 0, 0)),
            *w_in_specs,
        ],
        out_specs=[pl.BlockSpec((tb, gw), lambda g, n: (n, g)), *w_out_specs],
        compiler_params=_cparams(("arbitrary", "arbitrary")),
        name="swa_attn",
    )(sinks.reshape(1, SWA_HEADS), proj, proj, proj, proj, proj, table, *w_cast)
    return o, w_bf16


def _gla_kernel(q_ref, k_ref, v_ref, r_ref, ps_ref, wg_ref, bg_ref, gn_ref, wi_ref, o_ref, wo_ref,
                st_ref, oc_ref, *, tb, rb):
    C = GLA_CHUNK
    _cast_along([wi_ref], [wo_ref])

    @pl.when(pl.program_id(0) == 0)
    def _():
        st_ref[...] = jnp.zeros_like(st_ref)

    r = lax.broadcasted_iota(jnp.int32, (rb, rb), 0)
    c = lax.broadcasted_iota(jnp.int32, (rb, rb), 1)
    same = (r // C) == (c // C)
    tril = jnp.where(jnp.logical_and(same, c <= r), 1.0, 0.0).astype(BF16)
    whole = jnp.where(same, 1.0, 0.0).astype(BF16)
    lane = lax.broadcasted_iota(jnp.int32, (1, LANES), 1)
    causal = (lax.broadcasted_iota(jnp.int32, (C, C), 1) <= lax.broadcasted_iota(jnp.int32, (C, C), 0))

    def prepare(blk):
        glr = ps_ref[blk, :].astype(BF16)
        g = jax.nn.log_sigmoid(_dot(glr, wg_ref[...]) + bg_ref[...]) / GLA_TAU
        gh, gm, gl = _split3(g)
        b = _dot(tril, gh) + _dot(tril, gm) + _dot(tril, gl)
        b_last = _dot(whole, gh) + _dot(whole, gm) + _dot(whole, gl)
        q_t = (q_ref[blk, :].astype(F32) * jnp.exp(b)).astype(BF16)
        kf = k_ref[blk, :].astype(F32)
        k_t = kf * jnp.exp(-b)
        k_end = kf * jnp.exp(b_last - b)
        decay = jnp.exp(b_last)
        units = []
        for n in range(rb // C):
            rows = slice(n * C, (n + 1) * C)
            per_head = []
            for h in range(GLA_HEADS):
                pair = slice((h // 2) * LANES, (h // 2 + 1) * LANES)
                mine = (lane // GLA_DK) == (h % 2)
                ktm = jnp.where(mine, k_t[rows, pair], 0.0).astype(BF16)
                kem = jnp.where(mine, k_end[rows, pair], 0.0).astype(BF16)
                qh = q_t[rows, pair]
                vh = v_ref[blk, h * GLA_DV:(h + 1) * GLA_DV][rows]
                intra = _dot(jnp.where(causal, _dot_nt(qh, ktm), 0.0).astype(BF16), vh)
                per_head.append((qh, intra, _dot_tn(vh, kem), decay[n * C:n * C + 1, pair]))
            units.append(per_head)
        return units

    blocks = [slice(s, s + rb) for s in range(0, tb, rb)]
    prepared = [prepare(blk) for blk in blocks]
    sts = [st_ref[h] for h in range(GLA_HEADS)]
    for blk, units in zip(blocks, prepared):
        for n, per_head in enumerate(units):
            rows = slice(blk.start + n * C, blk.start + (n + 1) * C)
            for h, (qh, intra, kv_t, dec) in enumerate(per_head):
                oc_ref[rows, h * GLA_DV:(h + 1) * GLA_DV] = intra + _dot_nt(qh, sts[h].astype(BF16))
                sts[h] = dec * sts[h] + kv_t
    for h in range(GLA_HEADS):
        st_ref[h] = sts[h]
    for h in range(GLA_HEADS):
        cols = slice(h * GLA_DV, (h + 1) * GLA_DV)
        o = _rms(oc_ref[:, cols], gn_ref[...])
        o_ref[:, cols] = (o * jax.nn.silu(r_ref[:, cols].astype(F32))).astype(o_ref.dtype)


def gla(proj, ps, wg_pad, bg, gnorm, w_cast, layer, *, tb, rb):
    T = proj.shape[0]
    qw = GLA_HEADS * GLA_DK
    vw = GLA_HEADS * GLA_DV
    (w_in_spec,), (w_shape,), (w_out_spec,) = _cast_slabs([w_cast], layer, T // tb, lambda n: n)
    return pl.pallas_call(
        functools.partial(_gla_kernel, tb=tb, rb=rb),
        out_shape=[jax.ShapeDtypeStruct((T, vw), BF16), w_shape],
        grid=(T // tb,),
        in_specs=[
            pl.BlockSpec((tb, qw), lambda n: (n, C_GQ // qw)),
            pl.BlockSpec((tb, qw), lambda n: (n, C_GK // qw)),
            pl.BlockSpec((tb, vw), lambda n: (n, C_GV // vw)),
            pl.BlockSpec((tb, vw), lambda n: (n, C_GR // vw)),
            pl.BlockSpec((tb, LANES), lambda n: (n, 0)),
            pl.BlockSpec((LANES, qw), lambda n: (0, 0)),
            pl.BlockSpec((1, qw), lambda n: (0, 0)),
            pl.BlockSpec((1, GLA_DV), lambda n: (0, 0)),
            w_in_spec,
        ],
        out_specs=[pl.BlockSpec((tb, vw), lambda n: (n, 0)), w_out_spec],
        scratch_shapes=[pltpu.VMEM((GLA_HEADS, GLA_DV, LANES), F32),
                        pltpu.VMEM((tb, vw), F32)],
        compiler_params=_cparams(("arbitrary",)),
        name="gla",
    )(proj, proj, proj, proj, ps, wg_pad, bg, gnorm, w_cast)


def _mix_xattn_kernel(h_ref, of_ref, os_ref, og_ref, wo_ref, gx_ref, wq_ref, kv_ref, wx_ref, o_ref):
    mixed = jnp.concatenate([of_ref[...], os_ref[...], og_ref[...]], axis=1)
    h1 = h_ref[...] + _dot(mixed, wo_ref[...])
    xn = _rms(h1, gx_ref[...]).astype(BF16)
    q = (_dot(xn, wq_ref[...]) * (HEAD_DIM ** -0.5)).astype(BF16)
    xw = XATTN_HEADS * HEAD_DIM
    outs = []
    for hh in range(XATTN_HEADS):
        cols = slice(hh * HEAD_DIM, (hh + 1) * HEAD_DIM)
        k = kv_ref[:, cols]
        v = kv_ref[:, xw + hh * HEAD_DIM:xw + (hh + 1) * HEAD_DIM]
        s = _dot_nt(q[:, cols], k)
        p = jnp.exp(s - jnp.max(s, axis=1, keepdims=True))
        l = jnp.sum(p, axis=1, keepdims=True)
        outs.append((_dot(p.astype(BF16), v) / l).astype(BF16))
    o = jnp.concatenate(outs, axis=1)
    o_ref[...] = h1 + _dot(o, wx_ref[...])


def mix_xattn(h, o_fox, o_swa, o_gla, w_out, gx, wq, kv, wo, *, tm):
    T, D = h.shape
    const = lambda shape: pl.BlockSpec(shape, lambda i: (0, 0))
    row = lambda w: pl.BlockSpec((tm, w), lambda i: (i, 0))
    return pl.pallas_call(
        _mix_xattn_kernel,
        out_shape=jax.ShapeDtypeStruct((T, D), F32),
        grid=(T // tm,),
        in_specs=[row(D), row(o_fox.shape[1]), row(o_swa.shape[1]), row(o_gla.shape[1]),
                  const(w_out.shape), const((1, D)), const(wq.shape), const(kv.shape), const(wo.shape)],
        out_specs=row(D),
        compiler_params=_cparams(("arbitrary",)),
        name="mix_xattn",
    )(h, o_fox, o_swa, o_gla, w_out, gx.reshape(1, D), wq, kv, wo)


def _ffn_kernel(x_ref, g_ref, wg_ref, wv_ref, cwg_ref, cwv_ref, cbg_ref, cbv_ref, wd_ref, fg_ref,
                o_ref, xn_ref, hg_ref, hv_ref, ug_ref, uv_ref, *, tm, sub, final):
    i = pl.program_id(0)
    f = pl.program_id(1)
    H = SUBLANES
    fc = wd_ref.shape[0]

    @pl.when(f == 0)
    def _():
        x = x_ref[...]
        xn_ref[...] = _rms(x, g_ref[...]).astype(BF16)
        o_ref[...] = x

    @pl.when(i == 0)
    def _():
        hg_ref[f] = jnp.zeros(hg_ref.shape[1:], F32)
        hv_ref[f] = jnp.zeros(hv_ref.shape[1:], F32)

    xn = xn_ref[...]

    def up(c, slot):
        ug_ref[slot, H:, :] = _dot(xn, wg_ref[:, c:c + sub])
        uv_ref[slot, H:, :] = _dot(xn, wv_ref[:, c:c + sub])

    def conv(u_ref, slot, h_ref, cw_ref, cb_ref, c):
        u_ref[slot, 0:H, :] = h_ref[f, :, c:c + sub]
        h_ref[f, :, c:c + sub] = u_ref[slot, tm:tm + H, :]
        return (((cw_ref[0:1, c:c + sub] * u_ref[slot, H - 2:H - 2 + tm, :]
                  + cw_ref[1:2, c:c + sub] * u_ref[slot, H - 1:H - 1 + tm, :])
                 + cw_ref[2:3, c:c + sub] * u_ref[slot, H:H + tm, :]) + cb_ref[:, c:c + sub])

    def down(c, slot):
        gate = conv(ug_ref, slot, hg_ref, cwg_ref, cbg_ref, c)
        val = conv(uv_ref, slot, hv_ref, cwv_ref, cbv_ref, c)
        act = (jax.nn.silu(gate) * val).astype(BF16)
        o_ref[...] += _dot(act, wd_ref[c:c + sub, :])

    up(0, 0)
    for n, c in enumerate(range(0, fc, sub)):
        if c + sub < fc:
            up(c + sub, (n + 1) % 2)
        down(c, n % 2)

    if final:
        @pl.when(f == pl.num_programs(1) - 1)
        def _():
            o_ref[...] = _rms(o_ref[...], fg_ref[...])


def conv_ffn(x, g, w_up, conv_w, conv_b, w_down, layer, final_g, *, tm, fc, final):
    T, D = x.shape
    dff = w_down.shape[0]
    L = conv_w.shape[0]
    nf = dff // fc
    cb = conv_b.reshape(L, 1, 2 * dff)
    return pl.pallas_call(
        functools.partial(_ffn_kernel, tm=tm, sub=MXU_COLS, final=final),
        out_shape=jax.ShapeDtypeStruct((T, D), F32),
        grid=(T // tm, nf),
        in_specs=[
            pl.BlockSpec((tm, D), lambda i, f: (i, 0)),
            pl.BlockSpec((1, D), lambda i, f: (0, 0)),
            pl.BlockSpec((D, fc), lambda i, f: (0, f)),
            pl.BlockSpec((D, fc), lambda i, f: (0, nf + f)),
            pl.BlockSpec((None, CONV_WIDTH, fc), lambda i, f: (layer, 0, f)),
            pl.BlockSpec((None, CONV_WIDTH, fc), lambda i, f: (layer, 0, nf + f)),
            pl.BlockSpec((None, 1, fc), lambda i, f: (layer, 0, f)),
            pl.BlockSpec((None, 1, fc), lambda i, f: (layer, 0, nf + f)),
            pl.BlockSpec((fc, D), lambda i, f: (f, 0)),
            pl.BlockSpec((1, D), lambda i, f: (0, 0)),
        ],
        out_specs=pl.BlockSpec((tm, D), lambda i, f: (i, 0)),
        scratch_shapes=[pltpu.VMEM((tm, D), BF16), pltpu.VMEM((nf, SUBLANES, fc), F32),
                        pltpu.VMEM((nf, SUBLANES, fc), F32),
                        pltpu.VMEM((2, tm + SUBLANES, MXU_COLS), F32),
                        pltpu.VMEM((2, tm + SUBLANES, MXU_COLS), F32)],
        compiler_params=_cparams(("arbitrary", "arbitrary")),
        name="conv_ffn",
    )(x, g.reshape(1, D), w_up, w_up, conv_w, conv_w, cb, cb, w_down, final_g.reshape(1, D))


TR_PREP, TC_PREP = 512, 512
TM_PROJ, TN_PROJ = 1024, 1536
TB_GATE = 512
TQ_FOX, TK_FOX, NH_FOX = 512, 512, 4
TB_SWA = 512
TB_GLA, RB_GLA = 512, 256
TM_MIX = 512
TM_FFN, FC_FFN = 512, 512


def _main_colscale():
    cs = np.ones((1, MAIN_COLS), np.float32)
    cs[0, C_FQ:C_FK] = HEAD_DIM ** -0.5 * LOG2E
    cs[0, C_SQ:C_SK] = HEAD_DIM ** -0.5
    cs[0, C_GQ:C_GK] = GLA_DK ** -0.5
    return jnp.asarray(cs)


def kernel(x, mem, w_in, b_fox_f, swa_sinks, t5_bias, w_gla_gate, b_gla_gate, gla_norm, w_mix_out, norm_mix,
           norm_xattn, norm_mem, wq_x, wkv_x, wo_x, norm_ffn, w_up, conv_w, conv_b, w_down, final_norm):
    depth = w_in.shape[0]
    _, T, D = x.shape
    M = mem.shape[1]
    h = x.reshape(T, D)
    memf = mem.reshape(M, D)
    colscale = _main_colscale()
    ones_kv = jnp.ones((1, wkv_x.shape[2]), F32)
    table = t5_table(t5_bias)
    pad_lanes = LANES - S_GLR - GLA_GATE_RANK
    w_main, w_small = prep_w_in(w_in, tr=TR_PREP, tc=TC_PREP)
    for l in range(depth):
        bvec = jnp.concatenate([b_fox_f[l], jnp.zeros((LANES - FOX_HEADS,), F32)]).reshape(1, LANES)
        wg_pad = jnp.concatenate([jnp.zeros((S_GLR, GLA_HEADS * GLA_DK), F32), w_gla_gate[l],
                                  jnp.zeros((pad_lanes, GLA_HEADS * GLA_DK), F32)], axis=0).astype(BF16)

        proj, ps = norm_proj(h, norm_mix[l], w_main, l, colscale, w_small, tm=TM_PROJ, tn=TN_PROJ, w_t=True)
        kx, row = fox_gate(ps, bvec, tb=TB_GATE)
        o_fox, w_down_b = fox_attention(proj, kx, row, w_down, l, tq=TQ_FOX, tk=TK_FOX, nh=NH_FOX)
        o_swa, (w_out_b, wq_b, wkv_b, wo_b) = swa_attention(proj, table, swa_sinks[l],
                                                           (w_mix_out, wq_x, wkv_x, wo_x), l, tb=TB_SWA)
        o_gla, w_up_b = gla(proj, ps, wg_pad, b_gla_gate[l].reshape(1, -1), gla_norm[l].reshape(1, -1),
                            w_up, l, tb=TB_GLA, rb=RB_GLA)
        kv = norm_proj(memf, norm_mem[l], wkv_b[None], 0, ones_kv, tm=M, tn=wkv_x.shape[2])
        h = mix_xattn(h, o_fox, o_swa, o_gla, w_out_b, norm_xattn[l], wq_b, kv, wo_b, tm=TM_MIX)
        h = conv_ffn(h, norm_ffn[l], w_up_b, conv_w, conv_b, w_down_b, l, final_norm,
                     tm=TM_FFN, fc=FC_FFN, final=(l == depth - 1))
    return h.reshape(x.shape)
```

```python
import functools
import math

import numpy as np
import jax
import jax.numpy as jnp
from jax import lax
from jax.experimental import pallas as pl
from jax.experimental.pallas import tpu as pltpu

F32 = jnp.float32
BF16 = jnp.bfloat16

HEAD_DIM = 128
FOX_HEADS = 4
SWA_HEADS = 8
SWA_KV_HEADS = 2
SWA_GROUP = SWA_HEADS // SWA_KV_HEADS
GLA_HEADS = 4
GLA_DK = 64
GLA_DV = 128
GLA_GATE_RANK = 16
GLA_TAU = 16.0
GLA_CHUNK = 64
SWA_WINDOW = 128
NUM_BUCKETS = 32
T5_MAX_DISTANCE = 128
XATTN_HEADS = 4
CONV_WIDTH = 3
EPS = 1e-6
LOG2E = math.log2(math.e)

LANES = 128
SUBLANES = 8
BF16_ROWS = 16
MXU_COLS = 256
VMEM_LIMIT = 56 * 1024 * 1024

_FOX_W = FOX_HEADS * HEAD_DIM
_OFF_FF = 3 * _FOX_W
_OFF_SQ = _OFF_FF + FOX_HEADS
_MAIN_B = (SWA_HEADS + 2 * SWA_KV_HEADS) * HEAD_DIM + 2 * GLA_HEADS * GLA_DK + 2 * GLA_HEADS * GLA_DV
_OFF_GLR = _OFF_SQ + _MAIN_B
C_FQ, C_FK, C_FV = 0, 512, 1024
C_SQ, C_SK, C_SV = 1536, 2560, 2816
C_GQ, C_GK, C_GV, C_GR = 3072, 3328, 3584, 4096
MAIN_COLS = 4608
S_FF, S_GLR = 0, SUBLANES


def _cparams(sem):
    return pltpu.CompilerParams(dimension_semantics=sem, vmem_limit_bytes=VMEM_LIMIT)


def _rms(x, g):
    return x * lax.rsqrt(jnp.mean(x * x, axis=-1, keepdims=True) + EPS) * g


N_SPLIT = 3


def _split3(x):
    hi = x.astype(BF16)
    r1 = x - hi.astype(F32)
    mid = r1.astype(BF16)
    lo = (r1 - mid.astype(F32)).astype(BF16)
    return hi, mid, lo


def _dot(a, b):
    return jnp.dot(a, b, preferred_element_type=F32)


def _dot_nt(a, b):
    return lax.dot_general(a, b, (((1,), (1,)), ((), ())), preferred_element_type=F32)


def _dot_tn(a, b):
    return lax.dot_general(a, b, (((0,), (0,)), ((), ())), preferred_element_type=F32)


def _sel_dot(sel, x):
    hi, mid, lo = _split3(x)
    return _dot(sel, hi) + _dot(sel, mid) + _dot(sel, lo)


def _prep_w_in_kernel(cur_ref, ff_ref, glr_ref, wm_ref, ws_ref):
    L = wm_ref.shape[0]
    wm_ref[...] = pltpu.einshape("cld->lcd", cur_ref[...]).astype(BF16)

    @pl.when(pl.program_id(1) == 0)
    def _():
        pad = jnp.zeros((LANES - S_GLR - GLA_GATE_RANK, ws_ref.shape[2]), F32)
        for l in range(L):
            ws_ref[l] = jnp.concatenate([ff_ref[:, l, :], glr_ref[:, l, :], pad], axis=0).astype(BF16)


def prep_w_in(w_in, *, tr, tc):
    L, D, _ = w_in.shape
    wt = jnp.transpose(w_in, (2, 0, 1))
    assert _OFF_FF % tc == 0 and MAIN_COLS % tc == 0 and S_GLR % SUBLANES == 0

    def src(j):
        return j * tc + jnp.where(j * tc >= _OFF_FF, _OFF_SQ - _OFF_FF, 0)

    return pl.pallas_call(
        _prep_w_in_kernel,
        out_shape=[jax.ShapeDtypeStruct((L, MAIN_COLS, D), BF16),
                   jax.ShapeDtypeStruct((L, LANES, D), BF16)],
        grid=(D // tr, MAIN_COLS // tc),
        in_specs=[
            pl.BlockSpec((pl.Element(tc), pl.Element(L), pl.Element(tr)), lambda r, j: (src(j), 0, r * tr)),
            pl.BlockSpec((pl.Element(S_GLR), pl.Element(L), pl.Element(tr)), lambda r, j: (_OFF_FF, 0, r * tr)),
            pl.BlockSpec((pl.Element(GLA_GATE_RANK), pl.Element(L), pl.Element(tr)),
                         lambda r, j: (_OFF_GLR, 0, r * tr)),
        ],
        out_specs=[pl.BlockSpec((L, tc, tr), lambda r, j: (0, j, r)),
                   pl.BlockSpec((L, LANES, tr), lambda r, j: (0, 0, r))],
        compiler_params=_cparams(("arbitrary", "arbitrary")),
        name="prep_w_in",
    )(wt, wt, wt)


def _norm_proj_kernel(*refs, has_small, w_t):
    if has_small:
        x_ref, g_ref, w_ref, cs_ref, ws_ref, o_ref, os_ref, xn_ref = refs
    else:
        x_ref, g_ref, w_ref, cs_ref, o_ref, xn_ref = refs
    dot = _dot_nt if w_t else _dot

    @pl.when(pl.program_id(1) == 0)
    def _():
        xn = _rms(x_ref[...], g_ref[...]).astype(BF16)
        xn_ref[...] = xn
        if has_small:
            os_ref[...] = dot(xn, ws_ref[...])

    o_ref[...] = (dot(xn_ref[...], w_ref[...]) * cs_ref[...]).astype(o_ref.dtype)


def norm_proj(x, g, w, layer, colscale, w_small=None, *, tm, tn, w_t=False):
    T, D = x.shape
    N = w.shape[1] if w_t else w.shape[2]
    has_small = w_small is not None

    def wspec(n):
        if w_t:
            return pl.BlockSpec((None, n, D), lambda i, j: (layer, j, 0))
        return pl.BlockSpec((None, D, n), lambda i, j: (layer, 0, j))

    in_specs = [
        pl.BlockSpec((tm, D), lambda i, j: (i, 0)),
        pl.BlockSpec((1, D), lambda i, j: (0, 0)),
        wspec(tn),
        pl.BlockSpec((1, tn), lambda i, j: (0, j)),
    ]
    args = [x, g.reshape(1, D), w, colscale]
    out_shape = [jax.ShapeDtypeStruct((T, N), BF16)]
    out_specs = [pl.BlockSpec((tm, tn), lambda i, j: (i, j))]
    if has_small:
        in_specs.append(pl.BlockSpec((None,) + w_small.shape[1:], lambda i, j: (layer, 0, 0)))
        args.append(w_small)
        out_shape.append(jax.ShapeDtypeStruct((T, LANES), F32))
        out_specs.append(pl.BlockSpec((tm, LANES), lambda i, j: (i, 0)))
    outs = pl.pallas_call(
        functools.partial(_norm_proj_kernel, has_small=has_small, w_t=w_t),
        out_shape=out_shape,
        grid=(T // tm, N // tn),
        in_specs=in_specs,
        out_specs=out_specs,
        scratch_shapes=[pltpu.VMEM((tm, D), BF16)],
        compiler_params=_cparams(("arbitrary", "arbitrary")),
        name="norm_proj_small" if has_small else "norm_proj",
    )(*args)
    return outs if has_small else outs[0]


def _fox_gate_kernel(ps_ref, b_ref, kx_ref, row_ref, carry_ref, *, tb):
    @pl.when(pl.program_id(0) == 0)
    def _():
        carry_ref[...] = jnp.zeros_like(carry_ref)

    x = jax.nn.log_sigmoid(ps_ref[...] + b_ref[...])
    r = lax.broadcasted_iota(jnp.int32, (tb, tb), 0)
    c = lax.broadcasted_iota(jnp.int32, (tb, tb), 1)
    tril = jnp.where(c <= r, 1.0, 0.0).astype(BF16)
    csum = _sel_dot(tril, x) + carry_ref[0:1, :]
    carry_ref[...] = jnp.broadcast_to(csum[tb - 1:tb, :], carry_ref.shape)
    pieces = _split3(csum * LOG2E)
    neg = [-p for p in pieces]
    src = lax.broadcasted_iota(jnp.int32, (LANES, FOX_HEADS * LANES), 0)
    dst = lax.broadcasted_iota(jnp.int32, (LANES, FOX_HEADS * LANES), 1)
    kx = sum(_dot(neg[n], jnp.where((dst // LANES == src - S_FF) & (dst % LANES == n), 1.0, 0.0).astype(BF16))
             for n in range(len(neg))).astype(BF16)
    sub = lax.broadcasted_iota(jnp.int32, (FOX_HEADS * SUBLANES, LANES), 0)
    lane = lax.broadcasted_iota(jnp.int32, (FOX_HEADS * SUBLANES, LANES), 1)
    pick = jnp.where(lane - S_FF == sub // SUBLANES, 1.0, 0.0).astype(BF16)
    rows = sum(_dot_nt(pick, p) for p in pieces)
    for h in range(FOX_HEADS):
        kx_ref[h] = kx[:, h * LANES:(h + 1) * LANES]
        row_ref[h, 0] = rows[h * SUBLANES:(h + 1) * SUBLANES, :]


def fox_gate(ps, bvec, *, tb):
    T = ps.shape[0]
    nb = T // tb
    return pl.pallas_call(
        functools.partial(_fox_gate_kernel, tb=tb),
        out_shape=[jax.ShapeDtypeStruct((FOX_HEADS, T, LANES), BF16),
                   jax.ShapeDtypeStruct((FOX_HEADS, nb, SUBLANES, tb), F32)],
        grid=(nb,),
        in_specs=[pl.BlockSpec((tb, LANES), lambda i: (i, 0)),
                  pl.BlockSpec((1, LANES), lambda i: (0, 0))],
        out_specs=[pl.BlockSpec((FOX_HEADS, tb, LANES), lambda i: (0, i, 0)),
                   pl.BlockSpec((FOX_HEADS, 1, SUBLANES, tb), lambda i: (0, i, 0, 0))],
        scratch_shapes=[pltpu.VMEM((SUBLANES, LANES), F32)],
        compiler_params=_cparams(("arbitrary",)),
        name="fox_gate",
    )(ps, bvec)


def _fox_attn_kernel(q_ref, k_ref, v_ref, kx_ref, cq_ref, wi_ref, o_ref, wo_ref,
                     vt_ref, ya_ref, yb_ref, xa_ref, xb_ref, m_ref, acc_ref, *, tq, tk, nk, nh):
    i = pl.program_id(1)
    D = HEAD_DIM
    heads = range(nh)
    n_full = i // (tk // tq)
    _cast_along([wi_ref], [wo_ref])

    @pl.when(i == 0)
    def _():
        for hh in heads:
            for n in range(nk):
                vt_ref[hh, n, 0:D, :] = v_ref[n * tk:(n + 1) * tk, hh * D:(hh + 1) * D].T
                vt_ref[hh, n, D:, :] = jnp.ones((BF16_ROWS, tk), BF16)

    lane = lax.broadcasted_iota(jnp.int32, (tq, LANES), 1)
    ones = jnp.where(lane < N_SPLIT, 1.0, 0.0).astype(BF16)
    q_aug = [jnp.concatenate([q_ref[:, hh * D:(hh + 1) * D], ones], axis=1) for hh in heads]
    cq = [cq_ref[hh, 0, 0:1, :] for hh in heads]
    m_ref[...] = jnp.full(m_ref.shape, -jnp.inf, F32)
    acc_ref[...] = jnp.zeros(acc_ref.shape, F32)

    def scores(hh, j):
        start = pl.multiple_of(j * tk, tk)
        k_aug = jnp.concatenate([k_ref[pl.ds(start, tk), hh * D:(hh + 1) * D], kx_ref[hh, pl.ds(start, tk), :]],
                                axis=1)
        return _dot_nt(k_aug, q_aug[hh])

    def update(hh, j, y, ymax):
        m_old = m_ref[hh]
        m_new = jnp.maximum(m_old, ymax + cq[hh])
        alpha = jnp.exp2(m_old - m_new)
        p = jnp.exp2(y + (cq[hh] - m_new))
        acc_ref[hh] = alpha * acc_ref[hh] + _dot(vt_ref[hh, j], p.astype(BF16))
        m_ref[hh] = m_new

    def produce(hh, j, y_ref, ymax_ref):
        y = scores(hh, j)
        y_ref[hh] = y
        ymax_ref[hh] = jnp.max(y, axis=0, keepdims=True)

    for hh in heads:
        produce(hh, 0, ya_ref, xa_ref)

    def step(j, cur_ref, cmax_ref, nxt_ref, nmax_ref):
        for hh in heads:
            produce(hh, j + 1, nxt_ref, nmax_ref)
            update(hh, j, cur_ref[hh], cmax_ref[hh])

    def body(j, carry):
        pl.when(j % 2 == 0)(lambda: step(j, ya_ref, xa_ref, yb_ref, xb_ref))
        pl.when(j % 2 == 1)(lambda: step(j, yb_ref, xb_ref, ya_ref, xa_ref))
        return carry

    lax.fori_loop(0, n_full, body, 0)
    key = n_full * tk + lax.broadcasted_iota(jnp.int32, (tk, tq), 0)
    qry = i * tq + lax.broadcasted_iota(jnp.int32, (tk, tq), 1)

    def diagonal(cur_ref):
        for hh in heads:
            y = jnp.where(key <= qry, cur_ref[hh], -jnp.inf)
            update(hh, n_full, y, jnp.max(y, axis=0, keepdims=True))
            o_ref[:, hh * D:(hh + 1) * D] = (acc_ref[hh, 0:D, :] / acc_ref[hh, D:D + 1, :]).T.astype(o_ref.dtype)

    pl.when(n_full % 2 == 0)(lambda: diagonal(ya_ref))
    pl.when(n_full % 2 == 1)(lambda: diagonal(yb_ref))


def fox_attention(proj, kx, row, w_cast, layer, *, tq, tk, nh):
    T = proj.shape[0]
    nq = T // tq
    ng = FOX_HEADS // nh
    gw = nh * HEAD_DIM
    qb, kb, vb = C_FQ // gw, C_FK // gw, C_FV // gw
    assert tk % tq == 0 and T % tk == 0
    (w_in_spec,), (w_shape,), (w_out_spec,) = _cast_slabs([w_cast], layer, ng * nq, lambda g, i: g * nq + i)
    return pl.pallas_call(
        functools.partial(_fox_attn_kernel, tq=tq, tk=tk, nk=T // tk, nh=nh),
        out_shape=[jax.ShapeDtypeStruct((T, FOX_HEADS * HEAD_DIM), BF16), w_shape],
        grid=(ng, nq),
        in_specs=[
            pl.BlockSpec((tq, gw), lambda g, i: (i, qb + g)),
            pl.BlockSpec((T, gw), lambda g, i: (0, kb + g), pipeline_mode=pl.Buffered(1)),
            pl.BlockSpec((T, gw), lambda g, i: (0, vb + g), pipeline_mode=pl.Buffered(1)),
            pl.BlockSpec((nh, T, LANES), lambda g, i: (g, 0, 0), pipeline_mode=pl.Buffered(1)),
            pl.BlockSpec((nh, 1, SUBLANES, tq), lambda g, i: (g, i, 0, 0)),
            w_in_spec,
        ],
        out_specs=[pl.BlockSpec((tq, gw), lambda g, i: (i, g)), w_out_spec],
        scratch_shapes=[pltpu.VMEM((nh, T // tk, HEAD_DIM + BF16_ROWS, tk), BF16),
                        pltpu.VMEM((nh, tk, tq), F32), pltpu.VMEM((nh, tk, tq), F32),
                        pltpu.VMEM((nh, 1, tq), F32), pltpu.VMEM((nh, 1, tq), F32),
                        pltpu.VMEM((nh, 1, tq), F32),
                        pltpu.VMEM((nh, HEAD_DIM + BF16_ROWS, tq), F32)],
        compiler_params=_cparams(("arbitrary", "arbitrary")),
        name="fox_attn",
    )(proj, proj, proj, kx, row, w_cast)


def _t5_table_kernel(t5_ref, o_ref):
    h = pl.program_id(0)
    W = SWA_WINDOW
    i = lax.broadcasted_iota(jnp.int32, (W, 2 * W), 0)
    j = lax.broadcasted_iota(jnp.int32, (W, 2 * W), 1)
    rel = (W + i) - j
    n = jnp.maximum(rel, 0)
    max_exact = NUM_BUCKETS // 2
    nf = jnp.maximum(n, 1).astype(F32)
    large = max_exact + (jnp.log(nf / max_exact) / math.log(T5_MAX_DISTANCE / max_exact)
                         * (NUM_BUCKETS - max_exact)).astype(jnp.int32)
    large = jnp.minimum(large, NUM_BUCKETS - 1)
    bucket = jnp.where(n < max_exact, n, large)
    bias = jnp.zeros((W, 2 * W), F32)
    for b in range(NUM_BUCKETS):
        bias = jnp.where(bucket == b, t5_ref[b, h], bias)
    in_window = (rel >= 0) & (rel < W)
    o_ref[...] = jnp.where(in_window, bias, -jnp.inf)


def t5_table(t5_bias):
    W = SWA_WINDOW
    return pl.pallas_call(
        _t5_table_kernel,
        out_shape=jax.ShapeDtypeStruct((SWA_HEADS, W, 2 * W), F32),
        grid=(SWA_HEADS,),
        in_specs=[pl.BlockSpec(memory_space=pltpu.SMEM)],
        out_specs=pl.BlockSpec((None, W, 2 * W), lambda h: (h, 0, 0)),
        compiler_params=_cparams(("arbitrary",)),
        name="t5_table",
    )(t5_bias)


def _cast_slabs(ws, layer, nsteps, step):
    in_specs, shapes, out_specs = [], [], []
    for w in ws:
        _, rows, cols = w.shape
        slab = rows // nsteps
        assert slab * nsteps == rows and slab % BF16_ROWS == 0
        in_specs.append(pl.BlockSpec((None, slab, cols), lambda *ids: (layer, step(*ids), 0)))
        out_specs.append(pl.BlockSpec((slab, cols), lambda *ids: (step(*ids), 0)))
        shapes.append(jax.ShapeDtypeStruct((rows, cols), BF16))
    return in_specs, shapes, out_specs


def _cast_along(wi_refs, wo_refs):
    for wi_ref, wo_ref in zip(wi_refs, wo_refs, strict=True):
        wo_ref[...] = wi_ref[...].astype(wo_ref.dtype)


def _swa_kernel(sink_ref, q_ref, kc_ref, kp_ref, vc_ref, vp_ref, tab_ref, *refs, nsub, ncast):
    wi_refs, o_ref, wo_refs = refs[:ncast], refs[ncast], refs[ncast + 1:]
    g = pl.program_id(0)
    n = pl.program_id(1)
    W = SWA_WINDOW
    _cast_along(wi_refs, wo_refs)

    first_prev = lax.broadcasted_iota(jnp.int32, (W, 2 * W), 1) < W
    for sub in range(nsub):
        if sub == 0:
            kp, vp = kp_ref[...], vp_ref[...]
        else:
            kp, vp = kc_ref[(sub - 1) * W:sub * W, :], vc_ref[(sub - 1) * W:sub * W, :]
        kk = jnp.concatenate([kp, kc_ref[sub * W:(sub + 1) * W, :]], axis=0)
        vv = jnp.concatenate([vp, vc_ref[sub * W:(sub + 1) * W, :]], axis=0)
        q = jnp.concatenate([q_ref[sub * W:(sub + 1) * W, hh * HEAD_DIM:(hh + 1) * HEAD_DIM]
                             for hh in range(SWA_GROUP)], axis=0)
        qk = _dot_nt(q, kk)
        ps, ls = [], []
        for hh in range(SWA_GROUP):
            s = qk[hh * W:(hh + 1) * W, :] + tab_ref[hh]
            if sub == 0:
                s = jnp.where(jnp.logical_and(first_prev, n == 0), -jnp.inf, s)
            sink = sink_ref[0, g * SWA_GROUP + hh]
            m = jnp.maximum(jnp.max(s, axis=1, keepdims=True), sink)
            p = jnp.exp(s - m)
            ls.append(jnp.sum(p, axis=1, keepdims=True) + jnp.exp(sink - m))
            ps.append(p.astype(BF16))
        o = _dot(jnp.concatenate(ps, axis=0), vv)
        for hh in range(SWA_GROUP):
            oh = o[hh * W:(hh + 1) * W, :] / ls[hh]
            o_ref[sub * W:(sub + 1) * W, hh * HEAD_DIM:(hh + 1) * HEAD_DIM] = oh.astype(o_ref.dtype)


def swa_attention(proj, table, sinks, w_cast, layer, *, tb):
    T = proj.shape[0]
    W = SWA_WINDOW
    nsub = tb // W
    nt = T // tb
    gw = SWA_GROUP * HEAD_DIM
    qb, kb, vb = C_SQ // gw, C_SK // HEAD_DIM, C_SV // HEAD_DIM
    prev = lambda n: jnp.maximum(n * nsub - 1, 0)
    w_in_specs, w_shapes, w_out_specs = _cast_slabs(w_cast, layer, SWA_KV_HEADS * nt, lambda g, n: g * nt + n)
    o, *w_bf16 = pl.pallas_call(
        functools.partial(_swa_kernel, nsub=nsub, ncast=len(w_cast)),
        out_shape=[jax.ShapeDtypeStruct((T, SWA_HEADS * HEAD_DIM), BF16), *w_shapes],
        grid=(SWA_KV_HEADS, nt),
        in_specs=[
            pl.BlockSpec(memory_space=pltpu.SMEM),
            pl.BlockSpec((tb, gw), lambda g, n: (n, qb + g)),
            pl.BlockSpec((tb, HEAD_DIM), lambda g, n: (n, kb + g)),
            pl.BlockSpec((W, HEAD_DIM), lambda g, n: (prev(n), kb + g)),
            pl.BlockSpec((tb, HEAD_DIM), lambda g, n: (n, vb + g)),
            pl.BlockSpec((W, HEAD_DIM), lambda g, n: (prev(n), vb + g)),
            pl.BlockSpec((SWA_GROUP, W, 2 * W), lambda g, n: (g, 0, 0)),
            *w_in_specs,
        ],
        out_specs=[pl.BlockSpec((tb, gw), lambda g, n: (n, g)), *w_out_specs],
        compiler_params=_cparams(("arbitrary", "arbitrary")),
        name="swa_attn",
    )(sinks.reshape(1, SWA_HEADS), proj, proj, proj, proj, proj, table, *w_cast)
    return o, w_bf16


def _gla_kernel(q_ref, k_ref, v_ref, r_ref, ps_ref, wg_ref, bg_ref, gn_ref, wi_ref, o_ref, wo_ref,
                st_ref, oc_ref, *, tb, rb):
    C = GLA_CHUNK
    _cast_along([wi_ref], [wo_ref])

    @pl.when(pl.program_id(0) == 0)
    def _():
        st_ref[...] = jnp.zeros_like(st_ref)

    r = lax.broadcasted_iota(jnp.int32, (rb, rb), 0)
    c = lax.broadcasted_iota(jnp.int32, (rb, rb), 1)
    same = (r // C) == (c // C)
    tril = jnp.where(jnp.logical_and(same, c <= r), 1.0, 0.0).astype(BF16)
    whole = jnp.where(same, 1.0, 0.0).astype(BF16)
    lane = lax.broadcasted_iota(jnp.int32, (1, LANES), 1)
    causal = (lax.broadcasted_iota(jnp.int32, (C, C), 1) <= lax.broadcasted_iota(jnp.int32, (C, C), 0))

    def prepare(blk):
        glr = ps_ref[blk, :].astype(BF16)
        g = jax.nn.log_sigmoid(_dot(glr, wg_ref[...]) + bg_ref[...]) / GLA_TAU
        gh, gm, gl = _split3(g)
        b = _dot(tril, gh) + _dot(tril, gm) + _dot(tril, gl)
        b_last = _dot(whole, gh) + _dot(whole, gm) + _dot(whole, gl)
        q_t = (q_ref[blk, :].astype(F32) * jnp.exp(b)).astype(BF16)
        kf = k_ref[blk, :].astype(F32)
        k_t = kf * jnp.exp(-b)
        k_end = kf * jnp.exp(b_last - b)
        decay = jnp.exp(b_last)
        units = []
        for n in range(rb // C):
            rows = slice(n * C, (n + 1) * C)
            per_head = []
            for h in range(GLA_HEADS):
                pair = slice((h // 2) * LANES, (h // 2 + 1) * LANES)
                mine = (lane // GLA_DK) == (h % 2)
                ktm = jnp.where(mine, k_t[rows, pair], 0.0).astype(BF16)
                kem = jnp.where(mine, k_end[rows, pair], 0.0).astype(BF16)
                qh = q_t[rows, pair]
                vh = v_ref[blk, h * GLA_DV:(h + 1) * GLA_DV][rows]
                intra = _dot(jnp.where(causal, _dot_nt(qh, ktm), 0.0).astype(BF16), vh)
                per_head.append((qh, intra, _dot_tn(vh, kem), decay[n * C:n * C + 1, pair]))
            units.append(per_head)
        return units

    blocks = [slice(s, s + rb) for s in range(0, tb, rb)]
    prepared = [prepare(blk) for blk in blocks]
    sts = [st_ref[h] for h in range(GLA_HEADS)]
    for blk, units in zip(blocks, prepared):
        for n, per_head in enumerate(units):
            rows = slice(blk.start + n * C, blk.start + (n + 1) * C)
            for h, (qh, intra, kv_t, dec) in enumerate(per_head):
                oc_ref[rows, h * GLA_DV:(h + 1) * GLA_DV] = intra + _dot_nt(qh, sts[h].astype(BF16))
                sts[h] = dec * sts[h] + kv_t
    for h in range(GLA_HEADS):
        st_ref[h] = sts[h]
    for h in range(GLA_HEADS):
        cols = slice(h * GLA_DV, (h + 1) * GLA_DV)
        o = _rms(oc_ref[:, cols], gn_ref[...])
        o_ref[:, cols] = (o * jax.nn.silu(r_ref[:, cols].astype(F32))).astype(o_ref.dtype)


def gla(proj, ps, wg_pad, bg, gnorm, w_cast, layer, *, tb, rb):
    T = proj.shape[0]
    qw = GLA_HEADS * GLA_DK
    vw = GLA_HEADS * GLA_DV
    (w_in_spec,), (w_shape,), (w_out_spec,) = _cast_slabs([w_cast], layer, T // tb, lambda n: n)
    return pl.pallas_call(
        functools.partial(_gla_kernel, tb=tb, rb=rb),
        out_shape=[jax.ShapeDtypeStruct((T, vw), BF16), w_shape],
        grid=(T // tb,),
        in_specs=[
            pl.BlockSpec((tb, qw), lambda n: (n, C_GQ // qw)),
            pl.BlockSpec((tb, qw), lambda n: (n, C_GK // qw)),
            pl.BlockSpec((tb, vw), lambda n: (n, C_GV // vw)),
            pl.BlockSpec((tb, vw), lambda n: (n, C_GR // vw)),
            pl.BlockSpec((tb, LANES), lambda n: (n, 0)),
            pl.BlockSpec((LANES, qw), lambda n: (0, 0)),
            pl.BlockSpec((1, qw), lambda n: (0, 0)),
            pl.BlockSpec((1, GLA_DV), lambda n: (0, 0)),
            w_in_spec,
        ],
        out_specs=[pl.BlockSpec((tb, vw), lambda n: (n, 0)), w_out_spec],
        scratch_shapes=[pltpu.VMEM((GLA_HEADS, GLA_DV, LANES), F32),
                        pltpu.VMEM((tb, vw), F32)],
        compiler_params=_cparams(("arbitrary",)),
        name="gla",
    )(proj, proj, proj, proj, ps, wg_pad, bg, gnorm, w_cast)


def _mix_xattn_kernel(h_ref, of_ref, os_ref, og_ref, wo_ref, gx_ref, wq_ref, kv_ref, wx_ref, o_ref):
    mixed = jnp.concatenate([of_ref[...], os_ref[...], og_ref[...]], axis=1)
    h1 = h_ref[...] + _dot(mixed, wo_ref[...])
    xn = _rms(h1, gx_ref[...]).astype(BF16)
    q = (_dot(xn, wq_ref[...]) * (HEAD_DIM ** -0.5)).astype(BF16)
    xw = XATTN_HEADS * HEAD_DIM
    outs = []
    for hh in range(XATTN_HEADS):
        cols = slice(hh * HEAD_DIM, (hh + 1) * HEAD_DIM)
        k = kv_ref[:, cols]
        v = kv_ref[:, xw + hh * HEAD_DIM:xw + (hh + 1) * HEAD_DIM]
        s = _dot_nt(q[:, cols], k)
        p = jnp.exp(s - jnp.max(s, axis=1, keepdims=True))
        l = jnp.sum(p, axis=1, keepdims=True)
        outs.append((_dot(p.astype(BF16), v) / l).astype(BF16))
    o = jnp.concatenate(outs, axis=1)
    o_ref[...] = h1 + _dot(o, wx_ref[...])


def mix_xattn(h, o_fox, o_swa, o_gla, w_out, gx, wq, kv, wo, *, tm):
    T, D = h.shape
    const = lambda shape: pl.BlockSpec(shape, lambda i: (0, 0))
    row = lambda w: pl.BlockSpec((tm, w), lambda i: (i, 0))
    return pl.pallas_call(
        _mix_xattn_kernel,
        out_shape=jax.ShapeDtypeStruct((T, D), F32),
        grid=(T // tm,),
        in_specs=[row(D), row(o_fox.shape[1]), row(o_swa.shape[1]), row(o_gla.shape[1]),
                  const(w_out.shape), const((1, D)), const(wq.shape), const(kv.shape), const(wo.shape)],
        out_specs=row(D),
        compiler_params=_cparams(("arbitrary",)),
        name="mix_xattn",
    )(h, o_fox, o_swa, o_gla, w_out, gx.reshape(1, D), wq, kv, wo)


def _ffn_kernel(x_ref, g_ref, wg_ref, wv_ref, cwg_ref, cwv_ref, cbg_ref, cbv_ref, wd_ref, fg_ref,
                o_ref, xn_ref, hg_ref, hv_ref, ug_ref, uv_ref, *, tm, sub, final):
    i = pl.program_id(0)
    f = pl.program_id(1)
    H = SUBLANES
    fc = wd_ref.shape[0]

    @pl.when(f == 0)
    def _():
        x = x_ref[...]
        xn_ref[...] = _rms(x, g_ref[...]).astype(BF16)
        o_ref[...] = x

    @pl.when(i == 0)
    def _():
        hg_ref[f] = jnp.zeros(hg_ref.shape[1:], F32)
        hv_ref[f] = jnp.zeros(hv_ref.shape[1:], F32)

    xn = xn_ref[...]

    def up(c, slot):
        ug_ref[slot, H:, :] = _dot(xn, wg_ref[:, c:c + sub])
        uv_ref[slot, H:, :] = _dot(xn, wv_ref[:, c:c + sub])

    def conv(u_ref, slot, h_ref, cw_ref, cb_ref, c):
        u_ref[slot, 0:H, :] = h_ref[f, :, c:c + sub]
        h_ref[f, :, c:c + sub] = u_ref[slot, tm:tm + H, :]
        return (((cw_ref[0:1, c:c + sub] * u_ref[slot, H - 2:H - 2 + tm, :]
                  + cw_ref[1:2, c:c + sub] * u_ref[slot, H - 1:H - 1 + tm, :])
                 + cw_ref[2:3, c:c + sub] * u_ref[slot, H:H + tm, :]) + cb_ref[:, c:c + sub])

    def down(c, slot):
        gate = conv(ug_ref, slot, hg_ref, cwg_ref, cbg_ref, c)
        val = conv(uv_ref, slot, hv_ref, cwv_ref, cbv_ref, c)
        act = (jax.nn.silu(gate) * val).astype(BF16)
        o_ref[...] += _dot(act, wd_ref[c:c + sub, :])

    up(0, 0)
    for n, c in enumerate(range(0, fc, sub)):
        if c + sub < fc:
            up(c + sub, (n + 1) % 2)
        down(c, n % 2)

    if final:
        @pl.when(f == pl.num_programs(1) - 1)
        def _():
            o_ref[...] = _rms(o_ref[...], fg_ref[...])


def conv_ffn(x, g, w_up, conv_w, conv_b, w_down, layer, final_g, *, tm, fc, final):
    T, D = x.shape
    dff = w_down.shape[0]
    L = conv_w.shape[0]
    nf = dff // fc
    cb = conv_b.reshape(L, 1, 2 * dff)
    return pl.pallas_call(
        functools.partial(_ffn_kernel, tm=tm, sub=MXU_COLS, final=final),
        out_shape=jax.ShapeDtypeStruct((T, D), F32),
        grid=(T // tm, nf),
        in_specs=[
            pl.BlockSpec((tm, D), lambda i, f: (i, 0)),
            pl.BlockSpec((1, D), lambda i, f: (0, 0)),
            pl.BlockSpec((D, fc), lambda i, f: (0, f)),
            pl.BlockSpec((D, fc), lambda i, f: (0, nf + f)),
            pl.BlockSpec((None, CONV_WIDTH, fc), lambda i, f: (layer, 0, f)),
            pl.BlockSpec((None, CONV_WIDTH, fc), lambda i, f: (layer, 0, nf + f)),
            pl.BlockSpec((None, 1, fc), lambda i, f: (layer, 0, f)),
            pl.BlockSpec((None, 1, fc), lambda i, f: (layer, 0, nf + f)),
            pl.BlockSpec((fc, D), lambda i, f: (f, 0)),
            pl.BlockSpec((1, D), lambda i, f: (0, 0)),
        ],
        out_specs=pl.BlockSpec((tm, D), lambda i, f: (i, 0)),
        scratch_shapes=[pltpu.VMEM((tm, D), BF16), pltpu.VMEM((nf, SUBLANES, fc), F32),
                        pltpu.VMEM((nf, SUBLANES, fc), F32),
                        pltpu.VMEM((2, tm + SUBLANES, MXU_COLS), F32),
                        pltpu.VMEM((2, tm + SUBLANES, MXU_COLS), F32)],
        compiler_params=_cparams(("arbitrary", "arbitrary")),
        name="conv_ffn",
    )(x, g.reshape(1, D), w_up, w_up, conv_w, conv_w, cb, cb, w_down, final_g.reshape(1, D))


TR_PREP, TC_PREP = 512, 512
TM_PROJ, TN_PROJ = 1024, 2304
TB_GATE = 512
TQ_FOX, TK_FOX, NH_FOX = 512, 512, 4
TB_SWA = 512
TB_GLA, RB_GLA = 512, 256
TM_MIX = 512
TM_FFN, FC_FFN = 512, 512


def _main_colscale():
    cs = np.ones((1, MAIN_COLS), np.float32)
    cs[0, C_FQ:C_FK] = HEAD_DIM ** -0.5 * LOG2E
    cs[0, C_SQ:C_SK] = HEAD_DIM ** -0.5
    cs[0, C_GQ:C_GK] = GLA_DK ** -0.5
    return jnp.asarray(cs)


def kernel(x, mem, w_in, b_fox_f, swa_sinks, t5_bias, w_gla_gate, b_gla_gate, gla_norm, w_mix_out, norm_mix,
           norm_xattn, norm_mem, wq_x, wkv_x, wo_x, norm_ffn, w_up, conv_w, conv_b, w_down, final_norm):
    depth = w_in.shape[0]
    _, T, D = x.shape
    M = mem.shape[1]
    h = x.reshape(T, D)
    memf = mem.reshape(M, D)
    colscale = _main_colscale()
    ones_kv = jnp.ones((1, wkv_x.shape[2]), F32)
    table = t5_table(t5_bias)
    pad_lanes = LANES - S_GLR - GLA_GATE_RANK
    w_main, w_small = prep_w_in(w_in, tr=TR_PREP, tc=TC_PREP)
    for l in range(depth):
        bvec = jnp.concatenate([b_fox_f[l], jnp.zeros((LANES - FOX_HEADS,), F32)]).reshape(1, LANES)
        wg_pad = jnp.concatenate([jnp.zeros((S_GLR, GLA_HEADS * GLA_DK), F32), w_gla_gate[l],
                                  jnp.zeros((pad_lanes, GLA_HEADS * GLA_DK), F32)], axis=0).astype(BF16)

        proj, ps = norm_proj(h, norm_mix[l], w_main, l, colscale, w_small, tm=TM_PROJ, tn=TN_PROJ, w_t=True)
        kx, row = fox_gate(ps, bvec, tb=TB_GATE)
        o_fox, w_down_b = fox_attention(proj, kx, row, w_down, l, tq=TQ_FOX, tk=TK_FOX, nh=NH_FOX)
        o_swa, (w_out_b, wq_b, wkv_b, wo_b) = swa_attention(proj, table, swa_sinks[l],
                                                           (w_mix_out, wq_x, wkv_x, wo_x), l, tb=TB_SWA)
        o_gla, w_up_b = gla(proj, ps, wg_pad, b_gla_gate[l].reshape(1, -1), gla_norm[l].reshape(1, -1),
                            w_up, l, tb=TB_GLA, rb=RB_GLA)
        kv = norm_proj(memf, norm_mem[l], wkv_b[None], 0, ones_kv, tm=M, tn=wkv_x.shape[2])
        h = mix_xattn(h, o_fox, o_swa, o_gla, w_out_b, norm_xattn[l], wq_b, kv, wo_b, tm=TM_MIX)
        h = conv_ffn(h, norm_ffn[l], w_up_b, conv_w, conv_b, w_down_b, l, final_norm,
                     tm=TM_FFN, fc=FC_FFN, final=(l == depth - 1))
    return h.reshape(x.shape)
```

```python
import functools
import math

import numpy as np
import jax
import jax.numpy as jnp
from jax import lax
from jax.experimental import pallas as pl
from jax.experimental.pallas import tpu as pltpu

F32 = jnp.float32
BF16 = jnp.bfloat16

HEAD_DIM = 128
FOX_HEADS = 4
SWA_HEADS = 8
SWA_KV_HEADS = 2
SWA_GROUP = SWA_HEADS // SWA_KV_HEADS
GLA_HEADS = 4
GLA_DK = 64
GLA_DV = 128
GLA_GATE_RANK = 16
GLA_TAU = 16.0
GLA_CHUNK = 64
SWA_WINDOW = 128
NUM_BUCKETS = 32
T5_MAX_DISTANCE = 128
XATTN_HEADS = 4
CONV_WIDTH = 3
EPS = 1e-6
LOG2E = math.log2(math.e)

LANES = 128
SUBLANES = 8
BF16_ROWS = 16
MXU_COLS = 256
VMEM_LIMIT = 56 * 1024 * 1024

_FOX_W = FOX_HEADS * HEAD_DIM
_OFF_FF = 3 * _FOX_W
_OFF_SQ = _OFF_FF + FOX_HEADS
_MAIN_B = (SWA_HEADS + 2 * SWA_KV_HEADS) * HEAD_DIM + 2 * GLA_HEADS * GLA_DK + 2 * GLA_HEADS * GLA_DV
_OFF_GLR = _OFF_SQ + _MAIN_B
C_FQ, C_FK, C_FV = 0, 512, 1024
C_SQ, C_SK, C_SV = 1536, 2560, 2816
C_GQ, C_GK, C_GV, C_GR = 3072, 3328, 3584, 4096
MAIN_COLS = 4608
S_FF, S_GLR = 0, SUBLANES


def _cparams(sem):
    return pltpu.CompilerParams(dimension_semantics=sem, vmem_limit_bytes=VMEM_LIMIT)


def _rms(x, g):
    return x * lax.rsqrt(jnp.mean(x * x, axis=-1, keepdims=True) + EPS) * g


N_SPLIT = 3


def _split3(x):
    hi = x.astype(BF16)
    r1 = x - hi.astype(F32)
    mid = r1.astype(BF16)
    lo = (r1 - mid.astype(F32)).astype(BF16)
    return hi, mid, lo


def _dot(a, b):
    return jnp.dot(a, b, preferred_element_type=F32)


def _dot_nt(a, b):
    return lax.dot_general(a, b, (((1,), (1,)), ((), ())), preferred_element_type=F32)


def _dot_tn(a, b):
    return lax.dot_general(a, b, (((0,), (0,)), ((), ())), preferred_element_type=F32)


def _sel_dot(sel, x):
    hi, mid, lo = _split3(x)
    return _dot(sel, hi) + _dot(sel, mid) + _dot(sel, lo)


def _prep_w_in_kernel(cur_ref, ff_ref, glr_ref, wm_ref, ws_ref):
    L = wm_ref.shape[0]
    wm_ref[...] = pltpu.einshape("cld->lcd", cur_ref[...]).astype(BF16)

    @pl.when(pl.program_id(1) == 0)
    def _():
        pad = jnp.zeros((LANES - S_GLR - GLA_GATE_RANK, ws_ref.shape[2]), F32)
        for l in range(L):
            ws_ref[l] = jnp.concatenate([ff_ref[:, l, :], glr_ref[:, l, :], pad], axis=0).astype(BF16)


def prep_w_in(w_in, *, tr, tc):
    L, D, _ = w_in.shape
    wt = jnp.transpose(w_in, (2, 0, 1))
    assert _OFF_FF % tc == 0 and MAIN_COLS % tc == 0 and S_GLR % SUBLANES == 0

    def src(j):
        return j * tc + jnp.where(j * tc >= _OFF_FF, _OFF_SQ - _OFF_FF, 0)

    return pl.pallas_call(
        _prep_w_in_kernel,
        out_shape=[jax.ShapeDtypeStruct((L, MAIN_COLS, D), BF16),
                   jax.ShapeDtypeStruct((L, LANES, D), BF16)],
        grid=(D // tr, MAIN_COLS // tc),
        in_specs=[
            pl.BlockSpec((pl.Element(tc), pl.Element(L), pl.Element(tr)), lambda r, j: (src(j), 0, r * tr)),
            pl.BlockSpec((pl.Element(S_GLR), pl.Element(L), pl.Element(tr)), lambda r, j: (_OFF_FF, 0, r * tr)),
            pl.BlockSpec((pl.Element(GLA_GATE_RANK), pl.Element(L), pl.Element(tr)),
                         lambda r, j: (_OFF_GLR, 0, r * tr)),
        ],
        out_specs=[pl.BlockSpec((L, tc, tr), lambda r, j: (0, j, r)),
                   pl.BlockSpec((L, LANES, tr), lambda r, j: (0, 0, r))],
        compiler_params=_cparams(("arbitrary", "arbitrary")),
        name="prep_w_in",
    )(wt, wt, wt)


def _norm_proj_kernel(*refs, has_small, w_t):
    if has_small:
        x_ref, g_ref, w_ref, cs_ref, ws_ref, o_ref, os_ref, xn_ref = refs
    else:
        x_ref, g_ref, w_ref, cs_ref, o_ref, xn_ref = refs
    dot = _dot_nt if w_t else _dot

    @pl.when(pl.program_id(1) == 0)
    def _():
        xn = _rms(x_ref[...], g_ref[...]).astype(BF16)
        xn_ref[...] = xn
        if has_small:
            os_ref[...] = dot(xn, ws_ref[...])

    o_ref[...] = (dot(xn_ref[...], w_ref[...]) * cs_ref[...]).astype(o_ref.dtype)


def norm_proj(x, g, w, layer, colscale, w_small=None, *, tm, tn, w_t=False):
    T, D = x.shape
    N = w.shape[1] if w_t else w.shape[2]
    has_small = w_small is not None

    def wspec(n):
        if w_t:
            return pl.BlockSpec((None, n, D), lambda i, j: (layer, j, 0))
        return pl.BlockSpec((None, D, n), lambda i, j: (layer, 0, j))

    in_specs = [
        pl.BlockSpec((tm, D), lambda i, j: (i, 0)),
        pl.BlockSpec((1, D), lambda i, j: (0, 0)),
        wspec(tn),
        pl.BlockSpec((1, tn), lambda i, j: (0, j)),
    ]
    args = [x, g.reshape(1, D), w, colscale]
    out_shape = [jax.ShapeDtypeStruct((T, N), BF16)]
    out_specs = [pl.BlockSpec((tm, tn), lambda i, j: (i, j))]
    if has_small:
        in_specs.append(pl.BlockSpec((None,) + w_small.shape[1:], lambda i, j: (layer, 0, 0)))
        args.append(w_small)
        out_shape.append(jax.ShapeDtypeStruct((T, LANES), F32))
        out_specs.append(pl.BlockSpec((tm, LANES), lambda i, j: (i, 0)))
    outs = pl.pallas_call(
        functools.partial(_norm_proj_kernel, has_small=has_small, w_t=w_t),
        out_shape=out_shape,
        grid=(T // tm, N // tn),
        in_specs=in_specs,
        out_specs=out_specs,
        scratch_shapes=[pltpu.VMEM((tm, D), BF16)],
        compiler_params=_cparams(("arbitrary", "arbitrary")),
        name="norm_proj_small" if has_small else "norm_proj",
    )(*args)
    return outs if has_small else outs[0]


def _fox_gate_kernel(ps_ref, b_ref, kx_ref, row_ref, carry_ref, *, tb):
    @pl.when(pl.program_id(0) == 0)
    def _():
        carry_ref[...] = jnp.zeros_like(carry_ref)

    x = jax.nn.log_sigmoid(ps_ref[...] + b_ref[...])
    r = lax.broadcasted_iota(jnp.int32, (tb, tb), 0)
    c = lax.broadcasted_iota(jnp.int32, (tb, tb), 1)
    tril = jnp.where(c <= r, 1.0, 0.0).astype(BF16)
    csum = _sel_dot(tril, x) + carry_ref[0:1, :]
    carry_ref[...] = jnp.broadcast_to(csum[tb - 1:tb, :], carry_ref.shape)
    pieces = _split3(csum * LOG2E)
    neg = [-p for p in pieces]
    src = lax.broadcasted_iota(jnp.int32, (LANES, FOX_HEADS * LANES), 0)
    dst = lax.broadcasted_iota(jnp.int32, (LANES, FOX_HEADS * LANES), 1)
    kx = sum(_dot(neg[n], jnp.where((dst // LANES == src - S_FF) & (dst % LANES == n), 1.0, 0.0).astype(BF16))
             for n in range(len(neg))).astype(BF16)
    sub = lax.broadcasted_iota(jnp.int32, (FOX_HEADS * SUBLANES, LANES), 0)
    lane = lax.broadcasted_iota(jnp.int32, (FOX_HEADS * SUBLANES, LANES), 1)
    pick = jnp.where(lane - S_FF == sub // SUBLANES, 1.0, 0.0).astype(BF16)
    rows = sum(_dot_nt(pick, p) for p in pieces)
    for h in range(FOX_HEADS):
        kx_ref[h] = kx[:, h * LANES:(h + 1) * LANES]
        row_ref[h, 0] = rows[h * SUBLANES:(h + 1) * SUBLANES, :]


def fox_gate(ps, bvec, *, tb):
    T = ps.shape[0]
    nb = T // tb
    return pl.pallas_call(
        functools.partial(_fox_gate_kernel, tb=tb),
        out_shape=[jax.ShapeDtypeStruct((FOX_HEADS, T, LANES), BF16),
                   jax.ShapeDtypeStruct((FOX_HEADS, nb, SUBLANES, tb), F32)],
        grid=(nb,),
        in_specs=[pl.BlockSpec((tb, LANES), lambda i: (i, 0)),
                  pl.BlockSpec((1, LANES), lambda i: (0, 0))],
        out_specs=[pl.BlockSpec((FOX_HEADS, tb, LANES), lambda i: (0, i, 0)),
                   pl.BlockSpec((FOX_HEADS, 1, SUBLANES, tb), lambda i: (0, i, 0, 0))],
        scratch_shapes=[pltpu.VMEM((SUBLANES, LANES), F32)],
        compiler_params=_cparams(("arbitrary",)),
        name="fox_gate",
    )(ps, bvec)


def _fox_attn_kernel(q_ref, k_ref, v_ref, kx_ref, cq_ref, wi_ref, o_ref, wo_ref,
                     vt_ref, ya_ref, yb_ref, xa_ref, xb_ref, m_ref, acc_ref, *, tq, tk, nk, nh):
    i = pl.program_id(1)
    D = HEAD_DIM
    heads = range(nh)
    n_full = i // (tk // tq)
    _cast_along([wi_ref], [wo_ref])

    @pl.when(i == 0)
    def _():
        for hh in heads:
            for n in range(nk):
                vt_ref[hh, n, 0:D, :] = v_ref[n * tk:(n + 1) * tk, hh * D:(hh + 1) * D].T
                vt_ref[hh, n, D:, :] = jnp.ones((BF16_ROWS, tk), BF16)

    lane = lax.broadcasted_iota(jnp.int32, (tq, LANES), 1)
    ones = jnp.where(lane < N_SPLIT, 1.0, 0.0).astype(BF16)
    q_aug = [jnp.concatenate([q_ref[:, hh * D:(hh + 1) * D], ones], axis=1) for hh in heads]
    cq = [cq_ref[hh, 0, 0:1, :] for hh in heads]
    m_ref[...] = jnp.full(m_ref.shape, -jnp.inf, F32)
    acc_ref[...] = jnp.zeros(acc_ref.shape, F32)

    def scores(hh, j):
        start = pl.multiple_of(j * tk, tk)
        k_aug = jnp.concatenate([k_ref[pl.ds(start, tk), hh * D:(hh + 1) * D], kx_ref[hh, pl.ds(start, tk), :]],
                                axis=1)
        return _dot_nt(k_aug, q_aug[hh])

    def update(hh, j, y, ymax):
        m_old = m_ref[hh]
        m_new = jnp.maximum(m_old, ymax + cq[hh])
        alpha = jnp.exp2(m_old - m_new)
        p = jnp.exp2(y + (cq[hh] - m_new))
        acc_ref[hh] = alpha * acc_ref[hh] + _dot(vt_ref[hh, j], p.astype(BF16))
        m_ref[hh] = m_new

    def produce(hh, j, y_ref, ymax_ref):
        y = scores(hh, j)
        y_ref[hh] = y
        ymax_ref[hh] = jnp.max(y, axis=0, keepdims=True)

    for hh in heads:
        produce(hh, 0, ya_ref, xa_ref)

    def step(j, cur_ref, cmax_ref, nxt_ref, nmax_ref):
        for hh in heads:
            produce(hh, j + 1, nxt_ref, nmax_ref)
            update(hh, j, cur_ref[hh], cmax_ref[hh])

    def body(j, carry):
        pl.when(j % 2 == 0)(lambda: step(j, ya_ref, xa_ref, yb_ref, xb_ref))
        pl.when(j % 2 == 1)(lambda: step(j, yb_ref, xb_ref, ya_ref, xa_ref))
        return carry

    lax.fori_loop(0, n_full, body, 0)
    key = n_full * tk + lax.broadcasted_iota(jnp.int32, (tk, tq), 0)
    qry = i * tq + lax.broadcasted_iota(jnp.int32, (tk, tq), 1)

    def diagonal(cur_ref):
        for hh in heads:
            y = jnp.where(key <= qry, cur_ref[hh], -jnp.inf)
            update(hh, n_full, y, jnp.max(y, axis=0, keepdims=True))
            o_ref[:, hh * D:(hh + 1) * D] = (acc_ref[hh, 0:D, :] / acc_ref[hh, D:D + 1, :]).T.astype(o_ref.dtype)

    pl.when(n_full % 2 == 0)(lambda: diagonal(ya_ref))
    pl.when(n_full % 2 == 1)(lambda: diagonal(yb_ref))


def fox_attention(proj, kx, row, w_cast, layer, *, tq, tk, nh):
    T = proj.shape[0]
    nq = T // tq
    ng = FOX_HEADS // nh
    gw = nh * HEAD_DIM
    qb, kb, vb = C_FQ // gw, C_FK // gw, C_FV // gw
    assert tk % tq == 0 and T % tk == 0
    (w_in_spec,), (w_shape,), (w_out_spec,) = _cast_slabs([w_cast], layer, ng * nq, lambda g, i: g * nq + i)
    return pl.pallas_call(
        functools.partial(_fox_attn_kernel, tq=tq, tk=tk, nk=T // tk, nh=nh),
        out_shape=[jax.ShapeDtypeStruct((T, FOX_HEADS * HEAD_DIM), BF16), w_shape],
        grid=(ng, nq),
        in_specs=[
            pl.BlockSpec((tq, gw), lambda g, i: (i, qb + g)),
            pl.BlockSpec((T, gw), lambda g, i: (0, kb + g), pipeline_mode=pl.Buffered(1)),
            pl.BlockSpec((T, gw), lambda g, i: (0, vb + g), pipeline_mode=pl.Buffered(1)),
            pl.BlockSpec((nh, T, LANES), lambda g, i: (g, 0, 0), pipeline_mode=pl.Buffered(1)),
            pl.BlockSpec((nh, 1, SUBLANES, tq), lambda g, i: (g, i, 0, 0)),
            w_in_spec,
        ],
        out_specs=[pl.BlockSpec((tq, gw), lambda g, i: (i, g)), w_out_spec],
        scratch_shapes=[pltpu.VMEM((nh, T // tk, HEAD_DIM + BF16_ROWS, tk), BF16),
                        pltpu.VMEM((nh, tk, tq), F32), pltpu.VMEM((nh, tk, tq), F32),
                        pltpu.VMEM((nh, 1, tq), F32), pltpu.VMEM((nh, 1, tq), F32),
                        pltpu.VMEM((nh, 1, tq), F32),
                        pltpu.VMEM((nh, HEAD_DIM + BF16_ROWS, tq), F32)],
        compiler_params=_cparams(("arbitrary", "arbitrary")),
        name="fox_attn",
    )(proj, proj, proj, kx, row, w_cast)


def _t5_table_kernel(t5_ref, o_ref):
    h = pl.program_id(0)
    W = SWA_WINDOW
    i = lax.broadcasted_iota(jnp.int32, (W, 2 * W), 0)
    j = lax.broadcasted_iota(jnp.int32, (W, 2 * W), 1)
    rel = (W + i) - j
    n = jnp.maximum(rel, 0)
    max_exact = NUM_BUCKETS // 2
    nf = jnp.maximum(n, 1).astype(F32)
    large = max_exact + (jnp.log(nf / max_exact) / math.log(T5_MAX_DISTANCE / max_exact)
                         * (NUM_BUCKETS - max_exact)).astype(jnp.int32)
    large = jnp.minimum(large, NUM_BUCKETS - 1)
    bucket = jnp.where(n < max_exact, n, large)
    bias = jnp.zeros((W, 2 * W), F32)
    for b in range(NUM_BUCKETS):
        bias = jnp.where(bucket == b, t5_ref[b, h], bias)
    in_window = (rel >= 0) & (rel < W)
    o_ref[...] = jnp.where(in_window, bias, -jnp.inf)


def t5_table(t5_bias):
    W = SWA_WINDOW
    return pl.pallas_call(
        _t5_table_kernel,
        out_shape=jax.ShapeDtypeStruct((SWA_HEADS, W, 2 * W), F32),
        grid=(SWA_HEADS,),
        in_specs=[pl.BlockSpec(memory_space=pltpu.SMEM)],
        out_specs=pl.BlockSpec((None, W, 2 * W), lambda h: (h, 0, 0)),
        compiler_params=_cparams(("arbitrary",)),
        name="t5_table",
    )(t5_bias)


def _cast_slabs(ws, layer, nsteps, step):
    in_specs, shapes, out_specs = [], [], []
    for w in ws:
        _, rows, cols = w.shape
        slab = rows // nsteps
        assert slab * nsteps == rows and slab % BF16_ROWS == 0
        in_specs.append(pl.BlockSpec((None, slab, cols), lambda *ids: (layer, step(*ids), 0)))
        out_specs.append(pl.BlockSpec((slab, cols), lambda *ids: (step(*ids), 0)))
        shapes.append(jax.ShapeDtypeStruct((rows, cols), BF16))
    return in_specs, shapes, out_specs


def _cast_along(wi_refs, wo_refs):
    for wi_ref, wo_ref in zip(wi_refs, wo_refs, strict=True):
        wo_ref[...] = wi_ref[...].astype(wo_ref.dtype)


def _swa_kernel(sink_ref, q_ref, kc_ref, kp_ref, vc_ref, vp_ref, tab_ref, *refs, nsub, ncast):
    wi_refs, o_ref, wo_refs = refs[:ncast], refs[ncast], refs[ncast + 1:]
    g = pl.program_id(0)
    n = pl.program_id(1)
    W = SWA_WINDOW
    _cast_along(wi_refs, wo_refs)

    first_prev = lax.broadcasted_iota(jnp.int32, (W, 2 * W), 1) < W
    for sub in range(nsub):
        if sub == 0:
            kp, vp = kp_ref[...], vp_ref[...]
        else:
            kp, vp = kc_ref[(sub - 1) * W:sub * W, :], vc_ref[(sub - 1) * W:sub * W, :]
        kk = jnp.concatenate([kp, kc_ref[sub * W:(sub + 1) * W, :]], axis=0)
        vv = jnp.concatenate([vp, vc_ref[sub * W:(sub + 1) * W, :]], axis=0)
        q = jnp.concatenate([q_ref[sub * W:(sub + 1) * W, hh * HEAD_DIM:(hh + 1) * HEAD_DIM]
                             for hh in range(SWA_GROUP)], axis=0)
        qk = _dot_nt(q, kk)
        ps, ls = [], []
        for hh in range(SWA_GROUP):
            s = qk[hh * W:(hh + 1) * W, :] + tab_ref[hh]
            if sub == 0:
                s = jnp.where(jnp.logical_and(first_prev, n == 0), -jnp.inf, s)
            sink = sink_ref[0, g * SWA_GROUP + hh]
            m = jnp.maximum(jnp.max(s, axis=1, keepdims=True), sink)
            p = jnp.exp(s - m)
            ls.append(jnp.sum(p, axis=1, keepdims=True) + jnp.exp(sink - m))
            ps.append(p.astype(BF16))
        o = _dot(jnp.concatenate(ps, axis=0), vv)
        for hh in range(SWA_GROUP):
            oh = o[hh * W:(hh + 1) * W, :] / ls[hh]
            o_ref[sub * W:(sub + 1) * W, hh * HEAD_DIM:(hh + 1) * HEAD_DIM] = oh.astype(o_ref.dtype)


def swa_attention(proj, table, sinks, w_cast, layer, *, tb):
    T = proj.shape[0]
    W = SWA_WINDOW
    nsub = tb // W
    nt = T // tb
    gw = SWA_GROUP * HEAD_DIM
    qb, kb, vb = C_SQ // gw, C_SK // HEAD_DIM, C_SV // HEAD_DIM
    prev = lambda n: jnp.maximum(n * nsub - 1, 0)
    w_in_specs, w_shapes, w_out_specs = _cast_slabs(w_cast, layer, SWA_KV_HEADS * nt, lambda g, n: g * nt + n)
    o, *w_bf16 = pl.pallas_call(
        functools.partial(_swa_kernel, nsub=nsub, ncast=len(w_cast)),
        out_shape=[jax.ShapeDtypeStruct((T, SWA_HEADS * HEAD_DIM), BF16), *w_shapes],
        grid=(SWA_KV_HEADS, nt),
        in_specs=[
            pl.BlockSpec(memory_space=pltpu.SMEM),
            pl.BlockSpec((tb, gw), lambda g, n: (n, qb + g)),
            pl.BlockSpec((tb, HEAD_DIM), lambda g, n: (n, kb + g)),
            pl.BlockSpec((W, HEAD_DIM), lambda g, n: (prev(n), kb + g)),
            pl.BlockSpec((tb, HEAD_DIM), lambda g, n: (n, vb + g)),
            pl.BlockSpec((W, HEAD_DIM), lambda g, n: (prev(n), vb + g)),
            pl.BlockSpec((SWA_GROUP, W, 2 * W), lambda g, n: (g, 0, 0)),
            *w_in_specs,
        ],
        out_specs=[pl.BlockSpec((tb, gw), lambda g, n: (n, g)), *w_out_specs],
        compiler_params=_cparams(("arbitrary", "arbitrary")),
        name="swa_attn",
    )(sinks.reshape(1, SWA_HEADS), proj, proj, proj, proj, proj, table, *w_cast)
    return o, w_bf16


def _gla_kernel(q_ref, k_ref, v_ref, r_ref, ps_ref, wg_ref, bg_ref, gn_ref, wi_ref, o_ref, wo_ref,
                st_ref, oc_ref, *, tb, rb):
    C = GLA_CHUNK
    _cast_along([wi_ref], [wo_ref])

    @pl.when(pl.program_id(0) == 0)
    def _():
        st_ref[...] = jnp.zeros_like(st_ref)

    r = lax.broadcasted_iota(jnp.int32, (rb, rb), 0)
    c = lax.broadcasted_iota(jnp.int32, (rb, rb), 1)
    same = (r // C) == (c // C)
    tril = jnp.where(jnp.logical_and(same, c <= r), 1.0, 0.0).astype(BF16)
    whole = jnp.where(same, 1.0, 0.0).astype(BF16)
    lane = lax.broadcasted_iota(jnp.int32, (1, LANES), 1)
    causal = (lax.broadcasted_iota(jnp.int32, (C, C), 1) <= lax.broadcasted_iota(jnp.int32, (C, C), 0))

    def prepare(blk):
        glr = ps_ref[blk, :].astype(BF16)
        g = jax.nn.log_sigmoid(_dot(glr, wg_ref[...]) + bg_ref[...]) / GLA_TAU
        gh, gm, gl = _split3(g)
        b = _dot(tril, gh) + _dot(tril, gm) + _dot(tril, gl)
        b_last = _dot(whole, gh) + _dot(whole, gm) + _dot(whole, gl)
        q_t = (q_ref[blk, :].astype(F32) * jnp.exp(b)).astype(BF16)
        kf = k_ref[blk, :].astype(F32)
        k_t = kf * jnp.exp(-b)
        k_end = kf * jnp.exp(b_last - b)
        decay = jnp.exp(b_last)
        units = []
        for n in range(rb // C):
            rows = slice(n * C, (n + 1) * C)
            per_head = []
            for h in range(GLA_HEADS):
                pair = slice((h // 2) * LANES, (h // 2 + 1) * LANES)
                mine = (lane // GLA_DK) == (h % 2)
                ktm = jnp.where(mine, k_t[rows, pair], 0.0).astype(BF16)
                kem = jnp.where(mine, k_end[rows, pair], 0.0).astype(BF16)
                qh = q_t[rows, pair]
                vh = v_ref[blk, h * GLA_DV:(h + 1) * GLA_DV][rows]
                intra = _dot(jnp.where(causal, _dot_nt(qh, ktm), 0.0).astype(BF16), vh)
                per_head.append((qh, intra, _dot_tn(vh, kem), decay[n * C:n * C + 1, pair]))
            units.append(per_head)
        return units

    blocks = [slice(s, s + rb) for s in range(0, tb, rb)]
    prepared = [prepare(blk) for blk in blocks]
    sts = [st_ref[h] for h in range(GLA_HEADS)]
    for blk, units in zip(blocks, prepared):
        for n, per_head in enumerate(units):
            rows = slice(blk.start + n * C, blk.start + (n + 1) * C)
            for h, (qh, intra, kv_t, dec) in enumerate(per_head):
                oc_ref[rows, h * GLA_DV:(h + 1) * GLA_DV] = intra + _dot_nt(qh, sts[h].astype(BF16))
                sts[h] = dec * sts[h] + kv_t
    for h in range(GLA_HEADS):
        st_ref[h] = sts[h]
    for h in range(GLA_HEADS):
        cols = slice(h * GLA_DV, (h + 1) * GLA_DV)
        o = _rms(oc_ref[:, cols], gn_ref[...])
        o_ref[:, cols] = (o * jax.nn.silu(r_ref[:, cols].astype(F32))).astype(o_ref.dtype)


def gla(proj, ps, wg_pad, bg, gnorm, w_cast, layer, *, tb, rb):
    T = proj.shape[0]
    qw = GLA_HEADS * GLA_DK
    vw = GLA_HEADS * GLA_DV
    (w_in_spec,), (w_shape,), (w_out_spec,) = _cast_slabs([w_cast], layer, T // tb, lambda n: n)
    return pl.pallas_call(
        functools.partial(_gla_kernel, tb=tb, rb=rb),
        out_shape=[jax.ShapeDtypeStruct((T, vw), BF16), w_shape],
        grid=(T // tb,),
        in_specs=[
            pl.BlockSpec((tb, qw), lambda n: (n, C_GQ // qw)),
            pl.BlockSpec((tb, qw), lambda n: (n, C_GK // qw)),
            pl.BlockSpec((tb, vw), lambda n: (n, C_GV // vw)),
            pl.BlockSpec((tb, vw), lambda n: (n, C_GR // vw)),
            pl.BlockSpec((tb, LANES), lambda n: (n, 0)),
            pl.BlockSpec((LANES, qw), lambda n: (0, 0)),
            pl.BlockSpec((1, qw), lambda n: (0, 0)),
            pl.BlockSpec((1, GLA_DV), lambda n: (0, 0)),
            w_in_spec,
        ],
        out_specs=[pl.BlockSpec((tb, vw), lambda n: (n, 0)), w_out_spec],
        scratch_shapes=[pltpu.VMEM((GLA_HEADS, GLA_DV, LANES), F32),
                        pltpu.VMEM((tb, vw), F32)],
        compiler_params=_cparams(("arbitrary",)),
        name="gla",
    )(proj, proj, proj, proj, ps, wg_pad, bg, gnorm, w_cast)


def _mix_xattn_kernel(h_ref, of_ref, os_ref, og_ref, wo_ref, gx_ref, wq_ref, kv_ref, wx_ref, o_ref):
    mixed = jnp.concatenate([of_ref[...], os_ref[...], og_ref[...]], axis=1)
    h1 = h_ref[...] + _dot(mixed, wo_ref[...])
    xn = _rms(h1, gx_ref[...]).astype(BF16)
    q = (_dot(xn, wq_ref[...]) * (HEAD_DIM ** -0.5)).astype(BF16)
    xw = XATTN_HEADS * HEAD_DIM
    outs = []
    for hh in range(XATTN_HEADS):
        cols = slice(hh * HEAD_DIM, (hh + 1) * HEAD_DIM)
        k = kv_ref[:, cols]
        v = kv_ref[:, xw + hh * HEAD_DIM:xw + (hh + 1) * HEAD_DIM]
        s = _dot_nt(q[:, cols], k)
        p = jnp.exp(s - jnp.max(s, axis=1, keepdims=True))
        l = jnp.sum(p, axis=1, keepdims=True)
        outs.append((_dot(p.astype(BF16), v) / l).astype(BF16))
    o = jnp.concatenate(outs, axis=1)
    o_ref[...] = h1 + _dot(o, wx_ref[...])


def mix_xattn(h, o_fox, o_swa, o_gla, w_out, gx, wq, kv, wo, *, tm):
    T, D = h.shape
    const = lambda shape: pl.BlockSpec(shape, lambda i: (0, 0))
    row = lambda w: pl.BlockSpec((tm, w), lambda i: (i, 0))
    return pl.pallas_call(
        _mix_xattn_kernel,
        out_shape=jax.ShapeDtypeStruct((T, D), F32),
        grid=(T // tm,),
        in_specs=[row(D), row(o_fox.shape[1]), row(o_swa.shape[1]), row(o_gla.shape[1]),
                  const(w_out.shape), const((1, D)), const(wq.shape), const(kv.shape), const(wo.shape)],
        out_specs=row(D),
        compiler_params=_cparams(("arbitrary",)),
        name="mix_xattn",
    )(h, o_fox, o_swa, o_gla, w_out, gx.reshape(1, D), wq, kv, wo)


def _ffn_kernel(x_ref, g_ref, wg_ref, wv_ref, cwg_ref, cwv_ref, cbg_ref, cbv_ref, wd_ref, fg_ref,
                o_ref, xn_ref, hg_ref, hv_ref, ug_ref, uv_ref, *, tm, sub, final):
    i = pl.program_id(0)
    f = pl.program_id(1)
    H = SUBLANES
    fc = wd_ref.shape[0]

    @pl.when(f == 0)
    def _():
        x = x_ref[...]
        xn_ref[...] = _rms(x, g_ref[...]).astype(BF16)
        o_ref[...] = x

    @pl.when(i == 0)
    def _():
        hg_ref[f] = jnp.zeros(hg_ref.shape[1:], F32)
        hv_ref[f] = jnp.zeros(hv_ref.shape[1:], F32)

    xn = xn_ref[...]

    def up(c, slot):
        ug_ref[slot, H:, :] = _dot(xn, wg_ref[:, c:c + sub])
        uv_ref[slot, H:, :] = _dot(xn, wv_ref[:, c:c + sub])

    def conv(u_ref, slot, h_ref, cw_ref, cb_ref, c):
        u_ref[slot, 0:H, :] = h_ref[f, :, c:c + sub]
        h_ref[f, :, c:c + sub] = u_ref[slot, tm:tm + H, :]
        return (((cw_ref[0:1, c:c + sub] * u_ref[slot, H - 2:H - 2 + tm, :]
                  + cw_ref[1:2, c:c + sub] * u_ref[slot, H - 1:H - 1 + tm, :])
                 + cw_ref[2:3, c:c + sub] * u_ref[slot, H:H + tm, :]) + cb_ref[:, c:c + sub])

    def down(c, slot):
        gate = conv(ug_ref, slot, hg_ref, cwg_ref, cbg_ref, c)
        val = conv(uv_ref, slot, hv_ref, cwv_ref, cbv_ref, c)
        act = (jax.nn.silu(gate) * val).astype(BF16)
        o_ref[...] += _dot(act, wd_ref[c:c + sub, :])

    up(0, 0)
    for n, c in enumerate(range(0, fc, sub)):
        if c + sub < fc:
            up(c + sub, (n + 1) % 2)
        down(c, n % 2)

    if final:
        @pl.when(f == pl.num_programs(1) - 1)
        def _():
            o_ref[...] = _rms(o_ref[...], fg_ref[...])


def conv_ffn(x, g, w_up, conv_w, conv_b, w_down, layer, final_g, *, tm, fc, final):
    T, D = x.shape
    dff = w_down.shape[0]
    L = conv_w.shape[0]
    nf = dff // fc
    cb = conv_b.reshape(L, 1, 2 * dff)
    return pl.pallas_call(
        functools.partial(_ffn_kernel, tm=tm, sub=MXU_COLS, final=final),
        out_shape=jax.ShapeDtypeStruct((T, D), F32),
        grid=(T // tm, nf),
        in_specs=[
            pl.BlockSpec((tm, D), lambda i, f: (i, 0)),
            pl.BlockSpec((1, D), lambda i, f: (0, 0)),
            pl.BlockSpec((D, fc), lambda i, f: (0, f)),
            pl.BlockSpec((D, fc), lambda i, f: (0, nf + f)),
            pl.BlockSpec((None, CONV_WIDTH, fc), lambda i, f: (layer, 0, f)),
            pl.BlockSpec((None, CONV_WIDTH, fc), lambda i, f: (layer, 0, nf + f)),
            pl.BlockSpec((None, 1, fc), lambda i, f: (layer, 0, f)),
            pl.BlockSpec((None, 1, fc), lambda i, f: (layer, 0, nf + f)),
            pl.BlockSpec((fc, D), lambda i, f: (f, 0)),
            pl.BlockSpec((1, D), lambda i, f: (0, 0)),
        ],
        out_specs=pl.BlockSpec((tm, D), lambda i, f: (i, 0)),
        scratch_shapes=[pltpu.VMEM((tm, D), BF16), pltpu.VMEM((nf, SUBLANES, fc), F32),
                        pltpu.VMEM((nf, SUBLANES, fc), F32),
                        pltpu.VMEM((2, tm + SUBLANES, MXU_COLS), F32),
                        pltpu.VMEM((2, tm + SUBLANES, MXU_COLS), F32)],
        compiler_params=_cparams(("arbitrary", "arbitrary")),
        name="conv_ffn",
    )(x, g.reshape(1, D), w_up, w_up, conv_w, conv_w, cb, cb, w_down, final_g.reshape(1, D))


TR_PREP, TC_PREP = 512, 512
TM_PROJ, TN_PROJ = 1024, 2304
TB_GATE = 512
TQ_FOX, TK_FOX, NH_FOX = 512, 512, 4
TB_SWA = 1024
TB_GLA, RB_GLA = 1024, 256
TM_MIX = 512
TM_FFN, FC_FFN = 512, 512


def _main_colscale():
    cs = np.ones((1, MAIN_COLS), np.float32)
    cs[0, C_FQ:C_FK] = HEAD_DIM ** -0.5 * LOG2E
    cs[0, C_SQ:C_SK] = HEAD_DIM ** -0.5
    cs[0, C_GQ:C_GK] = GLA_DK ** -0.5
    return jnp.asarray(cs)


def kernel(x, mem, w_in, b_fox_f, swa_sinks, t5_bias, w_gla_gate, b_gla_gate, gla_norm, w_mix_out, norm_mix,
           norm_xattn, norm_mem, wq_x, wkv_x, wo_x, norm_ffn, w_up, conv_w, conv_b, w_down, final_norm):
    depth = w_in.shape[0]
    _, T, D = x.shape
    M = mem.shape[1]
    h = x.reshape(T, D)
    memf = mem.reshape(M, D)
    colscale = _main_colscale()
    ones_kv = jnp.ones((1, wkv_x.shape[2]), F32)
    table = t5_table(t5_bias)
    pad_lanes = LANES - S_GLR - GLA_GATE_RANK
    w_main, w_small = prep_w_in(w_in, tr=TR_PREP, tc=TC_PREP)
    for l in range(depth):
        bvec = jnp.concatenate([b_fox_f[l], jnp.zeros((LANES - FOX_HEADS,), F32)]).reshape(1, LANES)
        wg_pad = jnp.concatenate([jnp.zeros((S_GLR, GLA_HEADS * GLA_DK), F32), w_gla_gate[l],
                                  jnp.zeros((pad_lanes, GLA_HEADS * GLA_DK), F32)], axis=0).astype(BF16)

        proj, ps = norm_proj(h, norm_mix[l], w_main, l, colscale, w_small, tm=TM_PROJ, tn=TN_PROJ, w_t=True)
        kx, row = fox_gate(ps, bvec, tb=TB_GATE)
        o_fox, w_down_b = fox_attention(proj, kx, row, w_down, l, tq=TQ_FOX, tk=TK_FOX, nh=NH_FOX)
        o_swa, (w_out_b, wq_b, wkv_b, wo_b) = swa_attention(proj, table, swa_sinks[l],
                                                           (w_mix_out, wq_x, wkv_x, wo_x), l, tb=TB_SWA)
        o_gla, w_up_b = gla(proj, ps, wg_pad, b_gla_gate[l].reshape(1, -1), gla_norm[l].reshape(1, -1),
                            w_up, l, tb=TB_GLA, rb=RB_GLA)
        kv = norm_proj(memf, norm_mem[l], wkv_b[None], 0, ones_kv, tm=M, tn=wkv_x.shape[2])
        h = mix_xattn(h, o_fox, o_swa, o_gla, w_out_b, norm_xattn[l], wq_b, kv, wo_b, tm=TM_MIX)
        h = conv_ffn(h, norm_ffn[l], w_up_b, conv_w, conv_b, w_down_b, l, final_norm,
                     tm=TM_FFN, fc=FC_FFN, final=(l == depth - 1))
    return h.reshape(x.shape)
```

```python
import functools
import math

import numpy as np
import jax
import jax.numpy as jnp
from jax import lax
from jax.experimental import pallas as pl
from jax.experimental.pallas import tpu as pltpu

F32 = jnp.float32
BF16 = jnp.bfloat16

HEAD_DIM = 128
FOX_HEADS = 4
SWA_HEADS = 8
SWA_KV_HEADS = 2
SWA_GROUP = SWA_HEADS // SWA_KV_HEADS
GLA_HEADS = 4
GLA_DK = 64
GLA_DV = 128
GLA_GATE_RANK = 16
GLA_TAU = 16.0
GLA_CHUNK = 64
SWA_WINDOW = 128
NUM_BUCKETS = 32
T5_MAX_DISTANCE = 128
XATTN_HEADS = 4
CONV_WIDTH = 3
EPS = 1e-6
LOG2E = math.log2(math.e)

LANES = 128
SUBLANES = 8
BF16_ROWS = 16
MXU_COLS = 256
VMEM_LIMIT = 56 * 1024 * 1024

_FOX_W = FOX_HEADS * HEAD_DIM
_OFF_FF = 3 * _FOX_W
_OFF_SQ = _OFF_FF + FOX_HEADS
_MAIN_B = (SWA_HEADS + 2 * SWA_KV_HEADS) * HEAD_DIM + 2 * GLA_HEADS * GLA_DK + 2 * GLA_HEADS * GLA_DV
_OFF_GLR = _OFF_SQ + _MAIN_B
C_FQ, C_FK, C_FV = 0, 512, 1024
C_SQ, C_SK, C_SV = 1536, 2560, 2816
C_GQ, C_GK, C_GV, C_GR = 3072, 3328, 3584, 4096
MAIN_COLS = 4608
S_FF, S_GLR = 0, SUBLANES


def _cparams(sem):
    return pltpu.CompilerParams(dimension_semantics=sem, vmem_limit_bytes=VMEM_LIMIT)


def _rms(x, g):
    return x * lax.rsqrt(jnp.mean(x * x, axis=-1, keepdims=True) + EPS) * g


N_SPLIT = 3


def _split3(x):
    hi = x.astype(BF16)
    r1 = x - hi.astype(F32)
    mid = r1.astype(BF16)
    lo = (r1 - mid.astype(F32)).astype(BF16)
    return hi, mid, lo


def _dot(a, b):
    return jnp.dot(a, b, preferred_element_type=F32)


def _dot_nt(a, b):
    return lax.dot_general(a, b, (((1,), (1,)), ((), ())), preferred_element_type=F32)


def _dot_tn(a, b):
    return lax.dot_general(a, b, (((0,), (0,)), ((), ())), preferred_element_type=F32)


def _sel_dot(sel, x):
    hi, mid, lo = _split3(x)
    return _dot(sel, hi) + _dot(sel, mid) + _dot(sel, lo)


def _prep_w_in_kernel(cur_ref, ff_ref, glr_ref, wm_ref, ws_ref):
    L = wm_ref.shape[0]
    wm_ref[...] = pltpu.einshape("cld->lcd", cur_ref[...]).astype(BF16)

    @pl.when(pl.program_id(1) == 0)
    def _():
        pad = jnp.zeros((LANES - S_GLR - GLA_GATE_RANK, ws_ref.shape[2]), F32)
        for l in range(L):
            ws_ref[l] = jnp.concatenate([ff_ref[:, l, :], glr_ref[:, l, :], pad], axis=0).astype(BF16)


def prep_w_in(w_in, *, tr, tc):
    L, D, _ = w_in.shape
    wt = jnp.transpose(w_in, (2, 0, 1))
    assert _OFF_FF % tc == 0 and MAIN_COLS % tc == 0 and S_GLR % SUBLANES == 0

    def src(j):
        return j * tc + jnp.where(j * tc >= _OFF_FF, _OFF_SQ - _OFF_FF, 0)

    return pl.pallas_call(
        _prep_w_in_kernel,
        out_shape=[jax.ShapeDtypeStruct((L, MAIN_COLS, D), BF16),
                   jax.ShapeDtypeStruct((L, LANES, D), BF16)],
        grid=(D // tr, MAIN_COLS // tc),
        in_specs=[
            pl.BlockSpec((pl.Element(tc), pl.Element(L), pl.Element(tr)), lambda r, j: (src(j), 0, r * tr)),
            pl.BlockSpec((pl.Element(S_GLR), pl.Element(L), pl.Element(tr)), lambda r, j: (_OFF_FF, 0, r * tr)),
            pl.BlockSpec((pl.Element(GLA_GATE_RANK), pl.Element(L), pl.Element(tr)),
                         lambda r, j: (_OFF_GLR, 0, r * tr)),
        ],
        out_specs=[pl.BlockSpec((L, tc, tr), lambda r, j: (0, j, r)),
                   pl.BlockSpec((L, LANES, tr), lambda r, j: (0, 0, r))],
        compiler_params=_cparams(("arbitrary", "arbitrary")),
        name="prep_w_in",
    )(wt, wt, wt)


def _norm_proj_kernel(*refs, has_small, w_t):
    if has_small:
        x_ref, g_ref, w_ref, cs_ref, ws_ref, o_ref, os_ref, xn_ref = refs
    else:
        x_ref, g_ref, w_ref, cs_ref, o_ref, xn_ref = refs
    dot = _dot_nt if w_t else _dot

    @pl.when(pl.program_id(1) == 0)
    def _():
        xn = _rms(x_ref[...], g_ref[...]).astype(BF16)
        xn_ref[...] = xn
        if has_small:
            os_ref[...] = dot(xn, ws_ref[...])

    o_ref[...] = (dot(xn_ref[...], w_ref[...]) * cs_ref[...]).astype(o_ref.dtype)


def norm_proj(x, g, w, layer, colscale, w_small=None, *, tm, tn, w_t=False):
    T, D = x.shape
    N = w.shape[1] if w_t else w.shape[2]
    has_small = w_small is not None

    def wspec(n):
        if w_t:
            return pl.BlockSpec((None, n, D), lambda i, j: (layer, j, 0))
        return pl.BlockSpec((None, D, n), lambda i, j: (layer, 0, j))

    in_specs = [
        pl.BlockSpec((tm, D), lambda i, j: (i, 0)),
        pl.BlockSpec((1, D), lambda i, j: (0, 0)),
        wspec(tn),
        pl.BlockSpec((1, tn), lambda i, j: (0, j)),
    ]
    args = [x, g.reshape(1, D), w, colscale]
    out_shape = [jax.ShapeDtypeStruct((T, N), BF16)]
    out_specs = [pl.BlockSpec((tm, tn), lambda i, j: (i, j))]
    if has_small:
        in_specs.append(pl.BlockSpec((None,) + w_small.shape[1:], lambda i, j: (layer, 0, 0)))
        args.append(w_small)
        out_shape.append(jax.ShapeDtypeStruct((T, LANES), F32))
        out_specs.append(pl.BlockSpec((tm, LANES), lambda i, j: (i, 0)))
    outs = pl.pallas_call(
        functools.partial(_norm_proj_kernel, has_small=has_small, w_t=w_t),
        out_shape=out_shape,
        grid=(T // tm, N // tn),
        in_specs=in_specs,
        out_specs=out_specs,
        scratch_shapes=[pltpu.VMEM((tm, D), BF16)],
        compiler_params=_cparams(("arbitrary", "arbitrary")),
        name="norm_proj_small" if has_small else "norm_proj",
    )(*args)
    return outs if has_small else outs[0]


def _fox_gate_kernel(ps_ref, b_ref, kx_ref, row_ref, carry_ref, *, tb):
    @pl.when(pl.program_id(0) == 0)
    def _():
        carry_ref[...] = jnp.zeros_like(carry_ref)

    x = jax.nn.log_sigmoid(ps_ref[...] + b_ref[...])
    r = lax.broadcasted_iota(jnp.int32, (tb, tb), 0)
    c = lax.broadcasted_iota(jnp.int32, (tb, tb), 1)
    tril = jnp.where(c <= r, 1.0, 0.0).astype(BF16)
    csum = _sel_dot(tril, x) + carry_ref[0:1, :]
    carry_ref[...] = jnp.broadcast_to(csum[tb - 1:tb, :], carry_ref.shape)
    pieces = _split3(csum * LOG2E)
    neg = [-p for p in pieces]
    src = lax.broadcasted_iota(jnp.int32, (LANES, FOX_HEADS * LANES), 0)
    dst = lax.broadcasted_iota(jnp.int32, (LANES, FOX_HEADS * LANES), 1)
    kx = sum(_dot(neg[n], jnp.where((dst // LANES == src - S_FF) & (dst % LANES == n), 1.0, 0.0).astype(BF16))
             for n in range(len(neg))).astype(BF16)
    sub = lax.broadcasted_iota(jnp.int32, (FOX_HEADS * SUBLANES, LANES), 0)
    lane = lax.broadcasted_iota(jnp.int32, (FOX_HEADS * SUBLANES, LANES), 1)
    pick = jnp.where(lane - S_FF == sub // SUBLANES, 1.0, 0.0).astype(BF16)
    rows = sum(_dot_nt(pick, p) for p in pieces)
    for h in range(FOX_HEADS):
        kx_ref[h] = kx[:, h * LANES:(h + 1) * LANES]
        row_ref[h, 0] = rows[h * SUBLANES:(h + 1) * SUBLANES, :]


def fox_gate(ps, bvec, *, tb):
    T = ps.shape[0]
    nb = T // tb
    return pl.pallas_call(
        functools.partial(_fox_gate_kernel, tb=tb),
        out_shape=[jax.ShapeDtypeStruct((FOX_HEADS, T, LANES), BF16),
                   jax.ShapeDtypeStruct((FOX_HEADS, nb, SUBLANES, tb), F32)],
        grid=(nb,),
        in_specs=[pl.BlockSpec((tb, LANES), lambda i: (i, 0)),
                  pl.BlockSpec((1, LANES), lambda i: (0, 0))],
        out_specs=[pl.BlockSpec((FOX_HEADS, tb, LANES), lambda i: (0, i, 0)),
                   pl.BlockSpec((FOX_HEADS, 1, SUBLANES, tb), lambda i: (0, i, 0, 0))],
        scratch_shapes=[pltpu.VMEM((SUBLANES, LANES), F32)],
        compiler_params=_cparams(("arbitrary",)),
        name="fox_gate",
    )(ps, bvec)


def _fox_attn_kernel(q_ref, k_ref, v_ref, kx_ref, cq_ref, wi_ref, o_ref, wo_ref,
                     vt_ref, ya_ref, yb_ref, xa_ref, xb_ref, m_ref, acc_ref, *, tq, tk, nk, nh):
    i = pl.program_id(1)
    D = HEAD_DIM
    heads = range(nh)
    n_full = i // (tk // tq)
    _cast_along([wi_ref], [wo_ref])

    @pl.when(i == 0)
    def _():
        for hh in heads:
            for n in range(nk):
                vt_ref[hh, n, 0:D, :] = v_ref[n * tk:(n + 1) * tk, hh * D:(hh + 1) * D].T
                vt_ref[hh, n, D:, :] = jnp.ones((BF16_ROWS, tk), BF16)

    lane = lax.broadcasted_iota(jnp.int32, (tq, LANES), 1)
    ones = jnp.where(lane < N_SPLIT, 1.0, 0.0).astype(BF16)
    q_aug = [jnp.concatenate([q_ref[:, hh * D:(hh + 1) * D], ones], axis=1) for hh in heads]
    cq = [cq_ref[hh, 0, 0:1, :] for hh in heads]
    m_ref[...] = jnp.full(m_ref.shape, -jnp.inf, F32)
    acc_ref[...] = jnp.zeros(acc_ref.shape, F32)

    def scores(hh, j):
        start = pl.multiple_of(j * tk, tk)
        k_aug = jnp.concatenate([k_ref[pl.ds(start, tk), hh * D:(hh + 1) * D], kx_ref[hh, pl.ds(start, tk), :]],
                                axis=1)
        return _dot_nt(k_aug, q_aug[hh])

    def update(hh, j, y, ymax):
        m_old = m_ref[hh]
        m_new = jnp.maximum(m_old, ymax + cq[hh])
        alpha = jnp.exp2(m_old - m_new)
        p = jnp.exp2(y + (cq[hh] - m_new))
        acc_ref[hh] = alpha * acc_ref[hh] + _dot(vt_ref[hh, j], p.astype(BF16))
        m_ref[hh] = m_new

    def produce(hh, j, y_ref, ymax_ref):
        y = scores(hh, j)
        y_ref[hh] = y
        ymax_ref[hh] = jnp.max(y, axis=0, keepdims=True)

    for hh in heads:
        produce(hh, 0, ya_ref, xa_ref)

    def step(j, cur_ref, cmax_ref, nxt_ref, nmax_ref):
        for hh in heads:
            produce(hh, j + 1, nxt_ref, nmax_ref)
            update(hh, j, cur_ref[hh], cmax_ref[hh])

    def body(j, carry):
        pl.when(j % 2 == 0)(lambda: step(j, ya_ref, xa_ref, yb_ref, xb_ref))
        pl.when(j % 2 == 1)(lambda: step(j, yb_ref, xb_ref, ya_ref, xa_ref))
        return carry

    lax.fori_loop(0, n_full, body, 0)
    key = n_full * tk + lax.broadcasted_iota(jnp.int32, (tk, tq), 0)
    qry = i * tq + lax.broadcasted_iota(jnp.int32, (tk, tq), 1)

    def diagonal(cur_ref):
        for hh in heads:
            y = jnp.where(key <= qry, cur_ref[hh], -jnp.inf)
            update(hh, n_full, y, jnp.max(y, axis=0, keepdims=True))
            o_ref[:, hh * D:(hh + 1) * D] = (acc_ref[hh, 0:D, :] / acc_ref[hh, D:D + 1, :]).T.astype(o_ref.dtype)

    pl.when(n_full % 2 == 0)(lambda: diagonal(ya_ref))
    pl.when(n_full % 2 == 1)(lambda: diagonal(yb_ref))


def fox_attention(proj, kx, row, w_cast, layer, *, tq, tk, nh):
    T = proj.shape[0]
    nq = T // tq
    ng = FOX_HEADS // nh
    gw = nh * HEAD_DIM
    qb, kb, vb = C_FQ // gw, C_FK // gw, C_FV // gw
    assert tk % tq == 0 and T % tk == 0
    (w_in_spec,), (w_shape,), (w_out_spec,) = _cast_slabs([w_cast], layer, ng * nq, lambda g, i: g * nq + i)
    return pl.pallas_call(
        functools.partial(_fox_attn_kernel, tq=tq, tk=tk, nk=T // tk, nh=nh),
        out_shape=[jax.ShapeDtypeStruct((T, FOX_HEADS * HEAD_DIM), BF16), w_shape],
        grid=(ng, nq),
        in_specs=[
            pl.BlockSpec((tq, gw), lambda g, i: (i, qb + g)),
            pl.BlockSpec((T, gw), lambda g, i: (0, kb + g), pipeline_mode=pl.Buffered(1)),
            pl.BlockSpec((T, gw), lambda g, i: (0, vb + g), pipeline_mode=pl.Buffered(1)),
            pl.BlockSpec((nh, T, LANES), lambda g, i: (g, 0, 0), pipeline_mode=pl.Buffered(1)),
            pl.BlockSpec((nh, 1, SUBLANES, tq), lambda g, i: (g, i, 0, 0)),
            w_in_spec,
        ],
        out_specs=[pl.BlockSpec((tq, gw), lambda g, i: (i, g)), w_out_spec],
        scratch_shapes=[pltpu.VMEM((nh, T // tk, HEAD_DIM + BF16_ROWS, tk), BF16),
                        pltpu.VMEM((nh, tk, tq), F32), pltpu.VMEM((nh, tk, tq), F32),
                        pltpu.VMEM((nh, 1, tq), F32), pltpu.VMEM((nh, 1, tq), F32),
                        pltpu.VMEM((nh, 1, tq), F32),
                        pltpu.VMEM((nh, HEAD_DIM + BF16_ROWS, tq), F32)],
        compiler_params=_cparams(("arbitrary", "arbitrary")),
        name="fox_attn",
    )(proj, proj, proj, kx, row, w_cast)


def _t5_table_kernel(t5_ref, o_ref):
    h = pl.program_id(0)
    W = SWA_WINDOW
    i = lax.broadcasted_iota(jnp.int32, (W, 2 * W), 0)
    j = lax.broadcasted_iota(jnp.int32, (W, 2 * W), 1)
    rel = (W + i) - j
    n = jnp.maximum(rel, 0)
    max_exact = NUM_BUCKETS // 2
    nf = jnp.maximum(n, 1).astype(F32)
    large = max_exact + (jnp.log(nf / max_exact) / math.log(T5_MAX_DISTANCE / max_exact)
                         * (NUM_BUCKETS - max_exact)).astype(jnp.int32)
    large = jnp.minimum(large, NUM_BUCKETS - 1)
    bucket = jnp.where(n < max_exact, n, large)
    bias = jnp.zeros((W, 2 * W), F32)
    for b in range(NUM_BUCKETS):
        bias = jnp.where(bucket == b, t5_ref[b, h], bias)
    in_window = (rel >= 0) & (rel < W)
    o_ref[...] = jnp.where(in_window, bias, -jnp.inf)


def t5_table(t5_bias):
    W = SWA_WINDOW
    return pl.pallas_call(
        _t5_table_kernel,
        out_shape=jax.ShapeDtypeStruct((SWA_HEADS, W, 2 * W), F32),
        grid=(SWA_HEADS,),
        in_specs=[pl.BlockSpec(memory_space=pltpu.SMEM)],
        out_specs=pl.BlockSpec((None, W, 2 * W), lambda h: (h, 0, 0)),
        compiler_params=_cparams(("arbitrary",)),
        name="t5_table",
    )(t5_bias)


def _cast_slabs(ws, layer, nsteps, step):
    in_specs, shapes, out_specs = [], [], []
    for w in ws:
        _, rows, cols = w.shape
        slab = rows // nsteps
        assert slab * nsteps == rows and slab % BF16_ROWS == 0
        in_specs.append(pl.BlockSpec((None, slab, cols), lambda *ids: (layer, step(*ids), 0)))
        out_specs.append(pl.BlockSpec((slab, cols), lambda *ids: (step(*ids), 0)))
        shapes.append(jax.ShapeDtypeStruct((rows, cols), BF16))
    return in_specs, shapes, out_specs


def _cast_along(wi_refs, wo_refs):
    for wi_ref, wo_ref in zip(wi_refs, wo_refs, strict=True):
        wo_ref[...] = wi_ref[...].astype(wo_ref.dtype)


def _swa_kernel(sink_ref, q_ref, kc_ref, kp_ref, vc_ref, vp_ref, tab_ref, *refs, nsub, ncast):
    wi_refs, o_ref, wo_refs = refs[:ncast], refs[ncast], refs[ncast + 1:]
    g = pl.program_id(0)
    n = pl.program_id(1)
    W = SWA_WINDOW
    _cast_along(wi_refs, wo_refs)

    first_prev = lax.broadcasted_iota(jnp.int32, (W, 2 * W), 1) < W
    for sub in range(nsub):
        if sub == 0:
            kp, vp = kp_ref[...], vp_ref[...]
        else:
            kp, vp = kc_ref[(sub - 1) * W:sub * W, :], vc_ref[(sub - 1) * W:sub * W, :]
        kk = jnp.concatenate([kp, kc_ref[sub * W:(sub + 1) * W, :]], axis=0)
        vv = jnp.concatenate([vp, vc_ref[sub * W:(sub + 1) * W, :]], axis=0)
        q = jnp.concatenate([q_ref[sub * W:(sub + 1) * W, hh * HEAD_DIM:(hh + 1) * HEAD_DIM]
                             for hh in range(SWA_GROUP)], axis=0)
        qk = _dot_nt(q, kk)
        ps, ls = [], []
        for hh in range(SWA_GROUP):
            s = qk[hh * W:(hh + 1) * W, :] + tab_ref[hh]
            if sub == 0:
                s = jnp.where(jnp.logical_and(first_prev, n == 0), -jnp.inf, s)
            sink = sink_ref[0, g * SWA_GROUP + hh]
            m = jnp.maximum(jnp.max(s, axis=1, keepdims=True), sink)
            p = jnp.exp(s - m)
            ls.append(jnp.sum(p, axis=1, keepdims=True) + jnp.exp(sink - m))
            ps.append(p.astype(BF16))
        o = _dot(jnp.concatenate(ps, axis=0), vv)
        for hh in range(SWA_GROUP):
            oh = o[hh * W:(hh + 1) * W, :] / ls[hh]
            o_ref[sub * W:(sub + 1) * W, hh * HEAD_DIM:(hh + 1) * HEAD_DIM] = oh.astype(o_ref.dtype)


def swa_attention(proj, table, sinks, w_cast, layer, *, tb):
    T = proj.shape[0]
    W = SWA_WINDOW
    nsub = tb // W
    nt = T // tb
    gw = SWA_GROUP * HEAD_DIM
    qb, kb, vb = C_SQ // gw, C_SK // HEAD_DIM, C_SV // HEAD_DIM
    prev = lambda n: jnp.maximum(n * nsub - 1, 0)
    w_in_specs, w_shapes, w_out_specs = _cast_slabs(w_cast, layer, SWA_KV_HEADS * nt, lambda g, n: g * nt + n)
    o, *w_bf16 = pl.pallas_call(
        functools.partial(_swa_kernel, nsub=nsub, ncast=len(w_cast)),
        out_shape=[jax.ShapeDtypeStruct((T, SWA_HEADS * HEAD_DIM), BF16), *w_shapes],
        grid=(SWA_KV_HEADS, nt),
        in_specs=[
            pl.BlockSpec(memory_space=pltpu.SMEM),
            pl.BlockSpec((tb, gw), lambda g, n: (n, qb + g)),
            pl.BlockSpec((tb, HEAD_DIM), lambda g, n: (n, kb + g)),
            pl.BlockSpec((W, HEAD_DIM), lambda g, n: (prev(n), kb + g)),
            pl.BlockSpec((tb, HEAD_DIM), lambda g, n: (n, vb + g)),
            pl.BlockSpec((W, HEAD_DIM), lambda g, n: (prev(n), vb + g)),
            pl.BlockSpec((SWA_GROUP, W, 2 * W), lambda g, n: (g, 0, 0)),
            *w_in_specs,
        ],
        out_specs=[pl.BlockSpec((tb, gw), lambda g, n: (n, g)), *w_out_specs],
        compiler_params=_cparams(("arbitrary", "arbitrary")),
        name="swa_attn",
    )(sinks.reshape(1, SWA_HEADS), proj, proj, proj, proj, proj, table, *w_cast)
    return o, w_bf16


def _gla_kernel(q_ref, k_ref, v_ref, r_ref, ps_ref, wg_ref, bg_ref, gn_ref, wi_ref, o_ref, wo_ref,
                st_ref, oc_ref, *, tb, rb):
    C = GLA_CHUNK
    _cast_along([wi_ref], [wo_ref])

    @pl.when(pl.program_id(0) == 0)
    def _():
        st_ref[...] = jnp.zeros_like(st_ref)

    r = lax.broadcasted_iota(jnp.int32, (rb, rb), 0)
    c = lax.broadcasted_iota(jnp.int32, (rb, rb), 1)
    same = (r // C) == (c // C)
    tril = jnp.where(jnp.logical_and(same, c <= r), 1.0, 0.0).astype(BF16)
    whole = jnp.where(same, 1.0, 0.0).astype(BF16)
    lane = lax.broadcasted_iota(jnp.int32, (1, LANES), 1)
    causal = (lax.broadcasted_iota(jnp.int32, (C, C), 1) <= lax.broadcasted_iota(jnp.int32, (C, C), 0))

    def prepare(blk):
        glr = ps_ref[blk, :].astype(BF16)
        g = jax.nn.log_sigmoid(_dot(glr, wg_ref[...]) + bg_ref[...]) / GLA_TAU
        gh, gm, gl = _split3(g)
        b = _dot(tril, gh) + _dot(tril, gm) + _dot(tril, gl)
        b_last = _dot(whole, gh) + _dot(whole, gm) + _dot(whole, gl)
        q_t = (q_ref[blk, :].astype(F32) * jnp.exp(b)).astype(BF16)
        kf = k_ref[blk, :].astype(F32)
        k_t = kf * jnp.exp(-b)
        k_end = kf * jnp.exp(b_last - b)
        decay = jnp.exp(b_last)
        units = []
        for n in range(rb // C):
            rows = slice(n * C, (n + 1) * C)
            per_head = []
            for h in range(GLA_HEADS):
                pair = slice((h // 2) * LANES, (h // 2 + 1) * LANES)
                mine = (lane // GLA_DK) == (h % 2)
                ktm = jnp.where(mine, k_t[rows, pair], 0.0).astype(BF16)
                kem = jnp.where(mine, k_end[rows, pair], 0.0).astype(BF16)
                qh = q_t[rows, pair]
                vh = v_ref[blk, h * GLA_DV:(h + 1) * GLA_DV][rows]
                intra = _dot(jnp.where(causal, _dot_nt(qh, ktm), 0.0).astype(BF16), vh)
                per_head.append((qh, intra, _dot_tn(vh, kem), decay[n * C:n * C + 1, pair]))
            units.append(per_head)
        return units

    blocks = [slice(s, s + rb) for s in range(0, tb, rb)]
    prepared = [prepare(blk) for blk in blocks]
    sts = [st_ref[h] for h in range(GLA_HEADS)]
    for blk, units in zip(blocks, prepared):
        for n, per_head in enumerate(units):
            rows = slice(blk.start + n * C, blk.start + (n + 1) * C)
            for h, (qh, intra, kv_t, dec) in enumerate(per_head):
                oc_ref[rows, h * GLA_DV:(h + 1) * GLA_DV] = intra + _dot_nt(qh, sts[h].astype(BF16))
                sts[h] = dec * sts[h] + kv_t
    for h in range(GLA_HEADS):
        st_ref[h] = sts[h]
    for h in range(GLA_HEADS):
        cols = slice(h * GLA_DV, (h + 1) * GLA_DV)
        o = _rms(oc_ref[:, cols], gn_ref[...])
        o_ref[:, cols] = (o * jax.nn.silu(r_ref[:, cols].astype(F32))).astype(o_ref.dtype)


def gla(proj, ps, wg_pad, bg, gnorm, w_cast, layer, *, tb, rb):
    T = proj.shape[0]
    qw = GLA_HEADS * GLA_DK
    vw = GLA_HEADS * GLA_DV
    (w_in_spec,), (w_shape,), (w_out_spec,) = _cast_slabs([w_cast], layer, T // tb, lambda n: n)
    return pl.pallas_call(
        functools.partial(_gla_kernel, tb=tb, rb=rb),
        out_shape=[jax.ShapeDtypeStruct((T, vw), BF16), w_shape],
        grid=(T // tb,),
        in_specs=[
            pl.BlockSpec((tb, qw), lambda n: (n, C_GQ // qw)),
            pl.BlockSpec((tb, qw), lambda n: (n, C_GK // qw)),
            pl.BlockSpec((tb, vw), lambda n: (n, C_GV // vw)),
            pl.BlockSpec((tb, vw), lambda n: (n, C_GR // vw)),
            pl.BlockSpec((tb, LANES), lambda n: (n, 0)),
            pl.BlockSpec((LANES, qw), lambda n: (0, 0)),
            pl.BlockSpec((1, qw), lambda n: (0, 0)),
            pl.BlockSpec((1, GLA_DV), lambda n: (0, 0)),
            w_in_spec,
        ],
        out_specs=[pl.BlockSpec((tb, vw), lambda n: (n, 0)), w_out_spec],
        scratch_shapes=[pltpu.VMEM((GLA_HEADS, GLA_DV, LANES), F32),
                        pltpu.VMEM((tb, vw), F32)],
        compiler_params=_cparams(("arbitrary",)),
        name="gla",
    )(proj, proj, proj, proj, ps, wg_pad, bg, gnorm, w_cast)


def _mix_xattn_kernel(h_ref, of_ref, os_ref, og_ref, wo_ref, gx_ref, wq_ref, kv_ref, wx_ref, o_ref):
    mixed = jnp.concatenate([of_ref[...], os_ref[...], og_ref[...]], axis=1)
    h1 = h_ref[...] + _dot(mixed, wo_ref[...])
    xn = _rms(h1, gx_ref[...]).astype(BF16)
    q = (_dot(xn, wq_ref[...]) * (HEAD_DIM ** -0.5)).astype(BF16)
    xw = XATTN_HEADS * HEAD_DIM
    outs = []
    scores = [_dot_nt(q[:, hh * HEAD_DIM:(hh + 1) * HEAD_DIM], kv_ref[:, hh * HEAD_DIM:(hh + 1) * HEAD_DIM])
              for hh in range(XATTN_HEADS)]
    for hh in range(XATTN_HEADS):
        v = kv_ref[:, xw + hh * HEAD_DIM:xw + (hh + 1) * HEAD_DIM]
        s = scores[hh]
        p = jnp.exp(s - jnp.max(s, axis=1, keepdims=True))
        l = jnp.sum(p, axis=1, keepdims=True)
        outs.append((_dot(p.astype(BF16), v) / l).astype(BF16))
    o = jnp.concatenate(outs, axis=1)
    o_ref[...] = h1 + _dot(o, wx_ref[...])


def mix_xattn(h, o_fox, o_swa, o_gla, w_out, gx, wq, kv, wo, *, tm):
    T, D = h.shape
    const = lambda shape: pl.BlockSpec(shape, lambda i: (0, 0))
    row = lambda w: pl.BlockSpec((tm, w), lambda i: (i, 0))
    return pl.pallas_call(
        _mix_xattn_kernel,
        out_shape=jax.ShapeDtypeStruct((T, D), F32),
        grid=(T // tm,),
        in_specs=[row(D), row(o_fox.shape[1]), row(o_swa.shape[1]), row(o_gla.shape[1]),
                  const(w_out.shape), const((1, D)), const(wq.shape), const(kv.shape), const(wo.shape)],
        out_specs=row(D),
        compiler_params=_cparams(("arbitrary",)),
        name="mix_xattn",
    )(h, o_fox, o_swa, o_gla, w_out, gx.reshape(1, D), wq, kv, wo)


def _ffn_kernel(x_ref, g_ref, wg_ref, wv_ref, cwg_ref, cwv_ref, cbg_ref, cbv_ref, wd_ref, fg_ref,
                o_ref, xn_ref, hg_ref, hv_ref, ug_ref, uv_ref, *, tm, sub, final):
    i = pl.program_id(0)
    f = pl.program_id(1)
    H = SUBLANES
    fc = wd_ref.shape[0]

    @pl.when(f == 0)
    def _():
        x = x_ref[...]
        xn_ref[...] = _rms(x, g_ref[...]).astype(BF16)
        o_ref[...] = x

    @pl.when(i == 0)
    def _():
        hg_ref[f] = jnp.zeros(hg_ref.shape[1:], F32)
        hv_ref[f] = jnp.zeros(hv_ref.shape[1:], F32)

    xn = xn_ref[...]

    def up(c, slot):
        ug_ref[slot, H:, :] = _dot(xn, wg_ref[:, c:c + sub])
        uv_ref[slot, H:, :] = _dot(xn, wv_ref[:, c:c + sub])

    def conv(u_ref, slot, h_ref, cw_ref, cb_ref, c):
        u_ref[slot, 0:H, :] = h_ref[f, :, c:c + sub]
        h_ref[f, :, c:c + sub] = u_ref[slot, tm:tm + H, :]
        return (((cw_ref[0:1, c:c + sub] * u_ref[slot, H - 2:H - 2 + tm, :]
                  + cw_ref[1:2, c:c + sub] * u_ref[slot, H - 1:H - 1 + tm, :])
                 + cw_ref[2:3, c:c + sub] * u_ref[slot, H:H + tm, :]) + cb_ref[:, c:c + sub])

    def down(c, slot):
        gate = conv(ug_ref, slot, hg_ref, cwg_ref, cbg_ref, c)
        val = conv(uv_ref, slot, hv_ref, cwv_ref, cbv_ref, c)
        act = (jax.nn.silu(gate) * val).astype(BF16)
        o_ref[...] += _dot(act, wd_ref[c:c + sub, :])

    up(0, 0)
    for n, c in enumerate(range(0, fc, sub)):
        if c + sub < fc:
            up(c + sub, (n + 1) % 2)
        down(c, n % 2)

    if final:
        @pl.when(f == pl.num_programs(1) - 1)
        def _():
            o_ref[...] = _rms(o_ref[...], fg_ref[...])


def conv_ffn(x, g, w_up, conv_w, conv_b, w_down, layer, final_g, *, tm, fc, final):
    T, D = x.shape
    dff = w_down.shape[0]
    L = conv_w.shape[0]
    nf = dff // fc
    cb = conv_b.reshape(L, 1, 2 * dff)
    return pl.pallas_call(
        functools.partial(_ffn_kernel, tm=tm, sub=MXU_COLS, final=final),
        out_shape=jax.ShapeDtypeStruct((T, D), F32),
        grid=(T // tm, nf),
        in_specs=[
            pl.BlockSpec((tm, D), lambda i, f: (i, 0)),
            pl.BlockSpec((1, D), lambda i, f: (0, 0)),
            pl.BlockSpec((D, fc), lambda i, f: (0, f)),
            pl.BlockSpec((D, fc), lambda i, f: (0, nf + f)),
            pl.BlockSpec((None, CONV_WIDTH, fc), lambda i, f: (layer, 0, f)),
            pl.BlockSpec((None, CONV_WIDTH, fc), lambda i, f: (layer, 0, nf + f)),
            pl.BlockSpec((None, 1, fc), lambda i, f: (layer, 0, f)),
            pl.BlockSpec((None, 1, fc), lambda i, f: (layer, 0, nf + f)),
            pl.BlockSpec((fc, D), lambda i, f: (f, 0)),
            pl.BlockSpec((1, D), lambda i, f: (0, 0)),
        ],
        out_specs=pl.BlockSpec((tm, D), lambda i, f: (i, 0)),
        scratch_shapes=[pltpu.VMEM((tm, D), BF16), pltpu.VMEM((nf, SUBLANES, fc), F32),
                        pltpu.VMEM((nf, SUBLANES, fc), F32),
                        pltpu.VMEM((2, tm + SUBLANES, MXU_COLS), F32),
                        pltpu.VMEM((2, tm + SUBLANES, MXU_COLS), F32)],
        compiler_params=_cparams(("arbitrary", "arbitrary")),
        name="conv_ffn",
    )(x, g.reshape(1, D), w_up, w_up, conv_w, conv_w, cb, cb, w_down, final_g.reshape(1, D))


TR_PREP, TC_PREP = 512, 512
TM_PROJ, TN_PROJ = 1024, 2304
TB_GATE = 512
TQ_FOX, TK_FOX, NH_FOX = 512, 512, 4
TB_SWA = 2048
TB_GLA, RB_GLA = 1024, 256
TM_MIX = 512
TM_FFN, FC_FFN = 512, 512


def _main_colscale():
    cs = np.ones((1, MAIN_COLS), np.float32)
    cs[0, C_FQ:C_FK] = HEAD_DIM ** -0.5 * LOG2E
    cs[0, C_SQ:C_SK] = HEAD_DIM ** -0.5
    cs[0, C_GQ:C_GK] = GLA_DK ** -0.5
    return jnp.asarray(cs)


def kernel(x, mem, w_in, b_fox_f, swa_sinks, t5_bias, w_gla_gate, b_gla_gate, gla_norm, w_mix_out, norm_mix,
           norm_xattn, norm_mem, wq_x, wkv_x, wo_x, norm_ffn, w_up, conv_w, conv_b, w_down, final_norm):
    depth = w_in.shape[0]
    _, T, D = x.shape
    M = mem.shape[1]
    h = x.reshape(T, D)
    memf = mem.reshape(M, D)
    colscale = _main_colscale()
    ones_kv = jnp.ones((1, wkv_x.shape[2]), F32)
    table = t5_table(t5_bias)
    pad_lanes = LANES - S_GLR - GLA_GATE_RANK
    w_main, w_small = prep_w_in(w_in, tr=TR_PREP, tc=TC_PREP)
    for l in range(depth):
        bvec = jnp.concatenate([b_fox_f[l], jnp.zeros((LANES - FOX_HEADS,), F32)]).reshape(1, LANES)
        wg_pad = jnp.concatenate([jnp.zeros((S_GLR, GLA_HEADS * GLA_DK), F32), w_gla_gate[l],
                                  jnp.zeros((pad_lanes, GLA_HEADS * GLA_DK), F32)], axis=0).astype(BF16)

        proj, ps = norm_proj(h, norm_mix[l], w_main, l, colscale, w_small, tm=TM_PROJ, tn=TN_PROJ, w_t=True)
        kx, row = fox_gate(ps, bvec, tb=TB_GATE)
        o_fox, w_down_b = fox_attention(proj, kx, row, w_down, l, tq=TQ_FOX, tk=TK_FOX, nh=NH_FOX)
        o_swa, (w_out_b, wq_b, wkv_b, wo_b) = swa_attention(proj, table, swa_sinks[l],
                                                           (w_mix_out, wq_x, wkv_x, wo_x), l, tb=TB_SWA)
        o_gla, w_up_b = gla(proj, ps, wg_pad, b_gla_gate[l].reshape(1, -1), gla_norm[l].reshape(1, -1),
                            w_up, l, tb=TB_GLA, rb=RB_GLA)
        kv = norm_proj(memf, norm_mem[l], wkv_b[None], 0, ones_kv, tm=M, tn=wkv_x.shape[2])
        h = mix_xattn(h, o_fox, o_swa, o_gla, w_out_b, norm_xattn[l], wq_b, kv, wo_b, tm=TM_MIX)
        h = conv_ffn(h, norm_ffn[l], w_up_b, conv_w, conv_b, w_down_b, l, final_norm,
                     tm=TM_FFN, fc=FC_FFN, final=(l == depth - 1))
    return h.reshape(x.shape)
```
